```python
import math
import jax
import jax.numpy as jnp
from jax import lax
import numpy as np

D_MODEL = 2048
BATCH = 2
SEQ = 4096
DEPTH = 2
DEC_BATCH = 8
DEC_SEQ = 4
PAST_LEN = 16384
PAGE_SIZE = 128

F32 = jnp.float32
HEAD_DIM = 128
GROUP_WIDTH = D_MODEL // 4
MIX_WIDTH = 4 * GROUP_WIDTH
GDN_HEADS = GROUP_WIDTH // HEAD_DIM
RET_HEADS = GROUP_WIDTH // HEAD_DIM
FOX_HEADS = GROUP_WIDTH // HEAD_DIM
RWKV_HEAD = 64
RWKV_HEADS = GROUP_WIDTH // RWKV_HEAD
CONV_WIDTH = 4
CHUNK = 64
FOX_BLOCK = 128
RWKV_W_RANK = 64
RWKV_A_RANK = 64
RWKV_G_RANK = 128
RET_GAMMA_BASE = 5.0
ROPE_BASE = 10000.0
N_MEM = 256
XATTN_HEADS = 4
XATTN_DIM = 128
D_FF = ((8 * D_MODEL + 767) // 768) * 256
GDN_COLS = 4 * GROUP_WIDTH + 2 * GDN_HEADS
RET_COLS = 4 * GROUP_WIDTH
RWKV_COLS = 3 * GROUP_WIDTH + RWKV_W_RANK + RWKV_A_RANK + RWKV_G_RANK
FOX_COLS = 4 * GROUP_WIDTH + FOX_HEADS
N_IN = GDN_COLS + RET_COLS + RWKV_COLS + FOX_COLS
NORM_EPS = 1e-6
GN_EPS = 64e-5

kernel_name = 'hybrid_parallel_heads_decode_step'


def _split(x, sizes):
    return jnp.split(x, np.cumsum(sizes)[:-1].tolist(), axis=-1)


def rms_norm(x, w=None):
    xf = x.astype(F32)
    y = xf * lax.rsqrt(jnp.mean(xf * xf, axis=-1, keepdims=True) + NORM_EPS)
    if w is not None:
        y = y * w.astype(F32)
    return y.astype(x.dtype)


def _l2norm(x):
    return x * lax.rsqrt(jnp.sum(x * x, axis=-1, keepdims=True) + NORM_EPS)


def causal_conv(x, buf, w):
    L = x.shape[1]
    xx = jnp.concatenate([buf.astype(x.dtype), x], axis=1)
    y = xx[:, 0:L] * w[0]
    for i in range(1, CONV_WIDTH):
        y = y + xx[:, i:i + L] * w[i]
    return y, xx[:, xx.shape[1] - (CONV_WIDTH - 1):]


def rotary(x, pos):
    half = x.shape[-1] // 2
    inv = 1.0 / (ROPE_BASE ** jnp.linspace(0.0, 1.0, half, dtype=F32))
    ang = pos.astype(F32)[:, None] * inv[None, :]
    cos = jnp.cos(ang)[None, :, None, :]
    sin = jnp.sin(ang)[None, :, None, :]
    xf = x.astype(F32)
    x1, x2 = xf[..., :half], xf[..., half:]
    return jnp.concatenate([x1 * cos - x2 * sin, x1 * sin + x2 * cos], axis=-1)


def _to_chunks(t, n, c):
    t = t.reshape((t.shape[0], n, c) + t.shape[2:])
    return jnp.moveaxis(t, 3, 1)


def gdn_chunked(q, k, v, beta, g, S0):
    B, L, H, DK = q.shape
    DV = v.shape[-1]
    C = math.gcd(L, CHUNK)
    N = L // C
    q, k, v = _to_chunks(q, N, C), _to_chunks(k, N, C), _to_chunks(v, N, C)
    beta, g = _to_chunks(beta, N, C), _to_chunks(g, N, C)
    G = jnp.cumsum(g, axis=-1)
    idx = jnp.arange(C)
    lower = idx[:, None] >= idx[None, :]
    strict = idx[:, None] > idx[None, :]
    dec = jnp.exp(jnp.where(lower, G[..., :, None] - G[..., None, :], -jnp.inf))
    kk = jnp.einsum('bhnid,bhnjd->bhnij', k, k)
    A = jnp.where(strict, beta[..., :, None] * kk * dec, 0.0) + jnp.eye(C, dtype=F32)
    eG = jnp.exp(G)[..., None]
    rhs = jnp.concatenate([beta[..., None] * v, beta[..., None] * eG * k], axis=-1)
    sol = lax.linalg.triangular_solve(A, rhs, left_side=True, lower=True, unit_diagonal=True)
    Uv, W = sol[..., :DV], sol[..., DV:]
    qk = jnp.einsum('bhnid,bhnjd->bhnij', q, k) * dec
    q_dec = q * eG
    G_last = G[..., -1:]
    k_dec = k * jnp.exp(G_last - G)[..., None]
    gC = jnp.exp(G_last[..., 0])

    def step(S, xs):
        Uv_n, W_n, qk_n, qd_n, kd_n, gC_n = xs
        U = Uv_n - W_n @ S
        o = qd_n @ S + qk_n @ U
        S = S * gC_n[..., None, None] + jnp.swapaxes(kd_n, -1, -2) @ U
        return S, o

    xs = tuple(jnp.moveaxis(t, 2, 0) for t in (Uv, W, qk, q_dec, k_dec, gC))
    S, o = lax.scan(step, S0, xs)
    o = jnp.moveaxis(jnp.moveaxis(o, 0, 2), 1, 3).reshape(B, L, H, DV)
    return o, S


def retention_chunked(q, k, v, log_gamma, S0):
    B, L, H, DK = q.shape
    DV = v.shape[-1]
    C = math.gcd(L, CHUNK)
    N = L // C
    q, k, v = _to_chunks(q, N, C), _to_chunks(k, N, C), _to_chunks(v, N, C)
    idx = jnp.arange(C, dtype=F32)
    rel = idx[:, None] - idx[None, :]
    lg = log_gamma[:, None, None]
    D = jnp.where(rel >= 0, jnp.exp(jnp.maximum(rel, 0.0) * lg), 0.0)
    qk = jnp.einsum('bhnid,bhnjd->bhnij', q, k) * D[None, :, None]
    o_intra = qk @ v
    xi = jnp.exp((idx + 1.0)[None, :] * log_gamma[:, None])
    zeta = jnp.exp((C - 1.0 - idx)[None, :] * log_gamma[:, None])
    kv = jnp.einsum('bhnjd,bhnje->bhnde', k * zeta[None, :, None, :, None], v)
    gC = jnp.exp(C * log_gamma)

    def step(S, kv_n):
        return S * gC[None, :, None, None] + kv_n, S

    S, S_prev = lax.scan(step, S0, jnp.moveaxis(kv, 2, 0))
    S_prev = jnp.moveaxis(S_prev, 0, 2)
    o = o_intra + (q * xi[None, :, None, :, None]) @ S_prev
    o = jnp.moveaxis(o, 1, 3).reshape(B, L, H, DV)
    return o, S


def rwkv7_scan(r, w, k, v, a, b, S0):
    def step(S, xs):
        r_t, w_t, k_t, v_t, a_t, b_t = xs
        Sa = jnp.einsum('bhvk,bhk->bhv', S, a_t)
        S = S * w_t[:, :, None, :] + Sa[..., None] * b_t[:, :, None, :] + v_t[..., None] * k_t[:, :, None, :]
        return S, jnp.einsum('bhvk,bhk->bhv', S, r_t)

    xs = tuple(jnp.moveaxis(t, 1, 0) for t in (r, w, k, v, a, b))
    S, o = lax.scan(step, S0, xs)
    return jnp.moveaxis(o, 0, 1), S


def fox_attention(q, k, v, cq, ck, q_pos, k_pos):
    B, Lq, H, D = q.shape
    Bq = math.gcd(Lq, FOX_BLOCK)
    nb = Lq // Bq
    qb = jnp.moveaxis(q.reshape(B, nb, Bq, H, D), 1, 0)
    cb = jnp.moveaxis(cq.reshape(B, nb, Bq, H), 1, 0)
    pb = q_pos.reshape(nb, Bq)
    ckT = jnp.swapaxes(ck, 1, 2)[:, :, None, :]
    scale = D ** -0.5

    def block(args):
        qi, ci, pi = args
        s = jnp.einsum('bqhd,bkhd->bhqk', qi, k).astype(F32) * scale
        s = s + jnp.swapaxes(ci, 1, 2)[..., None] - ckT
        s = jnp.where(k_pos[None, :] <= pi[:, None], s, -jnp.inf)
        pr = jax.nn.softmax(s, axis=-1).astype(v.dtype)
        return jnp.einsum('bhqk,bkhd->bqhd', pr, v)

    o = lax.map(block, (qb, cb, pb))
    return jnp.moveaxis(o, 0, 1).reshape(B, Lq, H, D)


def gdn_mixer(p, lw, conv_buf, S0):
    B, L, _ = p.shape
    H, Dh = GDN_HEADS, HEAD_DIM
    qkv, a, b, z = _split(p, [3 * GROUP_WIDTH, H, H, GROUP_WIDTH])
    qkv, conv_buf = causal_conv(qkv, conv_buf, lw['gdn_conv_w'])
    q, k, v = _split(jax.nn.silu(qkv).astype(F32), [GROUP_WIDTH] * 3)
    q = _l2norm(q.reshape(B, L, H, Dh)) * Dh ** -0.5
    k = _l2norm(k.reshape(B, L, H, Dh))
    v = v.reshape(B, L, H, Dh)
    beta = jax.nn.sigmoid(b.astype(F32))
    g = -jnp.exp(lw['gdn_A_log'].astype(F32)) * jax.nn.softplus(a.astype(F32) + lw['gdn_dt_bias'].astype(F32))
    o, S = gdn_chunked(q, k, v, beta, g, S0.astype(F32))
    o = rms_norm(o, lw['gdn_norm']) * jax.nn.silu(z.astype(F32).reshape(B, L, H, Dh))
    return o.reshape(B, L, GROUP_WIDTH).astype(p.dtype), conv_buf, S.astype(p.dtype)


def retention_mixer(p, pos, S0):
    B, L, _ = p.shape
    H, Dh = RET_HEADS, HEAD_DIM
    q, k, v, g = _split(p, [GROUP_WIDTH] * 4)
    q = rotary(q.reshape(B, L, H, Dh), pos)
    k = rotary(k.reshape(B, L, H, Dh), pos) * Dh ** -0.5
    v = v.astype(F32).reshape(B, L, H, Dh)
    log_gamma = jnp.log(1.0 - jnp.exp2(-(RET_GAMMA_BASE + jnp.arange(H, dtype=F32))))
    o, S = retention_chunked(q, k, v, log_gamma, S0.astype(F32))
    o = rms_norm(o) * jax.nn.silu(g.astype(F32).reshape(B, L, H, Dh))
    return o.reshape(B, L, GROUP_WIDTH).astype(p.dtype), S.astype(p.dtype)


def rwkv_mixer(p, lw, shift_buf, S0):
    B, L, _ = p.shape
    H, N = RWKV_HEADS, RWKV_HEAD
    prev = jnp.concatenate([shift_buf.astype(p.dtype), p[:, :-1]], axis=1)
    pm = p + (prev - p) * lw['rwkv_mu']
    r, k, v, wd, ad, gd = _split(pm, [GROUP_WIDTH] * 3 + [RWKV_W_RANK, RWKV_A_RANK, RWKV_G_RANK])
    w = -jax.nn.softplus(-(lw['rwkv_w0'] + jnp.tanh(wd) @ lw['rwkv_w_up']).astype(F32)) - 0.5
    decay = jnp.exp(-jnp.exp(w))
    a = jax.nn.sigmoid((lw['rwkv_a0'] + ad @ lw['rwkv_a_up']).astype(F32))
    gate = (jax.nn.sigmoid(gd) @ lw['rwkv_g_up']).astype(F32)

    def heads(t):
        return t.astype(F32).reshape(B, L, H, N)

    kk = _l2norm(heads(k * lw['rwkv_k_k']))
    k = k.astype(F32) * (1.0 + (a - 1.0) * lw['rwkv_k_a'].astype(F32))
    r_h, k_h, v_h, a_h, w_h = heads(r), heads(k), heads(v), heads(a), heads(decay)
    o, S = rwkv7_scan(r_h, w_h, k_h, v_h, -kk, kk * a_h, S0.astype(F32))
    mu = jnp.mean(o, axis=-1, keepdims=True)
    var = jnp.mean(jnp.square(o - mu), axis=-1, keepdims=True)
    o = ((o - mu) * lax.rsqrt(var + GN_EPS)).reshape(B, L, GROUP_WIDTH)
    o = o * lw['rwkv_ln_w'].astype(F32) + lw['rwkv_ln_b'].astype(F32)
    bonus = jnp.sum(r_h * k_h * lw['rwkv_r_k'].astype(F32), axis=-1, keepdims=True) * v_h
    o = (o + bonus.reshape(B, L, GROUP_WIDTH)) * gate
    return o.astype(p.dtype), p[:, L - 1:], S.astype(p.dtype)


def fox_mixer(p, lw, pos, past):
    B, L, _ = p.shape
    H, Dh = FOX_HEADS, HEAD_DIM
    q, k, v, g, f = _split(p, [GROUP_WIDTH] * 4 + [H])
    q = rms_norm(q.reshape(B, L, H, Dh), lw['fox_q_norm'])
    k = rms_norm(k.reshape(B, L, H, Dh), lw['fox_k_norm'])
    v = v.reshape(B, L, H, Dh)
    logf = jax.nn.log_sigmoid(f.astype(F32) + lw['fox_b_f'].astype(F32))
    c = jnp.cumsum(logf, axis=1)
    if past is None:
        keys, vals, ck, k_pos = k, v, c, pos
    else:
        pk, pv, plf = past
        plf = plf.astype(F32)
        suffix = lax.cumsum(plf, axis=1, reverse=True) - plf
        keys = jnp.concatenate([pk.astype(k.dtype), k], axis=1)
        vals = jnp.concatenate([pv.astype(v.dtype), v], axis=1)
        ck = jnp.concatenate([-suffix, c], axis=1)
        k_pos = jnp.arange(pk.shape[1] + L, dtype=jnp.int32)
    o = fox_attention(q, keys, vals, c, ck, pos, k_pos)
    o = o * jax.nn.sigmoid(g.reshape(B, L, H, Dh))
    return o.reshape(B, L, GROUP_WIDTH), k, v, logf.astype(p.dtype)


def memory_kv(mem, lw):
    B, M, _ = mem.shape
    kv = rms_norm(mem, lw['norm_mem']) @ lw['xattn_wkv']
    k, v = _split(kv, [XATTN_HEADS * XATTN_DIM] * 2)
    k = rms_norm(k.reshape(B, M, XATTN_HEADS, XATTN_DIM), lw['xattn_k_norm'])
    return k, v.reshape(B, M, XATTN_HEADS, XATTN_DIM)


def cross_attention(h, lw, mem_k, mem_v):
    B, L, _ = h.shape
    q = rms_norm((h @ lw['xattn_wq']).reshape(B, L, XATTN_HEADS, XATTN_DIM), lw['xattn_q_norm'])
    s = jnp.einsum('blhd,bmhd->bhlm', q, mem_k.astype(q.dtype)).astype(F32) * XATTN_DIM ** -0.5
    pr = jax.nn.softmax(s, axis=-1).astype(h.dtype)
    o = jnp.einsum('bhlm,bmhd->blhd', pr, mem_v.astype(h.dtype)).reshape(B, L, XATTN_HEADS * XATTN_DIM)
    return o @ lw['xattn_wo']


def swiglu(h, lw):
    return (jax.nn.silu(h @ lw['ffn_w_gate']) * (h @ lw['ffn_w_up'])) @ lw['ffn_w_down']


def decoder_layer(x, lw, pos, st, past, mem_k, mem_v):
    conv_buf, gdn_S, ret_S, shift_buf, rwkv_S = st
    p = rms_norm(x, lw['norm_mix']) @ lw['w_in']
    p_gdn, p_ret, p_rwkv, p_fox = _split(p, [GDN_COLS, RET_COLS, RWKV_COLS, FOX_COLS])
    o_gdn, conv_buf, gdn_S = gdn_mixer(p_gdn, lw, conv_buf, gdn_S)
    o_ret, ret_S = retention_mixer(p_ret, pos, ret_S)
    o_rwkv, shift_buf, rwkv_S = rwkv_mixer(p_rwkv, lw, shift_buf, rwkv_S)
    o_fox, fk, fv, flf = fox_mixer(p_fox, lw, pos, past)
    x = x + jnp.concatenate([o_gdn, o_ret, o_rwkv, o_fox], axis=-1) @ lw['w_out']
    x = x + cross_attention(rms_norm(x, lw['norm_x']), lw, mem_k, mem_v)
    x = x + swiglu(rms_norm(x, lw['norm_ffn']), lw)
    return x, (fk, fv, flf, conv_buf, gdn_S, ret_S, shift_buf, rwkv_S)


def setup_inputs(seed: int = 0) -> dict:
    key = jax.random.key(seed)
    ks = iter(jax.random.split(key, 64))

    def nrm(shape, scale=1.0):
        return jax.random.normal(next(ks), shape, F32) * scale

    def unif(shape, lo, hi):
        return jax.random.uniform(next(ks), shape, F32, minval=lo, maxval=hi)

    def gain(shape):
        return 1.0 + nrm(shape, 0.02)

    n_pages = PAST_LEN // PAGE_SIZE
    n_used = DEC_BATCH * n_pages
    n_pool = n_used + n_used // 4
    perm = jax.random.permutation(next(ks), n_pool)
    page_table = perm[:n_used].reshape(DEC_BATCH, n_pages).astype(jnp.int32)
    GW = GROUP_WIDTH
    XW = XATTN_HEADS * XATTN_DIM
    return {
        'x_prompt': nrm((BATCH, SEQ, D_MODEL)),
        'x_sample': nrm((DEC_BATCH, DEC_SEQ, D_MODEL)),
        'cache_fox_k': nrm((DEPTH, n_pool, PAGE_SIZE, FOX_HEADS, HEAD_DIM)),
        'cache_fox_v': nrm((DEPTH, n_pool, PAGE_SIZE, FOX_HEADS, HEAD_DIM)),
        'cache_fox_logf': jax.nn.log_sigmoid(unif((DEPTH, n_pool, PAGE_SIZE, FOX_HEADS), 2.0, 6.0)),
        'cache_mem_k': nrm((DEPTH, DEC_BATCH, N_MEM, XATTN_HEADS, XATTN_DIM)),
        'cache_mem_v': nrm((DEPTH, DEC_BATCH, N_MEM, XATTN_HEADS, XATTN_DIM)),
        'state_gdn_conv': nrm((DEPTH, DEC_BATCH, CONV_WIDTH - 1, 3 * GW)),
        'state_gdn_S': nrm((DEPTH, DEC_BATCH, GDN_HEADS, HEAD_DIM, HEAD_DIM), 0.1),
        'state_ret_S': nrm((DEPTH, DEC_BATCH, RET_HEADS, HEAD_DIM, HEAD_DIM), 0.1),
        'state_rwkv_shift': nrm((DEPTH, DEC_BATCH, 1, RWKV_COLS)),
        'state_rwkv_S': nrm((DEPTH, DEC_BATCH, RWKV_HEADS, RWKV_HEAD, RWKV_HEAD), 0.1),
        'page_table': page_table,
        'mem_prompt': nrm((BATCH, N_MEM, D_MODEL)),
        'norm_mix': gain((DEPTH, D_MODEL)),
        'w_in': nrm((DEPTH, D_MODEL, N_IN), D_MODEL ** -0.5),
        'gdn_conv_w': nrm((DEPTH, CONV_WIDTH, 3 * GW), CONV_WIDTH ** -0.5),
        'gdn_A_log': jnp.log(unif((DEPTH, GDN_HEADS), 1.0, 16.0)),
        'gdn_dt_bias': jnp.log(jnp.expm1(unif((DEPTH, GDN_HEADS), 0.001, 0.1))),
        'gdn_norm': gain((DEPTH, HEAD_DIM)),
        'rwkv_mu': unif((DEPTH, RWKV_COLS), 0.0, 1.0),
        'rwkv_w0': unif((DEPTH, GW), -6.0, -1.0),
        'rwkv_w_up': nrm((DEPTH, RWKV_W_RANK, GW), 0.1 * RWKV_W_RANK ** -0.5),
        'rwkv_a0': nrm((DEPTH, GW), 0.1),
        'rwkv_a_up': nrm((DEPTH, RWKV_A_RANK, GW), RWKV_A_RANK ** -0.5),
        'rwkv_g_up': nrm((DEPTH, RWKV_G_RANK, GW), RWKV_G_RANK ** -0.5),
        'rwkv_k_k': 0.85 + nrm((DEPTH, GW), 0.02),
        'rwkv_k_a': gain((DEPTH, GW)),
        'rwkv_r_k': nrm((DEPTH, RWKV_HEADS, RWKV_HEAD), 0.1),
        'rwkv_ln_w': gain((DEPTH, GW)),
        'rwkv_ln_b': nrm((DEPTH, GW), 0.02),
        'fox_b_f': unif((DEPTH, FOX_HEADS), 2.0, 5.0),
        'fox_q_norm': gain((DEPTH, HEAD_DIM)),
        'fox_k_norm': gain((DEPTH, HEAD_DIM)),
        'w_out': nrm((DEPTH, MIX_WIDTH, D_MODEL), MIX_WIDTH ** -0.5),
        'norm_x': gain((DEPTH, D_MODEL)),
        'norm_mem': gain((DEPTH, D_MODEL)),
        'xattn_wq': nrm((DEPTH, D_MODEL, XW), D_MODEL ** -0.5),
        'xattn_wkv': nrm((DEPTH, D_MODEL, 2 * XW), D_MODEL ** -0.5),
        'xattn_q_norm': gain((DEPTH, XATTN_DIM)),
        'xattn_k_norm': gain((DEPTH, XATTN_DIM)),
        'xattn_wo': nrm((DEPTH, XW, D_MODEL), XW ** -0.5),
        'norm_ffn': gain((DEPTH, D_MODEL)),
        'ffn_w_gate': nrm((DEPTH, D_MODEL, D_FF), D_MODEL ** -0.5),
        'ffn_w_up': nrm((DEPTH, D_MODEL, D_FF), D_MODEL ** -0.5),
        'ffn_w_down': nrm((DEPTH, D_FF, D_MODEL), D_FF ** -0.5),
    }


def reference(x_prompt, x_sample, cache_fox_k, cache_fox_v, cache_fox_logf, cache_mem_k, cache_mem_v,
              state_gdn_conv, state_gdn_S, state_ret_S, state_rwkv_shift, state_rwkv_S, page_table, mem_prompt,
              norm_mix, w_in, gdn_conv_w, gdn_A_log, gdn_dt_bias, gdn_norm, rwkv_mu, rwkv_w0, rwkv_w_up,
              rwkv_a0, rwkv_a_up, rwkv_g_up, rwkv_k_k, rwkv_k_a, rwkv_r_k, rwkv_ln_w, rwkv_ln_b, fox_b_f,
              fox_q_norm, fox_k_norm, w_out, norm_x, norm_mem, xattn_wq, xattn_wkv, xattn_q_norm, xattn_k_norm,
              xattn_wo, norm_ffn, ffn_w_gate, ffn_w_up, ffn_w_down):
    weights = {
        'norm_mix': norm_mix, 'w_in': w_in, 'gdn_conv_w': gdn_conv_w, 'gdn_A_log': gdn_A_log,
        'gdn_dt_bias': gdn_dt_bias, 'gdn_norm': gdn_norm, 'rwkv_mu': rwkv_mu, 'rwkv_w0': rwkv_w0,
        'rwkv_w_up': rwkv_w_up, 'rwkv_a0': rwkv_a0, 'rwkv_a_up': rwkv_a_up, 'rwkv_g_up': rwkv_g_up,
        'rwkv_k_k': rwkv_k_k, 'rwkv_k_a': rwkv_k_a, 'rwkv_r_k': rwkv_r_k, 'rwkv_ln_w': rwkv_ln_w,
        'rwkv_ln_b': rwkv_ln_b, 'fox_b_f': fox_b_f, 'fox_q_norm': fox_q_norm, 'fox_k_norm': fox_k_norm,
        'w_out': w_out, 'norm_x': norm_x, 'norm_mem': norm_mem, 'xattn_wq': xattn_wq, 'xattn_wkv': xattn_wkv,
        'xattn_q_norm': xattn_q_norm, 'xattn_k_norm': xattn_k_norm, 'xattn_wo': xattn_wo,
        'norm_ffn': norm_ffn, 'ffn_w_gate': ffn_w_gate, 'ffn_w_up': ffn_w_up, 'ffn_w_down': ffn_w_down,
    }
    Bp, Lp, _ = x_prompt.shape
    Bs, Ls, _ = x_sample.shape
    past_len = page_table.shape[1] * PAGE_SIZE
    pos_p = jnp.arange(Lp, dtype=jnp.int32)
    pos_s = past_len + jnp.arange(Ls, dtype=jnp.int32)
    dt = x_prompt.dtype
    zero_state = (jnp.zeros((Bp, CONV_WIDTH - 1, 3 * GROUP_WIDTH), dt),
                  jnp.zeros((Bp, GDN_HEADS, HEAD_DIM, HEAD_DIM), dt),
                  jnp.zeros((Bp, RET_HEADS, HEAD_DIM, HEAD_DIM), dt),
                  jnp.zeros((Bp, 1, RWKV_COLS), dt),
                  jnp.zeros((Bp, RWKV_HEADS, RWKV_HEAD, RWKV_HEAD), dt))
    yp, ys = x_prompt, x_sample
    p_st, s_st, p_mk, p_mv = [], [], [], []
    for l in range(DEPTH):
        lw = {name: arr[l] for name, arr in weights.items()}
        mk, mv = memory_kv(mem_prompt, lw)
        yp, st_p = decoder_layer(yp, lw, pos_p, zero_state, None, mk, mv)
        past = (cache_fox_k[l][page_table].reshape(Bs, past_len, FOX_HEADS, HEAD_DIM),
                cache_fox_v[l][page_table].reshape(Bs, past_len, FOX_HEADS, HEAD_DIM),
                cache_fox_logf[l][page_table].reshape(Bs, past_len, FOX_HEADS))
        st_in = (state_gdn_conv[l], state_gdn_S[l], state_ret_S[l], state_rwkv_shift[l], state_rwkv_S[l])
        ys, st_s = decoder_layer(ys, lw, pos_s, st_in, past, cache_mem_k[l], cache_mem_v[l])
        p_st.append(st_p)
        s_st.append(st_s)
        p_mk.append(mk)
        p_mv.append(mv)

    def stk(seq, i):
        return jnp.stack([e[i] for e in seq], axis=0)

    p_fox_k, p_fox_v, p_fox_logf = stk(p_st, 0), stk(p_st, 1), stk(p_st, 2)
    p_gdn_conv, p_gdn_S, p_ret_S = stk(p_st, 3), stk(p_st, 4), stk(p_st, 5)
    p_rwkv_shift, p_rwkv_S = stk(p_st, 6), stk(p_st, 7)
    p_mem_k, p_mem_v = jnp.stack(p_mk, axis=0), jnp.stack(p_mv, axis=0)
    s_fox_k, s_fox_v, s_fox_logf = stk(s_st, 0), stk(s_st, 1), stk(s_st, 2)
    s_gdn_conv, s_gdn_S, s_ret_S = stk(s_st, 3), stk(s_st, 4), stk(s_st, 5)
    s_rwkv_shift, s_rwkv_S = stk(s_st, 6), stk(s_st, 7)
    return (yp, ys, p_fox_k, p_fox_v, p_fox_logf, p_mem_k, p_mem_v, p_gdn_conv, p_gdn_S, p_ret_S,
            p_rwkv_shift, p_rwkv_S, s_fox_k, s_fox_v, s_fox_logf, s_gdn_conv, s_gdn_S, s_ret_S,
            s_rwkv_shift, s_rwkv_S)
```

```python
import functools
import math

import jax
import jax.numpy as jnp
import numpy as np
from jax import lax
from jax.experimental import pallas as pl
from jax.experimental.pallas import tpu as pltpu

F32 = jnp.float32
BF16 = jnp.bfloat16

LANES = 128
SUBLANES = 8
VMEM_LIMIT = 56 * 1024 * 1024

D_MODEL = 2048
GW = D_MODEL // 4
HD = 128
NH = GW // HD
RWKV_HEAD = 64
CONV_WIDTH = 4
PAGE = 128
N_MEM = 256
XW = 512
D_FF = 5632
NORM_EPS = 1e-6
GN_EPS = 64e-5
RET_GAMMA_BASE = 5.0
ROPE_BASE = 10000.0
CHUNK = 64
SAMPLE_PAD = 64

NP_COLS = 8192
CB_FOX, CB_RET, CB_GDN, CB_GDN_Z, CB_RWKV, CB_SMALL = 0, 16, 32, 44, 48, 62
SM_A, SM_B, SM_F = 0, 4, 8


def _cparams(sem):
    return pltpu.CompilerParams(dimension_semantics=sem, vmem_limit_bytes=VMEM_LIMIT)


def _dot(a, b):
    return jnp.dot(a.astype(BF16), b.astype(BF16), preferred_element_type=F32)


def _dot_nt(a, b):
    return lax.dot_general(a.astype(BF16), b.astype(BF16), (((1,), (1,)), ((), ())),
                           preferred_element_type=F32)


def _dot_tn(a, b):
    return lax.dot_general(a.astype(BF16), b.astype(BF16), (((0,), (0,)), ((), ())),
                           preferred_element_type=F32)


def _split3(x):
    hi = x.astype(BF16)
    r = x - hi.astype(F32)
    mid = r.astype(BF16)
    lo = (r - mid.astype(F32)).astype(BF16)
    return hi, mid, lo


def _dot_exact_lhs(m, x):
    hi, mid, lo = _split3(x)
    d = lambda p: jnp.dot(m, p, preferred_element_type=F32)
    return d(hi) + d(mid) + d(lo)


def _dot_exact_rhs(x, m):
    hi, mid, lo = _split3(x)
    d = lambda p: jnp.dot(p, m, preferred_element_type=F32)
    return d(hi) + d(mid) + d(lo)


def _dot_hp(a, b):
    ah = a.astype(BF16)
    al = (a - ah.astype(F32)).astype(BF16)
    bh = b.astype(BF16)
    bl = (b - bh.astype(F32)).astype(BF16)
    d = lambda p, q: jnp.dot(p, q, preferred_element_type=F32)
    return d(ah, bh) + d(ah, bl) + d(al, bh)


def _unit_lower_inv(n):
    c = n.shape[0]
    ii = lax.broadcasted_iota(jnp.int32, (c, c), 0)
    jj = lax.broadcasted_iota(jnp.int32, (c, c), 1)
    p = jnp.where(ii == jj, 1.0, 0.0).astype(F32) - n
    q = n
    for _ in range(int(math.log2(c)) - 1):
        q = _dot_hp(q, q)
        p = p + _dot_hp(p, q)
    return p


def _chunk_tri(lb, chunk):
    ii = lax.broadcasted_iota(jnp.int32, (lb, lb), 0)
    jj = lax.broadcasted_iota(jnp.int32, (lb, lb), 1)
    same = (ii // chunk) == (jj // chunk)
    return jnp.where(jnp.logical_and(ii >= jj, same), 1.0, 0.0).astype(BF16)


def _shift_rows(x, prev8, s):
    rolled = pltpu.roll(x, s, axis=0)
    pr = pltpu.roll(prev8, s, axis=0)
    row = lax.broadcasted_iota(jnp.int32, (SUBLANES, x.shape[1]), 0)
    top = jnp.where(row < s, pr, rolled[:SUBLANES])
    return jnp.concatenate([top, rolled[SUBLANES:]], axis=0)


def _softplus(x):
    return jnp.maximum(x, 0.0) + jnp.log1p(jnp.exp(-jnp.abs(x)))


def _sigmoid(x):
    return jax.nn.sigmoid(x)


def _silu(x):
    return x * jax.nn.sigmoid(x)


def _lane_col(x, idx):
    lane = lax.broadcasted_iota(jnp.int32, x.shape, 1)
    return jnp.sum(jnp.where(lane == idx, x, 0.0), axis=-1, keepdims=True)


def _rms(x, w=None):
    y = x * lax.rsqrt(jnp.mean(x * x, axis=-1, keepdims=True) + NORM_EPS)
    return y if w is None else y * w


def _row_valid(lb_index, lb, lvalid, width):
    row = lax.broadcasted_iota(jnp.int32, (lb, width), 0) + lb_index * lb
    return row < lvalid


def _norm_matmul_kernel(x_ref, g_ref, w_ref, hw_ref, o_ref, xn_ref, *, norm_tiles):
    j = pl.program_id(1)

    @pl.when(j == 0)
    def _():
        xn_ref[...] = _rms(x_ref[...], g_ref[...]).astype(BF16)

    acc = jnp.dot(xn_ref[...], w_ref[...], preferred_element_type=F32)
    if norm_tiles == 0:
        o_ref[...] = acc
    else:
        @pl.when(j < norm_tiles)
        def _():
            hw = hw_ref[...]
            for h in range(acc.shape[1] // HD):
                sl = slice(h * HD, (h + 1) * HD)
                o_ref[:, sl] = _rms(acc[:, sl], hw)

        @pl.when(j >= norm_tiles)
        def _():
            o_ref[...] = acc


def _norm_matmul(x, g, w, *, tm, tn, head_w=None, norm_tiles=0):
    m, k = x.shape
    n = w.shape[1]
    if head_w is None:
        head_w = jnp.ones((HD,), F32)
    return pl.pallas_call(
        functools.partial(_norm_matmul_kernel, norm_tiles=norm_tiles),
        grid=(m // tm, n // tn),
        in_specs=[
            pl.BlockSpec((tm, k), lambda i, j: (i, 0)),
            pl.BlockSpec((1, k), lambda i, j: (0, 0)),
            pl.BlockSpec((k, tn), lambda i, j: (0, j)),
            pl.BlockSpec((1, HD), lambda i, j: (0, 0)),
        ],
        out_specs=pl.BlockSpec((tm, tn), lambda i, j: (i, j)),
        out_shape=jax.ShapeDtypeStruct((m, n), F32),
        scratch_shapes=[pltpu.VMEM((tm, k), BF16)],
        compiler_params=_cparams(("parallel", "arbitrary")),
        name="norm_matmul",
    )(x, g.reshape(1, k), w, head_w.reshape(1, HD))


def _matmul_res_kernel(a_ref, w_ref, r_ref, o_ref):
    o_ref[...] = r_ref[...] + jnp.dot(a_ref[...], w_ref[...], preferred_element_type=F32)


def _matmul_res(a, w, res, *, tm, tn):
    m, k = a.shape
    n = w.shape[1]
    return pl.pallas_call(
        _matmul_res_kernel,
        grid=(m // tm, n // tn),
        in_specs=[
            pl.BlockSpec((tm, k), lambda i, j: (i, 0)),
            pl.BlockSpec((k, tn), lambda i, j: (0, j)),
            pl.BlockSpec((tm, tn), lambda i, j: (i, j)),
        ],
        out_specs=pl.BlockSpec((tm, tn), lambda i, j: (i, j)),
        out_shape=jax.ShapeDtypeStruct((m, n), F32),
        compiler_params=_cparams(("parallel", "parallel")),
        name="matmul_res",
    )(a, w, res)


def _swiglu_up_kernel(x_ref, g_ref, wg_ref, wu_ref, o_ref, xn_ref):
    @pl.when(pl.program_id(1) == 0)
    def _():
        xn_ref[...] = _rms(x_ref[...], g_ref[...]).astype(BF16)

    xn = xn_ref[...]
    gate = jnp.dot(xn, wg_ref[...], preferred_element_type=F32)
    up = jnp.dot(xn, wu_ref[...], preferred_element_type=F32)
    o_ref[...] = (_silu(gate) * up).astype(BF16)


def _swiglu_up(x, g, wg, wu, *, tm, tn):
    m, k = x.shape
    n = wg.shape[1]
    return pl.pallas_call(
        _swiglu_up_kernel,
        grid=(m // tm, n // tn),
        in_specs=[
            pl.BlockSpec((tm, k), lambda i, j: (i, 0)),
            pl.BlockSpec((1, k), lambda i, j: (0, 0)),
            pl.BlockSpec((k, tn), lambda i, j: (0, j)),
            pl.BlockSpec((k, tn), lambda i, j: (0, j)),
        ],
        out_specs=pl.BlockSpec((tm, tn), lambda i, j: (i, j)),
        out_shape=jax.ShapeDtypeStruct((m, n), BF16),
        scratch_shapes=[pltpu.VMEM((tm, k), BF16)],
        compiler_params=_cparams(("parallel", "arbitrary")),
        name="swiglu_up",
    )(x, g.reshape(1, k), wg, wu)


def _xattn_kernel(q_ref, k_ref, v_ref, o_ref):
    scale = HD ** -0.5
    for h in range(NH):
        sl = slice(h * HD, (h + 1) * HD)
        s = _dot_nt(q_ref[:, sl], k_ref[:, sl]) * scale
        m = jnp.max(s, axis=-1, keepdims=True)
        p = jnp.exp(s - m)
        l = jnp.sum(p, axis=-1, keepdims=True)
        o_ref[:, sl] = (_dot(p, v_ref[:, sl]) / l).astype(BF16)


def _xattn(q, mem_k, mem_v, *, row0, nrows, tq, rows_per_seq):
    tiles_per_seq = rows_per_seq // tq
    t0 = row0 // tq
    return pl.pallas_call(
        _xattn_kernel,
        grid=(nrows // tq,),
        in_specs=[
            pl.BlockSpec((tq, XW), lambda i: (t0 + i, 0)),
            pl.BlockSpec((None, N_MEM, XW), lambda i: (i // tiles_per_seq, 0, 0)),
            pl.BlockSpec((None, N_MEM, XW), lambda i: (i // tiles_per_seq, 0, 0)),
        ],
        out_specs=pl.BlockSpec((tq, XW), lambda i: (i, 0)),
        out_shape=jax.ShapeDtypeStruct((nrows, XW), BF16),
        compiler_params=_cparams(("parallel",)),
        name="xattn",
    )(q, mem_k, mem_v)


def _gdn_kernel(q_ref, k_ref, v_ref, z_ref, sm_ref, cwq_ref, cwk_ref, cwv_ref, cq_ref, ck_ref, cv_ref,
                s0_ref, alog_ref, dtb_ref, nw_ref, o_ref, so_ref, s_ref, prev_ref, gt_ref,
                *, lb, lvalid, nlb):
    h = pl.program_id(1)
    ib = pl.program_id(2)
    masked = lvalid < nlb * lb

    @pl.when(ib == 0)
    def _():
        s_ref[...] = s0_ref[...]
        prev_ref[0] = cq_ref[...]
        prev_ref[1] = ck_ref[...]
        prev_ref[2] = cv_ref[...]

    def conv(x_ref, w_ref, i):
        x = x_ref[...]
        w = w_ref[...]
        prev = prev_ref[i]
        y = x * w[3:4]
        for s in (1, 2, 3):
            y = y + _shift_rows(x, prev, s) * w[3 - s:4 - s]
        prev_ref[i] = x[lb - SUBLANES:]
        return _silu(y)

    q = conv(q_ref, cwq_ref, 0)
    k = conv(k_ref, cwk_ref, 1)
    v = conv(v_ref, cwv_ref, 2)
    q = q * lax.rsqrt(jnp.sum(q * q, axis=-1, keepdims=True) + NORM_EPS) * (HD ** -0.5)
    k = k * lax.rsqrt(jnp.sum(k * k, axis=-1, keepdims=True) + NORM_EPS)

    sm = sm_ref[...]
    g_blk = -jnp.exp(alog_ref[...]) * _softplus(sm + dtb_ref[...])
    beta_blk = _sigmoid(sm)
    if masked:
        valid = _row_valid(ib, lb, lvalid, LANES)
        g_blk = jnp.where(valid, g_blk, 0.0)
        beta_blk = jnp.where(valid, beta_blk, 0.0)
    gc_blk = _dot_exact_lhs(_chunk_tri(lb, CHUNK), g_blk)
    gt_ref[...] = gc_blk.T
    g_col_all = _lane_col(gc_blk, SM_A + h)
    beta_all = _lane_col(beta_blk, SM_B + h)
    g_row_all = gt_ref[pl.ds(SM_A + h, 1), :]

    ii = lax.broadcasted_iota(jnp.int32, (CHUNK, CHUNK), 0)
    jj = lax.broadcasted_iota(jnp.int32, (CHUNK, CHUNK), 1)
    nw = nw_ref[...]
    for c in range(lb // CHUNK):
        r = slice(c * CHUNK, (c + 1) * CHUNK)
        qc, kc, vc = q[r], k[r], v[r]
        g_col = g_col_all[r]
        g_row = g_row_all[:, r]
        beta = beta_all[r]
        dec = jnp.exp(jnp.where(ii >= jj, g_col - g_row, -jnp.inf))
        n = jnp.where(ii > jj, beta * _dot_nt(kc, kc) * dec, 0.0)
        ainv = _unit_lower_inv(n)
        eg = jnp.exp(g_col)
        uv = _dot_hp(ainv, beta * vc)
        w = _dot_hp(ainv, beta * eg * kc)
        qk = _dot_nt(qc, kc) * dec
        s = s_ref[...]
        u = uv - _dot(w, s)
        o = _dot(qc * eg, s) + _dot(qk, u)
        g_last = g_col[CHUNK - 1:]
        s_ref[...] = s * jnp.exp(g_last) + _dot_tn(kc * jnp.exp(g_last - g_col), u)
        o_ref[r, :] = (_rms(o, nw) * _silu(z_ref[r, :])).astype(BF16)

    @pl.when(ib == nlb - 1)
    def _():
        so_ref[...] = s_ref[...]


def _gdn(p, conv_w, conv_init, s0, a_log, dt_bias, norm_w, *, row0, nseq, lpad, lvalid, lb):
    nlb = lpad // lb
    rb = lambda s, i: (row0 + s * lpad) // lb + i
    pblk = lambda cb: pl.BlockSpec((lb, HD), lambda s, h, i: (rb(s, i), cb + h))
    cwblk = lambda j: pl.BlockSpec((CONV_WIDTH, HD), lambda s, h, i: (0, j * NH + h))
    ciblk = lambda j: pl.BlockSpec((None, SUBLANES, HD), lambda s, h, i: (s, 0, j * NH + h))
    vec = pl.BlockSpec((1, LANES), lambda s, h, i: (0, 0))
    lane_pad = lambda x: jnp.zeros((1, LANES), F32).at[0, :x.shape[0]].set(x)
    return pl.pallas_call(
        functools.partial(_gdn_kernel, lb=lb, lvalid=lvalid, nlb=nlb),
        grid=(nseq, NH, nlb),
        in_specs=[
            pblk(CB_GDN), pblk(CB_GDN + NH), pblk(CB_GDN + 2 * NH), pblk(CB_GDN_Z),
            pl.BlockSpec((lb, LANES), lambda s, h, i: (rb(s, i), CB_SMALL)),
            cwblk(0), cwblk(1), cwblk(2), ciblk(0), ciblk(1), ciblk(2),
            pl.BlockSpec((None, None, HD, HD), lambda s, h, i: (s, h, 0, 0)),
            vec, vec, vec,
        ],
        out_specs=[
            pl.BlockSpec((lb, HD), lambda s, h, i: (s * nlb + i, h)),
            pl.BlockSpec((None, None, HD, HD), lambda s, h, i: (s, h, 0, 0)),
        ],
        out_shape=[
            jax.ShapeDtypeStruct((nseq * lpad, GW), BF16),
            jax.ShapeDtypeStruct((nseq, NH, HD, HD), F32),
        ],
        scratch_shapes=[
            pltpu.VMEM((HD, HD), F32),
            pltpu.VMEM((3, SUBLANES, HD), F32),
            pltpu.VMEM((LANES, lb), F32),
        ],
        compiler_params=_cparams(("parallel", "parallel", "arbitrary")),
        name="gdn",
    )(p, p, p, p, p, conv_w, conv_w, conv_w, conv_init, conv_init, conv_init, s0,
      lane_pad(a_log), lane_pad(dt_bias), norm_w.reshape(1, HD))


def _ret_kernel(q_ref, k_ref, v_ref, g_ref, cos_ref, sin_ref, lg_ref, s0_ref, o_ref, so_ref, s_ref,
                *, lb, cv, nlb):
    ib = pl.program_id(2)

    @pl.when(ib == 0)
    def _():
        s_ref[...] = s0_ref[...]

    cos = cos_ref[...]
    sin = sin_ref[...]
    rot = lambda x: x * cos + pltpu.roll(x, HD // 2, axis=1) * sin
    q = rot(q_ref[...])
    k = rot(k_ref[...]) * (HD ** -0.5)
    v = v_ref[...]
    lg = lg_ref[...][:, 0:1]

    ii = lax.broadcasted_iota(jnp.int32, (CHUNK, CHUNK), 0)
    jj = lax.broadcasted_iota(jnp.int32, (CHUNK, CHUNK), 1)
    rel = (ii - jj).astype(F32)
    dmat = jnp.where(rel >= 0, jnp.exp(jnp.maximum(rel, 0.0) * lg), 0.0)
    idx = lax.broadcasted_iota(jnp.int32, (CHUNK, 1), 0)
    idf = idx.astype(F32)
    xi = jnp.exp((idf + 1.0) * lg)
    zeta = jnp.where(idx < cv, jnp.exp((cv - 1.0 - idf) * lg), 0.0)
    gc = jnp.exp(cv * lg)
    for c in range(lb // CHUNK):
        r = slice(c * CHUNK, (c + 1) * CHUNK)
        qc, kc, vc = q[r], k[r], v[r]
        s = s_ref[...]
        qk = _dot_nt(qc, kc) * dmat
        o = _dot(qk, vc) + _dot(qc * xi, s)
        s_ref[...] = s * gc + _dot_tn(kc * zeta, vc)
        o_ref[r, :] = (_rms(o) * _silu(g_ref[r, :])).astype(BF16)

    @pl.when(ib == nlb - 1)
    def _():
        so_ref[...] = s_ref[...]


def _ret(p, cos_t, sin_t, log_gamma, s0, *, row0, nseq, lpad, lvalid, lb):
    nlb = lpad // lb
    cv = CHUNK if lvalid == lpad else lvalid
    assert cv == CHUNK or (lpad == CHUNK and 0 < lvalid < CHUNK)
    rb = lambda s, i: (row0 + s * lpad) // lb + i
    pblk = lambda cb: pl.BlockSpec((lb, HD), lambda s, h, i: (rb(s, i), cb + h))
    tblk = pl.BlockSpec((lb, HD), lambda s, h, i: (i, 0))
    sblk = pl.BlockSpec((None, None, HD, HD), lambda s, h, i: (s, h, 0, 0))
    lg = jnp.broadcast_to(log_gamma[:, None, None], (NH, 1, LANES))
    return pl.pallas_call(
        functools.partial(_ret_kernel, lb=lb, cv=cv, nlb=nlb),
        grid=(nseq, NH, nlb),
        in_specs=[
            pblk(CB_RET), pblk(CB_RET + NH), pblk(CB_RET + 2 * NH), pblk(CB_RET + 3 * NH),
            tblk, tblk,
            pl.BlockSpec((None, 1, LANES), lambda s, h, i: (h, 0, 0)),
            sblk,
        ],
        out_specs=[pl.BlockSpec((lb, HD), lambda s, h, i: (s * nlb + i, h)), sblk],
        out_shape=[
            jax.ShapeDtypeStruct((nseq * lpad, GW), BF16),
            jax.ShapeDtypeStruct((nseq, NH, HD, HD), F32),
        ],
        scratch_shapes=[pltpu.VMEM((HD, HD), F32)],
        compiler_params=_cparams(("parallel", "parallel", "arbitrary")),
        name="retention",
    )(p, p, p, p, cos_t, sin_t, lg, s0)


def _half_sum(x):
    lane = lax.broadcasted_iota(jnp.int32, x.shape, 1)
    lo = lane < RWKV_HEAD
    s_lo = jnp.sum(jnp.where(lo, x, 0.0), axis=-1, keepdims=True)
    s_hi = jnp.sum(jnp.where(lo, 0.0, x), axis=-1, keepdims=True)
    return jnp.where(lo, s_lo, s_hi)


def _rwkv_kernel(r_ref, k_ref, v_ref, wa_ref, gd_ref, mur_ref, muk_ref, muv_ref, muwa_ref, mugd_ref,
                 sh_r_ref, sh_k_ref, sh_v_ref, sh_wa_ref, sh_gd_ref,
                 wup_ref, aup_ref, gup_ref, w0_ref, a0_ref, kk_ref, ka_ref, rk_ref, lnw_ref, lnb_ref,
                 s0_ref, o_ref, so_ref, s_ref, prev_ref, *, lb, lvalid, nlb):
    ib = pl.program_id(2)
    masked = lvalid < nlb * lb

    @pl.when(ib == 0)
    def _():
        s_ref[...] = s0_ref[...]
        prev_ref[0] = sh_r_ref[...]
        prev_ref[1] = sh_k_ref[...]
        prev_ref[2] = sh_v_ref[...]
        prev_ref[3] = sh_wa_ref[...]
        prev_ref[4] = sh_gd_ref[...]

    def shifted(x_ref, mu_ref, i):
        x = x_ref[...]
        prev = _shift_rows(x, prev_ref[i], 1)
        prev_ref[i] = x[lb - SUBLANES:]
        return x + (prev - x) * mu_ref[...]

    r = shifted(r_ref, mur_ref, 0)
    k = shifted(k_ref, muk_ref, 1)
    v = shifted(v_ref, muv_ref, 2)
    wa = shifted(wa_ref, muwa_ref, 3)
    gd = shifted(gd_ref, mugd_ref, 4)

    w_raw = -_softplus(-(w0_ref[...] + _dot(jnp.tanh(wa), wup_ref[...]))) - 0.5
    logw = -jnp.exp(w_raw)
    a_sig = _sigmoid(a0_ref[...] + _dot(wa, aup_ref[...]))
    gate = _dot(_sigmoid(gd), gup_ref[...])
    kk = k * kk_ref[...]
    kk = kk * lax.rsqrt(_half_sum(kk * kk) + NORM_EPS)
    kp = k * (1.0 + (a_sig - 1.0) * ka_ref[...])
    rec_a = -kk
    rec_b = kk * a_sig
    if masked:
        valid = _row_valid(ib, lb, lvalid, LANES)
        zero = lambda x: jnp.where(valid, x, 0.0)
        logw, rec_a, rec_b, kp, v = zero(logw), zero(rec_a), zero(rec_b), zero(kp), zero(v)

    cum = _dot_exact_lhs(_chunk_tri(lb, CHUNK), logw)
    e_pos = jnp.exp(cum)
    e_neg = jnp.exp(-cum)
    at_all = rec_a * jnp.exp(cum - logw)
    bt_all = rec_b * e_neg
    kt_all = kp * e_neg
    rt_all = r * e_pos

    lane = lax.broadcasted_iota(jnp.int32, (1, LANES), 1)
    head_masks = [(lane < RWKV_HEAD).astype(F32), (lane >= RWKV_HEAD).astype(F32)]
    bi = lax.broadcasted_iota(jnp.int32, (LANES, LANES), 0) // RWKV_HEAD
    bj = lax.broadcasted_iota(jnp.int32, (LANES, LANES), 1) // RWKV_HEAD
    block_diag = bi == bj
    ii = lax.broadcasted_iota(jnp.int32, (CHUNK, CHUNK), 0)
    jj = lax.broadcasted_iota(jnp.int32, (CHUNK, CHUNK), 1)

    for c in range(lb // CHUNK):
        rs = slice(c * CHUNK, (c + 1) * CHUNK)
        at, bt, kt, rt, vc = at_all[rs], bt_all[rs], kt_all[rs], rt_all[rs], v[rs]
        cum_c = cum[rs]
        cum_end = cum_c[CHUNK - 1:]
        to_end = jnp.exp(cum_end - cum_c)
        s = s_ref[...]
        u = jnp.zeros((CHUNK, LANES), F32)
        o = jnp.zeros((CHUNK, LANES), F32)
        for me in head_masks:
            at_e = at * me
            rt_e = rt * me
            l_ab = jnp.where(ii > jj, _dot_nt(at_e, bt), 0.0)
            l_ak = jnp.where(ii > jj, _dot_nt(at_e, kt), 0.0)
            l_rb = jnp.where(ii >= jj, _dot_nt(rt_e, bt), 0.0)
            l_rk = jnp.where(ii >= jj, _dot_nt(rt_e, kt), 0.0)
            inv = _unit_lower_inv(-l_ab)
            u_e = _dot_hp(inv, _dot_nt(at_e, s) + _dot(l_ak, vc)) * me
            o = o + (_dot_nt(rt_e, s) + _dot(l_rb, u_e) + _dot(l_rk, vc)) * me
            u = u + u_e
        upd = _dot_tn(u, rec_b[rs] * to_end) + _dot_tn(vc, kp[rs] * to_end)
        s_ref[...] = s * jnp.exp(cum_end) + jnp.where(block_diag, upd, 0.0)

        inv_n = 1.0 / RWKV_HEAD
        mu = _half_sum(o) * inv_n
        var = _half_sum(jnp.square(o - mu)) * inv_n
        on = (o - mu) * lax.rsqrt(var + GN_EPS) * lnw_ref[...] + lnb_ref[...]
        bonus = _half_sum(r[rs] * kp[rs] * rk_ref[...]) * vc
        o_ref[rs, :] = ((on + bonus) * gate[rs]).astype(BF16)

    @pl.when(ib == nlb - 1)
    def _():
        so_ref[...] = s_ref[...]


def _rwkv(p, lw, shift_init, s0, *, row0, nseq, lpad, lvalid, lb):
    nlb = lpad // lb
    npair = GW // LANES
    rb = lambda s, i: (row0 + s * lpad) // lb + i
    pblk = lambda cb, per_pair: pl.BlockSpec(
        (lb, LANES), lambda s, j, i: (rb(s, i), cb + (j if per_pair else 0)))
    mublk = lambda cb, per_pair: pl.BlockSpec((1, LANES), lambda s, j, i: (0, cb + (j if per_pair else 0)))
    shblk = lambda cb, per_pair: pl.BlockSpec(
        (None, SUBLANES, LANES), lambda s, j, i: (s, 0, cb + (j if per_pair else 0)))
    pair_vec = pl.BlockSpec((1, LANES), lambda s, j, i: (0, j))
    pair_mat = pl.BlockSpec((LANES, LANES), lambda s, j, i: (0, j))
    sblk = pl.BlockSpec((None, None, LANES, LANES), lambda s, j, i: (s, j, 0, 0))
    mu = lw['rwkv_mu'].reshape(1, -1)
    zeros64 = jnp.zeros((64, GW), F32)
    wup = jnp.concatenate([lw['rwkv_w_up'], zeros64], axis=0).astype(BF16)
    aup = jnp.concatenate([zeros64, lw['rwkv_a_up']], axis=0).astype(BF16)
    row = lambda x: x.reshape(1, GW)
    blocks = [(0, True), (4, True), (8, True), (12, False), (13, False)]
    return pl.pallas_call(
        functools.partial(_rwkv_kernel, lb=lb, lvalid=lvalid, nlb=nlb),
        grid=(nseq, npair, nlb),
        in_specs=(
            [pblk(CB_RWKV + cb, pp) for cb, pp in blocks]
            + [mublk(cb, pp) for cb, pp in blocks]
            + [shblk(cb, pp) for cb, pp in blocks]
            + [pair_mat, pair_mat, pair_mat] + [pair_vec] * 7 + [sblk]
        ),
        out_specs=[pl.BlockSpec((lb, LANES), lambda s, j, i: (s * nlb + i, j)), sblk],
        out_shape=[
            jax.ShapeDtypeStruct((nseq * lpad, GW), BF16),
            jax.ShapeDtypeStruct((nseq, npair, LANES, LANES), F32),
        ],
        scratch_shapes=[pltpu.VMEM((LANES, LANES), F32), pltpu.VMEM((5, SUBLANES, LANES), F32)],
        compiler_params=_cparams(("parallel", "parallel", "arbitrary")),
        name="rwkv7",
    )(p, p, p, p, p, mu, mu, mu, mu, mu, shift_init, shift_init, shift_init, shift_init, shift_init,
      wup, aup, lw['rwkv_g_up'].astype(BF16), row(lw['rwkv_w0']), row(lw['rwkv_a0']), row(lw['rwkv_k_k']),
      row(lw['rwkv_k_a']), row(lw['rwkv_r_k']), row(lw['rwkv_ln_w']), row(lw['rwkv_ln_b']), s0)


def _fox_prep_kernel(q_ref, k_ref, v_ref, sm_ref, qw_ref, kw_ref, bf_ref,
                     qn_ref, kn_ref, kb_ref, vb_ref, lf_ref, c_ref, ct_ref, carry_ref, *, lb):
    @pl.when(pl.program_id(1) == 0)
    def _():
        carry_ref[...] = jnp.zeros_like(carry_ref)

    qw = qw_ref[...]
    kw = kw_ref[...]
    for h in range(NH):
        sl = slice(h * HD, (h + 1) * HD)
        qn_ref[:, sl] = _rms(q_ref[:, sl], qw).astype(BF16)
        kn = _rms(k_ref[:, sl], kw)
        kn_ref[:, sl] = kn
        kb_ref[:, sl] = kn.astype(BF16)
    vb_ref[...] = v_ref[...].astype(BF16)
    logf = -_softplus(-(sm_ref[...] + bf_ref[...]))
    lf_ref[...] = logf
    c = _dot_exact_lhs(_chunk_tri(lb, lb), logf) + carry_ref[0:1, :]
    c_ref[...] = c
    carry_ref[...] = jnp.broadcast_to(c[lb - 1:], carry_ref.shape)
    ct_ref[...] = c.T[SM_F:SM_F + SUBLANES]


def _fox_prep(p, q_w, k_w, b_f, *, nseq, lpad, lb):
    nlb = lpad // lb
    rb = lambda s, i: s * nlb + i
    seg = lambda j: pl.BlockSpec((lb, GW), lambda s, i: (rb(s, i), j))
    vec = pl.BlockSpec((1, LANES), lambda s, i: (0, 0))
    rows = nseq * lpad
    bf_lane = jnp.zeros((1, LANES), F32).at[0, SM_F:SM_F + NH].set(b_f)
    return pl.pallas_call(
        functools.partial(_fox_prep_kernel, lb=lb),
        grid=(nseq, nlb),
        in_specs=[seg(0), seg(1), seg(2),
                  pl.BlockSpec((lb, LANES), lambda s, i: (rb(s, i), CB_SMALL)), vec, vec, vec],
        out_specs=[
            pl.BlockSpec((lb, GW), lambda s, i: (rb(s, i), 0)),
            pl.BlockSpec((lb, GW), lambda s, i: (rb(s, i), 0)),
            pl.BlockSpec((lb, GW), lambda s, i: (rb(s, i), 0)),
            pl.BlockSpec((lb, GW), lambda s, i: (rb(s, i), 0)),
            pl.BlockSpec((lb, LANES), lambda s, i: (rb(s, i), 0)),
            pl.BlockSpec((lb, LANES), lambda s, i: (rb(s, i), 0)),
            pl.BlockSpec((SUBLANES, lb), lambda s, i: (0, rb(s, i))),
        ],
        out_shape=[
            jax.ShapeDtypeStruct((rows, GW), BF16),
            jax.ShapeDtypeStruct((rows, GW), F32),
            jax.ShapeDtypeStruct((rows, GW), BF16),
            jax.ShapeDtypeStruct((rows, GW), BF16),
            jax.ShapeDtypeStruct((rows, LANES), F32),
            jax.ShapeDtypeStruct((rows, LANES), F32),
            jax.ShapeDtypeStruct((SUBLANES, rows), F32),
        ],
        scratch_shapes=[pltpu.VMEM((SUBLANES, LANES), F32)],
        compiler_params=_cparams(("parallel", "arbitrary")),
        name="fox_prep",
    )(p, p, p, p, q_w.reshape(1, HD), k_w.reshape(1, HD), bf_lane)


def _fox_flash_kernel(q_ref, k_ref, v_ref, c_ref, ct_ref, g_ref, o_ref, m_ref, l_ref, acc_ref, *, tq):
    h = pl.program_id(1)
    qi = pl.program_id(2)
    scale = HD ** -0.5
    q = q_ref[...]
    c_col = _lane_col(c_ref[...], SM_F + h)
    m_ref[...] = jnp.full_like(m_ref, -1e30)
    l_ref[...] = jnp.zeros_like(l_ref)
    acc_ref[...] = jnp.zeros_like(acc_ref)

    def block(kj, diagonal):
        ks = pl.multiple_of(kj * tq, tq)
        kb = k_ref[pl.ds(ks, tq), :]
        vb = v_ref[pl.ds(ks, tq), :]
        s = _dot_nt(q, kb) * scale + c_col - ct_ref[pl.ds(h, 1), pl.ds(ks, tq)]
        if diagonal:
            ii = lax.broadcasted_iota(jnp.int32, (tq, tq), 0)
            jj = lax.broadcasted_iota(jnp.int32, (tq, tq), 1)
            s = jnp.where(jj <= ii, s, -jnp.inf)
        m_old = m_ref[...]
        m_new = jnp.maximum(m_old, jnp.max(s, axis=-1, keepdims=True))
        alpha = jnp.exp(m_old - m_new)
        pr = jnp.exp(s - m_new)
        l_ref[...] = alpha * l_ref[...] + jnp.sum(pr, axis=-1, keepdims=True)
        acc_ref[...] = alpha * acc_ref[...] + _dot(pr, vb)
        m_ref[...] = m_new

    def body(kj, carry):
        block(kj, False)
        return carry

    lax.fori_loop(0, qi, body, 0)
    block(qi, True)
    o_ref[...] = (acc_ref[...] / l_ref[...] * _sigmoid(g_ref[...])).astype(BF16)


def _fox_flash(qn, kb, vb, c_col, c_row, p, *, nseq, lpad, tq):
    nq = lpad // tq
    return pl.pallas_call(
        functools.partial(_fox_flash_kernel, tq=tq),
        grid=(nseq, NH, nq),
        in_specs=[
            pl.BlockSpec((tq, HD), lambda s, h, i: (s * nq + i, h)),
            pl.BlockSpec((lpad, HD), lambda s, h, i: (s, h)),
            pl.BlockSpec((lpad, HD), lambda s, h, i: (s, h)),
            pl.BlockSpec((tq, LANES), lambda s, h, i: (s * nq + i, 0)),
            pl.BlockSpec((SUBLANES, lpad), lambda s, h, i: (0, s)),
            pl.BlockSpec((tq, HD), lambda s, h, i: (s * nq + i, CB_FOX + 3 * NH + h)),
        ],
        out_specs=pl.BlockSpec((tq, HD), lambda s, h, i: (s * nq + i, h)),
        out_shape=jax.ShapeDtypeStruct((nseq * lpad, GW), BF16),
        scratch_shapes=[pltpu.VMEM((tq, 1), F32), pltpu.VMEM((tq, 1), F32), pltpu.VMEM((tq, HD), F32)],
        compiler_params=_cparams(("parallel", "parallel", "arbitrary")),
        name="fox_flash",
    )(qn, kb, vb, c_col, c_row, p)


def _fox_sample_kernel(pt_ref, pq_ref, sm_ref, kp_ref, vp_ref, lfp_ref, qw_ref, kw_ref, bf_ref,
                       o_ref, kn_ref, lf_ref, qn_s, c_s, m_s, l_s, acc_s, carry_s,
                       *, layer, lvalid, lpad, n_pages):
    del pt_ref, layer
    i = pl.program_id(1)
    scale = HD ** -0.5
    nq = SUBLANES

    def online(h, s, vals):
        m_old = m_s[h]
        m_new = jnp.maximum(m_old, jnp.max(s, axis=-1, keepdims=True))
        alpha = jnp.exp(m_old - m_new)
        pr = jnp.exp(s - m_new)
        l_s[h] = alpha * l_s[h] + jnp.sum(pr, axis=-1, keepdims=True)
        acc_s[h] = alpha * acc_s[h] + _dot(pr, vals)
        m_s[h] = m_new

    @pl.when(i == 0)
    def _():
        logf = -_softplus(-(sm_ref[...] + bf_ref[...]))
        valid = _row_valid(0, lpad, lvalid, LANES)
        logf = jnp.where(valid, logf, 0.0)
        lf_ref[...] = logf[:nq]
        c = _dot_exact_lhs(_chunk_tri(lpad, lpad), logf)
        c_s[...] = c[:nq]
        c_t = c.T
        carry_s[...] = jnp.zeros_like(carry_s)
        qi = lax.broadcasted_iota(jnp.int32, (nq, lpad), 0)
        kj = lax.broadcasted_iota(jnp.int32, (nq, lpad), 1)
        for h in range(NH):
            sl = slice(h * HD, (h + 1) * HD)
            qn = _rms(pq_ref[:nq, sl], qw_ref[...])
            qn_s[h] = qn
            kn = _rms(pq_ref[:, GW + h * HD:GW + (h + 1) * HD], kw_ref[...])
            kn_ref[:, sl] = kn[:nq]
            vn = pq_ref[:, 2 * GW + h * HD:2 * GW + (h + 1) * HD]
            m_s[h] = jnp.full((nq, 1), -1e30, F32)
            l_s[h] = jnp.zeros((nq, 1), F32)
            acc_s[h] = jnp.zeros((nq, HD), F32)
            s = (_dot_nt(qn, kn) * scale + c[:nq, SM_F + h:SM_F + h + 1]
                 - c_t[SM_F + h:SM_F + h + 1, :])
            s = jnp.where(kj <= qi, s, -jnp.inf)
            online(h, s, vn)

    lf = jnp.concatenate([lfp_ref[...], jnp.zeros((SUBLANES - NH, PAGE), F32)], axis=0)
    si = lax.broadcasted_iota(jnp.int32, (PAGE, PAGE), 0)
    sj = lax.broadcasted_iota(jnp.int32, (PAGE, PAGE), 1)
    upper = jnp.where(si > sj, 1.0, 0.0).astype(BF16)
    suffix = _dot_exact_rhs(lf, upper) + carry_s[...]
    carry_s[...] = carry_s[...] + jnp.sum(lf, axis=-1, keepdims=True)
    for h in range(NH):
        kh = kp_ref[:, h, :]
        vh = vp_ref[:, h, :]
        s = (_dot_nt(qn_s[h], kh) * scale + c_s[:, SM_F + h:SM_F + h + 1] + suffix[h:h + 1, :])
        online(h, s, vh)

    @pl.when(i == n_pages - 1)
    def _():
        o_ref[...] = jnp.zeros_like(o_ref)
        for h in range(NH):
            sl = slice(h * HD, (h + 1) * HD)
            g = pq_ref[:nq, 3 * GW + h * HD:3 * GW + (h + 1) * HD]
            o_ref[:nq, sl] = (acc_s[h] / l_s[h] * _sigmoid(g)).astype(BF16)


def _fox_sample(p, page_table, cache_k, cache_v, cache_lf_t, q_w, k_w, b_f, *, layer, row0, nseq, lpad, lvalid):
    n_pages = page_table.shape[1]
    bf_lane = jnp.zeros((1, LANES), F32).at[0, SM_F:SM_F + NH].set(b_f)
    rb = lambda b: row0 // lpad + b
    page = lambda b, i, pt: pt[b, n_pages - 1 - i]
    vec = pl.BlockSpec((1, LANES), lambda b, i, pt: (0, 0))
    nq = SUBLANES
    grid_spec = pltpu.PrefetchScalarGridSpec(
        num_scalar_prefetch=1,
        grid=(nseq, n_pages),
        in_specs=[
            pl.BlockSpec((lpad, 4 * GW), lambda b, i, pt: (rb(b), CB_FOX)),
            pl.BlockSpec((lpad, LANES), lambda b, i, pt: (rb(b), CB_SMALL)),
            pl.BlockSpec((None, None, PAGE, NH, HD), lambda b, i, pt: (layer, page(b, i, pt), 0, 0, 0)),
            pl.BlockSpec((None, None, PAGE, NH, HD), lambda b, i, pt: (layer, page(b, i, pt), 0, 0, 0)),
            pl.BlockSpec((None, None, NH, PAGE), lambda b, i, pt: (layer, page(b, i, pt), 0, 0)),
            vec, vec, vec,
        ],
        out_specs=[
            pl.BlockSpec((lpad, GW), lambda b, i, pt: (b, 0)),
            pl.BlockSpec((None, nq, GW), lambda b, i, pt: (b, 0, 0)),
            pl.BlockSpec((None, nq, LANES), lambda b, i, pt: (b, 0, 0)),
        ],
        scratch_shapes=[
            pltpu.VMEM((NH, nq, HD), F32),
            pltpu.VMEM((nq, LANES), F32),
            pltpu.VMEM((NH, nq, 1), F32),
            pltpu.VMEM((NH, nq, 1), F32),
            pltpu.VMEM((NH, nq, HD), F32),
            pltpu.VMEM((SUBLANES, PAGE), F32),
        ],
    )
    return pl.pallas_call(
        functools.partial(_fox_sample_kernel, layer=layer, lvalid=lvalid, lpad=lpad, n_pages=n_pages),
        grid_spec=grid_spec,
        out_shape=[
            jax.ShapeDtypeStruct((nseq * lpad, GW), BF16),
            jax.ShapeDtypeStruct((nseq, nq, GW), F32),
            jax.ShapeDtypeStruct((nseq, nq, LANES), F32),
        ],
        compiler_params=_cparams(("parallel", "arbitrary")),
        name="fox_sample",
    )(page_table, p, p, cache_k, cache_v, cache_lf_t, q_w.reshape(1, HD), k_w.reshape(1, HD), bf_lane)


def _rearrange_w_in(w):
    gdn_qkv, gdn_ab, gdn_z = w[:, 0:1536], w[:, 1536:1544], w[:, 1544:2056]
    ret = w[:, 2056:4104]
    rwkv = w[:, 4104:5896]
    fox, fox_f = w[:, 5896:7944], w[:, 7944:7948]
    pad = jnp.zeros((w.shape[0], NP_COLS - 7948), w.dtype)
    return jnp.concatenate([fox, ret, gdn_qkv, gdn_z, rwkv, gdn_ab, fox_f, pad], axis=1).astype(BF16)


def _rope_tables(pos):
    half = HD // 2
    inv = 1.0 / (ROPE_BASE ** jnp.linspace(0.0, 1.0, half, dtype=F32))
    ang = pos.astype(F32)[:, None] * inv[None, :]
    cos, sin = jnp.cos(ang), jnp.sin(ang)
    return jnp.concatenate([cos, cos], axis=-1), jnp.concatenate([-sin, sin], axis=-1)


def _state_tile(state, nrows):
    b, _, c = state.shape
    return jnp.concatenate([jnp.zeros((b, SUBLANES - nrows, c), F32), state], axis=1)


def _rwkv_pair_states(s):
    b = s.shape[0]
    s = s.reshape(b, 4, 2, RWKV_HEAD, RWKV_HEAD)
    z = jnp.zeros_like(s[:, :, 0])
    top = jnp.concatenate([s[:, :, 0], z], axis=-1)
    bot = jnp.concatenate([z, s[:, :, 1]], axis=-1)
    return jnp.concatenate([top, bot], axis=-2)


def _rwkv_unpair_states(sp):
    b = sp.shape[0]
    a = sp[:, :, :RWKV_HEAD, :RWKV_HEAD]
    c = sp[:, :, RWKV_HEAD:, RWKV_HEAD:]
    return jnp.stack([a, c], axis=2).reshape(b, 8, RWKV_HEAD, RWKV_HEAD)


def kernel(x_prompt, x_sample, cache_fox_k, cache_fox_v, cache_fox_logf, cache_mem_k, cache_mem_v, state_gdn_conv, state_gdn_S, state_ret_S, state_rwkv_shift, state_rwkv_S, page_table, mem_prompt, norm_mix, w_in, gdn_conv_w, gdn_A_log, gdn_dt_bias, gdn_norm, rwkv_mu, rwkv_w0, rwkv_w_up, rwkv_a0, rwkv_a_up, rwkv_g_up, rwkv_k_k, rwkv_k_a, rwkv_r_k, rwkv_ln_w, rwkv_ln_b, fox_b_f, fox_q_norm, fox_k_norm, w_out, norm_x, norm_mem, xattn_wq, xattn_wkv, xattn_q_norm, xattn_k_norm, xattn_wo, norm_ffn, ffn_w_gate, ffn_w_up, ffn_w_down):
    weights = {
        'norm_mix': norm_mix, 'w_in': w_in, 'gdn_conv_w': gdn_conv_w, 'gdn_A_log': gdn_A_log,
        'gdn_dt_bias': gdn_dt_bias, 'gdn_norm': gdn_norm, 'rwkv_mu': rwkv_mu, 'rwkv_w0': rwkv_w0,
        'rwkv_w_up': rwkv_w_up, 'rwkv_a0': rwkv_a0, 'rwkv_a_up': rwkv_a_up, 'rwkv_g_up': rwkv_g_up,
        'rwkv_k_k': rwkv_k_k, 'rwkv_k_a': rwkv_k_a, 'rwkv_r_k': rwkv_r_k, 'rwkv_ln_w': rwkv_ln_w,
        'rwkv_ln_b': rwkv_ln_b, 'fox_b_f': fox_b_f, 'fox_q_norm': fox_q_norm, 'fox_k_norm': fox_k_norm,
        'w_out': w_out, 'norm_x': norm_x, 'norm_mem': norm_mem, 'xattn_wq': xattn_wq, 'xattn_wkv': xattn_wkv,
        'xattn_q_norm': xattn_q_norm, 'xattn_k_norm': xattn_k_norm, 'xattn_wo': xattn_wo,
        'norm_ffn': norm_ffn, 'ffn_w_gate': ffn_w_gate, 'ffn_w_up': ffn_w_up, 'ffn_w_down': ffn_w_down,
    }
    depth = w_in.shape[0]
    bp, lp, d = x_prompt.shape
    bs, ls, _ = x_sample.shape
    n_pages = page_table.shape[1]
    past_len = n_pages * PAGE
    tp = bp * lp
    ts = bs * SAMPLE_PAD
    tt = tp + ts
    tm = 512 if tt % 512 == 0 else 256
    lb_p = min(256, lp)
    assert ls >= CONV_WIDTH - 1 and ls <= SUBLANES and tt % tm == 0 and lp % lb_p == 0 and tp % tm == 0

    xs_pad = jnp.zeros((bs, SAMPLE_PAD, d), F32).at[:, :ls].set(x_sample)
    x = jnp.concatenate([x_prompt.reshape(tp, d), xs_pad.reshape(ts, d)], axis=0)

    cos_p, sin_p = _rope_tables(jnp.arange(lp, dtype=jnp.int32))
    cos_s, sin_s = _rope_tables(past_len + jnp.arange(SAMPLE_PAD, dtype=jnp.int32))
    log_gamma = jnp.log(1.0 - jnp.exp2(-(RET_GAMMA_BASE + jnp.arange(NH, dtype=F32))))
    cache_lf_t = jnp.swapaxes(cache_fox_logf, -1, -2)
    zeros_s = jnp.zeros((bp, NH, HD, HD), F32)
    zeros_conv = jnp.zeros((bp, SUBLANES, 3 * GW), F32)
    zeros_shift = jnp.zeros((bp, SUBLANES, 1792), F32)

    outs_p, outs_s, mem_ks, mem_vs = [], [], [], []
    for l in range(depth):
        lw = {name: arr[l] for name, arr in weights.items()}
        p = _norm_matmul(x, lw['norm_mix'], _rearrange_w_in(lw['w_in']), tm=tm, tn=1024)

        gp = dict(row0=0, nseq=bp, lpad=lp, lvalid=lp, lb=lb_p)
        o_gdn_p, gdn_s_p = _gdn(p, lw['gdn_conv_w'], zeros_conv, zeros_s, lw['gdn_A_log'], lw['gdn_dt_bias'],
                                lw['gdn_norm'], **gp)
        o_ret_p, ret_s_p = _ret(p, cos_p, sin_p, log_gamma, zeros_s, **gp)
        o_rwkv_p, rwkv_s_p = _rwkv(p, lw, zeros_shift, zeros_s, **gp)
        qn, kn, kb, vb, lf, c_col, c_row = _fox_prep(p, lw['fox_q_norm'], lw['fox_k_norm'], lw['fox_b_f'],
                                                    nseq=bp, lpad=lp, lb=lb_p)
        o_fox_p = _fox_flash(qn, kb, vb, c_col, c_row, p, nseq=bp, lpad=lp, tq=lb_p)

        gs = dict(row0=tp, nseq=bs, lpad=SAMPLE_PAD, lvalid=ls, lb=SAMPLE_PAD)
        o_gdn_s, gdn_s_s = _gdn(p, lw['gdn_conv_w'], _state_tile(state_gdn_conv[l], CONV_WIDTH - 1),
                                state_gdn_S[l], lw['gdn_A_log'], lw['gdn_dt_bias'], lw['gdn_norm'], **gs)
        o_ret_s, ret_s_s = _ret(p, cos_s, sin_s, log_gamma, state_ret_S[l], **gs)
        o_rwkv_s, rwkv_s_s = _rwkv(p, lw, _state_tile(state_rwkv_shift[l], 1),
                                   _rwkv_pair_states(state_rwkv_S[l]), **gs)
        o_fox_s, kn_s, lf_s = _fox_sample(p, page_table, cache_fox_k, cache_fox_v, cache_lf_t,
                                          lw['fox_q_norm'], lw['fox_k_norm'], lw['fox_b_f'],
                                          layer=l, row0=tp, nseq=bs, lpad=SAMPLE_PAD, lvalid=ls)

        o_mix = jnp.concatenate([
            jnp.concatenate([o_gdn_p, o_ret_p, o_rwkv_p, o_fox_p], axis=1),
            jnp.concatenate([o_gdn_s, o_ret_s, o_rwkv_s, o_fox_s], axis=1)], axis=0)
        x = _matmul_res(o_mix, lw['w_out'].astype(BF16), x, tm=tm, tn=512)

        kv = _norm_matmul(mem_prompt.reshape(bp * N_MEM, d), lw['norm_mem'], lw['xattn_wkv'].astype(BF16),
                          tm=256, tn=XW, head_w=lw['xattn_k_norm'], norm_tiles=1)
        mk = kv[:, :XW].reshape(bp, N_MEM, XW)
        mv = kv[:, XW:].reshape(bp, N_MEM, XW)
        q = _norm_matmul(x, lw['norm_x'], lw['xattn_wq'].astype(BF16), tm=tm, tn=XW,
                         head_w=lw['xattn_q_norm'], norm_tiles=1)
        xo_p = _xattn(q, mk, mv, row0=0, nrows=tp, tq=lb_p, rows_per_seq=lp)
        xo_s = _xattn(q, cache_mem_k[l].reshape(bs, N_MEM, XW), cache_mem_v[l].reshape(bs, N_MEM, XW),
                      row0=tp, nrows=ts, tq=SAMPLE_PAD, rows_per_seq=SAMPLE_PAD)
        x = _matmul_res(jnp.concatenate([xo_p, xo_s], axis=0), lw['xattn_wo'].astype(BF16), x, tm=tm, tn=512)

        hidden = _swiglu_up(x, lw['norm_ffn'], lw['ffn_w_gate'].astype(BF16), lw['ffn_w_up'].astype(BF16),
                            tm=tm, tn=512)
        x = _matmul_res(hidden, lw['ffn_w_down'].astype(BF16), x, tm=tm, tn=512)

        pp = p[:tp].reshape(bp, lp, NP_COLS)
        ps = p[tp:].reshape(bs, SAMPLE_PAD, NP_COLS)
        c0 = CB_GDN * LANES
        r0 = CB_RWKV * LANES
        v0 = (CB_FOX + 2 * NH) * LANES
        f0 = CB_SMALL * LANES + SM_F
        outs_p.append((
            kn.reshape(bp, lp, NH, HD),
            pp[:, :, v0:v0 + GW].reshape(bp, lp, NH, HD),
            lf.reshape(bp, lp, LANES)[:, :, SM_F:SM_F + NH],
            pp[:, lp - (CONV_WIDTH - 1):, c0:c0 + 3 * GW],
            gdn_s_p, ret_s_p,
            pp[:, lp - 1:, r0:r0 + 1792],
            _rwkv_unpair_states(rwkv_s_p),
        ))
        outs_s.append((
            kn_s[:, :ls].reshape(bs, ls, NH, HD),
            ps[:, :ls, v0:v0 + GW].reshape(bs, ls, NH, HD),
            lf_s[:, :ls, SM_F:SM_F + NH],
            ps[:, ls - (CONV_WIDTH - 1):ls, c0:c0 + 3 * GW],
            gdn_s_s, ret_s_s,
            ps[:, ls - 1:ls, r0:r0 + 1792],
            _rwkv_unpair_states(rwkv_s_s),
        ))
        mem_ks.append(mk.reshape(bp, N_MEM, NH, HD))
        mem_vs.append(mv.reshape(bp, N_MEM, NH, HD))
        del f0

    stk = lambda seq, i: jnp.stack([e[i] for e in seq], axis=0)
    yp = x[:tp].reshape(bp, lp, d)
    ys = x[tp:].reshape(bs, SAMPLE_PAD, d)[:, :ls]
    return (yp, ys, stk(outs_p, 0), stk(outs_p, 1), stk(outs_p, 2), jnp.stack(mem_ks, 0), jnp.stack(mem_vs, 0),
            stk(outs_p, 3), stk(outs_p, 4), stk(outs_p, 5), stk(outs_p, 6), stk(outs_p, 7),
            stk(outs_s, 0), stk(outs_s, 1), stk(outs_s, 2), stk(outs_s, 3), stk(outs_s, 4), stk(outs_s, 5),
            stk(outs_s, 6), stk(outs_s, 7))
```

```python
import functools
import math

import jax
import jax.numpy as jnp
import numpy as np
from jax import lax
from jax.experimental import pallas as pl
from jax.experimental.pallas import tpu as pltpu

F32 = jnp.float32
BF16 = jnp.bfloat16

LANES = 128
SUBLANES = 8
VMEM_LIMIT = 56 * 1024 * 1024

D_MODEL = 2048
GW = D_MODEL // 4
HD = 128
NH = GW // HD
RWKV_HEAD = 64
CONV_WIDTH = 4
PAGE = 128
N_MEM = 256
XW = 512
D_FF = 5632
NORM_EPS = 1e-6
GN_EPS = 64e-5
RET_GAMMA_BASE = 5.0
ROPE_BASE = 10000.0
CHUNK = 64
SAMPLE_PAD = 64
FOX_TQ = 512

NP_COLS = 8192
CB_FOX, CB_RET, CB_GDN, CB_GDN_Z, CB_RWKV, CB_SMALL = 0, 16, 32, 44, 48, 62
SM_A, SM_B, SM_F = 0, 4, 8


def _cparams(sem):
    return pltpu.CompilerParams(dimension_semantics=sem, vmem_limit_bytes=VMEM_LIMIT)


def _dot(a, b):
    return jnp.dot(a.astype(BF16), b.astype(BF16), preferred_element_type=F32)


def _dot_nt(a, b):
    return lax.dot_general(a.astype(BF16), b.astype(BF16), (((1,), (1,)), ((), ())),
                           preferred_element_type=F32)


def _dot_tn(a, b):
    return lax.dot_general(a.astype(BF16), b.astype(BF16), (((0,), (0,)), ((), ())),
                           preferred_element_type=F32)


def _split3(x):
    hi = x.astype(BF16)
    r = x - hi.astype(F32)
    mid = r.astype(BF16)
    lo = (r - mid.astype(F32)).astype(BF16)
    return hi, mid, lo


def _dot_exact_lhs(m, x):
    hi, mid, lo = _split3(x)
    d = lambda p: jnp.dot(m, p, preferred_element_type=F32)
    return d(hi) + d(mid) + d(lo)


def _dot_exact_rhs(x, m):
    hi, mid, lo = _split3(x)
    d = lambda p: jnp.dot(p, m, preferred_element_type=F32)
    return d(hi) + d(mid) + d(lo)


def _unit_lower_inv(n, nil):
    c = n.shape[0]
    ii = lax.broadcasted_iota(jnp.int32, (c, c), 0)
    jj = lax.broadcasted_iota(jnp.int32, (c, c), 1)
    p = jnp.where(ii == jj, 1.0, 0.0).astype(F32) - n
    q = n
    for _ in range(int(math.log2(nil)) - 1):
        q = _dot(q, q)
        p = p + _dot(p, q)
    return p


def _chunk_tri(lb, chunk):
    ii = lax.broadcasted_iota(jnp.int32, (lb, lb), 0)
    jj = lax.broadcasted_iota(jnp.int32, (lb, lb), 1)
    same = (ii // chunk) == (jj // chunk)
    return jnp.where(jnp.logical_and(ii >= jj, same), 1.0, 0.0).astype(BF16)


def _shift_rows(x, prev8, s):
    rolled = pltpu.roll(x, s, axis=0)
    pr = pltpu.roll(prev8, s, axis=0)
    row = lax.broadcasted_iota(jnp.int32, (SUBLANES, x.shape[1]), 0)
    top = jnp.where(row < s, pr, rolled[:SUBLANES])
    return jnp.concatenate([top, rolled[SUBLANES:]], axis=0)


def _softplus(x):
    return jnp.maximum(x, 0.0) + jnp.log1p(jnp.exp(-jnp.abs(x)))


def _sigmoid(x):
    return jax.nn.sigmoid(x)


def _silu(x):
    return x * jax.nn.sigmoid(x)


def _lane_col(x, idx):
    lane = lax.broadcasted_iota(jnp.int32, x.shape, 1)
    return jnp.sum(jnp.where(lane == idx, x, 0.0), axis=-1, keepdims=True)


def _rms(x, w=None):
    y = x * lax.rsqrt(jnp.mean(x * x, axis=-1, keepdims=True) + NORM_EPS)
    return y if w is None else y * w


def _row_valid(lb_index, lb, lvalid, width):
    row = lax.broadcasted_iota(jnp.int32, (lb, width), 0) + lb_index * lb
    return row < lvalid


def _norm_matmul_kernel(x_ref, g_ref, w_ref, hw_ref, o_ref, xn_ref, *, norm_tiles):
    j = pl.program_id(1)

    @pl.when(j == 0)
    def _():
        xn_ref[...] = _rms(x_ref[...], g_ref[...]).astype(BF16)

    acc = jnp.dot(xn_ref[...], w_ref[...], preferred_element_type=F32)
    if norm_tiles == 0:
        o_ref[...] = acc
    else:
        @pl.when(j < norm_tiles)
        def _():
            hw = hw_ref[...]
            for h in range(acc.shape[1] // HD):
                sl = slice(h * HD, (h + 1) * HD)
                o_ref[:, sl] = _rms(acc[:, sl], hw)

        @pl.when(j >= norm_tiles)
        def _():
            o_ref[...] = acc


def _norm_matmul(x, g, w, *, tm, tn, head_w=None, norm_tiles=0):
    m, k = x.shape
    n = w.shape[1]
    if head_w is None:
        head_w = jnp.ones((HD,), F32)
    return pl.pallas_call(
        functools.partial(_norm_matmul_kernel, norm_tiles=norm_tiles),
        grid=(m // tm, n // tn),
        in_specs=[
            pl.BlockSpec((tm, k), lambda i, j: (i, 0)),
            pl.BlockSpec((1, k), lambda i, j: (0, 0)),
            pl.BlockSpec((k, tn), lambda i, j: (0, j)),
            pl.BlockSpec((1, HD), lambda i, j: (0, 0)),
        ],
        out_specs=pl.BlockSpec((tm, tn), lambda i, j: (i, j)),
        out_shape=jax.ShapeDtypeStruct((m, n), F32),
        scratch_shapes=[pltpu.VMEM((tm, k), BF16)],
        compiler_params=_cparams(("parallel", "arbitrary")),
        name="norm_matmul",
    )(x, g.reshape(1, k), w, head_w.reshape(1, HD))


def _matmul_res_kernel(a_ref, w_ref, r_ref, o_ref):
    o_ref[...] = r_ref[...] + jnp.dot(a_ref[...], w_ref[...], preferred_element_type=F32)


def _matmul_res(a, w, res, *, tm, tn):
    m, k = a.shape
    n = w.shape[1]
    return pl.pallas_call(
        _matmul_res_kernel,
        grid=(m // tm, n // tn),
        in_specs=[
            pl.BlockSpec((tm, k), lambda i, j: (i, 0)),
            pl.BlockSpec((k, tn), lambda i, j: (0, j)),
            pl.BlockSpec((tm, tn), lambda i, j: (i, j)),
        ],
        out_specs=pl.BlockSpec((tm, tn), lambda i, j: (i, j)),
        out_shape=jax.ShapeDtypeStruct((m, n), F32),
        compiler_params=_cparams(("parallel", "parallel")),
        name="matmul_res",
    )(a, w, res)


def _swiglu_up_kernel(x_ref, g_ref, wg_ref, wu_ref, o_ref, xn_ref):
    @pl.when(pl.program_id(1) == 0)
    def _():
        xn_ref[...] = _rms(x_ref[...], g_ref[...]).astype(BF16)

    xn = xn_ref[...]
    gate = jnp.dot(xn, wg_ref[...], preferred_element_type=F32)
    up = jnp.dot(xn, wu_ref[...], preferred_element_type=F32)
    o_ref[...] = (_silu(gate) * up).astype(BF16)


def _swiglu_up(x, g, wg, wu, *, tm, tn):
    m, k = x.shape
    n = wg.shape[1]
    return pl.pallas_call(
        _swiglu_up_kernel,
        grid=(m // tm, n // tn),
        in_specs=[
            pl.BlockSpec((tm, k), lambda i, j: (i, 0)),
            pl.BlockSpec((1, k), lambda i, j: (0, 0)),
            pl.BlockSpec((k, tn), lambda i, j: (0, j)),
            pl.BlockSpec((k, tn), lambda i, j: (0, j)),
        ],
        out_specs=pl.BlockSpec((tm, tn), lambda i, j: (i, j)),
        out_shape=jax.ShapeDtypeStruct((m, n), BF16),
        scratch_shapes=[pltpu.VMEM((tm, k), BF16)],
        compiler_params=_cparams(("parallel", "arbitrary")),
        name="swiglu_up",
    )(x, g.reshape(1, k), wg, wu)


def _xattn_kernel(q_ref, k_ref, v_ref, o_ref):
    scale = HD ** -0.5
    for h in range(NH):
        sl = slice(h * HD, (h + 1) * HD)
        s = _dot_nt(q_ref[:, sl], k_ref[:, sl]) * scale
        m = jnp.max(s, axis=-1, keepdims=True)
        p = jnp.exp(s - m)
        l = jnp.sum(p, axis=-1, keepdims=True)
        o_ref[:, sl] = (_dot(p, v_ref[:, sl]) / l).astype(BF16)


def _xattn(q, mem_k, mem_v, *, row0, nrows, tq, rows_per_seq):
    tiles_per_seq = rows_per_seq // tq
    t0 = row0 // tq
    return pl.pallas_call(
        _xattn_kernel,
        grid=(nrows // tq,),
        in_specs=[
            pl.BlockSpec((tq, XW), lambda i: (t0 + i, 0)),
            pl.BlockSpec((None, N_MEM, XW), lambda i: (i // tiles_per_seq, 0, 0)),
            pl.BlockSpec((None, N_MEM, XW), lambda i: (i // tiles_per_seq, 0, 0)),
        ],
        out_specs=pl.BlockSpec((tq, XW), lambda i: (i, 0)),
        out_shape=jax.ShapeDtypeStruct((nrows, XW), BF16),
        compiler_params=_cparams(("parallel",)),
        name="xattn",
    )(q, mem_k, mem_v)


def _gdn_kernel(q_ref, k_ref, v_ref, z_ref, sm_ref, cwq_ref, cwk_ref, cwv_ref, cq_ref, ck_ref, cv_ref,
                s0_ref, alog_ref, dtb_ref, nw_ref, o_ref, so_ref, s_ref, prev_ref, gt_ref,
                *, lb, lvalid, nlb):
    h = pl.program_id(1)
    ib = pl.program_id(2)
    masked = lvalid < nlb * lb

    @pl.when(ib == 0)
    def _():
        s_ref[...] = s0_ref[...]
        prev_ref[0] = cq_ref[...]
        prev_ref[1] = ck_ref[...]
        prev_ref[2] = cv_ref[...]

    def conv(x_ref, w_ref, i):
        x = x_ref[...]
        w = w_ref[...]
        prev = prev_ref[i]
        y = x * w[3:4]
        for s in (1, 2, 3):
            y = y + _shift_rows(x, prev, s) * w[3 - s:4 - s]
        prev_ref[i] = x[lb - SUBLANES:]
        return _silu(y)

    q = conv(q_ref, cwq_ref, 0)
    k = conv(k_ref, cwk_ref, 1)
    v = conv(v_ref, cwv_ref, 2)
    q = q * lax.rsqrt(jnp.sum(q * q, axis=-1, keepdims=True) + NORM_EPS) * (HD ** -0.5)
    k = k * lax.rsqrt(jnp.sum(k * k, axis=-1, keepdims=True) + NORM_EPS)

    sm = sm_ref[...]
    g_blk = -jnp.exp(alog_ref[...]) * _softplus(sm + dtb_ref[...])
    beta_blk = _sigmoid(sm)
    if masked:
        valid = _row_valid(ib, lb, lvalid, LANES)
        g_blk = jnp.where(valid, g_blk, 0.0)
        beta_blk = jnp.where(valid, beta_blk, 0.0)
    gc_blk = _dot_exact_lhs(_chunk_tri(lb, CHUNK), g_blk)
    gt_ref[...] = gc_blk.T
    g_col_all = _lane_col(gc_blk, SM_A + h)
    beta_all = _lane_col(beta_blk, SM_B + h)
    g_row_all = gt_ref[pl.ds(SM_A + h, 1), :]

    group = min(2 * CHUNK, lb)
    ii = lax.broadcasted_iota(jnp.int32, (group, group), 0)
    jj = lax.broadcasted_iota(jnp.int32, (group, group), 1)
    same_chunk = (ii // CHUNK) == (jj // CHUNK)
    lower = jnp.logical_and(ii >= jj, same_chunk)
    nw = nw_ref[...]
    for gi in range(lb // group):
        rg = slice(gi * group, (gi + 1) * group)
        qg, kg, vg = q[rg], k[rg], v[rg]
        g_col = g_col_all[rg]
        beta = beta_all[rg]
        dec = jnp.exp(jnp.where(lower, g_col - g_row_all[:, rg], -jnp.inf))
        n = jnp.where(ii > jj, beta * _dot_nt(kg, kg) * dec, 0.0)
        ainv = _unit_lower_inv(n, CHUNK)
        eg = jnp.exp(g_col)
        uv_g = _dot(ainv, beta * vg)
        w_g = _dot(ainv, beta * eg * kg)
        qk_g = _dot_nt(qg, kg) * dec
        qe_g = qg * eg
        for c in range(group // CHUNK):
            r = slice(c * CHUNK, (c + 1) * CHUNK)
            s = s_ref[...]
            u = uv_g[r] - _dot(w_g[r], s)
            o = _dot(qe_g[r], s) + _dot(qk_g[r, r], u)
            g_last = g_col[(c + 1) * CHUNK - 1:(c + 1) * CHUNK]
            s_ref[...] = s * jnp.exp(g_last) + _dot_tn(kg[r] * jnp.exp(g_last - g_col[r]), u)
            ro = slice(gi * group + c * CHUNK, gi * group + (c + 1) * CHUNK)
            o_ref[ro, :] = (_rms(o, nw) * _silu(z_ref[ro, :])).astype(BF16)

    @pl.when(ib == nlb - 1)
    def _():
        so_ref[...] = s_ref[...]


def _gdn(p, conv_w, conv_init, s0, a_log, dt_bias, norm_w, *, row0, nseq, lpad, lvalid, lb):
    nlb = lpad // lb
    rb = lambda s, i: (row0 + s * lpad) // lb + i
    pblk = lambda cb: pl.BlockSpec((lb, HD), lambda s, h, i: (rb(s, i), cb + h))
    cwblk = lambda j: pl.BlockSpec((CONV_WIDTH, HD), lambda s, h, i: (0, j * NH + h))
    ciblk = lambda j: pl.BlockSpec((None, SUBLANES, HD), lambda s, h, i: (s, 0, j * NH + h))
    vec = pl.BlockSpec((1, LANES), lambda s, h, i: (0, 0))
    lane_pad = lambda x: jnp.zeros((1, LANES), F32).at[0, :x.shape[0]].set(x)
    return pl.pallas_call(
        functools.partial(_gdn_kernel, lb=lb, lvalid=lvalid, nlb=nlb),
        grid=(nseq, NH, nlb),
        in_specs=[
            pblk(CB_GDN), pblk(CB_GDN + NH), pblk(CB_GDN + 2 * NH), pblk(CB_GDN_Z),
            pl.BlockSpec((lb, LANES), lambda s, h, i: (rb(s, i), CB_SMALL)),
            cwblk(0), cwblk(1), cwblk(2), ciblk(0), ciblk(1), ciblk(2),
            pl.BlockSpec((None, None, HD, HD), lambda s, h, i: (s, h, 0, 0)),
            vec, vec, vec,
        ],
        out_specs=[
            pl.BlockSpec((lb, HD), lambda s, h, i: (s * nlb + i, h)),
            pl.BlockSpec((None, None, HD, HD), lambda s, h, i: (s, h, 0, 0)),
        ],
        out_shape=[
            jax.ShapeDtypeStruct((nseq * lpad, GW), BF16),
            jax.ShapeDtypeStruct((nseq, NH, HD, HD), F32),
        ],
        scratch_shapes=[
            pltpu.VMEM((HD, HD), F32),
            pltpu.VMEM((3, SUBLANES, HD), F32),
            pltpu.VMEM((LANES, lb), F32),
        ],
        compiler_params=_cparams(("parallel", "parallel", "arbitrary")),
        name="gdn",
    )(p, p, p, p, p, conv_w, conv_w, conv_w, conv_init, conv_init, conv_init, s0,
      lane_pad(a_log), lane_pad(dt_bias), norm_w.reshape(1, HD))


def _ret_kernel(q_ref, k_ref, v_ref, g_ref, cos_ref, sin_ref, lg_ref, s0_ref, o_ref, so_ref, s_ref,
                *, lb, cv, nlb):
    ib = pl.program_id(2)

    @pl.when(ib == 0)
    def _():
        s_ref[...] = s0_ref[...]

    cos = cos_ref[...]
    sin = sin_ref[...]
    rot = lambda x: x * cos + pltpu.roll(x, HD // 2, axis=1) * sin
    q = rot(q_ref[...])
    k = rot(k_ref[...]) * (HD ** -0.5)
    v = v_ref[...]
    lg = lg_ref[...][:, 0:1]

    ii = lax.broadcasted_iota(jnp.int32, (CHUNK, CHUNK), 0)
    jj = lax.broadcasted_iota(jnp.int32, (CHUNK, CHUNK), 1)
    rel = (ii - jj).astype(F32)
    dmat = jnp.where(rel >= 0, jnp.exp(jnp.maximum(rel, 0.0) * lg), 0.0)
    idx = lax.broadcasted_iota(jnp.int32, (CHUNK, 1), 0)
    idf = idx.astype(F32)
    xi = jnp.exp((idf + 1.0) * lg)
    zeta = jnp.where(idx < cv, jnp.exp((cv - 1.0 - idf) * lg), 0.0)
    gc = jnp.exp(cv * lg)
    for c in range(lb // CHUNK):
        r = slice(c * CHUNK, (c + 1) * CHUNK)
        qc, kc, vc = q[r], k[r], v[r]
        s = s_ref[...]
        qk = _dot_nt(qc, kc) * dmat
        o = _dot(qk, vc) + _dot(qc * xi, s)
        s_ref[...] = s * gc + _dot_tn(kc * zeta, vc)
        o_ref[r, :] = (_rms(o) * _silu(g_ref[r, :])).astype(BF16)

    @pl.when(ib == nlb - 1)
    def _():
        so_ref[...] = s_ref[...]


def _ret(p, cos_t, sin_t, log_gamma, s0, *, row0, nseq, lpad, lvalid, lb):
    nlb = lpad // lb
    cv = CHUNK if lvalid == lpad else lvalid
    assert cv == CHUNK or (lpad == CHUNK and 0 < lvalid < CHUNK)
    rb = lambda s, i: (row0 + s * lpad) // lb + i
    pblk = lambda cb: pl.BlockSpec((lb, HD), lambda s, h, i: (rb(s, i), cb + h))
    tblk = pl.BlockSpec((lb, HD), lambda s, h, i: (i, 0))
    sblk = pl.BlockSpec((None, None, HD, HD), lambda s, h, i: (s, h, 0, 0))
    lg = jnp.broadcast_to(log_gamma[:, None, None], (NH, 1, LANES))
    return pl.pallas_call(
        functools.partial(_ret_kernel, lb=lb, cv=cv, nlb=nlb),
        grid=(nseq, NH, nlb),
        in_specs=[
            pblk(CB_RET), pblk(CB_RET + NH), pblk(CB_RET + 2 * NH), pblk(CB_RET + 3 * NH),
            tblk, tblk,
            pl.BlockSpec((None, 1, LANES), lambda s, h, i: (h, 0, 0)),
            sblk,
        ],
        out_specs=[pl.BlockSpec((lb, HD), lambda s, h, i: (s * nlb + i, h)), sblk],
        out_shape=[
            jax.ShapeDtypeStruct((nseq * lpad, GW), BF16),
            jax.ShapeDtypeStruct((nseq, NH, HD, HD), F32),
        ],
        scratch_shapes=[pltpu.VMEM((HD, HD), F32)],
        compiler_params=_cparams(("parallel", "parallel", "arbitrary")),
        name="retention",
    )(p, p, p, p, cos_t, sin_t, lg, s0)


def _half_sum(x):
    lane = lax.broadcasted_iota(jnp.int32, x.shape, 1)
    lo = lane < RWKV_HEAD
    s_lo = jnp.sum(jnp.where(lo, x, 0.0), axis=-1, keepdims=True)
    s_hi = jnp.sum(jnp.where(lo, 0.0, x), axis=-1, keepdims=True)
    return jnp.where(lo, s_lo, s_hi)


def _rwkv_kernel(r_ref, k_ref, v_ref, wa_ref, gd_ref, mur_ref, muk_ref, muv_ref, muwa_ref, mugd_ref,
                 sh_r_ref, sh_k_ref, sh_v_ref, sh_wa_ref, sh_gd_ref,
                 wup_ref, aup_ref, gup_ref, w0_ref, a0_ref, kk_ref, ka_ref, rk_ref, lnw_ref, lnb_ref,
                 s0_ref, o_ref, so_ref, s_ref, prev_ref, *, lb, lvalid, nlb):
    ib = pl.program_id(2)
    masked = lvalid < nlb * lb

    @pl.when(ib == 0)
    def _():
        s_ref[...] = s0_ref[...]
        prev_ref[0] = sh_r_ref[...]
        prev_ref[1] = sh_k_ref[...]
        prev_ref[2] = sh_v_ref[...]
        prev_ref[3] = sh_wa_ref[...]
        prev_ref[4] = sh_gd_ref[...]

    def shifted(x_ref, mu_ref, i):
        x = x_ref[...]
        prev = _shift_rows(x, prev_ref[i], 1)
        prev_ref[i] = x[lb - SUBLANES:]
        return x + (prev - x) * mu_ref[...]

    r = shifted(r_ref, mur_ref, 0)
    k = shifted(k_ref, muk_ref, 1)
    v = shifted(v_ref, muv_ref, 2)
    wa = shifted(wa_ref, muwa_ref, 3)
    gd = shifted(gd_ref, mugd_ref, 4)

    w_raw = -_softplus(-(w0_ref[...] + _dot(jnp.tanh(wa), wup_ref[...]))) - 0.5
    logw = -jnp.exp(w_raw)
    a_sig = _sigmoid(a0_ref[...] + _dot(wa, aup_ref[...]))
    gate = _dot(_sigmoid(gd), gup_ref[...])
    kk = k * kk_ref[...]
    kk = kk * lax.rsqrt(_half_sum(kk * kk) + NORM_EPS)
    kp = k * (1.0 + (a_sig - 1.0) * ka_ref[...])
    rec_a = -kk
    rec_b = kk * a_sig
    if masked:
        valid = _row_valid(ib, lb, lvalid, LANES)
        zero = lambda x: jnp.where(valid, x, 0.0)
        logw, rec_a, rec_b, kp, v = zero(logw), zero(rec_a), zero(rec_b), zero(kp), zero(v)

    cum = _dot_exact_lhs(_chunk_tri(lb, CHUNK), logw)
    e_pos = jnp.exp(cum)
    e_neg = jnp.exp(-cum)
    at_all = rec_a * jnp.exp(cum - logw)
    bt_all = rec_b * e_neg
    kt_all = kp * e_neg
    rt_all = r * e_pos

    stacked = 2 * CHUNK
    row_head = lax.broadcasted_iota(jnp.int32, (stacked, LANES), 0) // CHUNK
    lane_head = lax.broadcasted_iota(jnp.int32, (stacked, LANES), 1) // RWKV_HEAD
    own = row_head == lane_head
    stack = lambda x: jnp.where(own, jnp.concatenate([x, x], axis=0), 0.0)
    block_diag = (lax.broadcasted_iota(jnp.int32, (LANES, LANES), 0) // RWKV_HEAD) == (
        lax.broadcasted_iota(jnp.int32, (LANES, LANES), 1) // RWKV_HEAD)
    ti = lax.broadcasted_iota(jnp.int32, (stacked, stacked), 0) % CHUNK
    tj = lax.broadcasted_iota(jnp.int32, (stacked, stacked), 1) % CHUNK

    for c in range(lb // CHUNK):
        rs = slice(c * CHUNK, (c + 1) * CHUNK)
        vc = v[rs]
        at, bt, kt, rt = stack(at_all[rs]), stack(bt_all[rs]), stack(kt_all[rs]), stack(rt_all[rs])
        v2 = jnp.concatenate([vc, vc], axis=0)
        cum_c = cum[rs]
        cum_end = cum_c[CHUNK - 1:]
        to_end = jnp.exp(cum_end - cum_c)
        s = s_ref[...]
        l_ab = jnp.where(ti > tj, _dot_nt(at, bt), 0.0)
        l_ak = jnp.where(ti > tj, _dot_nt(at, kt), 0.0)
        l_rb = jnp.where(ti >= tj, _dot_nt(rt, bt), 0.0)
        l_rk = jnp.where(ti >= tj, _dot_nt(rt, kt), 0.0)
        inv = _unit_lower_inv(-l_ab, CHUNK)
        u2 = jnp.where(own, _dot(inv, _dot_nt(at, s) + _dot(l_ak, v2)), 0.0)
        o2 = jnp.where(own, _dot_nt(rt, s) + _dot(l_rb, u2) + _dot(l_rk, v2), 0.0)
        u = u2[:CHUNK] + u2[CHUNK:]
        o = o2[:CHUNK] + o2[CHUNK:]
        upd = _dot_tn(u, rec_b[rs] * to_end) + _dot_tn(vc, kp[rs] * to_end)
        s_ref[...] = s * jnp.exp(cum_end) + jnp.where(block_diag, upd, 0.0)

        inv_n = 1.0 / RWKV_HEAD
        mu = _half_sum(o) * inv_n
        var = _half_sum(jnp.square(o - mu)) * inv_n
        on = (o - mu) * lax.rsqrt(var + GN_EPS) * lnw_ref[...] + lnb_ref[...]
        bonus = _half_sum(r[rs] * kp[rs] * rk_ref[...]) * vc
        o_ref[rs, :] = ((on + bonus) * gate[rs]).astype(BF16)

    @pl.when(ib == nlb - 1)
    def _():
        so_ref[...] = s_ref[...]


def _rwkv(p, lw, shift_init, s0, *, row0, nseq, lpad, lvalid, lb):
    nlb = lpad // lb
    npair = GW // LANES
    rb = lambda s, i: (row0 + s * lpad) // lb + i
    pblk = lambda cb, per_pair: pl.BlockSpec(
        (lb, LANES), lambda s, j, i: (rb(s, i), cb + (j if per_pair else 0)))
    mublk = lambda cb, per_pair: pl.BlockSpec((1, LANES), lambda s, j, i: (0, cb + (j if per_pair else 0)))
    shblk = lambda cb, per_pair: pl.BlockSpec(
        (None, SUBLANES, LANES), lambda s, j, i: (s, 0, cb + (j if per_pair else 0)))
    pair_vec = pl.BlockSpec((1, LANES), lambda s, j, i: (0, j))
    pair_mat = pl.BlockSpec((LANES, LANES), lambda s, j, i: (0, j))
    sblk = pl.BlockSpec((None, None, LANES, LANES), lambda s, j, i: (s, j, 0, 0))
    mu = lw['rwkv_mu'].reshape(1, -1)
    zeros64 = jnp.zeros((64, GW), F32)
    wup = jnp.concatenate([lw['rwkv_w_up'], zeros64], axis=0).astype(BF16)
    aup = jnp.concatenate([zeros64, lw['rwkv_a_up']], axis=0).astype(BF16)
    row = lambda x: x.reshape(1, GW)
    blocks = [(0, True), (4, True), (8, True), (12, False), (13, False)]
    return pl.pallas_call(
        functools.partial(_rwkv_kernel, lb=lb, lvalid=lvalid, nlb=nlb),
        grid=(nseq, npair, nlb),
        in_specs=(
            [pblk(CB_RWKV + cb, pp) for cb, pp in blocks]
            + [mublk(cb, pp) for cb, pp in blocks]
            + [shblk(cb, pp) for cb, pp in blocks]
            + [pair_mat, pair_mat, pair_mat] + [pair_vec] * 7 + [sblk]
        ),
        out_specs=[pl.BlockSpec((lb, LANES), lambda s, j, i: (s * nlb + i, j)), sblk],
        out_shape=[
            jax.ShapeDtypeStruct((nseq * lpad, GW), BF16),
            jax.ShapeDtypeStruct((nseq, npair, LANES, LANES), F32),
        ],
        scratch_shapes=[pltpu.VMEM((LANES, LANES), F32), pltpu.VMEM((5, SUBLANES, LANES), F32)],
        compiler_params=_cparams(("parallel", "parallel", "arbitrary")),
        name="rwkv7",
    )(p, p, p, p, p, mu, mu, mu, mu, mu, shift_init, shift_init, shift_init, shift_init, shift_init,
      wup, aup, lw['rwkv_g_up'].astype(BF16), row(lw['rwkv_w0']), row(lw['rwkv_a0']), row(lw['rwkv_k_k']),
      row(lw['rwkv_k_a']), row(lw['rwkv_r_k']), row(lw['rwkv_ln_w']), row(lw['rwkv_ln_b']), s0)


def _fox_prep_kernel(q_ref, k_ref, v_ref, sm_ref, qw_ref, kw_ref, bf_ref,
                     qn_ref, kn_ref, kb_ref, vb_ref, lf_ref, c_ref, ct_ref, carry_ref, *, lb):
    @pl.when(pl.program_id(1) == 0)
    def _():
        carry_ref[...] = jnp.zeros_like(carry_ref)

    qw = qw_ref[...]
    kw = kw_ref[...]
    for h in range(NH):
        sl = slice(h * HD, (h + 1) * HD)
        qn_ref[:, sl] = _rms(q_ref[:, sl], qw).astype(BF16)
        kn = _rms(k_ref[:, sl], kw)
        kn_ref[:, sl] = kn
        kb_ref[:, sl] = kn.astype(BF16)
    vb_ref[...] = v_ref[...].astype(BF16)
    logf = -_softplus(-(sm_ref[...] + bf_ref[...]))
    lf_ref[...] = logf
    c = _dot_exact_lhs(_chunk_tri(lb, lb), logf) + carry_ref[0:1, :]
    c_ref[...] = c
    carry_ref[...] = jnp.broadcast_to(c[lb - 1:], carry_ref.shape)
    ct_ref[...] = c.T[SM_F:SM_F + SUBLANES]


def _fox_prep(p, q_w, k_w, b_f, *, nseq, lpad, lb):
    nlb = lpad // lb
    rb = lambda s, i: s * nlb + i
    seg = lambda j: pl.BlockSpec((lb, GW), lambda s, i: (rb(s, i), j))
    vec = pl.BlockSpec((1, LANES), lambda s, i: (0, 0))
    rows = nseq * lpad
    bf_lane = jnp.zeros((1, LANES), F32).at[0, SM_F:SM_F + NH].set(b_f)
    return pl.pallas_call(
        functools.partial(_fox_prep_kernel, lb=lb),
        grid=(nseq, nlb),
        in_specs=[seg(0), seg(1), seg(2),
                  pl.BlockSpec((lb, LANES), lambda s, i: (rb(s, i), CB_SMALL)), vec, vec, vec],
        out_specs=[
            pl.BlockSpec((lb, GW), lambda s, i: (rb(s, i), 0)),
            pl.BlockSpec((lb, GW), lambda s, i: (rb(s, i), 0)),
            pl.BlockSpec((lb, GW), lambda s, i: (rb(s, i), 0)),
            pl.BlockSpec((lb, GW), lambda s, i: (rb(s, i), 0)),
            pl.BlockSpec((lb, LANES), lambda s, i: (rb(s, i), 0)),
            pl.BlockSpec((lb, LANES), lambda s, i: (rb(s, i), 0)),
            pl.BlockSpec((SUBLANES, lb), lambda s, i: (0, rb(s, i))),
        ],
        out_shape=[
            jax.ShapeDtypeStruct((rows, GW), BF16),
            jax.ShapeDtypeStruct((rows, GW), F32),
            jax.ShapeDtypeStruct((rows, GW), BF16),
            jax.ShapeDtypeStruct((rows, GW), BF16),
            jax.ShapeDtypeStruct((rows, LANES), F32),
            jax.ShapeDtypeStruct((rows, LANES), F32),
            jax.ShapeDtypeStruct((SUBLANES, rows), F32),
        ],
        scratch_shapes=[pltpu.VMEM((SUBLANES, LANES), F32)],
        compiler_params=_cparams(("parallel", "arbitrary")),
        name="fox_prep",
    )(p, p, p, p, q_w.reshape(1, HD), k_w.reshape(1, HD), bf_lane)


def _fox_flash_kernel(q_ref, k_ref, v_ref, c_ref, ct_ref, g_ref, o_ref, m_ref, l_ref, acc_ref, *, tq):
    h = pl.program_id(1)
    qi = pl.program_id(2)
    scale = HD ** -0.5
    q = q_ref[...]
    c_col = _lane_col(c_ref[...], SM_F + h)
    m_ref[...] = jnp.full_like(m_ref, -1e30)
    l_ref[...] = jnp.zeros_like(l_ref)
    acc_ref[...] = jnp.zeros_like(acc_ref)

    def block(ks, width, diagonal):
        kb = k_ref[pl.ds(ks, width), :]
        vb = v_ref[pl.ds(ks, width), :]
        s = _dot_nt(q, kb) * scale + c_col - ct_ref[pl.ds(h, 1), pl.ds(ks, width)]
        if diagonal:
            ii = lax.broadcasted_iota(jnp.int32, (tq, width), 0)
            jj = lax.broadcasted_iota(jnp.int32, (tq, width), 1)
            s = jnp.where(jj <= ii, s, -jnp.inf)
        m_old = m_ref[...]
        m_new = jnp.maximum(m_old, jnp.max(s, axis=-1, keepdims=True))
        alpha = jnp.exp(m_old - m_new)
        pr = jnp.exp(s - m_new)
        l_ref[...] = alpha * l_ref[...] + jnp.sum(pr, axis=-1, keepdims=True)
        acc_ref[...] = alpha * acc_ref[...] + _dot(pr, vb)
        m_ref[...] = m_new

    def body(kp, carry):
        block(pl.multiple_of(kp * 2 * tq, 2 * tq), 2 * tq, False)
        return carry

    lax.fori_loop(0, qi // 2, body, 0)

    @pl.when(qi % 2 == 1)
    def _():
        block(pl.multiple_of((qi - 1) * tq, tq), tq, False)

    block(pl.multiple_of(qi * tq, tq), tq, True)
    o_ref[...] = (acc_ref[...] / l_ref[...] * _sigmoid(g_ref[...])).astype(BF16)


def _fox_flash(qn, kb, vb, c_col, c_row, p, *, nseq, lpad, tq):
    nq = lpad // tq
    return pl.pallas_call(
        functools.partial(_fox_flash_kernel, tq=tq),
        grid=(nseq, NH, nq),
        in_specs=[
            pl.BlockSpec((tq, HD), lambda s, h, i: (s * nq + i, h)),
            pl.BlockSpec((lpad, HD), lambda s, h, i: (s, h)),
            pl.BlockSpec((lpad, HD), lambda s, h, i: (s, h)),
            pl.BlockSpec((tq, LANES), lambda s, h, i: (s * nq + i, 0)),
            pl.BlockSpec((SUBLANES, lpad), lambda s, h, i: (0, s)),
            pl.BlockSpec((tq, HD), lambda s, h, i: (s * nq + i, CB_FOX + 3 * NH + h)),
        ],
        out_specs=pl.BlockSpec((tq, HD), lambda s, h, i: (s * nq + i, h)),
        out_shape=jax.ShapeDtypeStruct((nseq * lpad, GW), BF16),
        scratch_shapes=[pltpu.VMEM((tq, 1), F32), pltpu.VMEM((tq, 1), F32), pltpu.VMEM((tq, HD), F32)],
        compiler_params=_cparams(("parallel", "parallel", "arbitrary")),
        name="fox_flash",
    )(qn, kb, vb, c_col, c_row, p)


PAGE_GROUP = 8
PAGE_COLS = PAGE * NH
NQ_PAD = SUBLANES


def _fox_sample_kernel(pt_ref, pq_ref, sm_ref, *rest, lvalid, lpad, n_steps):
    del pt_ref
    g = PAGE_GROUP
    kps, vps, lfs = rest[:g], rest[g:2 * g], rest[2 * g:3 * g]
    (upper_ref, heads_ref, qw_ref, kw_ref, bf_ref, o_ref, kn_ref, lf_ref,
     qn_s, cq_s, m_s, l_s, acc_s, carry_s) = rest[3 * g:]
    i = pl.program_id(1)
    scale = HD ** -0.5
    nq = NQ_PAD
    nrow = NH * nq

    @pl.when(i == 0)
    def _():
        logf = -_softplus(-(sm_ref[...] + bf_ref[...]))
        logf = jnp.where(_row_valid(0, lpad, lvalid, LANES), logf, 0.0)
        lf_ref[...] = logf[:nq]
        c = _dot_exact_lhs(_chunk_tri(lpad, lpad), logf)
        c_t = c.T
        carry_s[...] = jnp.zeros_like(carry_s)
        qi = lax.broadcasted_iota(jnp.int32, (nq, lpad), 0)
        kj = lax.broadcasted_iota(jnp.int32, (nq, lpad), 1)
        for h in range(NH):
            sl = slice(h * HD, (h + 1) * HD)
            rows = slice(h * nq, (h + 1) * nq)
            qn = _rms(pq_ref[:nq, sl], qw_ref[...])
            kn = _rms(pq_ref[:, GW + h * HD:GW + (h + 1) * HD], kw_ref[...])
            vn = pq_ref[:, 2 * GW + h * HD:2 * GW + (h + 1) * HD]
            c_h = c[:nq, SM_F + h:SM_F + h + 1]
            qn_s[rows, :] = qn
            cq_s[rows, :] = c_h
            kn_ref[:, sl] = kn[:nq]
            s = _dot_nt(qn, kn) * scale + c_h - c_t[SM_F + h:SM_F + h + 1, :]
            s = jnp.where(kj <= qi, s, -jnp.inf)
            m = jnp.max(s, axis=-1, keepdims=True)
            pr = jnp.exp(s - m)
            m_s[rows, :] = m
            l_s[rows, :] = jnp.sum(pr, axis=-1, keepdims=True)
            acc_s[rows, :] = _dot(pr, vn)

    lf = jnp.concatenate([r[...] for r in lfs], axis=0)
    within = _dot_exact_rhs(lf, upper_ref[...])
    totals = _dot_exact_rhs(lf, heads_ref[...])
    run = carry_s[...]
    suffix = [None] * g
    for j in reversed(range(g)):
        suffix[j] = within[j:j + 1] + run
        run = run + totals[j:j + 1]
    carry_s[...] = run

    row_head = lax.broadcasted_iota(jnp.int32, (nrow, PAGE_COLS), 0) // nq
    col_head = lax.broadcasted_iota(jnp.int32, (nrow, PAGE_COLS), 1) % NH
    own = row_head == col_head
    qs = qn_s[...].astype(BF16)
    bias = cq_s[...]
    tiles = [jnp.where(own, _dot_nt(qs, kps[j][...]) * scale + bias + suffix[j], -jnp.inf) for j in range(g)]
    m_old = m_s[...]
    m_new = m_old
    for t in tiles:
        m_new = jnp.maximum(m_new, jnp.max(t, axis=-1, keepdims=True))
    alpha = jnp.exp(m_old - m_new)
    l_new = alpha * l_s[...]
    acc = alpha * acc_s[...]
    for j, t in enumerate(tiles):
        pr = jnp.exp(t - m_new)
        l_new = l_new + jnp.sum(pr, axis=-1, keepdims=True)
        acc = acc + _dot(pr, vps[j][...])
    m_s[...] = m_new
    l_s[...] = l_new
    acc_s[...] = acc

    @pl.when(i == n_steps - 1)
    def _():
        o_ref[...] = jnp.zeros_like(o_ref)
        out = acc_s[...] / l_s[...]
        for h in range(NH):
            sl = slice(h * HD, (h + 1) * HD)
            gate = pq_ref[:nq, 3 * GW + h * HD:3 * GW + (h + 1) * HD]
            o_ref[:nq, sl] = (out[h * nq:(h + 1) * nq] * _sigmoid(gate)).astype(BF16)


def _fox_sample(p, page_table, cache_k, cache_v, cache_lf, q_w, k_w, b_f, *, layer, row0, nseq, lpad, lvalid):
    n_pages = page_table.shape[1]
    g = PAGE_GROUP
    assert n_pages % g == 0 and lvalid <= NQ_PAD
    n_steps = n_pages // g
    bf_lane = jnp.zeros((1, LANES), F32).at[0, SM_F:SM_F + NH].set(b_f)
    idx = np.arange(PAGE_COLS)
    same_head = (idx[:, None] % NH) == (idx[None, :] % NH)
    upper = jnp.asarray(same_head & (idx[:, None] // NH > idx[None, :] // NH), BF16)
    heads = jnp.asarray(same_head, BF16)
    rb = lambda b: row0 // lpad + b
    vec = pl.BlockSpec((1, LANES), lambda b, i, pt: (0, 0))
    const = pl.BlockSpec((PAGE_COLS, PAGE_COLS), lambda b, i, pt: (0, 0))

    def page_spec(shape, j):
        zeros = (0,) * len(shape)
        return pl.BlockSpec((None, None) + shape,
                            lambda b, i, pt: (layer, pt[b, n_pages - g * (i + 1) + j]) + zeros)

    grid_spec = pltpu.PrefetchScalarGridSpec(
        num_scalar_prefetch=1,
        grid=(nseq, n_steps),
        in_specs=(
            [pl.BlockSpec((lpad, 4 * GW), lambda b, i, pt: (rb(b), CB_FOX)),
             pl.BlockSpec((lpad, LANES), lambda b, i, pt: (rb(b), CB_SMALL))]
            + [page_spec((PAGE_COLS, HD), j) for j in range(g)]
            + [page_spec((PAGE_COLS, HD), j) for j in range(g)]
            + [page_spec((1, PAGE_COLS), j) for j in range(g)]
            + [const, const, vec, vec, vec]
        ),
        out_specs=[
            pl.BlockSpec((lpad, GW), lambda b, i, pt: (b, 0)),
            pl.BlockSpec((None, NQ_PAD, GW), lambda b, i, pt: (b, 0, 0)),
            pl.BlockSpec((None, NQ_PAD, LANES), lambda b, i, pt: (b, 0, 0)),
        ],
        scratch_shapes=[
            pltpu.VMEM((NH * NQ_PAD, HD), F32),
            pltpu.VMEM((NH * NQ_PAD, 1), F32),
            pltpu.VMEM((NH * NQ_PAD, 1), F32),
            pltpu.VMEM((NH * NQ_PAD, 1), F32),
            pltpu.VMEM((NH * NQ_PAD, HD), F32),
            pltpu.VMEM((1, PAGE_COLS), F32),
        ],
    )
    return pl.pallas_call(
        functools.partial(_fox_sample_kernel, lvalid=lvalid, lpad=lpad, n_steps=n_steps),
        grid_spec=grid_spec,
        out_shape=[
            jax.ShapeDtypeStruct((nseq * lpad, GW), BF16),
            jax.ShapeDtypeStruct((nseq, NQ_PAD, GW), F32),
            jax.ShapeDtypeStruct((nseq, NQ_PAD, LANES), F32),
        ],
        compiler_params=_cparams(("parallel", "arbitrary")),
        name="fox_sample",
    )(page_table, p, p, *([cache_k] * g), *([cache_v] * g), *([cache_lf] * g), upper, heads,
      q_w.reshape(1, HD), k_w.reshape(1, HD), bf_lane)


W_IN_SHIFT = 8
_COPY, _SHIFT, _SMALL, _ZERO = 0, 1, 2, 3


def _w_in_plan():
    kind = np.zeros(NP_COLS // LANES, np.int32)
    src = np.zeros(NP_COLS // LANES, np.int32)

    def put(cb, n, first_src_block, k):
        kind[cb:cb + n] = k
        src[cb:cb + n] = first_src_block + np.arange(n)

    put(CB_FOX, 16, (5896 - W_IN_SHIFT) // LANES, _SHIFT)
    put(CB_RET, 16, (2056 - W_IN_SHIFT) // LANES, _SHIFT)
    put(CB_GDN, 12, 0, _COPY)
    put(CB_GDN_Z, 4, (1544 - W_IN_SHIFT) // LANES, _SHIFT)
    put(CB_RWKV, 14, (4104 - W_IN_SHIFT) // LANES, _SHIFT)
    put(CB_SMALL, 1, 1536 // LANES, _SMALL)
    put(CB_SMALL + 1, 1, 0, _ZERO)
    src_b = np.where(kind == _SHIFT, src + 1, np.where(kind == _SMALL, (7944 - W_IN_SHIFT) // LANES, src))
    return jnp.asarray(kind), jnp.asarray(src), jnp.asarray(src_b.astype(np.int32))


def _prep_w_in_kernel(kind_ref, sa_ref, sb_ref, a_ref, b_ref, o_ref):
    del sa_ref, sb_ref
    kind = kind_ref[pl.program_id(0)]
    lane = lax.broadcasted_iota(jnp.int32, a_ref.shape, 1)

    @pl.when(kind == _COPY)
    def _():
        o_ref[...] = a_ref[...].astype(BF16)

    @pl.when(kind == _SHIFT)
    def _():
        ra = pltpu.roll(a_ref[...], LANES - W_IN_SHIFT, axis=1)
        rb = pltpu.roll(b_ref[...], LANES - W_IN_SHIFT, axis=1)
        o_ref[...] = jnp.where(lane < LANES - W_IN_SHIFT, ra, rb).astype(BF16)

    @pl.when(kind == _SMALL)
    def _():
        small = jnp.where(lane < SM_F, a_ref[...], jnp.where(lane < SM_F + NH, b_ref[...], 0.0))
        o_ref[...] = small.astype(BF16)

    @pl.when(kind == _ZERO)
    def _():
        o_ref[...] = jnp.zeros_like(o_ref)


def _prep_w_in(w_in, layer):
    k = w_in.shape[1]
    kind, src_a, src_b = _w_in_plan()
    grid_spec = pltpu.PrefetchScalarGridSpec(
        num_scalar_prefetch=3,
        grid=(NP_COLS // LANES,),
        in_specs=[
            pl.BlockSpec((None, k, LANES), lambda j, kd, sa, sb: (layer, 0, sa[j])),
            pl.BlockSpec((None, k, LANES), lambda j, kd, sa, sb: (layer, 0, sb[j])),
        ],
        out_specs=pl.BlockSpec((k, LANES), lambda j, kd, sa, sb: (0, j)),
    )
    return pl.pallas_call(
        _prep_w_in_kernel,
        grid_spec=grid_spec,
        out_shape=jax.ShapeDtypeStruct((k, NP_COLS), BF16),
        compiler_params=_cparams(("parallel",)),
        name="prep_w_in",
    )(kind, src_a, src_b, w_in, w_in)


def _rope_tables(pos):
    half = HD // 2
    inv = 1.0 / (ROPE_BASE ** jnp.linspace(0.0, 1.0, half, dtype=F32))
    ang = pos.astype(F32)[:, None] * inv[None, :]
    cos, sin = jnp.cos(ang), jnp.sin(ang)
    return jnp.concatenate([cos, cos], axis=-1), jnp.concatenate([-sin, sin], axis=-1)


def _state_tile(state, nrows):
    b, _, c = state.shape
    return jnp.concatenate([jnp.zeros((b, SUBLANES - nrows, c), F32), state], axis=1)


def _rwkv_pair_states(s):
    b = s.shape[0]
    s = s.reshape(b, 4, 2, RWKV_HEAD, RWKV_HEAD)
    z = jnp.zeros_like(s[:, :, 0])
    top = jnp.concatenate([s[:, :, 0], z], axis=-1)
    bot = jnp.concatenate([z, s[:, :, 1]], axis=-1)
    return jnp.concatenate([top, bot], axis=-2)


def _rwkv_unpair_states(sp):
    b = sp.shape[0]
    a = sp[:, :, :RWKV_HEAD, :RWKV_HEAD]
    c = sp[:, :, RWKV_HEAD:, RWKV_HEAD:]
    return jnp.stack([a, c], axis=2).reshape(b, 8, RWKV_HEAD, RWKV_HEAD)


def kernel(x_prompt, x_sample, cache_fox_k, cache_fox_v, cache_fox_logf, cache_mem_k, cache_mem_v, state_gdn_conv, state_gdn_S, state_ret_S, state_rwkv_shift, state_rwkv_S, page_table, mem_prompt, norm_mix, w_in, gdn_conv_w, gdn_A_log, gdn_dt_bias, gdn_norm, rwkv_mu, rwkv_w0, rwkv_w_up, rwkv_a0, rwkv_a_up, rwkv_g_up, rwkv_k_k, rwkv_k_a, rwkv_r_k, rwkv_ln_w, rwkv_ln_b, fox_b_f, fox_q_norm, fox_k_norm, w_out, norm_x, norm_mem, xattn_wq, xattn_wkv, xattn_q_norm, xattn_k_norm, xattn_wo, norm_ffn, ffn_w_gate, ffn_w_up, ffn_w_down):
    weights = {
        'norm_mix': norm_mix, 'w_in': w_in, 'gdn_conv_w': gdn_conv_w, 'gdn_A_log': gdn_A_log,
        'gdn_dt_bias': gdn_dt_bias, 'gdn_norm': gdn_norm, 'rwkv_mu': rwkv_mu, 'rwkv_w0': rwkv_w0,
        'rwkv_w_up': rwkv_w_up, 'rwkv_a0': rwkv_a0, 'rwkv_a_up': rwkv_a_up, 'rwkv_g_up': rwkv_g_up,
        'rwkv_k_k': rwkv_k_k, 'rwkv_k_a': rwkv_k_a, 'rwkv_r_k': rwkv_r_k, 'rwkv_ln_w': rwkv_ln_w,
        'rwkv_ln_b': rwkv_ln_b, 'fox_b_f': fox_b_f, 'fox_q_norm': fox_q_norm, 'fox_k_norm': fox_k_norm,
        'w_out': w_out, 'norm_x': norm_x, 'norm_mem': norm_mem, 'xattn_wq': xattn_wq, 'xattn_wkv': xattn_wkv,
        'xattn_q_norm': xattn_q_norm, 'xattn_k_norm': xattn_k_norm, 'xattn_wo': xattn_wo,
        'norm_ffn': norm_ffn, 'ffn_w_gate': ffn_w_gate, 'ffn_w_up': ffn_w_up, 'ffn_w_down': ffn_w_down,
    }
    depth = w_in.shape[0]
    bp, lp, d = x_prompt.shape
    bs, ls, _ = x_sample.shape
    n_pages = page_table.shape[1]
    past_len = n_pages * PAGE
    tp = bp * lp
    ts = bs * SAMPLE_PAD
    tt = tp + ts
    tm = 512 if tt % 512 == 0 else 256
    lb_p = min(256, lp)
    assert ls >= CONV_WIDTH - 1 and ls <= SUBLANES and tt % tm == 0 and lp % lb_p == 0 and tp % tm == 0

    xs_pad = jnp.zeros((bs, SAMPLE_PAD, d), F32).at[:, :ls].set(x_sample)
    x = jnp.concatenate([x_prompt.reshape(tp, d), xs_pad.reshape(ts, d)], axis=0)

    cos_p, sin_p = _rope_tables(jnp.arange(lp, dtype=jnp.int32))
    cos_s, sin_s = _rope_tables(past_len + jnp.arange(SAMPLE_PAD, dtype=jnp.int32))
    log_gamma = jnp.log(1.0 - jnp.exp2(-(RET_GAMMA_BASE + jnp.arange(NH, dtype=F32))))
    n_pool = cache_fox_k.shape[1]
    cache_k = cache_fox_k.reshape(depth, n_pool, PAGE_COLS, HD)
    cache_v = cache_fox_v.reshape(depth, n_pool, PAGE_COLS, HD)
    cache_lf = cache_fox_logf.reshape(depth, n_pool, 1, PAGE_COLS)
    zeros_s = jnp.zeros((bp, NH, HD, HD), F32)
    zeros_conv = jnp.zeros((bp, SUBLANES, 3 * GW), F32)
    zeros_shift = jnp.zeros((bp, SUBLANES, 1792), F32)

    outs_p, outs_s, mem_ks, mem_vs = [], [], [], []
    for l in range(depth):
        lw = {name: arr[l] for name, arr in weights.items()}
        p = _norm_matmul(x, lw['norm_mix'], _prep_w_in(w_in, l), tm=tm, tn=1024)

        gp = dict(row0=0, nseq=bp, lpad=lp, lvalid=lp, lb=lb_p)
        o_gdn_p, gdn_s_p = _gdn(p, lw['gdn_conv_w'], zeros_conv, zeros_s, lw['gdn_A_log'], lw['gdn_dt_bias'],
                                lw['gdn_norm'], **gp)
        o_ret_p, ret_s_p = _ret(p, cos_p, sin_p, log_gamma, zeros_s, **gp)
        o_rwkv_p, rwkv_s_p = _rwkv(p, lw, zeros_shift, zeros_s, **gp)
        qn, kn, kb, vb, lf, c_col, c_row = _fox_prep(p, lw['fox_q_norm'], lw['fox_k_norm'], lw['fox_b_f'],
                                                    nseq=bp, lpad=lp, lb=lb_p)
        o_fox_p = _fox_flash(qn, kb, vb, c_col, c_row, p, nseq=bp, lpad=lp, tq=min(FOX_TQ, lp))

        gs = dict(row0=tp, nseq=bs, lpad=SAMPLE_PAD, lvalid=ls, lb=SAMPLE_PAD)
        o_gdn_s, gdn_s_s = _gdn(p, lw['gdn_conv_w'], _state_tile(state_gdn_conv[l], CONV_WIDTH - 1),
                                state_gdn_S[l], lw['gdn_A_log'], lw['gdn_dt_bias'], lw['gdn_norm'], **gs)
        o_ret_s, ret_s_s = _ret(p, cos_s, sin_s, log_gamma, state_ret_S[l], **gs)
        o_rwkv_s, rwkv_s_s = _rwkv(p, lw, _state_tile(state_rwkv_shift[l], 1),
                                   _rwkv_pair_states(state_rwkv_S[l]), **gs)
        o_fox_s, kn_s, lf_s = _fox_sample(p, page_table, cache_k, cache_v, cache_lf,
                                          lw['fox_q_norm'], lw['fox_k_norm'], lw['fox_b_f'],
                                          layer=l, row0=tp, nseq=bs, lpad=SAMPLE_PAD, lvalid=ls)

        o_mix = jnp.concatenate([
            jnp.concatenate([o_gdn_p, o_ret_p, o_rwkv_p, o_fox_p], axis=1),
            jnp.concatenate([o_gdn_s, o_ret_s, o_rwkv_s, o_fox_s], axis=1)], axis=0)
        x = _matmul_res(o_mix, lw['w_out'].astype(BF16), x, tm=tm, tn=512)

        kv = _norm_matmul(mem_prompt.reshape(bp * N_MEM, d), lw['norm_mem'], lw['xattn_wkv'].astype(BF16),
                          tm=256, tn=XW, head_w=lw['xattn_k_norm'], norm_tiles=1)
        mk = kv[:, :XW].reshape(bp, N_MEM, XW)
        mv = kv[:, XW:].reshape(bp, N_MEM, XW)
        q = _norm_matmul(x, lw['norm_x'], lw['xattn_wq'].astype(BF16), tm=tm, tn=XW,
                         head_w=lw['xattn_q_norm'], norm_tiles=1)
        xo_p = _xattn(q, mk, mv, row0=0, nrows=tp, tq=lb_p, rows_per_seq=lp)
        xo_s = _xattn(q, cache_mem_k[l].reshape(bs, N_MEM, XW), cache_mem_v[l].reshape(bs, N_MEM, XW),
                      row0=tp, nrows=ts, tq=SAMPLE_PAD, rows_per_seq=SAMPLE_PAD)
        x = _matmul_res(jnp.concatenate([xo_p, xo_s], axis=0), lw['xattn_wo'].astype(BF16), x, tm=tm, tn=512)

        hidden = _swiglu_up(x, lw['norm_ffn'], lw['ffn_w_gate'].astype(BF16), lw['ffn_w_up'].astype(BF16),
                            tm=tm, tn=512)
        x = _matmul_res(hidden, lw['ffn_w_down'].astype(BF16), x, tm=tm, tn=512)

        c0 = CB_GDN * LANES
        r0 = CB_RWKV * LANES
        v0 = (CB_FOX + 2 * NH) * LANES

        def last_rows(row_end, n, col0, width, nseq, stride):
            return jnp.stack([lax.slice(p, (b * stride + row_end - n, col0), (b * stride + row_end, col0 + width))
                              for b in range(nseq)], axis=0)

        ps_v = lax.slice(p, (tp, v0), (tt, v0 + GW)).reshape(bs, SAMPLE_PAD, NH, HD)
        outs_p.append((
            kn.reshape(bp, lp, NH, HD),
            lax.slice(p, (0, v0), (tp, v0 + GW)).reshape(bp, lp, NH, HD),
            lf.reshape(bp, lp, LANES)[:, :, SM_F:SM_F + NH],
            last_rows(lp, CONV_WIDTH - 1, c0, 3 * GW, bp, lp),
            gdn_s_p, ret_s_p,
            last_rows(lp, 1, r0, 1792, bp, lp),
            _rwkv_unpair_states(rwkv_s_p),
        ))
        outs_s.append((
            kn_s[:, :ls].reshape(bs, ls, NH, HD),
            ps_v[:, :ls],
            lf_s[:, :ls, SM_F:SM_F + NH],
            last_rows(tp + ls, CONV_WIDTH - 1, c0, 3 * GW, bs, SAMPLE_PAD),
            gdn_s_s, ret_s_s,
            last_rows(tp + ls, 1, r0, 1792, bs, SAMPLE_PAD),
            _rwkv_unpair_states(rwkv_s_s),
        ))
        mem_ks.append(mk.reshape(bp, N_MEM, NH, HD))
        mem_vs.append(mv.reshape(bp, N_MEM, NH, HD))

    stk = lambda seq, i: jnp.stack([e[i] for e in seq], axis=0)
    yp = x[:tp].reshape(bp, lp, d)
    ys = x[tp:].reshape(bs, SAMPLE_PAD, d)[:, :ls]
    return (yp, ys, stk(outs_p, 0), stk(outs_p, 1), stk(outs_p, 2), jnp.stack(mem_ks, 0), jnp.stack(mem_vs, 0),
            stk(outs_p, 3), stk(outs_p, 4), stk(outs_p, 5), stk(outs_p, 6), stk(outs_p, 7),
            stk(outs_s, 0), stk(outs_s, 1), stk(outs_s, 2), stk(outs_s, 3), stk(outs_s, 4), stk(outs_s, 5),
            stk(outs_s, 6), stk(outs_s, 7))
```

```python
import functools
import math

import jax
import jax.numpy as jnp
import numpy as np
from jax import lax
from jax.experimental import pallas as pl
from jax.experimental.pallas import tpu as pltpu

F32 = jnp.float32
BF16 = jnp.bfloat16

LANES = 128
SUBLANES = 8
VMEM_LIMIT = 56 * 1024 * 1024

D_MODEL = 2048
GW = D_MODEL // 4
HD = 128
NH = GW // HD
RWKV_HEAD = 64
CONV_WIDTH = 4
PAGE = 128
N_MEM = 256
XW = 512
D_FF = 5632
NORM_EPS = 1e-6
GN_EPS = 64e-5
RET_GAMMA_BASE = 5.0
ROPE_BASE = 10000.0
CHUNK = 64
SAMPLE_PAD = 64
FOX_TQ = 512
GDN_LB = 512
RWKV_LB = 512

NP_COLS = 8192
CB_FOX, CB_RET, CB_GDN, CB_GDN_Z, CB_RWKV, CB_SMALL = 0, 16, 32, 44, 48, 62
SM_A, SM_B, SM_F = 0, 4, 8


def _cparams(sem):
    return pltpu.CompilerParams(dimension_semantics=sem, vmem_limit_bytes=VMEM_LIMIT)


def _dot(a, b):
    return jnp.dot(a.astype(BF16), b.astype(BF16), preferred_element_type=F32)


def _dot_nt(a, b):
    return lax.dot_general(a.astype(BF16), b.astype(BF16), (((1,), (1,)), ((), ())),
                           preferred_element_type=F32)


def _dot_tn(a, b):
    return lax.dot_general(a.astype(BF16), b.astype(BF16), (((0,), (0,)), ((), ())),
                           preferred_element_type=F32)


def _split3(x):
    hi = x.astype(BF16)
    r = x - hi.astype(F32)
    mid = r.astype(BF16)
    lo = (r - mid.astype(F32)).astype(BF16)
    return hi, mid, lo


def _dot_exact_lhs(m, x):
    hi, mid, lo = _split3(x)
    d = lambda p: jnp.dot(m, p, preferred_element_type=F32)
    return d(hi) + d(mid) + d(lo)


def _dot_exact_rhs(x, m):
    hi, mid, lo = _split3(x)
    d = lambda p: jnp.dot(p, m, preferred_element_type=F32)
    return d(hi) + d(mid) + d(lo)


def _bmm(a, b):
    return lax.dot_general(a.astype(BF16), b.astype(BF16), (((2,), (1,)), ((0,), (0,))),
                           preferred_element_type=F32)


def _bmm_nt(a, b):
    return lax.dot_general(a.astype(BF16), b.astype(BF16), (((2,), (2,)), ((0,), (0,))),
                           preferred_element_type=F32)


def _unit_lower_inv(n, nil):
    c = n.shape[-1]
    ii = lax.broadcasted_iota(jnp.int32, (c, c), 0)
    jj = lax.broadcasted_iota(jnp.int32, (c, c), 1)
    p = jnp.where(ii == jj, 1.0, 0.0).astype(F32) - n
    q = n
    for _ in range(int(math.log2(nil)) - 1):
        q = _bmm(q, q)
        p = p + _bmm(p, q)
    return p


def _chunk_tri(lb, chunk):
    ii = lax.broadcasted_iota(jnp.int32, (lb, lb), 0)
    jj = lax.broadcasted_iota(jnp.int32, (lb, lb), 1)
    same = (ii // chunk) == (jj // chunk)
    return jnp.where(jnp.logical_and(ii >= jj, same), 1.0, 0.0).astype(BF16)


def _shift_rows(x, prev8, s):
    rolled = pltpu.roll(x, s, axis=0)
    pr = pltpu.roll(prev8, s, axis=0)
    row = lax.broadcasted_iota(jnp.int32, (SUBLANES, x.shape[1]), 0)
    top = jnp.where(row < s, pr, rolled[:SUBLANES])
    return jnp.concatenate([top, rolled[SUBLANES:]], axis=0)


def _softplus(x):
    return jnp.maximum(x, 0.0) + jnp.log1p(jnp.exp(-jnp.abs(x)))


def _sigmoid(x):
    return jax.nn.sigmoid(x)


def _silu(x):
    return x * jax.nn.sigmoid(x)


def _lane_col(x, idx):
    lane = lax.broadcasted_iota(jnp.int32, x.shape, 1)
    return jnp.sum(jnp.where(lane == idx, x, 0.0), axis=-1, keepdims=True)


def _rms(x, w=None):
    y = x * lax.rsqrt(jnp.mean(x * x, axis=-1, keepdims=True) + NORM_EPS)
    return y if w is None else y * w


def _row_valid(lb_index, lb, lvalid, width):
    row = lax.broadcasted_iota(jnp.int32, (lb, width), 0) + lb_index * lb
    return row < lvalid


def _norm_matmul_kernel(x_ref, g_ref, w_ref, hw_ref, o_ref, xn_ref, *, norm_tiles):
    j = pl.program_id(1)

    @pl.when(j == 0)
    def _():
        xn_ref[...] = _rms(x_ref[...], g_ref[...]).astype(BF16)

    acc = jnp.dot(xn_ref[...], w_ref[...].astype(BF16), preferred_element_type=F32)
    if norm_tiles == 0:
        o_ref[...] = acc
    else:
        @pl.when(j < norm_tiles)
        def _():
            hw = hw_ref[...]
            for h in range(acc.shape[1] // HD):
                sl = slice(h * HD, (h + 1) * HD)
                o_ref[:, sl] = _rms(acc[:, sl], hw)

        @pl.when(j >= norm_tiles)
        def _():
            o_ref[...] = acc


def _row_tile(m, cap=1100):
    return next(t for t in range(cap - cap % 16, 0, -16) if m % t == 0)


def _norm_matmul(x, g, w, layer, *, tm, tn, head_w=None, norm_tiles=0):
    m, k = x.shape
    n = w.shape[2]
    if head_w is None:
        head_w = jnp.ones((HD,), F32)
    return pl.pallas_call(
        functools.partial(_norm_matmul_kernel, norm_tiles=norm_tiles),
        grid=(m // tm, n // tn),
        in_specs=[
            pl.BlockSpec((tm, k), lambda i, j: (i, 0)),
            pl.BlockSpec((1, k), lambda i, j: (0, 0)),
            pl.BlockSpec((None, k, tn), lambda i, j: (layer, 0, j)),
            pl.BlockSpec((1, HD), lambda i, j: (0, 0)),
        ],
        out_specs=pl.BlockSpec((tm, tn), lambda i, j: (i, j)),
        out_shape=jax.ShapeDtypeStruct((m, n), F32),
        scratch_shapes=[pltpu.VMEM((tm, k), BF16)],
        compiler_params=_cparams(("parallel", "arbitrary")),
        name="norm_matmul",
    )(x, g.reshape(1, k), w, head_w.reshape(1, HD))


def _matmul_res_kernel(a_ref, w_ref, r_ref, o_ref):
    o_ref[...] = r_ref[...] + jnp.dot(a_ref[...], w_ref[...].astype(BF16), preferred_element_type=F32)


def _matmul_res(a, w, layer, res, *, tm, tn):
    m, k = a.shape
    n = w.shape[2]
    return pl.pallas_call(
        _matmul_res_kernel,
        grid=(m // tm, n // tn),
        in_specs=[
            pl.BlockSpec((tm, k), lambda i, j: (i, 0)),
            pl.BlockSpec((None, k, tn), lambda i, j: (layer, 0, j)),
            pl.BlockSpec((tm, tn), lambda i, j: (i, j)),
        ],
        out_specs=pl.BlockSpec((tm, tn), lambda i, j: (i, j)),
        out_shape=jax.ShapeDtypeStruct((m, n), F32),
        compiler_params=_cparams(("parallel", "parallel")),
        name="matmul_res",
    )(a, w, res)


def _swiglu_up_kernel(x_ref, g_ref, wg_ref, wu_ref, o_ref, xn_ref):
    @pl.when(pl.program_id(1) == 0)
    def _():
        xn_ref[...] = _rms(x_ref[...], g_ref[...]).astype(BF16)

    xn = xn_ref[...]
    gate = jnp.dot(xn, wg_ref[...].astype(BF16), preferred_element_type=F32)
    up = jnp.dot(xn, wu_ref[...].astype(BF16), preferred_element_type=F32)
    o_ref[...] = (_silu(gate) * up).astype(BF16)


def _swiglu_up(x, g, wg, wu, layer, *, tm, tn):
    m, k = x.shape
    n = wg.shape[2]
    return pl.pallas_call(
        _swiglu_up_kernel,
        grid=(m // tm, n // tn),
        in_specs=[
            pl.BlockSpec((tm, k), lambda i, j: (i, 0)),
            pl.BlockSpec((1, k), lambda i, j: (0, 0)),
            pl.BlockSpec((None, k, tn), lambda i, j: (layer, 0, j)),
            pl.BlockSpec((None, k, tn), lambda i, j: (layer, 0, j)),
        ],
        out_specs=pl.BlockSpec((tm, tn), lambda i, j: (i, j)),
        out_shape=jax.ShapeDtypeStruct((m, n), BF16),
        scratch_shapes=[pltpu.VMEM((tm, k), BF16)],
        compiler_params=_cparams(("parallel", "arbitrary")),
        name="swiglu_up",
    )(x, g.reshape(1, k), wg, wu)


def _xattn_kernel(q_ref, k_ref, v_ref, o_ref):
    scale = HD ** -0.5
    for h in range(NH):
        sl = slice(h * HD, (h + 1) * HD)
        s = _dot_nt(q_ref[:, sl], k_ref[:, sl]) * scale
        m = jnp.max(s, axis=-1, keepdims=True)
        p = jnp.exp(s - m)
        l = jnp.sum(p, axis=-1, keepdims=True)
        o_ref[:, sl] = (_dot(p, v_ref[:, sl]) / l).astype(BF16)


def _xattn(q, mem_k, mem_v, *, row0, nrows, tq, rows_per_seq):
    tiles_per_seq = rows_per_seq // tq
    t0 = row0 // tq
    return pl.pallas_call(
        _xattn_kernel,
        grid=(nrows // tq,),
        in_specs=[
            pl.BlockSpec((tq, XW), lambda i: (t0 + i, 0)),
            pl.BlockSpec((None, N_MEM, XW), lambda i: (i // tiles_per_seq, 0, 0)),
            pl.BlockSpec((None, N_MEM, XW), lambda i: (i // tiles_per_seq, 0, 0)),
        ],
        out_specs=pl.BlockSpec((tq, XW), lambda i: (i, 0)),
        out_shape=jax.ShapeDtypeStruct((nrows, XW), BF16),
        compiler_params=_cparams(("parallel",)),
        name="xattn",
    )(q, mem_k, mem_v)


def _gdn_kernel(q_ref, k_ref, v_ref, z_ref, sm_ref, cwq_ref, cwk_ref, cwv_ref, cq_ref, ck_ref, cv_ref,
                s0_ref, alog_ref, dtb_ref, nw_ref, o_ref, so_ref, s_ref, prev_ref, gt_ref,
                *, lb, lvalid, nlb):
    h = pl.program_id(1)
    ib = pl.program_id(2)
    masked = lvalid < nlb * lb

    @pl.when(ib == 0)
    def _():
        s_ref[...] = s0_ref[...]
        prev_ref[0] = cq_ref[...]
        prev_ref[1] = ck_ref[...]
        prev_ref[2] = cv_ref[...]

    def conv(x_ref, w_ref, i):
        x = x_ref[...]
        w = w_ref[...]
        prev = prev_ref[i]
        y = x * w[3:4]
        for s in (1, 2, 3):
            y = y + _shift_rows(x, prev, s) * w[3 - s:4 - s]
        prev_ref[i] = x[lb - SUBLANES:]
        return _silu(y)

    q = conv(q_ref, cwq_ref, 0)
    k = conv(k_ref, cwk_ref, 1)
    v = conv(v_ref, cwv_ref, 2)
    q = q * lax.rsqrt(jnp.sum(q * q, axis=-1, keepdims=True) + NORM_EPS) * (HD ** -0.5)
    k = k * lax.rsqrt(jnp.sum(k * k, axis=-1, keepdims=True) + NORM_EPS)

    sm = sm_ref[...]
    g_blk = -jnp.exp(alog_ref[...]) * _softplus(sm + dtb_ref[...])
    beta_blk = _sigmoid(sm)
    if masked:
        valid = _row_valid(ib, lb, lvalid, LANES)
        g_blk = jnp.where(valid, g_blk, 0.0)
        beta_blk = jnp.where(valid, beta_blk, 0.0)
    gc_blk = _dot_exact_lhs(_chunk_tri(lb, CHUNK), g_blk)
    gt_ref[...] = gc_blk.T
    g_col_all = _lane_col(gc_blk, SM_A + h)
    beta_all = _lane_col(beta_blk, SM_B + h)
    g_row_all = gt_ref[pl.ds(SM_A + h, 1), :]

    group = min(2 * CHUNK, lb)
    ng = lb // group
    nchunk = lb // CHUNK
    ii = lax.broadcasted_iota(jnp.int32, (group, group), 0)
    jj = lax.broadcasted_iota(jnp.int32, (group, group), 1)
    lower = jnp.logical_and(ii >= jj, (ii // CHUNK) == (jj // CHUNK))
    to3 = lambda x: x.reshape(ng, group, x.shape[-1])
    q3, k3, v3 = to3(q), to3(k), to3(v)
    g_col3 = to3(g_col_all)
    beta3 = to3(beta_all)
    g_row3 = jnp.stack([g_row_all[:, i * group:(i + 1) * group] for i in range(ng)], axis=0)
    dec3 = jnp.exp(jnp.where(lower, g_col3 - g_row3, -jnp.inf))
    n3 = jnp.where(ii > jj, beta3 * _bmm_nt(k3, k3) * dec3, 0.0)
    ainv3 = _unit_lower_inv(n3, CHUNK)
    eg3 = jnp.exp(g_col3)
    uv3 = _bmm(ainv3, beta3 * v3)
    w3 = _bmm(ainv3, beta3 * eg3 * k3)
    qk3 = _bmm_nt(q3, k3) * dec3
    qp = (q3 * eg3 - _bmm(qk3, w3)).reshape(lb, HD)
    op = _bmm(qk3, uv3).reshape(lb, HD)
    w_all = w3.reshape(lb, HD)
    uv_all = uv3.reshape(lb, HD)
    g_chunks = g_col_all.reshape(nchunk, CHUNK, 1)
    g_end = jnp.broadcast_to(g_chunks[:, CHUNK - 1:, :], g_chunks.shape).reshape(lb, 1)
    kd = k * jnp.exp(g_end - g_col_all)
    s_decay = jnp.exp(g_end)
    nw = nw_ref[...]
    for c in range(nchunk):
        r = slice(c * CHUNK, (c + 1) * CHUNK)
        an = _dot_tn(kd[r], jnp.concatenate([w_all[r], uv_all[r]], axis=1))
        s = s_ref[...]
        o = _dot(qp[r], s) + op[r]
        s_ref[...] = s * s_decay[c * CHUNK:c * CHUNK + 1] - _dot(an[:, :HD], s) + an[:, HD:]
        o_ref[r, :] = (_rms(o, nw) * _silu(z_ref[r, :])).astype(BF16)

    @pl.when(ib == nlb - 1)
    def _():
        so_ref[...] = s_ref[...]


def _gdn(p, conv_w, conv_init, s0, a_log, dt_bias, norm_w, *, row0, nseq, lpad, lvalid, lb):
    nlb = lpad // lb
    rb = lambda s, i: (row0 + s * lpad) // lb + i
    pblk = lambda cb: pl.BlockSpec((lb, HD), lambda s, h, i: (rb(s, i), cb + h))
    cwblk = lambda j: pl.BlockSpec((CONV_WIDTH, HD), lambda s, h, i: (0, j * NH + h))
    ciblk = lambda j: pl.BlockSpec((None, SUBLANES, HD), lambda s, h, i: (s, 0, j * NH + h))
    vec = pl.BlockSpec((1, LANES), lambda s, h, i: (0, 0))
    lane_pad = lambda x: jnp.zeros((1, LANES), F32).at[0, :x.shape[0]].set(x)
    return pl.pallas_call(
        functools.partial(_gdn_kernel, lb=lb, lvalid=lvalid, nlb=nlb),
        grid=(nseq, NH, nlb),
        in_specs=[
            pblk(CB_GDN), pblk(CB_GDN + NH), pblk(CB_GDN + 2 * NH), pblk(CB_GDN_Z),
            pl.BlockSpec((lb, LANES), lambda s, h, i: (rb(s, i), CB_SMALL)),
            cwblk(0), cwblk(1), cwblk(2), ciblk(0), ciblk(1), ciblk(2),
            pl.BlockSpec((None, None, HD, HD), lambda s, h, i: (s, h, 0, 0)),
            vec, vec, vec,
        ],
        out_specs=[
            pl.BlockSpec((lb, HD), lambda s, h, i: (s * nlb + i, h)),
            pl.BlockSpec((None, None, HD, HD), lambda s, h, i: (s, h, 0, 0)),
        ],
        out_shape=[
            jax.ShapeDtypeStruct((nseq * lpad, GW), BF16),
            jax.ShapeDtypeStruct((nseq, NH, HD, HD), F32),
        ],
        scratch_shapes=[
            pltpu.VMEM((HD, HD), F32),
            pltpu.VMEM((3, SUBLANES, HD), F32),
            pltpu.VMEM((LANES, lb), F32),
        ],
        compiler_params=_cparams(("parallel", "parallel", "arbitrary")),
        name="gdn",
    )(p, p, p, p, p, conv_w, conv_w, conv_w, conv_init, conv_init, conv_init, s0,
      lane_pad(a_log), lane_pad(dt_bias), norm_w.reshape(1, HD))


def _ret_kernel(q_ref, k_ref, v_ref, g_ref, cos_ref, sin_ref, lg_ref, s0_ref, o_ref, so_ref, s_ref,
                *, lb, cv, nlb):
    ib = pl.program_id(2)

    @pl.when(ib == 0)
    def _():
        s_ref[...] = s0_ref[...]

    cos = cos_ref[...]
    sin = sin_ref[...]
    rot = lambda x: x * cos + pltpu.roll(x, HD // 2, axis=1) * sin
    q = rot(q_ref[...])
    k = rot(k_ref[...]) * (HD ** -0.5)
    v = v_ref[...]
    lg = lg_ref[...][:, 0:1]

    ii = lax.broadcasted_iota(jnp.int32, (CHUNK, CHUNK), 0)
    jj = lax.broadcasted_iota(jnp.int32, (CHUNK, CHUNK), 1)
    rel = (ii - jj).astype(F32)
    dmat = jnp.where(rel >= 0, jnp.exp(jnp.maximum(rel, 0.0) * lg), 0.0)
    idx = lax.broadcasted_iota(jnp.int32, (CHUNK, 1), 0)
    idf = idx.astype(F32)
    xi = jnp.exp((idf + 1.0) * lg)
    zeta = jnp.where(idx < cv, jnp.exp((cv - 1.0 - idf) * lg), 0.0)
    gc = jnp.exp(cv * lg)
    for c in range(lb // CHUNK):
        r = slice(c * CHUNK, (c + 1) * CHUNK)
        qc, kc, vc = q[r], k[r], v[r]
        s = s_ref[...]
        qk = _dot_nt(qc, kc) * dmat
        o = _dot(qk, vc) + _dot(qc * xi, s)
        s_ref[...] = s * gc + _dot_tn(kc * zeta, vc)
        o_ref[r, :] = (_rms(o) * _silu(g_ref[r, :])).astype(BF16)

    @pl.when(ib == nlb - 1)
    def _():
        so_ref[...] = s_ref[...]


def _ret(p, cos_t, sin_t, log_gamma, s0, *, row0, nseq, lpad, lvalid, lb):
    nlb = lpad // lb
    cv = CHUNK if lvalid == lpad else lvalid
    assert cv == CHUNK or (lpad == CHUNK and 0 < lvalid < CHUNK)
    rb = lambda s, i: (row0 + s * lpad) // lb + i
    pblk = lambda cb: pl.BlockSpec((lb, HD), lambda s, h, i: (rb(s, i), cb + h))
    tblk = pl.BlockSpec((lb, HD), lambda s, h, i: (i, 0))
    sblk = pl.BlockSpec((None, None, HD, HD), lambda s, h, i: (s, h, 0, 0))
    lg = jnp.broadcast_to(log_gamma[:, None, None], (NH, 1, LANES))
    return pl.pallas_call(
        functools.partial(_ret_kernel, lb=lb, cv=cv, nlb=nlb),
        grid=(nseq, NH, nlb),
        in_specs=[
            pblk(CB_RET), pblk(CB_RET + NH), pblk(CB_RET + 2 * NH), pblk(CB_RET + 3 * NH),
            tblk, tblk,
            pl.BlockSpec((None, 1, LANES), lambda s, h, i: (h, 0, 0)),
            sblk,
        ],
        out_specs=[pl.BlockSpec((lb, HD), lambda s, h, i: (s * nlb + i, h)), sblk],
        out_shape=[
            jax.ShapeDtypeStruct((nseq * lpad, GW), BF16),
            jax.ShapeDtypeStruct((nseq, NH, HD, HD), F32),
        ],
        scratch_shapes=[pltpu.VMEM((HD, HD), F32)],
        compiler_params=_cparams(("parallel", "parallel", "arbitrary")),
        name="retention",
    )(p, p, p, p, cos_t, sin_t, lg, s0)


def _half_sum(x):
    lane = lax.broadcasted_iota(jnp.int32, x.shape, 1)
    lo = lane < RWKV_HEAD
    s_lo = jnp.sum(jnp.where(lo, x, 0.0), axis=-1, keepdims=True)
    s_hi = jnp.sum(jnp.where(lo, 0.0, x), axis=-1, keepdims=True)
    return jnp.where(lo, s_lo, s_hi)


def _rwkv_kernel(r_ref, k_ref, v_ref, wa_ref, gd_ref, mur_ref, muk_ref, muv_ref, muwa_ref, mugd_ref,
                 sh_r_ref, sh_k_ref, sh_v_ref, sh_wa_ref, sh_gd_ref,
                 wup_ref, aup_ref, gup_ref, w0_ref, a0_ref, kk_ref, ka_ref, rk_ref, lnw_ref, lnb_ref,
                 s0_ref, o_ref, so_ref, s_ref, prev_ref, *, lb, lvalid, nlb):
    ib = pl.program_id(2)
    masked = lvalid < nlb * lb

    @pl.when(ib == 0)
    def _():
        s_ref[...] = s0_ref[...]
        prev_ref[0] = sh_r_ref[...]
        prev_ref[1] = sh_k_ref[...]
        prev_ref[2] = sh_v_ref[...]
        prev_ref[3] = sh_wa_ref[...]
        prev_ref[4] = sh_gd_ref[...]

    def shifted(x_ref, mu_ref, i):
        x = x_ref[...]
        prev = _shift_rows(x, prev_ref[i], 1)
        prev_ref[i] = x[lb - SUBLANES:]
        return x + (prev - x) * mu_ref[...]

    r = shifted(r_ref, mur_ref, 0)
    k = shifted(k_ref, muk_ref, 1)
    v = shifted(v_ref, muv_ref, 2)
    wa = shifted(wa_ref, muwa_ref, 3)
    gd = shifted(gd_ref, mugd_ref, 4)

    w_raw = -_softplus(-(w0_ref[...] + _dot(jnp.tanh(wa), wup_ref[...]))) - 0.5
    logw = -jnp.exp(w_raw)
    a_sig = _sigmoid(a0_ref[...] + _dot(wa, aup_ref[...]))
    gate = _dot(_sigmoid(gd), gup_ref[...])
    kk = k * kk_ref[...]
    kk = kk * lax.rsqrt(_half_sum(kk * kk) + NORM_EPS)
    kp = k * (1.0 + (a_sig - 1.0) * ka_ref[...])
    rec_a = -kk
    rec_b = kk * a_sig
    if masked:
        valid = _row_valid(ib, lb, lvalid, LANES)
        zero = lambda x: jnp.where(valid, x, 0.0)
        logw, rec_a, rec_b, kp, v = zero(logw), zero(rec_a), zero(rec_b), zero(kp), zero(v)

    cum = _dot_exact_lhs(_chunk_tri(lb, CHUNK), logw)
    e_pos = jnp.exp(cum)
    e_neg = jnp.exp(-cum)
    at_all = rec_a * jnp.exp(cum - logw)
    bt_all = rec_b * e_neg
    kt_all = kp * e_neg
    rt_all = r * e_pos

    stacked = 2 * CHUNK
    row_head = lax.broadcasted_iota(jnp.int32, (stacked, LANES), 0) // CHUNK
    lane_head = lax.broadcasted_iota(jnp.int32, (stacked, LANES), 1) // RWKV_HEAD
    own = row_head == lane_head
    nchunk = lb // CHUNK
    dup3 = lambda x: jnp.concatenate([x.reshape(nchunk, CHUNK, LANES)] * 2, axis=1)
    stack3 = lambda x: jnp.where(own, dup3(x), 0.0)
    block_diag = (lax.broadcasted_iota(jnp.int32, (LANES, LANES), 0) // RWKV_HEAD) == (
        lax.broadcasted_iota(jnp.int32, (LANES, LANES), 1) // RWKV_HEAD)
    ti = lax.broadcasted_iota(jnp.int32, (stacked, stacked), 0) % CHUNK
    tj = lax.broadcasted_iota(jnp.int32, (stacked, stacked), 1) % CHUNK

    at3, bt3, kt3, rt3, v3 = stack3(at_all), stack3(bt_all), stack3(kt_all), stack3(rt_all), dup3(v)
    l_ab = jnp.where(ti > tj, _bmm_nt(at3, bt3), 0.0)
    l_ak = jnp.where(ti > tj, _bmm_nt(at3, kt3), 0.0)
    l_rb = jnp.where(ti >= tj, _bmm_nt(rt3, bt3), 0.0)
    l_rk = jnp.where(ti >= tj, _bmm_nt(rt3, kt3), 0.0)
    inv = _unit_lower_inv(-l_ab, CHUNK)
    t1 = _bmm(inv, at3)
    t2 = jnp.where(own, _bmm(inv, _bmm(l_ak, v3)), 0.0)
    rp = rt3 + _bmm(l_rb, t1)
    op = jnp.where(own, _bmm(l_rb, t2) + _bmm(l_rk, v3), 0.0)
    cum3 = cum.reshape(nchunk, CHUNK, LANES)
    cum_end = jnp.broadcast_to(cum3[:, CHUNK - 1:, :], cum3.shape).reshape(lb, LANES)
    to_end = jnp.exp(cum_end - cum)
    b_end = rec_b * to_end
    k_end = kp * to_end
    s_decay = jnp.exp(cum_end)

    for c in range(nchunk):
        rs = slice(c * CHUNK, (c + 1) * CHUNK)
        vc = v[rs]
        b2 = jnp.concatenate([b_end[rs], b_end[rs]], axis=0)
        gh = _dot_tn(jnp.concatenate([t1[c], t2[c]], axis=1), b2)
        h = gh[LANES:] + _dot_tn(vc, k_end[rs])
        s = s_ref[...]
        o2 = jnp.where(own, _dot_nt(rp[c], s), 0.0) + op[c]
        o = o2[:CHUNK] + o2[CHUNK:]
        s_ref[...] = s * s_decay[c * CHUNK:c * CHUNK + 1] + jnp.where(block_diag, _dot(s, gh[:LANES]) + h, 0.0)

        inv_n = 1.0 / RWKV_HEAD
        mu = _half_sum(o) * inv_n
        var = _half_sum(jnp.square(o - mu)) * inv_n
        on = (o - mu) * lax.rsqrt(var + GN_EPS) * lnw_ref[...] + lnb_ref[...]
        bonus = _half_sum(r[rs] * kp[rs] * rk_ref[...]) * vc
        o_ref[rs, :] = ((on + bonus) * gate[rs]).astype(BF16)

    @pl.when(ib == nlb - 1)
    def _():
        so_ref[...] = s_ref[...]


def _rwkv(p, lw, shift_init, s0, *, row0, nseq, lpad, lvalid, lb):
    nlb = lpad // lb
    npair = GW // LANES
    rb = lambda s, i: (row0 + s * lpad) // lb + i
    pblk = lambda cb, per_pair: pl.BlockSpec(
        (lb, LANES), lambda s, j, i: (rb(s, i), cb + (j if per_pair else 0)))
    mublk = lambda cb, per_pair: pl.BlockSpec((1, LANES), lambda s, j, i: (0, cb + (j if per_pair else 0)))
    shblk = lambda cb, per_pair: pl.BlockSpec(
        (None, SUBLANES, LANES), lambda s, j, i: (s, 0, cb + (j if per_pair else 0)))
    pair_vec = pl.BlockSpec((1, LANES), lambda s, j, i: (0, j))
    pair_mat = pl.BlockSpec((LANES, LANES), lambda s, j, i: (0, j))
    sblk = pl.BlockSpec((None, None, LANES, LANES), lambda s, j, i: (s, j, 0, 0))
    mu = lw['rwkv_mu'].reshape(1, -1)
    zeros64 = jnp.zeros((64, GW), F32)
    wup = jnp.concatenate([lw['rwkv_w_up'], zeros64], axis=0).astype(BF16)
    aup = jnp.concatenate([zeros64, lw['rwkv_a_up']], axis=0).astype(BF16)
    row = lambda x: x.reshape(1, GW)
    blocks = [(0, True), (4, True), (8, True), (12, False), (13, False)]
    return pl.pallas_call(
        functools.partial(_rwkv_kernel, lb=lb, lvalid=lvalid, nlb=nlb),
        grid=(nseq, npair, nlb),
        in_specs=(
            [pblk(CB_RWKV + cb, pp) for cb, pp in blocks]
            + [mublk(cb, pp) for cb, pp in blocks]
            + [shblk(cb, pp) for cb, pp in blocks]
            + [pair_mat, pair_mat, pair_mat] + [pair_vec] * 7 + [sblk]
        ),
        out_specs=[pl.BlockSpec((lb, LANES), lambda s, j, i: (s * nlb + i, j)), sblk],
        out_shape=[
            jax.ShapeDtypeStruct((nseq * lpad, GW), BF16),
            jax.ShapeDtypeStruct((nseq, npair, LANES, LANES), F32),
        ],
        scratch_shapes=[pltpu.VMEM((LANES, LANES), F32), pltpu.VMEM((5, SUBLANES, LANES), F32)],
        compiler_params=_cparams(("parallel", "parallel", "arbitrary")),
        name="rwkv7",
    )(p, p, p, p, p, mu, mu, mu, mu, mu, shift_init, shift_init, shift_init, shift_init, shift_init,
      wup, aup, lw['rwkv_g_up'].astype(BF16), row(lw['rwkv_w0']), row(lw['rwkv_a0']), row(lw['rwkv_k_k']),
      row(lw['rwkv_k_a']), row(lw['rwkv_r_k']), row(lw['rwkv_ln_w']), row(lw['rwkv_ln_b']), s0)


def _fox_prep_kernel(q_ref, k_ref, v_ref, sm_ref, qw_ref, kw_ref, bf_ref,
                     qn_ref, kn_ref, kb_ref, vb_ref, lf_ref, c_ref, ct_ref, carry_ref, *, lb):
    @pl.when(pl.program_id(1) == 0)
    def _():
        carry_ref[...] = jnp.zeros_like(carry_ref)

    qw = qw_ref[...]
    kw = kw_ref[...]
    for h in range(NH):
        sl = slice(h * HD, (h + 1) * HD)
        qn_ref[:, sl] = _rms(q_ref[:, sl], qw).astype(BF16)
        kn = _rms(k_ref[:, sl], kw)
        kn_ref[:, sl] = kn
        kb_ref[:, sl] = kn.astype(BF16)
    vb_ref[...] = v_ref[...].astype(BF16)
    logf = -_softplus(-(sm_ref[...] + bf_ref[...]))
    lf_ref[...] = logf
    c = _dot_exact_lhs(_chunk_tri(lb, lb), logf) + carry_ref[0:1, :]
    c_ref[...] = c
    carry_ref[...] = jnp.broadcast_to(c[lb - 1:], carry_ref.shape)
    ct_ref[...] = c.T[SM_F:SM_F + SUBLANES]


def _fox_prep(p, q_w, k_w, b_f, *, nseq, lpad, lb):
    nlb = lpad // lb
    rb = lambda s, i: s * nlb + i
    seg = lambda j: pl.BlockSpec((lb, GW), lambda s, i: (rb(s, i), j))
    vec = pl.BlockSpec((1, LANES), lambda s, i: (0, 0))
    rows = nseq * lpad
    bf_lane = jnp.zeros((1, LANES), F32).at[0, SM_F:SM_F + NH].set(b_f)
    return pl.pallas_call(
        functools.partial(_fox_prep_kernel, lb=lb),
        grid=(nseq, nlb),
        in_specs=[seg(0), seg(1), seg(2),
                  pl.BlockSpec((lb, LANES), lambda s, i: (rb(s, i), CB_SMALL)), vec, vec, vec],
        out_specs=[
            pl.BlockSpec((lb, GW), lambda s, i: (rb(s, i), 0)),
            pl.BlockSpec((lb, GW), lambda s, i: (rb(s, i), 0)),
            pl.BlockSpec((lb, GW), lambda s, i: (rb(s, i), 0)),
            pl.BlockSpec((lb, GW), lambda s, i: (rb(s, i), 0)),
            pl.BlockSpec((lb, LANES), lambda s, i: (rb(s, i), 0)),
            pl.BlockSpec((lb, LANES), lambda s, i: (rb(s, i), 0)),
            pl.BlockSpec((SUBLANES, lb), lambda s, i: (0, rb(s, i))),
        ],
        out_shape=[
            jax.ShapeDtypeStruct((rows, GW), BF16),
            jax.ShapeDtypeStruct((rows, GW), F32),
            jax.ShapeDtypeStruct((rows, GW), BF16),
            jax.ShapeDtypeStruct((rows, GW), BF16),
            jax.ShapeDtypeStruct((rows, LANES), F32),
            jax.ShapeDtypeStruct((rows, LANES), F32),
            jax.ShapeDtypeStruct((SUBLANES, rows), F32),
        ],
        scratch_shapes=[pltpu.VMEM((SUBLANES, LANES), F32)],
        compiler_params=_cparams(("parallel", "arbitrary")),
        name="fox_prep",
    )(p, p, p, p, q_w.reshape(1, HD), k_w.reshape(1, HD), bf_lane)


def _fox_flash_kernel(q_ref, k_ref, v_ref, c_ref, ct_ref, g_ref, o_ref, m_ref, l_ref, acc_ref, *, tq):
    h = pl.program_id(1)
    qi = pl.program_id(2)
    log2e = 1.0 / math.log(2.0)
    scale = HD ** -0.5 * log2e
    q = q_ref[...]
    c_col = _lane_col(c_ref[...], SM_F + h) * log2e
    m_ref[...] = jnp.full_like(m_ref, -1e30)
    l_ref[...] = jnp.zeros_like(l_ref)
    acc_ref[...] = jnp.zeros_like(acc_ref)

    def block(ks, width, diagonal):
        kb = k_ref[pl.ds(ks, width), :]
        vb = v_ref[pl.ds(ks, width), :]
        t = _dot_nt(q, kb) * scale - ct_ref[pl.ds(h, 1), pl.ds(ks, width)] * log2e
        if diagonal:
            ii = lax.broadcasted_iota(jnp.int32, (tq, width), 0)
            jj = lax.broadcasted_iota(jnp.int32, (tq, width), 1)
            t = jnp.where(jj <= ii, t, -jnp.inf)
        m_old = m_ref[...]
        m_new = jnp.maximum(m_old, jnp.max(t, axis=-1, keepdims=True) + c_col)
        alpha = jnp.exp2(m_old - m_new)
        pr = jnp.exp2(t - (m_new - c_col))
        l_ref[...] = alpha * l_ref[...] + jnp.sum(pr, axis=-1, keepdims=True)
        acc_ref[...] = alpha * acc_ref[...] + _dot(pr, vb)
        m_ref[...] = m_new

    def body(kp, carry):
        block(pl.multiple_of(kp * 2 * tq, 2 * tq), 2 * tq, False)
        return carry

    lax.fori_loop(0, qi // 2, body, 0)

    @pl.when(qi % 2 == 1)
    def _():
        block(pl.multiple_of((qi - 1) * tq, tq), tq, False)

    block(pl.multiple_of(qi * tq, tq), tq, True)
    o_ref[...] = (acc_ref[...] / l_ref[...] * _sigmoid(g_ref[...])).astype(BF16)


def _fox_flash(qn, kb, vb, c_col, c_row, p, *, nseq, lpad, tq):
    nq = lpad // tq
    return pl.pallas_call(
        functools.partial(_fox_flash_kernel, tq=tq),
        grid=(nseq, NH, nq),
        in_specs=[
            pl.BlockSpec((tq, HD), lambda s, h, i: (s * nq + i, h)),
            pl.BlockSpec((lpad, HD), lambda s, h, i: (s, h)),
            pl.BlockSpec((lpad, HD), lambda s, h, i: (s, h)),
            pl.BlockSpec((tq, LANES), lambda s, h, i: (s * nq + i, 0)),
            pl.BlockSpec((SUBLANES, lpad), lambda s, h, i: (0, s)),
            pl.BlockSpec((tq, HD), lambda s, h, i: (s * nq + i, CB_FOX + 3 * NH + h)),
        ],
        out_specs=pl.BlockSpec((tq, HD), lambda s, h, i: (s * nq + i, h)),
        out_shape=jax.ShapeDtypeStruct((nseq * lpad, GW), BF16),
        scratch_shapes=[pltpu.VMEM((tq, 1), F32), pltpu.VMEM((tq, 1), F32), pltpu.VMEM((tq, HD), F32)],
        compiler_params=_cparams(("parallel", "parallel", "arbitrary")),
        name="fox_flash",
    )(qn, kb, vb, c_col, c_row, p)


PAGE_GROUP = 8
PAGE_COLS = PAGE * NH
NQ_PAD = SUBLANES


def _fox_sample_kernel(pt_ref, pq_ref, sm_ref, *rest, lvalid, lpad, n_steps):
    del pt_ref
    g = PAGE_GROUP
    kps, vps, lfs = rest[:g], rest[g:2 * g], rest[2 * g:3 * g]
    (upper_ref, heads_ref, qw_ref, kw_ref, bf_ref, o_ref, kn_ref, lf_ref,
     qn_s, cq_s, m_s, l_s, acc_s, carry_s) = rest[3 * g:]
    i = pl.program_id(1)
    scale = HD ** -0.5
    nq = NQ_PAD
    nrow = NH * nq

    @pl.when(i == 0)
    def _():
        logf = -_softplus(-(sm_ref[...] + bf_ref[...]))
        logf = jnp.where(_row_valid(0, lpad, lvalid, LANES), logf, 0.0)
        lf_ref[...] = logf[:nq]
        c = _dot_exact_lhs(_chunk_tri(lpad, lpad), logf)
        c_t = c.T
        carry_s[...] = jnp.zeros_like(carry_s)
        qi = lax.broadcasted_iota(jnp.int32, (nq, lpad), 0)
        kj = lax.broadcasted_iota(jnp.int32, (nq, lpad), 1)
        for h in range(NH):
            sl = slice(h * HD, (h + 1) * HD)
            rows = slice(h * nq, (h + 1) * nq)
            qn = _rms(pq_ref[:nq, sl], qw_ref[...])
            kn = _rms(pq_ref[:, GW + h * HD:GW + (h + 1) * HD], kw_ref[...])
            vn = pq_ref[:, 2 * GW + h * HD:2 * GW + (h + 1) * HD]
            c_h = c[:nq, SM_F + h:SM_F + h + 1]
            qn_s[rows, :] = qn
            cq_s[rows, :] = c_h
            kn_ref[:, sl] = kn[:nq]
            s = _dot_nt(qn, kn) * scale + c_h - c_t[SM_F + h:SM_F + h + 1, :]
            s = jnp.where(kj <= qi, s, -jnp.inf)
            m = jnp.max(s, axis=-1, keepdims=True)
            pr = jnp.exp(s - m)
            m_s[rows, :] = m
            l_s[rows, :] = jnp.sum(pr, axis=-1, keepdims=True)
            acc_s[rows, :] = _dot(pr, vn)

    lf = jnp.concatenate([r[...] for r in lfs], axis=0)
    within = _dot_exact_rhs(lf, upper_ref[...])
    totals = _dot_exact_rhs(lf, heads_ref[...])
    run = carry_s[...]
    suffix = [None] * g
    for j in reversed(range(g)):
        suffix[j] = within[j:j + 1] + run
        run = run + totals[j:j + 1]
    carry_s[...] = run

    row_head = lax.broadcasted_iota(jnp.int32, (nrow, PAGE_COLS), 0) // nq
    col_head = lax.broadcasted_iota(jnp.int32, (nrow, PAGE_COLS), 1) % NH
    own = row_head == col_head
    qs = qn_s[...].astype(BF16)
    bias = cq_s[...]
    tiles = [jnp.where(own, _dot_nt(qs, kps[j][...]) * scale + bias + suffix[j], -jnp.inf) for j in range(g)]
    m_old = m_s[...]
    m_new = m_old
    for t in tiles:
        m_new = jnp.maximum(m_new, jnp.max(t, axis=-1, keepdims=True))
    alpha = jnp.exp(m_old - m_new)
    l_new = alpha * l_s[...]
    acc = alpha * acc_s[...]
    for j, t in enumerate(tiles):
        pr = jnp.exp(t - m_new)
        l_new = l_new + jnp.sum(pr, axis=-1, keepdims=True)
        acc = acc + _dot(pr, vps[j][...])
    m_s[...] = m_new
    l_s[...] = l_new
    acc_s[...] = acc

    @pl.when(i == n_steps - 1)
    def _():
        o_ref[...] = jnp.zeros_like(o_ref)
        out = acc_s[...] / l_s[...]
        for h in range(NH):
            sl = slice(h * HD, (h + 1) * HD)
            gate = pq_ref[:nq, 3 * GW + h * HD:3 * GW + (h + 1) * HD]
            o_ref[:nq, sl] = (out[h * nq:(h + 1) * nq] * _sigmoid(gate)).astype(BF16)


def _fox_sample(p, page_table, cache_k, cache_v, cache_lf, q_w, k_w, b_f, *, layer, row0, nseq, lpad, lvalid):
    n_pages = page_table.shape[1]
    g = PAGE_GROUP
    assert n_pages % g == 0 and lvalid <= NQ_PAD
    n_steps = n_pages // g
    bf_lane = jnp.zeros((1, LANES), F32).at[0, SM_F:SM_F + NH].set(b_f)
    idx = np.arange(PAGE_COLS)
    same_head = (idx[:, None] % NH) == (idx[None, :] % NH)
    upper = jnp.asarray(same_head & (idx[:, None] // NH > idx[None, :] // NH), BF16)
    heads = jnp.asarray(same_head, BF16)
    rb = lambda b: row0 // lpad + b
    vec = pl.BlockSpec((1, LANES), lambda b, i, pt: (0, 0))
    const = pl.BlockSpec((PAGE_COLS, PAGE_COLS), lambda b, i, pt: (0, 0))

    def page_spec(shape, j):
        zeros = (0,) * len(shape)
        return pl.BlockSpec((None, None) + shape,
                            lambda b, i, pt: (layer, pt[b, n_pages - g * (i + 1) + j]) + zeros)

    grid_spec = pltpu.PrefetchScalarGridSpec(
        num_scalar_prefetch=1,
        grid=(nseq, n_steps),
        in_specs=(
            [pl.BlockSpec((lpad, 4 * GW), lambda b, i, pt: (rb(b), CB_FOX)),
             pl.BlockSpec((lpad, LANES), lambda b, i, pt: (rb(b), CB_SMALL))]
            + [page_spec((PAGE_COLS, HD), j) for j in range(g)]
            + [page_spec((PAGE_COLS, HD), j) for j in range(g)]
            + [page_spec((1, PAGE_COLS), j) for j in range(g)]
            + [const, const, vec, vec, vec]
        ),
        out_specs=[
            pl.BlockSpec((lpad, GW), lambda b, i, pt: (b, 0)),
            pl.BlockSpec((None, NQ_PAD, GW), lambda b, i, pt: (b, 0, 0)),
            pl.BlockSpec((None, NQ_PAD, LANES), lambda b, i, pt: (b, 0, 0)),
        ],
        scratch_shapes=[
            pltpu.VMEM((NH * NQ_PAD, HD), F32),
            pltpu.VMEM((NH * NQ_PAD, 1), F32),
            pltpu.VMEM((NH * NQ_PAD, 1), F32),
            pltpu.VMEM((NH * NQ_PAD, 1), F32),
            pltpu.VMEM((NH * NQ_PAD, HD), F32),
            pltpu.VMEM((1, PAGE_COLS), F32),
        ],
    )
    return pl.pallas_call(
        functools.partial(_fox_sample_kernel, lvalid=lvalid, lpad=lpad, n_steps=n_steps),
        grid_spec=grid_spec,
        out_shape=[
            jax.ShapeDtypeStruct((nseq * lpad, GW), BF16),
            jax.ShapeDtypeStruct((nseq, NQ_PAD, GW), F32),
            jax.ShapeDtypeStruct((nseq, NQ_PAD, LANES), F32),
        ],
        compiler_params=_cparams(("parallel", "arbitrary")),
        name="fox_sample",
    )(page_table, p, p, *([cache_k] * g), *([cache_v] * g), *([cache_lf] * g), upper, heads,
      q_w.reshape(1, HD), k_w.reshape(1, HD), bf_lane)


W_IN_SHIFT = 8
_COPY, _SHIFT, _SMALL, _ZERO = 0, 1, 2, 3


def _w_in_plan():
    kind = np.zeros(NP_COLS // LANES, np.int32)
    src = np.zeros(NP_COLS // LANES, np.int32)

    def put(cb, n, first_src_block, k):
        kind[cb:cb + n] = k
        src[cb:cb + n] = first_src_block + np.arange(n)

    put(CB_FOX, 16, (5896 - W_IN_SHIFT) // LANES, _SHIFT)
    put(CB_RET, 16, (2056 - W_IN_SHIFT) // LANES, _SHIFT)
    put(CB_GDN, 12, 0, _COPY)
    put(CB_GDN_Z, 4, (1544 - W_IN_SHIFT) // LANES, _SHIFT)
    put(CB_RWKV, 14, (4104 - W_IN_SHIFT) // LANES, _SHIFT)
    put(CB_SMALL, 1, 1536 // LANES, _SMALL)
    put(CB_SMALL + 1, 1, 0, _ZERO)
    src_b = np.where(kind == _SHIFT, src + 1, np.where(kind == _SMALL, (7944 - W_IN_SHIFT) // LANES, src))
    return jnp.asarray(kind), jnp.asarray(src), jnp.asarray(src_b.astype(np.int32))


def _prep_w_in_kernel(kind_ref, sa_ref, sb_ref, a_ref, b_ref, o_ref):
    del sa_ref, sb_ref
    kind = kind_ref[pl.program_id(0)]
    lane = lax.broadcasted_iota(jnp.int32, a_ref.shape, 1)

    @pl.when(kind == _COPY)
    def _():
        o_ref[...] = a_ref[...].astype(BF16)

    @pl.when(kind == _SHIFT)
    def _():
        ra = pltpu.roll(a_ref[...], LANES - W_IN_SHIFT, axis=1)
        rb = pltpu.roll(b_ref[...], LANES - W_IN_SHIFT, axis=1)
        o_ref[...] = jnp.where(lane < LANES - W_IN_SHIFT, ra, rb).astype(BF16)

    @pl.when(kind == _SMALL)
    def _():
        small = jnp.where(lane < SM_F, a_ref[...], jnp.where(lane < SM_F + NH, b_ref[...], 0.0))
        o_ref[...] = small.astype(BF16)

    @pl.when(kind == _ZERO)
    def _():
        o_ref[...] = jnp.zeros_like(o_ref)


def _prep_w_in(w_in, layer):
    k = w_in.shape[1]
    kind, src_a, src_b = _w_in_plan()
    grid_spec = pltpu.PrefetchScalarGridSpec(
        num_scalar_prefetch=3,
        grid=(NP_COLS // LANES,),
        in_specs=[
            pl.BlockSpec((None, k, LANES), lambda j, kd, sa, sb: (layer, 0, sa[j])),
            pl.BlockSpec((None, k, LANES), lambda j, kd, sa, sb: (layer, 0, sb[j])),
        ],
        out_specs=pl.BlockSpec((k, LANES), lambda j, kd, sa, sb: (0, j)),
    )
    return pl.pallas_call(
        _prep_w_in_kernel,
        grid_spec=grid_spec,
        out_shape=jax.ShapeDtypeStruct((k, NP_COLS), BF16),
        compiler_params=_cparams(("parallel",)),
        name="prep_w_in",
    )(kind, src_a, src_b, w_in, w_in)


def _rope_tables(pos):
    half = HD // 2
    inv = 1.0 / (ROPE_BASE ** jnp.linspace(0.0, 1.0, half, dtype=F32))
    ang = pos.astype(F32)[:, None] * inv[None, :]
    cos, sin = jnp.cos(ang), jnp.sin(ang)
    return jnp.concatenate([cos, cos], axis=-1), jnp.concatenate([-sin, sin], axis=-1)


def _state_tile(state, nrows):
    b, _, c = state.shape
    return jnp.concatenate([jnp.zeros((b, SUBLANES - nrows, c), F32), state], axis=1)


def _rwkv_pair_states(s):
    b = s.shape[0]
    s = s.reshape(b, 4, 2, RWKV_HEAD, RWKV_HEAD)
    z = jnp.zeros_like(s[:, :, 0])
    top = jnp.concatenate([s[:, :, 0], z], axis=-1)
    bot = jnp.concatenate([z, s[:, :, 1]], axis=-1)
    return jnp.concatenate([top, bot], axis=-2)


def _rwkv_unpair_states(sp):
    b = sp.shape[0]
    a = sp[:, :, :RWKV_HEAD, :RWKV_HEAD]
    c = sp[:, :, RWKV_HEAD:, RWKV_HEAD:]
    return jnp.stack([a, c], axis=2).reshape(b, 8, RWKV_HEAD, RWKV_HEAD)


def kernel(x_prompt, x_sample, cache_fox_k, cache_fox_v, cache_fox_logf, cache_mem_k, cache_mem_v, state_gdn_conv, state_gdn_S, state_ret_S, state_rwkv_shift, state_rwkv_S, page_table, mem_prompt, norm_mix, w_in, gdn_conv_w, gdn_A_log, gdn_dt_bias, gdn_norm, rwkv_mu, rwkv_w0, rwkv_w_up, rwkv_a0, rwkv_a_up, rwkv_g_up, rwkv_k_k, rwkv_k_a, rwkv_r_k, rwkv_ln_w, rwkv_ln_b, fox_b_f, fox_q_norm, fox_k_norm, w_out, norm_x, norm_mem, xattn_wq, xattn_wkv, xattn_q_norm, xattn_k_norm, xattn_wo, norm_ffn, ffn_w_gate, ffn_w_up, ffn_w_down):
    weights = {
        'norm_mix': norm_mix, 'w_in': w_in, 'gdn_conv_w': gdn_conv_w, 'gdn_A_log': gdn_A_log,
        'gdn_dt_bias': gdn_dt_bias, 'gdn_norm': gdn_norm, 'rwkv_mu': rwkv_mu, 'rwkv_w0': rwkv_w0,
        'rwkv_w_up': rwkv_w_up, 'rwkv_a0': rwkv_a0, 'rwkv_a_up': rwkv_a_up, 'rwkv_g_up': rwkv_g_up,
        'rwkv_k_k': rwkv_k_k, 'rwkv_k_a': rwkv_k_a, 'rwkv_r_k': rwkv_r_k, 'rwkv_ln_w': rwkv_ln_w,
        'rwkv_ln_b': rwkv_ln_b, 'fox_b_f': fox_b_f, 'fox_q_norm': fox_q_norm, 'fox_k_norm': fox_k_norm,
        'w_out': w_out, 'norm_x': norm_x, 'norm_mem': norm_mem, 'xattn_wq': xattn_wq, 'xattn_wkv': xattn_wkv,
        'xattn_q_norm': xattn_q_norm, 'xattn_k_norm': xattn_k_norm, 'xattn_wo': xattn_wo,
        'norm_ffn': norm_ffn, 'ffn_w_gate': ffn_w_gate, 'ffn_w_up': ffn_w_up, 'ffn_w_down': ffn_w_down,
    }
    depth = w_in.shape[0]
    bp, lp, d = x_prompt.shape
    bs, ls, _ = x_sample.shape
    n_pages = page_table.shape[1]
    past_len = n_pages * PAGE
    tp = bp * lp
    ts = bs * SAMPLE_PAD
    tt = tp + ts
    tm = _row_tile(tt)
    lb_p = min(256, lp)
    assert ls >= CONV_WIDTH - 1 and ls <= SUBLANES and lp % lb_p == 0

    xs_pad = jnp.zeros((bs, SAMPLE_PAD, d), F32).at[:, :ls].set(x_sample)
    x = jnp.concatenate([x_prompt.reshape(tp, d), xs_pad.reshape(ts, d)], axis=0)

    cos_p, sin_p = _rope_tables(jnp.arange(lp, dtype=jnp.int32))
    cos_s, sin_s = _rope_tables(past_len + jnp.arange(SAMPLE_PAD, dtype=jnp.int32))
    log_gamma = jnp.log(1.0 - jnp.exp2(-(RET_GAMMA_BASE + jnp.arange(NH, dtype=F32))))
    n_pool = cache_fox_k.shape[1]
    cache_k = cache_fox_k.reshape(depth, n_pool, PAGE_COLS, HD)
    cache_v = cache_fox_v.reshape(depth, n_pool, PAGE_COLS, HD)
    cache_lf = cache_fox_logf.reshape(depth, n_pool, 1, PAGE_COLS)
    zeros_s = jnp.zeros((bp, NH, HD, HD), F32)
    zeros_conv = jnp.zeros((bp, SUBLANES, 3 * GW), F32)
    zeros_shift = jnp.zeros((bp, SUBLANES, 1792), F32)

    outs_p, outs_s, mem_ks, mem_vs = [], [], [], []
    for l in range(depth):
        lw = {name: arr[l] for name, arr in weights.items()}
        p = _norm_matmul(x, lw['norm_mix'], _prep_w_in(w_in, l)[None], 0, tm=tm, tn=1024)

        gp = dict(row0=0, nseq=bp, lpad=lp, lvalid=lp, lb=lb_p)
        o_gdn_p, gdn_s_p = _gdn(p, lw['gdn_conv_w'], zeros_conv, zeros_s, lw['gdn_A_log'], lw['gdn_dt_bias'],
                                lw['gdn_norm'], **{**gp, 'lb': min(GDN_LB, lp)})
        o_ret_p, ret_s_p = _ret(p, cos_p, sin_p, log_gamma, zeros_s, **gp)
        o_rwkv_p, rwkv_s_p = _rwkv(p, lw, zeros_shift, zeros_s, **{**gp, 'lb': min(RWKV_LB, lp)})
        qn, kn, kb, vb, lf, c_col, c_row = _fox_prep(p, lw['fox_q_norm'], lw['fox_k_norm'], lw['fox_b_f'],
                                                    nseq=bp, lpad=lp, lb=lb_p)
        o_fox_p = _fox_flash(qn, kb, vb, c_col, c_row, p, nseq=bp, lpad=lp, tq=min(FOX_TQ, lp))

        gs = dict(row0=tp, nseq=bs, lpad=SAMPLE_PAD, lvalid=ls, lb=SAMPLE_PAD)
        o_gdn_s, gdn_s_s = _gdn(p, lw['gdn_conv_w'], _state_tile(state_gdn_conv[l], CONV_WIDTH - 1),
                                state_gdn_S[l], lw['gdn_A_log'], lw['gdn_dt_bias'], lw['gdn_norm'], **gs)
        o_ret_s, ret_s_s = _ret(p, cos_s, sin_s, log_gamma, state_ret_S[l], **gs)
        o_rwkv_s, rwkv_s_s = _rwkv(p, lw, _state_tile(state_rwkv_shift[l], 1),
                                   _rwkv_pair_states(state_rwkv_S[l]), **gs)
        o_fox_s, kn_s, lf_s = _fox_sample(p, page_table, cache_k, cache_v, cache_lf,
                                          lw['fox_q_norm'], lw['fox_k_norm'], lw['fox_b_f'],
                                          layer=l, row0=tp, nseq=bs, lpad=SAMPLE_PAD, lvalid=ls)

        o_mix = jnp.concatenate([
            jnp.concatenate([o_gdn_p, o_ret_p, o_rwkv_p, o_fox_p], axis=1),
            jnp.concatenate([o_gdn_s, o_ret_s, o_rwkv_s, o_fox_s], axis=1)], axis=0)
        x = _matmul_res(o_mix, w_out, l, x, tm=tm, tn=512)

        kv = _norm_matmul(mem_prompt.reshape(bp * N_MEM, d), lw['norm_mem'], xattn_wkv, l,
                          tm=256, tn=XW, head_w=lw['xattn_k_norm'], norm_tiles=1)
        mk = kv[:, :XW].reshape(bp, N_MEM, XW)
        mv = kv[:, XW:].reshape(bp, N_MEM, XW)
        q = _norm_matmul(x, lw['norm_x'], xattn_wq, l, tm=tm, tn=XW,
                         head_w=lw['xattn_q_norm'], norm_tiles=1)
        xo_p = _xattn(q, mk, mv, row0=0, nrows=tp, tq=lb_p, rows_per_seq=lp)
        xo_s = _xattn(q, cache_mem_k[l].reshape(bs, N_MEM, XW), cache_mem_v[l].reshape(bs, N_MEM, XW),
                      row0=tp, nrows=ts, tq=SAMPLE_PAD, rows_per_seq=SAMPLE_PAD)
        x = _matmul_res(jnp.concatenate([xo_p, xo_s], axis=0), xattn_wo, l, x, tm=tm, tn=512)

        hidden = _swiglu_up(x, lw['norm_ffn'], ffn_w_gate, ffn_w_up, l, tm=tm, tn=512)
        x = _matmul_res(hidden, ffn_w_down, l, x, tm=tm, tn=256)

        c0 = CB_GDN * LANES
        r0 = CB_RWKV * LANES
        v0 = (CB_FOX + 2 * NH) * LANES

        def last_rows(row_end, n, col0, width, nseq, stride):
            return jnp.stack([lax.slice(p, (b * stride + row_end - n, col0), (b * stride + row_end, col0 + width))
                              for b in range(nseq)], axis=0)

        ps_v = lax.slice(p, (tp, v0), (tt, v0 + GW)).reshape(bs, SAMPLE_PAD, NH, HD)
        outs_p.append((
            kn.reshape(bp, lp, NH, HD),
            lax.slice(p, (0, v0), (tp, v0 + GW)).reshape(bp, lp, NH, HD),
            lf.reshape(bp, lp, LANES)[:, :, SM_F:SM_F + NH],
            last_rows(lp, CONV_WIDTH - 1, c0, 3 * GW, bp, lp),
            gdn_s_p, ret_s_p,
            last_rows(lp, 1, r0, 1792, bp, lp),
            _rwkv_unpair_states(rwkv_s_p),
        ))
        outs_s.append((
            kn_s[:, :ls].reshape(bs, ls, NH, HD),
            ps_v[:, :ls],
            lf_s[:, :ls, SM_F:SM_F + NH],
            last_rows(tp + ls, CONV_WIDTH - 1, c0, 3 * GW, bs, SAMPLE_PAD),
            gdn_s_s, ret_s_s,
            last_rows(tp + ls, 1, r0, 1792, bs, SAMPLE_PAD),
            _rwkv_unpair_states(rwkv_s_s),
        ))
        mem_ks.append(mk.reshape(bp, N_MEM, NH, HD))
        mem_vs.append(mv.reshape(bp, N_MEM, NH, HD))

    stk = lambda seq, i: jnp.stack([e[i] for e in seq], axis=0)
    yp = x[:tp].reshape(bp, lp, d)
    ys = x[tp:].reshape(bs, SAMPLE_PAD, d)[:, :ls]
    return (yp, ys, stk(outs_p, 0), stk(outs_p, 1), stk(outs_p, 2), jnp.stack(mem_ks, 0), jnp.stack(mem_vs, 0),
            stk(outs_p, 3), stk(outs_p, 4), stk(outs_p, 5), stk(outs_p, 6), stk(outs_p, 7),
            stk(outs_s, 0), stk(outs_s, 1), stk(outs_s, 2), stk(outs_s, 3), stk(outs_s, 4), stk(outs_s, 5),
            stk(outs_s, 6), stk(outs_s, 7))
```

```python
import functools
import math

import jax
import jax.numpy as jnp
import numpy as np
from jax import lax
from jax.experimental import pallas as pl
from jax.experimental.pallas import tpu as pltpu

F32 = jnp.float32
BF16 = jnp.bfloat16

LANES = 128
SUBLANES = 8
VMEM_LIMIT = 56 * 1024 * 1024

D_MODEL = 2048
GW = D_MODEL // 4
HD = 128
NH = GW // HD
RWKV_HEAD = 64
CONV_WIDTH = 4
PAGE = 128
N_MEM = 256
XW = 512
D_FF = 5632
NORM_EPS = 1e-6
GN_EPS = 64e-5
RET_GAMMA_BASE = 5.0
ROPE_BASE = 10000.0
CHUNK = 64
SAMPLE_PAD = 64
FOX_TQ = 512
FOX_HEADS_PER_STEP = 2
GDN_LB = 512
RWKV_LB = 512
RET_LB = 512

NP_COLS = 8192
CB_FOX, CB_RET, CB_GDN, CB_GDN_Z, CB_RWKV, CB_SMALL = 0, 16, 32, 44, 48, 62
SM_A, SM_B, SM_F = 0, 4, 8


def _cparams(sem):
    return pltpu.CompilerParams(dimension_semantics=sem, vmem_limit_bytes=VMEM_LIMIT)


def _dot(a, b):
    return jnp.dot(a.astype(BF16), b.astype(BF16), preferred_element_type=F32)


def _dot_nt(a, b):
    return lax.dot_general(a.astype(BF16), b.astype(BF16), (((1,), (1,)), ((), ())),
                           preferred_element_type=F32)


def _dot_tn(a, b):
    return lax.dot_general(a.astype(BF16), b.astype(BF16), (((0,), (0,)), ((), ())),
                           preferred_element_type=F32)


def _split3(x):
    hi = x.astype(BF16)
    r = x - hi.astype(F32)
    mid = r.astype(BF16)
    lo = (r - mid.astype(F32)).astype(BF16)
    return hi, mid, lo


def _dot_exact_lhs(m, x):
    hi, mid, lo = _split3(x)
    d = lambda p: jnp.dot(m, p, preferred_element_type=F32)
    return d(hi) + d(mid) + d(lo)


def _dot_exact_rhs(x, m):
    hi, mid, lo = _split3(x)
    d = lambda p: jnp.dot(p, m, preferred_element_type=F32)
    return d(hi) + d(mid) + d(lo)


def _bmm(a, b):
    return lax.dot_general(a.astype(BF16), b.astype(BF16), (((2,), (1,)), ((0,), (0,))),
                           preferred_element_type=F32)


def _bmm_nt(a, b):
    return lax.dot_general(a.astype(BF16), b.astype(BF16), (((2,), (2,)), ((0,), (0,))),
                           preferred_element_type=F32)


def _unit_lower_inv(n, nil):
    c = n.shape[-1]
    ii = lax.broadcasted_iota(jnp.int32, (c, c), 0)
    jj = lax.broadcasted_iota(jnp.int32, (c, c), 1)
    p = jnp.where(ii == jj, 1.0, 0.0).astype(F32) - n
    q = n
    for _ in range(int(math.log2(nil)) - 1):
        q = _bmm(q, q)
        p = p + _bmm(p, q)
    return p


def _chunk_tri(lb, chunk):
    ii = lax.broadcasted_iota(jnp.int32, (lb, lb), 0)
    jj = lax.broadcasted_iota(jnp.int32, (lb, lb), 1)
    same = (ii // chunk) == (jj // chunk)
    return jnp.where(jnp.logical_and(ii >= jj, same), 1.0, 0.0).astype(BF16)


def _shift_rows(x, prev8, s):
    rolled = pltpu.roll(x, s, axis=0)
    pr = pltpu.roll(prev8, s, axis=0)
    row = lax.broadcasted_iota(jnp.int32, (SUBLANES, x.shape[1]), 0)
    top = jnp.where(row < s, pr, rolled[:SUBLANES])
    return jnp.concatenate([top, rolled[SUBLANES:]], axis=0)


def _softplus(x):
    return jnp.maximum(x, 0.0) + jnp.log1p(jnp.exp(-jnp.abs(x)))


def _sigmoid(x):
    return jax.nn.sigmoid(x)


def _silu(x):
    return x * jax.nn.sigmoid(x)


def _lane_col(x, idx):
    lane = lax.broadcasted_iota(jnp.int32, x.shape, 1)
    return jnp.sum(jnp.where(lane == idx, x, 0.0), axis=-1, keepdims=True)


def _rms(x, w=None):
    y = x * lax.rsqrt(jnp.mean(x * x, axis=-1, keepdims=True) + NORM_EPS)
    return y if w is None else y * w


def _row_valid(lb_index, lb, lvalid, width):
    row = lax.broadcasted_iota(jnp.int32, (lb, width), 0) + lb_index * lb
    return row < lvalid


def _norm_matmul_kernel(x_ref, g_ref, w_ref, hw_ref, o_ref, xn_ref, *, norm_tiles, transposed_w):
    j = pl.program_id(1)

    @pl.when(j == 0)
    def _():
        xn_ref[...] = _rms(x_ref[...], g_ref[...]).astype(BF16)

    if transposed_w:
        acc = _dot_nt(xn_ref[...], w_ref[...])
    else:
        acc = jnp.dot(xn_ref[...], w_ref[...].astype(BF16), preferred_element_type=F32)
    if norm_tiles == 0:
        o_ref[...] = acc
    else:
        @pl.when(j < norm_tiles)
        def _():
            hw = hw_ref[...]
            for h in range(acc.shape[1] // HD):
                sl = slice(h * HD, (h + 1) * HD)
                o_ref[:, sl] = _rms(acc[:, sl], hw)

        @pl.when(j >= norm_tiles)
        def _():
            o_ref[...] = acc


def _row_tile(m, cap=1100):
    return next(t for t in range(cap - cap % 16, 0, -16) if m % t == 0)


def _norm_matmul(x, g, w, layer, *, tm, tn, head_w=None, norm_tiles=0, transposed_w=False):
    m, k = x.shape
    n = w.shape[1] if transposed_w else w.shape[2]
    if head_w is None:
        head_w = jnp.ones((HD,), F32)
    if transposed_w:
        w_spec = pl.BlockSpec((None, tn, k), lambda i, j: (layer, j, 0))
    else:
        w_spec = pl.BlockSpec((None, k, tn), lambda i, j: (layer, 0, j))
    return pl.pallas_call(
        functools.partial(_norm_matmul_kernel, norm_tiles=norm_tiles, transposed_w=transposed_w),
        grid=(m // tm, n // tn),
        in_specs=[
            pl.BlockSpec((tm, k), lambda i, j: (i, 0)),
            pl.BlockSpec((1, k), lambda i, j: (0, 0)),
            w_spec,
            pl.BlockSpec((1, HD), lambda i, j: (0, 0)),
        ],
        out_specs=pl.BlockSpec((tm, tn), lambda i, j: (i, j)),
        out_shape=jax.ShapeDtypeStruct((m, n), F32),
        scratch_shapes=[pltpu.VMEM((tm, k), BF16)],
        compiler_params=_cparams(("parallel", "arbitrary")),
        name="norm_matmul",
    )(x, g.reshape(1, k), w, head_w.reshape(1, HD))


def _matmul_res_kernel(a_ref, w_ref, r_ref, o_ref):
    o_ref[...] = r_ref[...] + jnp.dot(a_ref[...], w_ref[...].astype(BF16), preferred_element_type=F32)


def _matmul_res(a, w, layer, res, *, tm, tn):
    m, k = a.shape
    n = w.shape[2]
    return pl.pallas_call(
        _matmul_res_kernel,
        grid=(m // tm, n // tn),
        in_specs=[
            pl.BlockSpec((tm, k), lambda i, j: (i, 0)),
            pl.BlockSpec((None, k, tn), lambda i, j: (layer, 0, j)),
            pl.BlockSpec((tm, tn), lambda i, j: (i, j)),
        ],
        out_specs=pl.BlockSpec((tm, tn), lambda i, j: (i, j)),
        out_shape=jax.ShapeDtypeStruct((m, n), F32),
        compiler_params=_cparams(("parallel", "parallel")),
        name="matmul_res",
    )(a, w, res)


def _swiglu_up_kernel(x_ref, g_ref, wg_ref, wu_ref, o_ref, xn_ref):
    @pl.when(pl.program_id(1) == 0)
    def _():
        xn_ref[...] = _rms(x_ref[...], g_ref[...]).astype(BF16)

    xn = xn_ref[...]
    gate = jnp.dot(xn, wg_ref[...].astype(BF16), preferred_element_type=F32)
    up = jnp.dot(xn, wu_ref[...].astype(BF16), preferred_element_type=F32)
    o_ref[...] = (_silu(gate) * up).astype(BF16)


def _swiglu_up(x, g, wg, wu, layer, *, tm, tn):
    m, k = x.shape
    n = wg.shape[2]
    return pl.pallas_call(
        _swiglu_up_kernel,
        grid=(m // tm, n // tn),
        in_specs=[
            pl.BlockSpec((tm, k), lambda i, j: (i, 0)),
            pl.BlockSpec((1, k), lambda i, j: (0, 0)),
            pl.BlockSpec((None, k, tn), lambda i, j: (layer, 0, j)),
            pl.BlockSpec((None, k, tn), lambda i, j: (layer, 0, j)),
        ],
        out_specs=pl.BlockSpec((tm, tn), lambda i, j: (i, j)),
        out_shape=jax.ShapeDtypeStruct((m, n), BF16),
        scratch_shapes=[pltpu.VMEM((tm, k), BF16)],
        compiler_params=_cparams(("parallel", "arbitrary")),
        name="swiglu_up",
    )(x, g.reshape(1, k), wg, wu)


def _xattn_kernel(q_ref, k_ref, v_ref, o_ref):
    scale = HD ** -0.5
    for h in range(NH):
        sl = slice(h * HD, (h + 1) * HD)
        s = _dot_nt(q_ref[:, sl], k_ref[:, sl]) * scale
        m = jnp.max(s, axis=-1, keepdims=True)
        p = jnp.exp(s - m)
        l = jnp.sum(p, axis=-1, keepdims=True)
        o_ref[:, sl] = (_dot(p, v_ref[:, sl]) / l).astype(BF16)


def _xattn(q, mem_k, mem_v, *, row0, nrows, tq, rows_per_seq):
    tiles_per_seq = rows_per_seq // tq
    t0 = row0 // tq
    return pl.pallas_call(
        _xattn_kernel,
        grid=(nrows // tq,),
        in_specs=[
            pl.BlockSpec((tq, XW), lambda i: (t0 + i, 0)),
            pl.BlockSpec((None, N_MEM, XW), lambda i: (i // tiles_per_seq, 0, 0)),
            pl.BlockSpec((None, N_MEM, XW), lambda i: (i // tiles_per_seq, 0, 0)),
        ],
        out_specs=pl.BlockSpec((tq, XW), lambda i: (i, 0)),
        out_shape=jax.ShapeDtypeStruct((nrows, XW), BF16),
        compiler_params=_cparams(("parallel",)),
        name="xattn",
    )(q, mem_k, mem_v)


def _gdn_kernel(q_ref, k_ref, v_ref, z_ref, sm_ref, cwq_ref, cwk_ref, cwv_ref, cq_ref, ck_ref, cv_ref,
                s0_ref, alog_ref, dtb_ref, nw_ref, mix_ref, o_ref, so_ref, s_ref, prev_ref, gt_ref,
                *, lb, lvalid, nlb):
    del mix_ref
    h = pl.program_id(1)
    ib = pl.program_id(2)
    masked = lvalid < nlb * lb

    @pl.when(ib == 0)
    def _():
        s_ref[...] = s0_ref[...]
        prev_ref[0] = cq_ref[...]
        prev_ref[1] = ck_ref[...]
        prev_ref[2] = cv_ref[...]

    def conv(x_ref, w_ref, i):
        x = x_ref[...]
        w = w_ref[...]
        prev = prev_ref[i]
        y = x * w[3:4]
        for s in (1, 2, 3):
            y = y + _shift_rows(x, prev, s) * w[3 - s:4 - s]
        prev_ref[i] = x[lb - SUBLANES:]
        return _silu(y)

    q = conv(q_ref, cwq_ref, 0)
    k = conv(k_ref, cwk_ref, 1)
    v = conv(v_ref, cwv_ref, 2)
    q = q * lax.rsqrt(jnp.sum(q * q, axis=-1, keepdims=True) + NORM_EPS) * (HD ** -0.5)
    k = k * lax.rsqrt(jnp.sum(k * k, axis=-1, keepdims=True) + NORM_EPS)

    sm = sm_ref[...]
    g_blk = -jnp.exp(alog_ref[...]) * _softplus(sm + dtb_ref[...])
    beta_blk = _sigmoid(sm)
    if masked:
        valid = _row_valid(ib, lb, lvalid, LANES)
        g_blk = jnp.where(valid, g_blk, 0.0)
        beta_blk = jnp.where(valid, beta_blk, 0.0)
    gc_blk = _dot_exact_lhs(_chunk_tri(lb, CHUNK), g_blk)
    gt_ref[...] = gc_blk.T
    g_col_all = _lane_col(gc_blk, SM_A + h)
    beta_all = _lane_col(beta_blk, SM_B + h)
    g_row_all = gt_ref[pl.ds(SM_A + h, 1), :]

    group = min(2 * CHUNK, lb)
    ng = lb // group
    nchunk = lb // CHUNK
    ii = lax.broadcasted_iota(jnp.int32, (group, group), 0)
    jj = lax.broadcasted_iota(jnp.int32, (group, group), 1)
    lower = jnp.logical_and(ii >= jj, (ii // CHUNK) == (jj // CHUNK))
    to3 = lambda x: x.reshape(ng, group, x.shape[-1])
    q3, k3, v3 = to3(q), to3(k), to3(v)
    g_col3 = to3(g_col_all)
    beta3 = to3(beta_all)
    g_row3 = jnp.stack([g_row_all[:, i * group:(i + 1) * group] for i in range(ng)], axis=0)
    dec3 = jnp.exp(jnp.where(lower, g_col3 - g_row3, -jnp.inf))
    n3 = jnp.where(ii > jj, beta3 * _bmm_nt(k3, k3) * dec3, 0.0)
    ainv3 = _unit_lower_inv(n3, CHUNK)
    eg3 = jnp.exp(g_col3)
    uv3 = _bmm(ainv3, beta3 * v3)
    w3 = _bmm(ainv3, beta3 * eg3 * k3)
    qk3 = _bmm_nt(q3, k3) * dec3
    qp = (q3 * eg3 - _bmm(qk3, w3)).reshape(lb, HD)
    op = _bmm(qk3, uv3).reshape(lb, HD)
    w_all = w3.reshape(lb, HD)
    uv_all = uv3.reshape(lb, HD)
    g_chunks = g_col_all.reshape(nchunk, CHUNK, 1)
    g_end = jnp.broadcast_to(g_chunks[:, CHUNK - 1:, :], g_chunks.shape).reshape(lb, 1)
    kd = k * jnp.exp(g_end - g_col_all)
    s_decay = jnp.exp(g_end)
    nw = nw_ref[...]
    for c in range(nchunk):
        r = slice(c * CHUNK, (c + 1) * CHUNK)
        an = _dot_tn(kd[r], jnp.concatenate([w_all[r], uv_all[r]], axis=1))
        s = s_ref[...]
        o = _dot(qp[r], s) + op[r]
        s_ref[...] = s * s_decay[c * CHUNK:c * CHUNK + 1] - _dot(an[:, :HD], s) + an[:, HD:]
        o_ref[r, :] = (_rms(o, nw) * _silu(z_ref[r, :])).astype(BF16)

    @pl.when(ib == nlb - 1)
    def _():
        so_ref[...] = s_ref[...]


MIX_ANY = pl.BlockSpec(memory_space=pl.ANY)
MIX_GDN, MIX_RET, MIX_RWKV, MIX_FOX = 0, 1, 2, 3


def _gdn(p, mix, conv_w, conv_init, s0, a_log, dt_bias, norm_w, *, row0, nseq, lpad, lvalid, lb):
    nlb = lpad // lb
    rb = lambda s, i: (row0 + s * lpad) // lb + i
    pblk = lambda cb: pl.BlockSpec((lb, HD), lambda s, h, i: (rb(s, i), cb + h))
    cwblk = lambda j: pl.BlockSpec((CONV_WIDTH, HD), lambda s, h, i: (0, j * NH + h))
    ciblk = lambda j: pl.BlockSpec((None, SUBLANES, HD), lambda s, h, i: (s, 0, j * NH + h))
    vec = pl.BlockSpec((1, LANES), lambda s, h, i: (0, 0))
    lane_pad = lambda x: jnp.zeros((1, LANES), F32).at[0, :x.shape[0]].set(x)
    return pl.pallas_call(
        functools.partial(_gdn_kernel, lb=lb, lvalid=lvalid, nlb=nlb),
        grid=(nseq, NH, nlb),
        in_specs=[
            pblk(CB_GDN), pblk(CB_GDN + NH), pblk(CB_GDN + 2 * NH), pblk(CB_GDN_Z),
            pl.BlockSpec((lb, LANES), lambda s, h, i: (rb(s, i), CB_SMALL)),
            cwblk(0), cwblk(1), cwblk(2), ciblk(0), ciblk(1), ciblk(2),
            pl.BlockSpec((None, None, HD, HD), lambda s, h, i: (s, h, 0, 0)),
            vec, vec, vec, MIX_ANY,
        ],
        out_specs=[
            pl.BlockSpec((lb, HD), lambda s, h, i: (rb(s, i), MIX_GDN * NH + h)),
            pl.BlockSpec((None, None, HD, HD), lambda s, h, i: (s, h, 0, 0)),
        ],
        out_shape=[
            jax.ShapeDtypeStruct(mix.shape, mix.dtype),
            jax.ShapeDtypeStruct((nseq, NH, HD, HD), F32),
        ],
        input_output_aliases={15: 0},
        scratch_shapes=[
            pltpu.VMEM((HD, HD), F32),
            pltpu.VMEM((3, SUBLANES, HD), F32),
            pltpu.VMEM((LANES, lb), F32),
        ],
        compiler_params=_cparams(("parallel", "parallel", "arbitrary")),
        name="gdn",
    )(p, p, p, p, p, conv_w, conv_w, conv_w, conv_init, conv_init, conv_init, s0,
      lane_pad(a_log), lane_pad(dt_bias), norm_w.reshape(1, HD), mix)


def _ret_kernel(q_ref, k_ref, v_ref, g_ref, cos_ref, sin_ref, lg_ref, s0_ref, mix_ref, o_ref, so_ref, s_ref,
                *, lb, cv, nlb):
    del mix_ref
    ib = pl.program_id(2)

    @pl.when(ib == 0)
    def _():
        s_ref[...] = s0_ref[...]

    cos = cos_ref[...]
    sin = sin_ref[...]
    rot = lambda x: x * cos + pltpu.roll(x, HD // 2, axis=1) * sin
    q = rot(q_ref[...])
    k = rot(k_ref[...]) * (HD ** -0.5)
    v = v_ref[...]
    lg = lg_ref[...][:, 0:1]

    ii = lax.broadcasted_iota(jnp.int32, (CHUNK, CHUNK), 0)
    jj = lax.broadcasted_iota(jnp.int32, (CHUNK, CHUNK), 1)
    rel = (ii - jj).astype(F32)
    dmat = jnp.where(rel >= 0, jnp.exp(jnp.maximum(rel, 0.0) * lg), 0.0)
    idx = lax.broadcasted_iota(jnp.int32, (CHUNK, 1), 0)
    idf = idx.astype(F32)
    xi = jnp.exp((idf + 1.0) * lg)
    zeta = jnp.where(idx < cv, jnp.exp((cv - 1.0 - idf) * lg), 0.0)
    gc = jnp.exp(cv * lg)
    nchunk = lb // CHUNK
    to3 = lambda x: x.reshape(nchunk, CHUNK, HD)
    q3, k3, v3 = to3(q), to3(k), to3(v)
    o_intra = _bmm(_bmm_nt(q3, k3) * dmat, v3)
    qx = q3 * xi
    kz = k3 * zeta
    s = s_ref[...]
    for c in range(nchunk):
        r = slice(c * CHUNK, (c + 1) * CHUNK)
        o = o_intra[c] + _dot(qx[c], s)
        s = s * gc + _dot_tn(kz[c], v3[c])
        o_ref[r, :] = (_rms(o) * _silu(g_ref[r, :])).astype(BF16)
    s_ref[...] = s

    @pl.when(ib == nlb - 1)
    def _():
        so_ref[...] = s


def _ret(p, mix, cos_t, sin_t, log_gamma, s0, *, row0, nseq, lpad, lvalid, lb):
    nlb = lpad // lb
    cv = CHUNK if lvalid == lpad else lvalid
    assert cv == CHUNK or (lpad == CHUNK and 0 < lvalid < CHUNK)
    rb = lambda s, i: (row0 + s * lpad) // lb + i
    pblk = lambda cb: pl.BlockSpec((lb, HD), lambda s, h, i: (rb(s, i), cb + h))
    tblk = pl.BlockSpec((lb, HD), lambda s, h, i: (i, 0))
    sblk = pl.BlockSpec((None, None, HD, HD), lambda s, h, i: (s, h, 0, 0))
    lg = jnp.broadcast_to(log_gamma[:, None, None], (NH, 1, LANES))
    return pl.pallas_call(
        functools.partial(_ret_kernel, lb=lb, cv=cv, nlb=nlb),
        grid=(nseq, NH, nlb),
        in_specs=[
            pblk(CB_RET), pblk(CB_RET + NH), pblk(CB_RET + 2 * NH), pblk(CB_RET + 3 * NH),
            tblk, tblk,
            pl.BlockSpec((None, 1, LANES), lambda s, h, i: (h, 0, 0)),
            sblk, MIX_ANY,
        ],
        out_specs=[pl.BlockSpec((lb, HD), lambda s, h, i: (rb(s, i), MIX_RET * NH + h)), sblk],
        out_shape=[
            jax.ShapeDtypeStruct(mix.shape, mix.dtype),
            jax.ShapeDtypeStruct((nseq, NH, HD, HD), F32),
        ],
        input_output_aliases={8: 0},
        scratch_shapes=[pltpu.VMEM((HD, HD), F32)],
        compiler_params=_cparams(("parallel", "parallel", "arbitrary")),
        name="retention",
    )(p, p, p, p, cos_t, sin_t, lg, s0, mix)


def _half_sum(x):
    lane = lax.broadcasted_iota(jnp.int32, x.shape, 1)
    lo = lane < RWKV_HEAD
    s_lo = jnp.sum(jnp.where(lo, x, 0.0), axis=-1, keepdims=True)
    s_hi = jnp.sum(jnp.where(lo, 0.0, x), axis=-1, keepdims=True)
    return jnp.where(lo, s_lo, s_hi)


def _rwkv_kernel(r_ref, k_ref, v_ref, wa_ref, gd_ref, mur_ref, muk_ref, muv_ref, muwa_ref, mugd_ref,
                 sh_r_ref, sh_k_ref, sh_v_ref, sh_wa_ref, sh_gd_ref,
                 wup_ref, aup_ref, gup_ref, w0_ref, a0_ref, kk_ref, ka_ref, rk_ref, lnw_ref, lnb_ref,
                 s0_ref, mix_ref, o_ref, so_ref, s_ref, prev_ref, *, lb, lvalid, nlb):
    del mix_ref
    ib = pl.program_id(2)
    masked = lvalid < nlb * lb

    @pl.when(ib == 0)
    def _():
        s_ref[...] = s0_ref[...]
        prev_ref[0] = sh_r_ref[...]
        prev_ref[1] = sh_k_ref[...]
        prev_ref[2] = sh_v_ref[...]
        prev_ref[3] = sh_wa_ref[...]
        prev_ref[4] = sh_gd_ref[...]

    def shifted(x_ref, mu_ref, i):
        x = x_ref[...]
        prev = _shift_rows(x, prev_ref[i], 1)
        prev_ref[i] = x[lb - SUBLANES:]
        return x + (prev - x) * mu_ref[...]

    r = shifted(r_ref, mur_ref, 0)
    k = shifted(k_ref, muk_ref, 1)
    v = shifted(v_ref, muv_ref, 2)
    wa = shifted(wa_ref, muwa_ref, 3)
    gd = shifted(gd_ref, mugd_ref, 4)

    w_raw = -_softplus(-(w0_ref[...] + _dot(jnp.tanh(wa), wup_ref[...]))) - 0.5
    logw = -jnp.exp(w_raw)
    a_sig = _sigmoid(a0_ref[...] + _dot(wa, aup_ref[...]))
    gate = _dot(_sigmoid(gd), gup_ref[...])
    kk = k * kk_ref[...]
    kk = kk * lax.rsqrt(_half_sum(kk * kk) + NORM_EPS)
    kp = k * (1.0 + (a_sig - 1.0) * ka_ref[...])
    rec_a = -kk
    rec_b = kk * a_sig
    if masked:
        valid = _row_valid(ib, lb, lvalid, LANES)
        zero = lambda x: jnp.where(valid, x, 0.0)
        logw, rec_a, rec_b, kp, v = zero(logw), zero(rec_a), zero(rec_b), zero(kp), zero(v)

    cum = _dot_exact_lhs(_chunk_tri(lb, CHUNK), logw)
    e_pos = jnp.exp(cum)
    e_neg = jnp.exp(-cum)
    at_all = rec_a * jnp.exp(cum - logw)
    bt_all = rec_b * e_neg
    kt_all = kp * e_neg
    rt_all = r * e_pos

    stacked = 2 * CHUNK
    row_head = lax.broadcasted_iota(jnp.int32, (stacked, LANES), 0) // CHUNK
    lane_head = lax.broadcasted_iota(jnp.int32, (stacked, LANES), 1) // RWKV_HEAD
    own = row_head == lane_head
    nchunk = lb // CHUNK
    dup3 = lambda x: jnp.concatenate([x.reshape(nchunk, CHUNK, LANES)] * 2, axis=1)
    stack3 = lambda x: jnp.where(own, dup3(x), 0.0)
    block_diag = (lax.broadcasted_iota(jnp.int32, (LANES, LANES), 0) // RWKV_HEAD) == (
        lax.broadcasted_iota(jnp.int32, (LANES, LANES), 1) // RWKV_HEAD)
    ti = lax.broadcasted_iota(jnp.int32, (stacked, stacked), 0) % CHUNK
    tj = lax.broadcasted_iota(jnp.int32, (stacked, stacked), 1) % CHUNK

    at3, bt3, kt3, rt3, v3 = stack3(at_all), stack3(bt_all), stack3(kt_all), stack3(rt_all), dup3(v)
    l_ab = jnp.where(ti > tj, _bmm_nt(at3, bt3), 0.0)
    l_ak = jnp.where(ti > tj, _bmm_nt(at3, kt3), 0.0)
    l_rb = jnp.where(ti >= tj, _bmm_nt(rt3, bt3), 0.0)
    l_rk = jnp.where(ti >= tj, _bmm_nt(rt3, kt3), 0.0)
    inv = _unit_lower_inv(-l_ab, CHUNK)
    t1 = _bmm(inv, at3)
    t2 = jnp.where(own, _bmm(inv, _bmm(l_ak, v3)), 0.0)
    rp = rt3 + _bmm(l_rb, t1)
    op = jnp.where(own, _bmm(l_rb, t2) + _bmm(l_rk, v3), 0.0)
    cum3 = cum.reshape(nchunk, CHUNK, LANES)
    cum_end = jnp.broadcast_to(cum3[:, CHUNK - 1:, :], cum3.shape).reshape(lb, LANES)
    to_end = jnp.exp(cum_end - cum)
    b_end = rec_b * to_end
    k_end = kp * to_end
    s_decay = jnp.exp(cum_end)

    for c in range(nchunk):
        rs = slice(c * CHUNK, (c + 1) * CHUNK)
        vc = v[rs]
        b2 = jnp.concatenate([b_end[rs], b_end[rs]], axis=0)
        gh = _dot_tn(jnp.concatenate([t1[c], t2[c]], axis=1), b2)
        h = gh[LANES:] + _dot_tn(vc, k_end[rs])
        s = s_ref[...]
        o2 = jnp.where(own, _dot_nt(rp[c], s), 0.0) + op[c]
        o = o2[:CHUNK] + o2[CHUNK:]
        s_ref[...] = s * s_decay[c * CHUNK:c * CHUNK + 1] + jnp.where(block_diag, _dot(s, gh[:LANES]) + h, 0.0)

        inv_n = 1.0 / RWKV_HEAD
        mu = _half_sum(o) * inv_n
        var = _half_sum(jnp.square(o - mu)) * inv_n
        on = (o - mu) * lax.rsqrt(var + GN_EPS) * lnw_ref[...] + lnb_ref[...]
        bonus = _half_sum(r[rs] * kp[rs] * rk_ref[...]) * vc
        o_ref[rs, :] = ((on + bonus) * gate[rs]).astype(BF16)

    @pl.when(ib == nlb - 1)
    def _():
        so_ref[...] = s_ref[...]


def _rwkv(p, mix, lw, shift_init, s0, *, row0, nseq, lpad, lvalid, lb):
    nlb = lpad // lb
    npair = GW // LANES
    rb = lambda s, i: (row0 + s * lpad) // lb + i
    pblk = lambda cb, per_pair: pl.BlockSpec(
        (lb, LANES), lambda s, j, i: (rb(s, i), cb + (j if per_pair else 0)))
    mublk = lambda cb, per_pair: pl.BlockSpec((1, LANES), lambda s, j, i: (0, cb + (j if per_pair else 0)))
    shblk = lambda cb, per_pair: pl.BlockSpec(
        (None, SUBLANES, LANES), lambda s, j, i: (s, 0, cb + (j if per_pair else 0)))
    pair_vec = pl.BlockSpec((1, LANES), lambda s, j, i: (0, j))
    pair_mat = pl.BlockSpec((LANES, LANES), lambda s, j, i: (0, j))
    sblk = pl.BlockSpec((None, None, LANES, LANES), lambda s, j, i: (s, j, 0, 0))
    mu = lw['rwkv_mu'].reshape(1, -1)
    zeros64 = jnp.zeros((64, GW), F32)
    wup = jnp.concatenate([lw['rwkv_w_up'], zeros64], axis=0).astype(BF16)
    aup = jnp.concatenate([zeros64, lw['rwkv_a_up']], axis=0).astype(BF16)
    row = lambda x: x.reshape(1, GW)
    blocks = [(0, True), (4, True), (8, True), (12, False), (13, False)]
    return pl.pallas_call(
        functools.partial(_rwkv_kernel, lb=lb, lvalid=lvalid, nlb=nlb),
        grid=(nseq, npair, nlb),
        in_specs=(
            [pblk(CB_RWKV + cb, pp) for cb, pp in blocks]
            + [mublk(cb, pp) for cb, pp in blocks]
            + [shblk(cb, pp) for cb, pp in blocks]
            + [pair_mat, pair_mat, pair_mat] + [pair_vec] * 7 + [sblk, MIX_ANY]
        ),
        out_specs=[pl.BlockSpec((lb, LANES), lambda s, j, i: (rb(s, i), MIX_RWKV * npair + j)), sblk],
        out_shape=[
            jax.ShapeDtypeStruct(mix.shape, mix.dtype),
            jax.ShapeDtypeStruct((nseq, npair, LANES, LANES), F32),
        ],
        input_output_aliases={26: 0},
        scratch_shapes=[pltpu.VMEM((LANES, LANES), F32), pltpu.VMEM((5, SUBLANES, LANES), F32)],
        compiler_params=_cparams(("parallel", "parallel", "arbitrary")),
        name="rwkv7",
    )(p, p, p, p, p, mu, mu, mu, mu, mu, shift_init, shift_init, shift_init, shift_init, shift_init,
      wup, aup, lw['rwkv_g_up'].astype(BF16), row(lw['rwkv_w0']), row(lw['rwkv_a0']), row(lw['rwkv_k_k']),
      row(lw['rwkv_k_a']), row(lw['rwkv_r_k']), row(lw['rwkv_ln_w']), row(lw['rwkv_ln_b']), s0, mix)


def _fox_prep_kernel(q_ref, k_ref, v_ref, sm_ref, qw_ref, kw_ref, bf_ref,
                     qn_ref, kn_ref, kb_ref, vb_ref, lf_ref, c_ref, ct_ref, carry_ref, *, lb):
    @pl.when(pl.program_id(1) == 0)
    def _():
        carry_ref[...] = jnp.zeros_like(carry_ref)

    qw = qw_ref[...]
    kw = kw_ref[...]
    for h in range(NH):
        sl = slice(h * HD, (h + 1) * HD)
        qn_ref[:, sl] = _rms(q_ref[:, sl], qw).astype(BF16)
        kn = _rms(k_ref[:, sl], kw)
        kn_ref[:, sl] = kn
        kb_ref[:, sl] = kn.astype(BF16)
    vb_ref[...] = v_ref[...].astype(BF16)
    logf = -_softplus(-(sm_ref[...] + bf_ref[...]))
    lf_ref[...] = logf
    c = _dot_exact_lhs(_chunk_tri(lb, lb), logf) + carry_ref[0:1, :]
    c_ref[...] = c
    carry_ref[...] = jnp.broadcast_to(c[lb - 1:], carry_ref.shape)
    ct_ref[...] = c.T[SM_F:SM_F + SUBLANES]


def _fox_prep(p, q_w, k_w, b_f, *, nseq, lpad, lb):
    nlb = lpad // lb
    rb = lambda s, i: s * nlb + i
    seg = lambda j: pl.BlockSpec((lb, GW), lambda s, i: (rb(s, i), j))
    vec = pl.BlockSpec((1, LANES), lambda s, i: (0, 0))
    rows = nseq * lpad
    bf_lane = jnp.zeros((1, LANES), F32).at[0, SM_F:SM_F + NH].set(b_f)
    return pl.pallas_call(
        functools.partial(_fox_prep_kernel, lb=lb),
        grid=(nseq, nlb),
        in_specs=[seg(0), seg(1), seg(2),
                  pl.BlockSpec((lb, LANES), lambda s, i: (rb(s, i), CB_SMALL)), vec, vec, vec],
        out_specs=[
            pl.BlockSpec((lb, GW), lambda s, i: (rb(s, i), 0)),
            pl.BlockSpec((lb, GW), lambda s, i: (rb(s, i), 0)),
            pl.BlockSpec((lb, GW), lambda s, i: (rb(s, i), 0)),
            pl.BlockSpec((lb, GW), lambda s, i: (rb(s, i), 0)),
            pl.BlockSpec((lb, LANES), lambda s, i: (rb(s, i), 0)),
            pl.BlockSpec((lb, LANES), lambda s, i: (rb(s, i), 0)),
            pl.BlockSpec((SUBLANES, lb), lambda s, i: (0, rb(s, i))),
        ],
        out_shape=[
            jax.ShapeDtypeStruct((rows, GW), BF16),
            jax.ShapeDtypeStruct((rows, GW), F32),
            jax.ShapeDtypeStruct((rows, GW), BF16),
            jax.ShapeDtypeStruct((rows, GW), BF16),
            jax.ShapeDtypeStruct((rows, LANES), F32),
            jax.ShapeDtypeStruct((rows, LANES), F32),
            jax.ShapeDtypeStruct((SUBLANES, rows), F32),
        ],
        scratch_shapes=[pltpu.VMEM((SUBLANES, LANES), F32)],
        compiler_params=_cparams(("parallel", "arbitrary")),
        name="fox_prep",
    )(p, p, p, p, q_w.reshape(1, HD), k_w.reshape(1, HD), bf_lane)


def _fox_flash_kernel(q_ref, k_ref, v_ref, c_ref, ct_ref, g_ref, mix_ref, o_ref, m_ref, l_ref, acc_ref, *, tq):
    del mix_ref
    h0 =pl.program_id(1) * FOX_HEADS_PER_STEP
    qi = pl.program_id(2)
    log2e = 1.0 / math.log(2.0)
    scale = HD ** -0.5 * log2e
    c_all = c_ref[...]
    c_cols = [_lane_col(c_all, SM_F + h0 + e) * log2e for e in range(FOX_HEADS_PER_STEP)]
    m_ref[...] = jnp.full_like(m_ref, -1e30)
    l_ref[...] = jnp.zeros_like(l_ref)
    acc_ref[...] = jnp.zeros_like(acc_ref)

    def block(ks, width, diagonal):
        for e in range(FOX_HEADS_PER_STEP):
            sl = slice(e * HD, (e + 1) * HD)
            kb = k_ref[pl.ds(ks, width), sl]
            vb = v_ref[pl.ds(ks, width), sl]
            t = _dot_nt(q_ref[:, sl], kb) * scale - ct_ref[pl.ds(h0 + e, 1), pl.ds(ks, width)] * log2e
            if diagonal:
                ii = lax.broadcasted_iota(jnp.int32, (tq, width), 0)
                jj = lax.broadcasted_iota(jnp.int32, (tq, width), 1)
                t = jnp.where(jj <= ii, t, -jnp.inf)
            m_old = m_ref[e]
            m_new = jnp.maximum(m_old, jnp.max(t, axis=-1, keepdims=True) + c_cols[e])
            alpha = jnp.exp2(m_old - m_new)
            pr = jnp.exp2(t - (m_new - c_cols[e]))
            l_ref[e] = alpha * l_ref[e] + jnp.sum(pr, axis=-1, keepdims=True)
            acc_ref[e] = alpha * acc_ref[e] + _dot(pr, vb)
            m_ref[e] = m_new

    def body(kp, carry):
        block(pl.multiple_of(kp * 2 * tq, 2 * tq), 2 * tq, False)
        return carry

    lax.fori_loop(0, qi // 2, body, 0)

    @pl.when(qi % 2 == 1)
    def _():
        block(pl.multiple_of((qi - 1) * tq, tq), tq, False)

    block(pl.multiple_of(qi * tq, tq), tq, True)
    for e in range(FOX_HEADS_PER_STEP):
        sl = slice(e * HD, (e + 1) * HD)
        o_ref[:, sl] = (acc_ref[e] / l_ref[e] * _sigmoid(g_ref[:, sl])).astype(BF16)


def _fox_flash(qn, kb, vb, c_col, c_row, p, mix, *, nseq, lpad, tq):
    nq = lpad // tq
    hw = FOX_HEADS_PER_STEP * HD
    g_block0 = (CB_FOX + 3 * NH) // FOX_HEADS_PER_STEP
    o_block0 = MIX_FOX * (GW // hw)
    return pl.pallas_call(
        functools.partial(_fox_flash_kernel, tq=tq),
        grid=(nseq, NH // FOX_HEADS_PER_STEP, nq),
        in_specs=[
            pl.BlockSpec((tq, hw), lambda s, h, i: (s * nq + i, h)),
            pl.BlockSpec((lpad, hw), lambda s, h, i: (s, h)),
            pl.BlockSpec((lpad, hw), lambda s, h, i: (s, h)),
            pl.BlockSpec((tq, LANES), lambda s, h, i: (s * nq + i, 0)),
            pl.BlockSpec((SUBLANES, lpad), lambda s, h, i: (0, s)),
            pl.BlockSpec((tq, hw), lambda s, h, i: (s * nq + i, g_block0 + h)),
            MIX_ANY,
        ],
        out_specs=pl.BlockSpec((tq, hw), lambda s, h, i: (s * nq + i, o_block0 + h)),
        out_shape=jax.ShapeDtypeStruct(mix.shape, mix.dtype),
        input_output_aliases={6: 0},
        scratch_shapes=[pltpu.VMEM((FOX_HEADS_PER_STEP, tq, 1), F32), pltpu.VMEM((FOX_HEADS_PER_STEP, tq, 1), F32),
                        pltpu.VMEM((FOX_HEADS_PER_STEP, tq, HD), F32)],
        compiler_params=_cparams(("parallel", "parallel", "arbitrary")),
        name="fox_flash",
    )(qn, kb, vb, c_col, c_row, p, mix)


PAGE_GROUP = 8
PAGE_COLS = PAGE * NH
NQ_PAD = SUBLANES


def _fox_sample_kernel(pt_ref, pq_ref, sm_ref, *rest, lvalid, lpad, n_steps):
    del pt_ref
    g = PAGE_GROUP
    kps, vps, lfs = rest[:g], rest[g:2 * g], rest[2 * g:3 * g]
    (upper_ref, heads_ref, qw_ref, kw_ref, bf_ref, mix_ref, o_ref, kn_ref, lf_ref,
     qn_s, cq_s, m_s, l_s, acc_s, carry_s) = rest[3 * g:]
    del mix_ref
    i = pl.program_id(1)
    scale = HD ** -0.5
    nq = NQ_PAD
    nrow = NH * nq

    @pl.when(i == 0)
    def _():
        logf = -_softplus(-(sm_ref[...] + bf_ref[...]))
        logf = jnp.where(_row_valid(0, lpad, lvalid, LANES), logf, 0.0)
        lf_ref[...] = logf[:nq]
        c = _dot_exact_lhs(_chunk_tri(lpad, lpad), logf)
        c_t = c.T
        carry_s[...] = jnp.zeros_like(carry_s)
        qi = lax.broadcasted_iota(jnp.int32, (nq, lpad), 0)
        kj = lax.broadcasted_iota(jnp.int32, (nq, lpad), 1)
        for h in range(NH):
            sl = slice(h * HD, (h + 1) * HD)
            rows = slice(h * nq, (h + 1) * nq)
            qn = _rms(pq_ref[:nq, sl], qw_ref[...])
            kn = _rms(pq_ref[:, GW + h * HD:GW + (h + 1) * HD], kw_ref[...])
            vn = pq_ref[:, 2 * GW + h * HD:2 * GW + (h + 1) * HD]
            c_h = c[:nq, SM_F + h:SM_F + h + 1]
            qn_s[rows, :] = qn
            cq_s[rows, :] = c_h
            kn_ref[:, sl] = kn[:nq]
            s = _dot_nt(qn, kn) * scale + c_h - c_t[SM_F + h:SM_F + h + 1, :]
            s = jnp.where(kj <= qi, s, -jnp.inf)
            m = jnp.max(s, axis=-1, keepdims=True)
            pr = jnp.exp(s - m)
            m_s[rows, :] = m
            l_s[rows, :] = jnp.sum(pr, axis=-1, keepdims=True)
            acc_s[rows, :] = _dot(pr, vn)

    lf = jnp.concatenate([r[...] for r in lfs], axis=0)
    within = _dot_exact_rhs(lf, upper_ref[...])
    totals = _dot_exact_rhs(lf, heads_ref[...])
    run = carry_s[...]
    suffix = [None] * g
    for j in reversed(range(g)):
        suffix[j] = within[j:j + 1] + run
        run = run + totals[j:j + 1]
    carry_s[...] = run

    row_head = lax.broadcasted_iota(jnp.int32, (nrow, PAGE_COLS), 0) // nq
    col_head = lax.broadcasted_iota(jnp.int32, (nrow, PAGE_COLS), 1) % NH
    own = row_head == col_head
    qs = qn_s[...].astype(BF16)
    bias = cq_s[...]
    tiles = [jnp.where(own, _dot_nt(qs, kps[j][...]) * scale + bias + suffix[j], -jnp.inf) for j in range(g)]
    m_old = m_s[...]
    m_new = m_old
    for t in tiles:
        m_new = jnp.maximum(m_new, jnp.max(t, axis=-1, keepdims=True))
    alpha = jnp.exp(m_old - m_new)
    l_new = alpha * l_s[...]
    acc = alpha * acc_s[...]
    for j, t in enumerate(tiles):
        pr = jnp.exp(t - m_new)
        l_new = l_new + jnp.sum(pr, axis=-1, keepdims=True)
        acc = acc + _dot(pr, vps[j][...])
    m_s[...] = m_new
    l_s[...] = l_new
    acc_s[...] = acc

    @pl.when(i == n_steps - 1)
    def _():
        o_ref[...] = jnp.zeros_like(o_ref)
        out = acc_s[...] / l_s[...]
        for h in range(NH):
            sl = slice(h * HD, (h + 1) * HD)
            gate = pq_ref[:nq, 3 * GW + h * HD:3 * GW + (h + 1) * HD]
            o_ref[:nq, sl] = (out[h * nq:(h + 1) * nq] * _sigmoid(gate)).astype(BF16)


def _fox_sample(p, mix, page_table, cache_k, cache_v, cache_lf, q_w, k_w, b_f, *, layer, row0, nseq, lpad, lvalid):
    n_pages = page_table.shape[1]
    g = PAGE_GROUP
    assert n_pages % g == 0 and lvalid <= NQ_PAD
    n_steps = n_pages // g
    bf_lane = jnp.zeros((1, LANES), F32).at[0, SM_F:SM_F + NH].set(b_f)
    idx = np.arange(PAGE_COLS)
    same_head = (idx[:, None] % NH) == (idx[None, :] % NH)
    upper = jnp.asarray(same_head & (idx[:, None] // NH > idx[None, :] // NH), BF16)
    heads = jnp.asarray(same_head, BF16)
    rb = lambda b: row0 // lpad + b
    vec = pl.BlockSpec((1, LANES), lambda b, i, pt: (0, 0))
    const = pl.BlockSpec((PAGE_COLS, PAGE_COLS), lambda b, i, pt: (0, 0))

    def page_spec(shape, j):
        zeros = (0,) * len(shape)
        return pl.BlockSpec((None, None) + shape,
                            lambda b, i, pt: (layer, pt[b, n_pages - g * (i + 1) + j]) + zeros)

    grid_spec = pltpu.PrefetchScalarGridSpec(
        num_scalar_prefetch=1,
        grid=(nseq, n_steps),
        in_specs=(
            [pl.BlockSpec((lpad, 4 * GW), lambda b, i, pt: (rb(b), CB_FOX)),
             pl.BlockSpec((lpad, LANES), lambda b, i, pt: (rb(b), CB_SMALL))]
            + [page_spec((PAGE_COLS, HD), j) for j in range(g)]
            + [page_spec((PAGE_COLS, HD), j) for j in range(g)]
            + [page_spec((1, PAGE_COLS), j) for j in range(g)]
            + [const, const, vec, vec, vec, MIX_ANY]
        ),
        out_specs=[
            pl.BlockSpec((lpad, GW), lambda b, i, pt: (rb(b), MIX_FOX)),
            pl.BlockSpec((None, NQ_PAD, GW), lambda b, i, pt: (b, 0, 0)),
            pl.BlockSpec((None, NQ_PAD, LANES), lambda b, i, pt: (b, 0, 0)),
        ],
        scratch_shapes=[
            pltpu.VMEM((NH * NQ_PAD, HD), F32),
            pltpu.VMEM((NH * NQ_PAD, 1), F32),
            pltpu.VMEM((NH * NQ_PAD, 1), F32),
            pltpu.VMEM((NH * NQ_PAD, 1), F32),
            pltpu.VMEM((NH * NQ_PAD, HD), F32),
            pltpu.VMEM((1, PAGE_COLS), F32),
        ],
    )
    return pl.pallas_call(
        functools.partial(_fox_sample_kernel, lvalid=lvalid, lpad=lpad, n_steps=n_steps),
        grid_spec=grid_spec,
        out_shape=[
            jax.ShapeDtypeStruct(mix.shape, mix.dtype),
            jax.ShapeDtypeStruct((nseq, NQ_PAD, GW), F32),
            jax.ShapeDtypeStruct((nseq, NQ_PAD, LANES), F32),
        ],
        input_output_aliases={3 * g + 8: 0},
        compiler_params=_cparams(("parallel", "arbitrary")),
        name="fox_sample",
    )(page_table, p, p, *([cache_k] * g), *([cache_v] * g), *([cache_lf] * g), upper, heads,
      q_w.reshape(1, HD), k_w.reshape(1, HD), bf_lane, mix)


W_IN_SHIFT = 8
_COPY, _SHIFT, _SMALL, _ZERO = 0, 1, 2, 3


def _w_in_plan():
    kind = np.zeros(NP_COLS // LANES, np.int32)
    src = np.zeros(NP_COLS // LANES, np.int32)

    def put(cb, n, first_src_block, k):
        kind[cb:cb + n] = k
        src[cb:cb + n] = first_src_block + np.arange(n)

    put(CB_FOX, 16, (5896 - W_IN_SHIFT) // LANES, _SHIFT)
    put(CB_RET, 16, (2056 - W_IN_SHIFT) // LANES, _SHIFT)
    put(CB_GDN, 12, 0, _COPY)
    put(CB_GDN_Z, 4, (1544 - W_IN_SHIFT) // LANES, _SHIFT)
    put(CB_RWKV, 14, (4104 - W_IN_SHIFT) // LANES, _SHIFT)
    put(CB_SMALL, 1, 1536 // LANES, _SMALL)
    put(CB_SMALL + 1, 1, 0, _ZERO)
    src_b = np.where(kind == _SHIFT, src + 1, np.where(kind == _SMALL, (7944 - W_IN_SHIFT) // LANES, src))
    return jnp.asarray(kind), jnp.asarray(src), jnp.asarray(src_b.astype(np.int32))


def _prep_w_in_kernel(kind_ref, sa_ref, sb_ref, a_ref, b_ref, o_ref):
    del sa_ref, sb_ref
    kind = kind_ref[pl.program_id(0)]
    depth = o_ref.shape[0]
    row = lax.broadcasted_iota(jnp.int32, o_ref.shape[1:], 0)

    @pl.when(kind == _COPY)
    def _():
        for l in range(depth):
            o_ref[l] = a_ref[:, l, :].astype(BF16)

    @pl.when(kind == _SHIFT)
    def _():
        for l in range(depth):
            a = a_ref[:, l, :]
            b = b_ref[:, l, :]
            o_ref[l] = jnp.concatenate([a[W_IN_SHIFT:], b[:W_IN_SHIFT]], axis=0).astype(BF16)

    @pl.when(kind == _SMALL)
    def _():
        for l in range(depth):
            small = jnp.where(row < SM_F, a_ref[:, l, :], jnp.where(row < SM_F + NH, b_ref[:, l, :], 0.0))
            o_ref[l] = small.astype(BF16)

    @pl.when(kind == _ZERO)
    def _():
        o_ref[...] = jnp.zeros_like(o_ref)


def _prep_w_in(w_in):
    depth, k, _ = w_in.shape
    w_t = jnp.transpose(w_in, (2, 0, 1))
    kind, src_a, src_b = _w_in_plan()
    grid_spec = pltpu.PrefetchScalarGridSpec(
        num_scalar_prefetch=3,
        grid=(NP_COLS // LANES,),
        in_specs=[
            pl.BlockSpec((LANES, depth, k), lambda j, kd, sa, sb: (sa[j], 0, 0)),
            pl.BlockSpec((LANES, depth, k), lambda j, kd, sa, sb: (sb[j], 0, 0)),
        ],
        out_specs=pl.BlockSpec((depth, LANES, k), lambda j, kd, sa, sb: (0, j, 0)),
    )
    return pl.pallas_call(
        _prep_w_in_kernel,
        grid_spec=grid_spec,
        out_shape=jax.ShapeDtypeStruct((depth, NP_COLS, k), BF16),
        compiler_params=_cparams(("parallel",)),
        name="prep_w_in",
    )(kind, src_a, src_b, w_t, w_t)


def _rope_tables(pos):
    half = HD // 2
    inv = 1.0 / (ROPE_BASE ** jnp.linspace(0.0, 1.0, half, dtype=F32))
    ang = pos.astype(F32)[:, None] * inv[None, :]
    cos, sin = jnp.cos(ang), jnp.sin(ang)
    return jnp.concatenate([cos, cos], axis=-1), jnp.concatenate([-sin, sin], axis=-1)


def _state_tile(state, nrows):
    b, _, c = state.shape
    return jnp.concatenate([jnp.zeros((b, SUBLANES - nrows, c), F32), state], axis=1)


def _rwkv_pair_states(s):
    b = s.shape[0]
    s = s.reshape(b, 4, 2, RWKV_HEAD, RWKV_HEAD)
    z = jnp.zeros_like(s[:, :, 0])
    top = jnp.concatenate([s[:, :, 0], z], axis=-1)
    bot = jnp.concatenate([z, s[:, :, 1]], axis=-1)
    return jnp.concatenate([top, bot], axis=-2)


def _rwkv_unpair_states(sp):
    b = sp.shape[0]
    a = sp[:, :, :RWKV_HEAD, :RWKV_HEAD]
    c = sp[:, :, RWKV_HEAD:, RWKV_HEAD:]
    return jnp.stack([a, c], axis=2).reshape(b, 8, RWKV_HEAD, RWKV_HEAD)


def kernel(x_prompt, x_sample, cache_fox_k, cache_fox_v, cache_fox_logf, cache_mem_k, cache_mem_v, state_gdn_conv, state_gdn_S, state_ret_S, state_rwkv_shift, state_rwkv_S, page_table, mem_prompt, norm_mix, w_in, gdn_conv_w, gdn_A_log, gdn_dt_bias, gdn_norm, rwkv_mu, rwkv_w0, rwkv_w_up, rwkv_a0, rwkv_a_up, rwkv_g_up, rwkv_k_k, rwkv_k_a, rwkv_r_k, rwkv_ln_w, rwkv_ln_b, fox_b_f, fox_q_norm, fox_k_norm, w_out, norm_x, norm_mem, xattn_wq, xattn_wkv, xattn_q_norm, xattn_k_norm, xattn_wo, norm_ffn, ffn_w_gate, ffn_w_up, ffn_w_down):
    weights = {
        'norm_mix': norm_mix, 'w_in': w_in, 'gdn_conv_w': gdn_conv_w, 'gdn_A_log': gdn_A_log,
        'gdn_dt_bias': gdn_dt_bias, 'gdn_norm': gdn_norm, 'rwkv_mu': rwkv_mu, 'rwkv_w0': rwkv_w0,
        'rwkv_w_up': rwkv_w_up, 'rwkv_a0': rwkv_a0, 'rwkv_a_up': rwkv_a_up, 'rwkv_g_up': rwkv_g_up,
        'rwkv_k_k': rwkv_k_k, 'rwkv_k_a': rwkv_k_a, 'rwkv_r_k': rwkv_r_k, 'rwkv_ln_w': rwkv_ln_w,
        'rwkv_ln_b': rwkv_ln_b, 'fox_b_f': fox_b_f, 'fox_q_norm': fox_q_norm, 'fox_k_norm': fox_k_norm,
        'w_out': w_out, 'norm_x': norm_x, 'norm_mem': norm_mem, 'xattn_wq': xattn_wq, 'xattn_wkv': xattn_wkv,
        'xattn_q_norm': xattn_q_norm, 'xattn_k_norm': xattn_k_norm, 'xattn_wo': xattn_wo,
        'norm_ffn': norm_ffn, 'ffn_w_gate': ffn_w_gate, 'ffn_w_up': ffn_w_up, 'ffn_w_down': ffn_w_down,
    }
    depth = w_in.shape[0]
    bp, lp, d = x_prompt.shape
    bs, ls, _ = x_sample.shape
    n_pages = page_table.shape[1]
    past_len = n_pages * PAGE
    tp = bp * lp
    ts = bs * SAMPLE_PAD
    tt = tp + ts
    tm = _row_tile(tt)
    lb_p = min(256, lp)
    assert ls >= CONV_WIDTH - 1 and ls <= SUBLANES and lp % lb_p == 0

    xs_pad = jnp.zeros((bs, SAMPLE_PAD, d), F32).at[:, :ls].set(x_sample)
    x = jnp.concatenate([x_prompt.reshape(tp, d), xs_pad.reshape(ts, d)], axis=0)

    cos_p, sin_p = _rope_tables(jnp.arange(lp, dtype=jnp.int32))
    cos_s, sin_s = _rope_tables(past_len + jnp.arange(SAMPLE_PAD, dtype=jnp.int32))
    log_gamma = jnp.log(1.0 - jnp.exp2(-(RET_GAMMA_BASE + jnp.arange(NH, dtype=F32))))
    n_pool = cache_fox_k.shape[1]
    cache_k = cache_fox_k.reshape(depth, n_pool, PAGE_COLS, HD)
    cache_v = cache_fox_v.reshape(depth, n_pool, PAGE_COLS, HD)
    cache_lf = cache_fox_logf.reshape(depth, n_pool, 1, PAGE_COLS)
    zeros_s = jnp.zeros((bp, NH, HD, HD), F32)
    zeros_conv = jnp.zeros((bp, SUBLANES, 3 * GW), F32)
    zeros_shift = jnp.zeros((bp, SUBLANES, 1792), F32)

    w_in_t = _prep_w_in(w_in)

    outs_p, outs_s, mem_ks, mem_vs = [], [], [], []
    for l in range(depth):
        lw = {name: arr[l] for name, arr in weights.items()}
        p = _norm_matmul(x, lw['norm_mix'], w_in_t, l, tm=tm, tn=1024,
                         transposed_w=True)

        o_mix = jnp.zeros((tt, 4 * GW), BF16)
        gp = dict(row0=0, nseq=bp, lpad=lp, lvalid=lp, lb=lb_p)
        o_mix, gdn_s_p = _gdn(p, o_mix, lw['gdn_conv_w'], zeros_conv, zeros_s, lw['gdn_A_log'], lw['gdn_dt_bias'],
                              lw['gdn_norm'], **{**gp, 'lb': min(GDN_LB, lp)})
        o_mix, ret_s_p = _ret(p, o_mix, cos_p, sin_p, log_gamma, zeros_s, **{**gp, 'lb': min(RET_LB, lp)})
        o_mix, rwkv_s_p = _rwkv(p, o_mix, lw, zeros_shift, zeros_s, **{**gp, 'lb': min(RWKV_LB, lp)})
        qn, kn, kb, vb, lf, c_col, c_row = _fox_prep(p, lw['fox_q_norm'], lw['fox_k_norm'], lw['fox_b_f'],
                                                    nseq=bp, lpad=lp, lb=lb_p)
        o_mix = _fox_flash(qn, kb, vb, c_col, c_row, p, o_mix, nseq=bp, lpad=lp, tq=min(FOX_TQ, lp))

        gs = dict(row0=tp, nseq=bs, lpad=SAMPLE_PAD, lvalid=ls, lb=SAMPLE_PAD)
        o_mix, gdn_s_s = _gdn(p, o_mix, lw['gdn_conv_w'], _state_tile(state_gdn_conv[l], CONV_WIDTH - 1),
                              state_gdn_S[l], lw['gdn_A_log'], lw['gdn_dt_bias'], lw['gdn_norm'], **gs)
        o_mix, ret_s_s = _ret(p, o_mix, cos_s, sin_s, log_gamma, state_ret_S[l], **gs)
        o_mix, rwkv_s_s = _rwkv(p, o_mix, lw, _state_tile(state_rwkv_shift[l], 1),
                                _rwkv_pair_states(state_rwkv_S[l]), **gs)
        o_mix, kn_s, lf_s = _fox_sample(p, o_mix, page_table, cache_k, cache_v, cache_lf,
                                        lw['fox_q_norm'], lw['fox_k_norm'], lw['fox_b_f'],
                                        layer=l, row0=tp, nseq=bs, lpad=SAMPLE_PAD, lvalid=ls)
        x = _matmul_res(o_mix, w_out, l, x, tm=tm, tn=512)

        kv = _norm_matmul(mem_prompt.reshape(bp * N_MEM, d), lw['norm_mem'], xattn_wkv, l,
                          tm=256, tn=XW, head_w=lw['xattn_k_norm'], norm_tiles=1)
        mk = kv[:, :XW].reshape(bp, N_MEM, XW)
        mv = kv[:, XW:].reshape(bp, N_MEM, XW)
        q = _norm_matmul(x, lw['norm_x'], xattn_wq, l, tm=tm, tn=XW,
                         head_w=lw['xattn_q_norm'], norm_tiles=1)
        xo_p = _xattn(q, mk, mv, row0=0, nrows=tp, tq=lb_p, rows_per_seq=lp)
        xo_s = _xattn(q, cache_mem_k[l].reshape(bs, N_MEM, XW), cache_mem_v[l].reshape(bs, N_MEM, XW),
                      row0=tp, nrows=ts, tq=SAMPLE_PAD, rows_per_seq=SAMPLE_PAD)
        x = _matmul_res(jnp.concatenate([xo_p, xo_s], axis=0), xattn_wo, l, x, tm=tm, tn=512)

        hidden = _swiglu_up(x, lw['norm_ffn'], ffn_w_gate, ffn_w_up, l, tm=tm, tn=512)
        x = _matmul_res(hidden, ffn_w_down, l, x, tm=tm, tn=256)

        c0 = CB_GDN * LANES
        r0 = CB_RWKV * LANES
        v0 = (CB_FOX + 2 * NH) * LANES

        def last_rows(row_end, n, col0, width, nseq, stride):
            return jnp.stack([lax.slice(p, (b * stride + row_end - n, col0), (b * stride + row_end, col0 + width))
                              for b in range(nseq)], axis=0)

        ps_v = lax.slice(p, (tp, v0), (tt, v0 + GW)).reshape(bs, SAMPLE_PAD, NH, HD)
        outs_p.append((
            kn.reshape(bp, lp, NH, HD),
            lax.slice(p, (0, v0), (tp, v0 + GW)).reshape(bp, lp, NH, HD),
            lf.reshape(bp, lp, LANES)[:, :, SM_F:SM_F + NH],
            last_rows(lp, CONV_WIDTH - 1, c0, 3 * GW, bp, lp),
            gdn_s_p, ret_s_p,
            last_rows(lp, 1, r0, 1792, bp, lp),
            _rwkv_unpair_states(rwkv_s_p),
        ))
        outs_s.append((
            kn_s[:, :ls].reshape(bs, ls, NH, HD),
            ps_v[:, :ls],
            lf_s[:, :ls, SM_F:SM_F + NH],
            last_rows(tp + ls, CONV_WIDTH - 1, c0, 3 * GW, bs, SAMPLE_PAD),
            gdn_s_s, ret_s_s,
            last_rows(tp + ls, 1, r0, 1792, bs, SAMPLE_PAD),
            _rwkv_unpair_states(rwkv_s_s),
        ))
        mem_ks.append(mk.reshape(bp, N_MEM, NH, HD))
        mem_vs.append(mv.reshape(bp, N_MEM, NH, HD))

    stk = lambda seq, i: jnp.stack([e[i] for e in seq], axis=0)
    yp = x[:tp].reshape(bp, lp, d)
    ys = x[tp:].reshape(bs, SAMPLE_PAD, d)[:, :ls]
    return (yp, ys, stk(outs_p, 0), stk(outs_p, 1), stk(outs_p, 2), jnp.stack(mem_ks, 0), jnp.stack(mem_vs, 0),
            stk(outs_p, 3), stk(outs_p, 4), stk(outs_p, 5), stk(outs_p, 6), stk(outs_p, 7),
            stk(outs_s, 0), stk(outs_s, 1), stk(outs_s, 2), stk(outs_s, 3), stk(outs_s, 4), stk(outs_s, 5),
            stk(outs_s, 6), stk(outs_s, 7))
```

```python
import functools
import math

import jax
import jax.numpy as jnp
import numpy as np
from jax import lax
from jax.experimental import pallas as pl
from jax.experimental.pallas import tpu as pltpu

F32 = jnp.float32
BF16 = jnp.bfloat16

LANES = 128
SUBLANES = 8
VMEM_LIMIT = 56 * 1024 * 1024

D_MODEL = 2048
GW = D_MODEL // 4
HD = 128
NH = GW // HD
RWKV_HEAD = 64
CONV_WIDTH = 4
PAGE = 128
N_MEM = 256
XW = 512
D_FF = 5632
NORM_EPS = 1e-6
GN_EPS = 64e-5
RET_GAMMA_BASE = 5.0
ROPE_BASE = 10000.0
CHUNK = 64
SAMPLE_PAD = 64
FOX_TQ = 512
FOX_HEADS_PER_STEP = 2
GDN_LB = 512
RWKV_LB = 512
RET_LB = 512

NP_COLS = 8192
CB_FOX, CB_RET, CB_GDN, CB_GDN_Z, CB_RWKV, CB_SMALL = 0, 16, 32, 44, 48, 62
SM_A, SM_B, SM_F = 0, 4, 8


def _cparams(sem):
    return pltpu.CompilerParams(dimension_semantics=sem, vmem_limit_bytes=VMEM_LIMIT)


def _dot(a, b):
    return jnp.dot(a.astype(BF16), b.astype(BF16), preferred_element_type=F32)


def _dot_nt(a, b):
    return lax.dot_general(a.astype(BF16), b.astype(BF16), (((1,), (1,)), ((), ())),
                           preferred_element_type=F32)


def _dot_tn(a, b):
    return lax.dot_general(a.astype(BF16), b.astype(BF16), (((0,), (0,)), ((), ())),
                           preferred_element_type=F32)


def _split3(x):
    hi = x.astype(BF16)
    r = x - hi.astype(F32)
    mid = r.astype(BF16)
    lo = (r - mid.astype(F32)).astype(BF16)
    return hi, mid, lo


def _dot_exact_lhs(m, x):
    hi, mid, lo = _split3(x)
    d = lambda p: jnp.dot(m, p, preferred_element_type=F32)
    return d(hi) + d(mid) + d(lo)


def _dot_exact_rhs(x, m):
    hi, mid, lo = _split3(x)
    d = lambda p: jnp.dot(p, m, preferred_element_type=F32)
    return d(hi) + d(mid) + d(lo)


def _bmm(a, b):
    return lax.dot_general(a.astype(BF16), b.astype(BF16), (((2,), (1,)), ((0,), (0,))),
                           preferred_element_type=F32)


def _bmm_nt(a, b):
    return lax.dot_general(a.astype(BF16), b.astype(BF16), (((2,), (2,)), ((0,), (0,))),
                           preferred_element_type=F32)


def _unit_lower_inv(n, nil):
    c = n.shape[-1]
    ii = lax.broadcasted_iota(jnp.int32, (c, c), 0)
    jj = lax.broadcasted_iota(jnp.int32, (c, c), 1)
    p = jnp.where(ii == jj, 1.0, 0.0).astype(F32) - n
    q = n
    for _ in range(int(math.log2(nil)) - 1):
        q = _bmm(q, q)
        p = p + _bmm(p, q)
    return p


def _chunk_tri(lb, chunk):
    ii = lax.broadcasted_iota(jnp.int32, (lb, lb), 0)
    jj = lax.broadcasted_iota(jnp.int32, (lb, lb), 1)
    same = (ii // chunk) == (jj // chunk)
    return jnp.where(jnp.logical_and(ii >= jj, same), 1.0, 0.0).astype(BF16)


def _shift_rows(x, prev, s):
    nsb = prev.shape[0]
    seq_rows = x.shape[0] // nsb
    rolled = pltpu.roll(x, s, axis=0)
    row = lax.broadcasted_iota(jnp.int32, (SUBLANES, x.shape[1]), 0)
    pieces = []
    for b in range(nsb):
        piece = rolled[b * seq_rows:(b + 1) * seq_rows]
        top = jnp.where(row < s, pltpu.roll(prev[b], s, axis=0), piece[:SUBLANES])
        pieces += [top, piece[SUBLANES:]]
    return jnp.concatenate(pieces, axis=0)


def _last_tiles(x, nsb):
    seq_rows = x.shape[0] // nsb
    return x.reshape(nsb, seq_rows, x.shape[1])[:, seq_rows - SUBLANES:, :]


def _softplus(x):
    return jnp.maximum(x, 0.0) + jnp.log1p(jnp.exp(-jnp.abs(x)))


def _sigmoid(x):
    return jax.nn.sigmoid(x)


def _silu(x):
    return x * jax.nn.sigmoid(x)


def _lane_col(x, idx):
    lane = lax.broadcasted_iota(jnp.int32, x.shape, 1)
    return jnp.sum(jnp.where(lane == idx, x, 0.0), axis=-1, keepdims=True)


def _rms(x, w=None):
    y = x * lax.rsqrt(jnp.mean(x * x, axis=-1, keepdims=True) + NORM_EPS)
    return y if w is None else y * w


def _row_valid(lb_index, lb, lvalid, width, nsb=1):
    row = lax.broadcasted_iota(jnp.int32, (lb, width), 0) % (lb // nsb) + lb_index * lb
    return row < lvalid


def _norm_matmul_kernel(x_ref, g_ref, w_ref, hw_ref, o_ref, xn_ref, *, norm_tiles, transposed_w):
    j = pl.program_id(1)

    @pl.when(j == 0)
    def _():
        xn_ref[...] = _rms(x_ref[...], g_ref[...]).astype(BF16)

    if transposed_w:
        acc = _dot_nt(xn_ref[...], w_ref[...])
    else:
        acc = jnp.dot(xn_ref[...], w_ref[...].astype(BF16), preferred_element_type=F32)
    if norm_tiles == 0:
        o_ref[...] = acc
    else:
        @pl.when(j < norm_tiles)
        def _():
            hw = hw_ref[...]
            for h in range(acc.shape[1] // HD):
                sl = slice(h * HD, (h + 1) * HD)
                o_ref[:, sl] = _rms(acc[:, sl], hw)

        @pl.when(j >= norm_tiles)
        def _():
            o_ref[...] = acc


def _row_tile(m, cap=1100):
    return next(t for t in range(cap - cap % 16, 0, -16) if m % t == 0)


def _norm_matmul(x, g, w, layer, *, tm, tn, head_w=None, norm_tiles=0, transposed_w=False):
    m, k = x.shape
    n = w.shape[1] if transposed_w else w.shape[2]
    if head_w is None:
        head_w = jnp.ones((HD,), F32)
    if transposed_w:
        w_spec = pl.BlockSpec((None, tn, k), lambda i, j: (layer, j, 0))
    else:
        w_spec = pl.BlockSpec((None, k, tn), lambda i, j: (layer, 0, j))
    return pl.pallas_call(
        functools.partial(_norm_matmul_kernel, norm_tiles=norm_tiles, transposed_w=transposed_w),
        grid=(m // tm, n // tn),
        in_specs=[
            pl.BlockSpec((tm, k), lambda i, j: (i, 0)),
            pl.BlockSpec((1, k), lambda i, j: (0, 0)),
            w_spec,
            pl.BlockSpec((1, HD), lambda i, j: (0, 0)),
        ],
        out_specs=pl.BlockSpec((tm, tn), lambda i, j: (i, j)),
        out_shape=jax.ShapeDtypeStruct((m, n), F32),
        scratch_shapes=[pltpu.VMEM((tm, k), BF16)],
        compiler_params=_cparams(("parallel", "arbitrary")),
        name="norm_matmul",
    )(x, g.reshape(1, k), w, head_w.reshape(1, HD))


def _matmul_res_kernel(a_ref, w_ref, r_ref, o_ref):
    o_ref[...] = r_ref[...] + jnp.dot(a_ref[...], w_ref[...].astype(BF16), preferred_element_type=F32)


def _matmul_res(a, w, layer, res, *, tm, tn):
    m, k = a.shape
    n = w.shape[2]
    return pl.pallas_call(
        _matmul_res_kernel,
        grid=(m // tm, n // tn),
        in_specs=[
            pl.BlockSpec((tm, k), lambda i, j: (i, 0)),
            pl.BlockSpec((None, k, tn), lambda i, j: (layer, 0, j)),
            pl.BlockSpec((tm, tn), lambda i, j: (i, j)),
        ],
        out_specs=pl.BlockSpec((tm, tn), lambda i, j: (i, j)),
        out_shape=jax.ShapeDtypeStruct((m, n), F32),
        compiler_params=_cparams(("parallel", "parallel")),
        name="matmul_res",
    )(a, w, res)


def _swiglu_up_kernel(x_ref, g_ref, wg_ref, wu_ref, o_ref, xn_ref):
    @pl.when(pl.program_id(1) == 0)
    def _():
        xn_ref[...] = _rms(x_ref[...], g_ref[...]).astype(BF16)

    xn = xn_ref[...]
    gate = jnp.dot(xn, wg_ref[...].astype(BF16), preferred_element_type=F32)
    up = jnp.dot(xn, wu_ref[...].astype(BF16), preferred_element_type=F32)
    o_ref[...] = (_silu(gate) * up).astype(BF16)


def _swiglu_up(x, g, wg, wu, layer, *, tm, tn):
    m, k = x.shape
    n = wg.shape[2]
    return pl.pallas_call(
        _swiglu_up_kernel,
        grid=(m // tm, n // tn),
        in_specs=[
            pl.BlockSpec((tm, k), lambda i, j: (i, 0)),
            pl.BlockSpec((1, k), lambda i, j: (0, 0)),
            pl.BlockSpec((None, k, tn), lambda i, j: (layer, 0, j)),
            pl.BlockSpec((None, k, tn), lambda i, j: (layer, 0, j)),
        ],
        out_specs=pl.BlockSpec((tm, tn), lambda i, j: (i, j)),
        out_shape=jax.ShapeDtypeStruct((m, n), BF16),
        scratch_shapes=[pltpu.VMEM((tm, k), BF16)],
        compiler_params=_cparams(("parallel", "arbitrary")),
        name="swiglu_up",
    )(x, g.reshape(1, k), wg, wu)


def _xattn_kernel(q_ref, k_ref, v_ref, o_ref):
    scale = HD ** -0.5
    for h in range(NH):
        sl = slice(h * HD, (h + 1) * HD)
        s = _dot_nt(q_ref[:, sl], k_ref[:, sl]) * scale
        m = jnp.max(s, axis=-1, keepdims=True)
        p = jnp.exp(s - m)
        l = jnp.sum(p, axis=-1, keepdims=True)
        o_ref[:, sl] = (_dot(p, v_ref[:, sl]) / l).astype(BF16)


def _xattn(q, mem_k, mem_v, *, row0, nrows, tq, rows_per_seq):
    tiles_per_seq = rows_per_seq // tq
    t0 = row0 // tq
    return pl.pallas_call(
        _xattn_kernel,
        grid=(nrows // tq,),
        in_specs=[
            pl.BlockSpec((tq, XW), lambda i: (t0 + i, 0)),
            pl.BlockSpec((None, N_MEM, XW), lambda i: (i // tiles_per_seq, 0, 0)),
            pl.BlockSpec((None, N_MEM, XW), lambda i: (i // tiles_per_seq, 0, 0)),
        ],
        out_specs=pl.BlockSpec((tq, XW), lambda i: (i, 0)),
        out_shape=jax.ShapeDtypeStruct((nrows, XW), BF16),
        compiler_params=_cparams(("parallel",)),
        name="xattn",
    )(q, mem_k, mem_v)


def _gdn_kernel(q_ref, k_ref, v_ref, z_ref, sm_ref, cwq_ref, cwk_ref, cwv_ref, cq_ref, ck_ref, cv_ref,
                s0_ref, alog_ref, dtb_ref, nw_ref, mix_ref, o_ref, so_ref, s_ref, prev_ref, gt_ref,
                *, lb, lvalid, nlb, nsb):
    del mix_ref
    h = pl.program_id(1)
    ib = pl.program_id(2)
    masked = lvalid < nlb * lb // nsb

    @pl.when(ib == 0)
    def _():
        s_ref[...] = s0_ref[...]
        prev_ref[0] = cq_ref[...]
        prev_ref[1] = ck_ref[...]
        prev_ref[2] = cv_ref[...]

    def conv(x_ref, w_ref, i):
        x = x_ref[...]
        w = w_ref[...]
        prev = prev_ref[i]
        y = x * w[3:4]
        for s in (1, 2, 3):
            y = y + _shift_rows(x, prev, s) * w[3 - s:4 - s]
        prev_ref[i] = _last_tiles(x, nsb)
        return _silu(y)

    q = conv(q_ref, cwq_ref, 0)
    k = conv(k_ref, cwk_ref, 1)
    v = conv(v_ref, cwv_ref, 2)
    q = q * lax.rsqrt(jnp.sum(q * q, axis=-1, keepdims=True) + NORM_EPS) * (HD ** -0.5)
    k = k * lax.rsqrt(jnp.sum(k * k, axis=-1, keepdims=True) + NORM_EPS)

    sm = sm_ref[...]
    g_blk = -jnp.exp(alog_ref[...]) * _softplus(sm + dtb_ref[...])
    beta_blk = _sigmoid(sm)
    if masked:
        valid = _row_valid(ib, lb, lvalid, LANES, nsb)
        g_blk = jnp.where(valid, g_blk, 0.0)
        beta_blk = jnp.where(valid, beta_blk, 0.0)
    gc_blk = _dot_exact_lhs(_chunk_tri(lb, CHUNK), g_blk)
    gt_ref[...] = gc_blk.T
    g_col_all = _lane_col(gc_blk, SM_A + h)
    beta_all = _lane_col(beta_blk, SM_B + h)
    g_row_all = gt_ref[pl.ds(SM_A + h, 1), :]

    group = min(2 * CHUNK, lb)
    ng = lb // group
    nchunk = lb // CHUNK
    ii = lax.broadcasted_iota(jnp.int32, (group, group), 0)
    jj = lax.broadcasted_iota(jnp.int32, (group, group), 1)
    lower = jnp.logical_and(ii >= jj, (ii // CHUNK) == (jj // CHUNK))
    to3 = lambda x: x.reshape(ng, group, x.shape[-1])
    q3, k3, v3 = to3(q), to3(k), to3(v)
    g_col3 = to3(g_col_all)
    beta3 = to3(beta_all)
    g_row3 = jnp.stack([g_row_all[:, i * group:(i + 1) * group] for i in range(ng)], axis=0)
    dec3 = jnp.exp(jnp.where(lower, g_col3 - g_row3, -jnp.inf))
    scores = _bmm_nt(jnp.concatenate([k3, q3], axis=1), k3)
    n3 = jnp.where(ii > jj, beta3 * scores[:, :group] * dec3, 0.0)
    qk3 = scores[:, group:] * dec3
    ainv3 = _unit_lower_inv(n3, CHUNK)
    eg3 = jnp.exp(g_col3)
    wuv3 = _bmm(ainv3, jnp.concatenate([beta3 * eg3 * k3, beta3 * v3], axis=2))
    qk_wuv = _bmm(qk3, wuv3)
    qp = (q3 * eg3 - qk_wuv[:, :, :HD]).reshape(lb, HD)
    op = qk_wuv[:, :, HD:].reshape(lb, HD)
    wuv_all = wuv3.reshape(lb, 2 * HD)
    g_chunks = g_col_all.reshape(nchunk, CHUNK, 1)
    g_end = jnp.broadcast_to(g_chunks[:, CHUNK - 1:, :], g_chunks.shape).reshape(lb, 1)
    kd = k * jnp.exp(g_end - g_col_all)
    s_decay = jnp.exp(g_end)
    nw = nw_ref[...]
    chunks_per_seq = nchunk // nsb
    for c in range(nchunk):
        r = slice(c * CHUNK, (c + 1) * CHUNK)
        b = c // chunks_per_seq
        an = _dot_tn(kd[r], wuv_all[r])
        s = s_ref[b]
        o = _dot(qp[r], s) + op[r]
        s_ref[b] = s * s_decay[c * CHUNK:c * CHUNK + 1] - _dot(an[:, :HD], s) + an[:, HD:]
        o_ref[r, :] = (_rms(o, nw) * _silu(z_ref[r, :])).astype(BF16)

    @pl.when(ib == nlb - 1)
    def _():
        so_ref[...] = s_ref[...]


MIX_ANY = pl.BlockSpec(memory_space=pl.ANY)
MIX_GDN, MIX_RET, MIX_RWKV, MIX_FOX = 0, 1, 2, 3


def _seq_blocking(row0, nseq, lpad, lb, nsb):
    assert (nsb == 1 and lpad % lb == 0) or (lb == nsb * lpad and nseq % nsb == 0)
    nlb = lpad * nsb // lb
    return nseq // nsb, nlb, lambda s, i: (row0 + s * lpad * nsb) // lb + i


def _gdn(p, mix, conv_w, conv_init, s0, a_log, dt_bias, norm_w, *, row0, nseq, lpad, lvalid, lb, nsb=1):
    ngroup, nlb, rb = _seq_blocking(row0, nseq, lpad, lb, nsb)
    pblk = lambda cb: pl.BlockSpec((lb, HD), lambda s, h, i: (rb(s, i), cb + h))
    cwblk = lambda j: pl.BlockSpec((CONV_WIDTH, HD), lambda s, h, i: (0, j * NH + h))
    ciblk = lambda j: pl.BlockSpec((nsb, SUBLANES, HD), lambda s, h, i: (s, 0, j * NH + h))
    sblk = pl.BlockSpec((nsb, None, HD, HD), lambda s, h, i: (s, h, 0, 0))
    vec = pl.BlockSpec((1, LANES), lambda s, h, i: (0, 0))
    lane_pad = lambda x: jnp.zeros((1, LANES), F32).at[0, :x.shape[0]].set(x)
    return pl.pallas_call(
        functools.partial(_gdn_kernel, lb=lb, lvalid=lvalid, nlb=nlb, nsb=nsb),
        grid=(ngroup, NH, nlb),
        in_specs=[
            pblk(CB_GDN), pblk(CB_GDN + NH), pblk(CB_GDN + 2 * NH), pblk(CB_GDN_Z),
            pl.BlockSpec((lb, LANES), lambda s, h, i: (rb(s, i), CB_SMALL)),
            cwblk(0), cwblk(1), cwblk(2), ciblk(0), ciblk(1), ciblk(2),
            sblk,
            vec, vec, vec, MIX_ANY,
        ],
        out_specs=[
            pl.BlockSpec((lb, HD), lambda s, h, i: (rb(s, i), MIX_GDN * NH + h)),
            sblk,
        ],
        out_shape=[
            jax.ShapeDtypeStruct(mix.shape, mix.dtype),
            jax.ShapeDtypeStruct((nseq, NH, HD, HD), F32),
        ],
        input_output_aliases={15: 0},
        scratch_shapes=[
            pltpu.VMEM((nsb, HD, HD), F32),
            pltpu.VMEM((3, nsb, SUBLANES, HD), F32),
            pltpu.VMEM((LANES, lb), F32),
        ],
        compiler_params=_cparams(("parallel", "parallel", "arbitrary")),
        name="gdn",
    )(p, p, p, p, p, conv_w, conv_w, conv_w, conv_init, conv_init, conv_init, s0,
      lane_pad(a_log), lane_pad(dt_bias), norm_w.reshape(1, HD), mix)


def _ret_kernel(q_ref, k_ref, v_ref, g_ref, cos_ref, sin_ref, lg_ref, s0_ref, mix_ref, o_ref, so_ref, s_ref,
                *, lb, cv, nlb, nsb):
    del mix_ref
    ib = pl.program_id(2)

    @pl.when(ib == 0)
    def _():
        s_ref[...] = s0_ref[...]

    cos = cos_ref[...]
    sin = sin_ref[...]
    rot = lambda x: x * cos + pltpu.roll(x, HD // 2, axis=1) * sin
    q = rot(q_ref[...])
    k = rot(k_ref[...]) * (HD ** -0.5)
    v = v_ref[...]
    lg = lg_ref[...][:, 0:1]

    ii = lax.broadcasted_iota(jnp.int32, (CHUNK, CHUNK), 0)
    jj = lax.broadcasted_iota(jnp.int32, (CHUNK, CHUNK), 1)
    rel = (ii - jj).astype(F32)
    dmat = jnp.where(rel >= 0, jnp.exp(jnp.maximum(rel, 0.0) * lg), 0.0)
    idx = lax.broadcasted_iota(jnp.int32, (CHUNK, 1), 0)
    idf = idx.astype(F32)
    xi = jnp.exp((idf + 1.0) * lg)
    zeta = jnp.where(idx < cv, jnp.exp((cv - 1.0 - idf) * lg), 0.0)
    gc = jnp.exp(cv * lg)
    nchunk = lb // CHUNK
    to3 = lambda x: x.reshape(nchunk, CHUNK, HD)
    q3, k3, v3 = to3(q), to3(k), to3(v)
    o_intra = _bmm(_bmm_nt(q3, k3) * dmat, v3)
    qx = q3 * xi
    kz = k3 * zeta
    chunks_per_seq = nchunk // nsb
    for b in range(nsb):
        s = s_ref[b]
        for c in range(b * chunks_per_seq, (b + 1) * chunks_per_seq):
            r = slice(c * CHUNK, (c + 1) * CHUNK)
            o = o_intra[c] + _dot(qx[c], s)
            s = s * gc + _dot_tn(kz[c], v3[c])
            o_ref[r, :] = (_rms(o) * _silu(g_ref[r, :])).astype(BF16)
        s_ref[b] = s

    @pl.when(ib == nlb - 1)
    def _():
        so_ref[...] = s_ref[...]


def _ret(p, mix, cos_t, sin_t, log_gamma, s0, *, row0, nseq, lpad, lvalid, lb, nsb=1):
    ngroup, nlb, rb = _seq_blocking(row0, nseq, lpad, lb, nsb)
    cv = CHUNK if lvalid == lpad else lvalid
    assert cv == CHUNK or (lpad == CHUNK and 0 < lvalid < CHUNK)
    pblk = lambda cb: pl.BlockSpec((lb, HD), lambda s, h, i: (rb(s, i), cb + h))
    tblk = pl.BlockSpec((lb, HD), lambda s, h, i: (i, 0))
    sblk = pl.BlockSpec((nsb, None, HD, HD), lambda s, h, i: (s, h, 0, 0))
    lg = jnp.broadcast_to(log_gamma[:, None, None], (NH, 1, LANES))
    return pl.pallas_call(
        functools.partial(_ret_kernel, lb=lb, cv=cv, nlb=nlb, nsb=nsb),
        grid=(ngroup, NH, nlb),
        in_specs=[
            pblk(CB_RET), pblk(CB_RET + NH), pblk(CB_RET + 2 * NH), pblk(CB_RET + 3 * NH),
            tblk, tblk,
            pl.BlockSpec((None, 1, LANES), lambda s, h, i: (h, 0, 0)),
            sblk, MIX_ANY,
        ],
        out_specs=[pl.BlockSpec((lb, HD), lambda s, h, i: (rb(s, i), MIX_RET * NH + h)), sblk],
        out_shape=[
            jax.ShapeDtypeStruct(mix.shape, mix.dtype),
            jax.ShapeDtypeStruct((nseq, NH, HD, HD), F32),
        ],
        input_output_aliases={8: 0},
        scratch_shapes=[pltpu.VMEM((nsb, HD, HD), F32)],
        compiler_params=_cparams(("parallel", "parallel", "arbitrary")),
        name="retention",
    )(p, p, p, p, cos_t, sin_t, lg, s0, mix)


def _half_sum(x):
    lane = lax.broadcasted_iota(jnp.int32, x.shape, 1)
    lo = lane < RWKV_HEAD
    s_lo = jnp.sum(jnp.where(lo, x, 0.0), axis=-1, keepdims=True)
    s_hi = jnp.sum(jnp.where(lo, 0.0, x), axis=-1, keepdims=True)
    return jnp.where(lo, s_lo, s_hi)


def _rwkv_kernel(r_ref, k_ref, v_ref, wa_ref, gd_ref, mur_ref, muk_ref, muv_ref, muwa_ref, mugd_ref,
                 sh_r_ref, sh_k_ref, sh_v_ref, sh_wa_ref, sh_gd_ref,
                 wup_ref, aup_ref, gup_ref, w0_ref, a0_ref, kk_ref, ka_ref, rk_ref, lnw_ref, lnb_ref,
                 s0_ref, mix_ref, o_ref, so_ref, s_ref, prev_ref, *, lb, lvalid, nlb, nsb):
    del mix_ref
    ib = pl.program_id(2)
    masked = lvalid < nlb * lb // nsb

    @pl.when(ib == 0)
    def _():
        s_ref[...] = s0_ref[...]
        prev_ref[0] = sh_r_ref[...]
        prev_ref[1] = sh_k_ref[...]
        prev_ref[2] = sh_v_ref[...]
        prev_ref[3] = sh_wa_ref[...]
        prev_ref[4] = sh_gd_ref[...]

    def shifted(x_ref, mu_ref, i):
        x = x_ref[...]
        prev = _shift_rows(x, prev_ref[i], 1)
        prev_ref[i] = _last_tiles(x, nsb)
        return x + (prev - x) * mu_ref[...]

    r = shifted(r_ref, mur_ref, 0)
    k = shifted(k_ref, muk_ref, 1)
    v = shifted(v_ref, muv_ref, 2)
    wa = shifted(wa_ref, muwa_ref, 3)
    gd = shifted(gd_ref, mugd_ref, 4)

    w_raw = -_softplus(-(w0_ref[...] + _dot(jnp.tanh(wa), wup_ref[...]))) - 0.5
    logw = -jnp.exp(w_raw)
    a_sig = _sigmoid(a0_ref[...] + _dot(wa, aup_ref[...]))
    gate = _dot(_sigmoid(gd), gup_ref[...])
    kk = k * kk_ref[...]
    kk = kk * lax.rsqrt(_half_sum(kk * kk) + NORM_EPS)
    kp = k * (1.0 + (a_sig - 1.0) * ka_ref[...])
    rec_a = -kk
    rec_b = kk * a_sig
    if masked:
        valid = _row_valid(ib, lb, lvalid, LANES, nsb)
        zero = lambda x: jnp.where(valid, x, 0.0)
        logw, rec_a, rec_b, kp, v = zero(logw), zero(rec_a), zero(rec_b), zero(kp), zero(v)

    cum = _dot_exact_lhs(_chunk_tri(lb, CHUNK), logw)
    e_pos = jnp.exp(cum)
    e_neg = jnp.exp(-cum)
    at_all = rec_a * jnp.exp(cum - logw)
    bt_all = rec_b * e_neg
    kt_all = kp * e_neg
    rt_all = r * e_pos

    stacked = 2 * CHUNK
    row_head = lax.broadcasted_iota(jnp.int32, (stacked, LANES), 0) // CHUNK
    lane_head = lax.broadcasted_iota(jnp.int32, (stacked, LANES), 1) // RWKV_HEAD
    own = row_head == lane_head
    nchunk = lb // CHUNK
    dup3 = lambda x: jnp.concatenate([x.reshape(nchunk, CHUNK, LANES)] * 2, axis=1)
    stack3 = lambda x: jnp.where(own, dup3(x), 0.0)
    block_diag = (lax.broadcasted_iota(jnp.int32, (LANES, LANES), 0) // RWKV_HEAD) == (
        lax.broadcasted_iota(jnp.int32, (LANES, LANES), 1) // RWKV_HEAD)
    ti = lax.broadcasted_iota(jnp.int32, (stacked, stacked), 0) % CHUNK
    tj = lax.broadcasted_iota(jnp.int32, (stacked, stacked), 1) % CHUNK

    at3, bt3, kt3, rt3, v3 = stack3(at_all), stack3(bt_all), stack3(kt_all), stack3(rt_all), dup3(v)
    scores = _bmm_nt(jnp.concatenate([at3, rt3], axis=1), jnp.concatenate([bt3, kt3], axis=1))
    l_ab = jnp.where(ti > tj, scores[:, :stacked, :stacked], 0.0)
    l_ak = jnp.where(ti > tj, scores[:, :stacked, stacked:], 0.0)
    l_rb = jnp.where(ti >= tj, scores[:, stacked:, :stacked], 0.0)
    l_rk = jnp.where(ti >= tj, scores[:, stacked:, stacked:], 0.0)
    inv = _unit_lower_inv(-l_ab, CHUNK)
    t12 = _bmm(inv, jnp.concatenate([at3, _bmm(l_ak, v3)], axis=2))
    t1 = t12[:, :, :LANES]
    t2 = jnp.where(own, t12[:, :, LANES:], 0.0)
    rb12 = _bmm(l_rb, jnp.concatenate([t1, t2], axis=2))
    rp = rt3 + rb12[:, :, :LANES]
    op = jnp.where(own, rb12[:, :, LANES:] + _bmm(l_rk, v3), 0.0)
    cum3 = cum.reshape(nchunk, CHUNK, LANES)
    cum_end = jnp.broadcast_to(cum3[:, CHUNK - 1:, :], cum3.shape).reshape(lb, LANES)
    to_end = jnp.exp(cum_end - cum)
    b_end = rec_b * to_end
    k_end = kp * to_end
    s_decay = jnp.exp(cum_end)

    chunks_per_seq = nchunk // nsb
    for c in range(nchunk):
        rs = slice(c * CHUNK, (c + 1) * CHUNK)
        b = c // chunks_per_seq
        vc = v[rs]
        b2 = jnp.concatenate([b_end[rs], b_end[rs]], axis=0)
        gh = _dot_tn(jnp.concatenate([t1[c], t2[c]], axis=1), b2)
        h = gh[LANES:] + _dot_tn(vc, k_end[rs])
        s = s_ref[b]
        o2 = jnp.where(own, _dot_nt(rp[c], s), 0.0) + op[c]
        o = o2[:CHUNK] + o2[CHUNK:]
        s_ref[b] = s * s_decay[c * CHUNK:c * CHUNK + 1] + jnp.where(block_diag, _dot(s, gh[:LANES]) + h, 0.0)

        inv_n = 1.0 / RWKV_HEAD
        mu = _half_sum(o) * inv_n
        var = _half_sum(jnp.square(o - mu)) * inv_n
        on = (o - mu) * lax.rsqrt(var + GN_EPS) * lnw_ref[...] + lnb_ref[...]
        bonus = _half_sum(r[rs] * kp[rs] * rk_ref[...]) * vc
        o_ref[rs, :] = ((on + bonus) * gate[rs]).astype(BF16)

    @pl.when(ib == nlb - 1)
    def _():
        so_ref[...] = s_ref[...]


def _rwkv(p, mix, lw, shift_init, s0, *, row0, nseq, lpad, lvalid, lb, nsb=1):
    ngroup, nlb, rb = _seq_blocking(row0, nseq, lpad, lb, nsb)
    npair = GW // LANES
    pblk = lambda cb, per_pair: pl.BlockSpec(
        (lb, LANES), lambda s, j, i: (rb(s, i), cb + (j if per_pair else 0)))
    mublk = lambda cb, per_pair: pl.BlockSpec((1, LANES), lambda s, j, i: (0, cb + (j if per_pair else 0)))
    shblk = lambda cb, per_pair: pl.BlockSpec(
        (nsb, SUBLANES, LANES), lambda s, j, i: (s, 0, cb + (j if per_pair else 0)))
    pair_vec = pl.BlockSpec((1, LANES), lambda s, j, i: (0, j))
    pair_mat = pl.BlockSpec((LANES, LANES), lambda s, j, i: (0, j))
    sblk = pl.BlockSpec((nsb, None, LANES, LANES), lambda s, j, i: (s, j, 0, 0))
    mu = lw['rwkv_mu'].reshape(1, -1)
    zeros64 = jnp.zeros((64, GW), F32)
    wup = jnp.concatenate([lw['rwkv_w_up'], zeros64], axis=0).astype(BF16)
    aup = jnp.concatenate([zeros64, lw['rwkv_a_up']], axis=0).astype(BF16)
    row = lambda x: x.reshape(1, GW)
    blocks = [(0, True), (4, True), (8, True), (12, False), (13, False)]
    return pl.pallas_call(
        functools.partial(_rwkv_kernel, lb=lb, lvalid=lvalid, nlb=nlb, nsb=nsb),
        grid=(ngroup, npair, nlb),
        in_specs=(
            [pblk(CB_RWKV + cb, pp) for cb, pp in blocks]
            + [mublk(cb, pp) for cb, pp in blocks]
            + [shblk(cb, pp) for cb, pp in blocks]
            + [pair_mat, pair_mat, pair_mat] + [pair_vec] * 7 + [sblk, MIX_ANY]
        ),
        out_specs=[pl.BlockSpec((lb, LANES), lambda s, j, i: (rb(s, i), MIX_RWKV * npair + j)), sblk],
        out_shape=[
            jax.ShapeDtypeStruct(mix.shape, mix.dtype),
            jax.ShapeDtypeStruct((nseq, npair, LANES, LANES), F32),
        ],
        input_output_aliases={26: 0},
        scratch_shapes=[pltpu.VMEM((nsb, LANES, LANES), F32), pltpu.VMEM((5, nsb, SUBLANES, LANES), F32)],
        compiler_params=_cparams(("parallel", "parallel", "arbitrary")),
        name="rwkv7",
    )(p, p, p, p, p, mu, mu, mu, mu, mu, shift_init, shift_init, shift_init, shift_init, shift_init,
      wup, aup, lw['rwkv_g_up'].astype(BF16), row(lw['rwkv_w0']), row(lw['rwkv_a0']), row(lw['rwkv_k_k']),
      row(lw['rwkv_k_a']), row(lw['rwkv_r_k']), row(lw['rwkv_ln_w']), row(lw['rwkv_ln_b']), s0, mix)


def _fox_prep_kernel(q_ref, k_ref, v_ref, sm_ref, qw_ref, kw_ref, bf_ref,
                     qn_ref, kn_ref, kb_ref, vb_ref, lf_ref, c_ref, ct_ref, carry_ref, *, lb):
    @pl.when(pl.program_id(1) == 0)
    def _():
        carry_ref[...] = jnp.zeros_like(carry_ref)

    qw = qw_ref[...]
    kw = kw_ref[...]
    for h in range(NH):
        sl = slice(h * HD, (h + 1) * HD)
        qn_ref[:, sl] = _rms(q_ref[:, sl], qw).astype(BF16)
        kn = _rms(k_ref[:, sl], kw)
        kn_ref[:, sl] = kn
        kb_ref[:, sl] = kn.astype(BF16)
    vb_ref[...] = v_ref[...].astype(BF16)
    logf = -_softplus(-(sm_ref[...] + bf_ref[...]))
    lf_ref[...] = logf
    c = _dot_exact_lhs(_chunk_tri(lb, lb), logf) + carry_ref[0:1, :]
    c_ref[...] = c
    carry_ref[...] = jnp.broadcast_to(c[lb - 1:], carry_ref.shape)
    ct_ref[...] = c.T[SM_F:SM_F + SUBLANES]


def _fox_prep(p, q_w, k_w, b_f, *, nseq, lpad, lb):
    nlb = lpad // lb
    rb = lambda s, i: s * nlb + i
    seg = lambda j: pl.BlockSpec((lb, GW), lambda s, i: (rb(s, i), j))
    vec = pl.BlockSpec((1, LANES), lambda s, i: (0, 0))
    rows = nseq * lpad
    bf_lane = jnp.zeros((1, LANES), F32).at[0, SM_F:SM_F + NH].set(b_f)
    return pl.pallas_call(
        functools.partial(_fox_prep_kernel, lb=lb),
        grid=(nseq, nlb),
        in_specs=[seg(0), seg(1), seg(2),
                  pl.BlockSpec((lb, LANES), lambda s, i: (rb(s, i), CB_SMALL)), vec, vec, vec],
        out_specs=[
            pl.BlockSpec((lb, GW), lambda s, i: (rb(s, i), 0)),
            pl.BlockSpec((lb, GW), lambda s, i: (rb(s, i), 0)),
            pl.BlockSpec((lb, GW), lambda s, i: (rb(s, i), 0)),
            pl.BlockSpec((lb, GW), lambda s, i: (rb(s, i), 0)),
            pl.BlockSpec((lb, LANES), lambda s, i: (rb(s, i), 0)),
            pl.BlockSpec((lb, LANES), lambda s, i: (rb(s, i), 0)),
            pl.BlockSpec((SUBLANES, lb), lambda s, i: (0, rb(s, i))),
        ],
        out_shape=[
            jax.ShapeDtypeStruct((rows, GW), BF16),
            jax.ShapeDtypeStruct((rows, GW), F32),
            jax.ShapeDtypeStruct((rows, GW), BF16),
            jax.ShapeDtypeStruct((rows, GW), BF16),
            jax.ShapeDtypeStruct((rows, LANES), F32),
            jax.ShapeDtypeStruct((rows, LANES), F32),
            jax.ShapeDtypeStruct((SUBLANES, rows), F32),
        ],
        scratch_shapes=[pltpu.VMEM((SUBLANES, LANES), F32)],
        compiler_params=_cparams(("parallel", "arbitrary")),
        name="fox_prep",
    )(p, p, p, p, q_w.reshape(1, HD), k_w.reshape(1, HD), bf_lane)


def _fox_flash_kernel(q_ref, k_ref, v_ref, c_ref, ct_ref, g_ref, mix_ref, o_ref, m_ref, l_ref, acc_ref, *, tq):
    del mix_ref
    h0 =pl.program_id(1) * FOX_HEADS_PER_STEP
    qi = pl.program_id(2)
    log2e = 1.0 / math.log(2.0)
    scale = HD ** -0.5 * log2e
    c_all = c_ref[...]
    c_cols = [_lane_col(c_all, SM_F + h0 + e) * log2e for e in range(FOX_HEADS_PER_STEP)]
    m_ref[...] = jnp.full_like(m_ref, -1e30)
    l_ref[...] = jnp.zeros_like(l_ref)
    acc_ref[...] = jnp.zeros_like(acc_ref)

    def block(ks, width, diagonal):
        for e in range(FOX_HEADS_PER_STEP):
            sl = slice(e * HD, (e + 1) * HD)
            kb = k_ref[pl.ds(ks, width), sl]
            vb = v_ref[pl.ds(ks, width), sl]
            t = _dot_nt(q_ref[:, sl], kb) * scale - ct_ref[pl.ds(h0 + e, 1), pl.ds(ks, width)] * log2e
            if diagonal:
                ii = lax.broadcasted_iota(jnp.int32, (tq, width), 0)
                jj = lax.broadcasted_iota(jnp.int32, (tq, width), 1)
                t = jnp.where(jj <= ii, t, -jnp.inf)
            m_old = m_ref[e]
            m_new = jnp.maximum(m_old, jnp.max(t, axis=-1, keepdims=True) + c_cols[e])
            alpha = jnp.exp2(m_old - m_new)
            pr = jnp.exp2(t - (m_new - c_cols[e]))
            l_ref[e] = alpha * l_ref[e] + jnp.sum(pr, axis=-1, keepdims=True)
            acc_ref[e] = alpha * acc_ref[e] + _dot(pr, vb)
            m_ref[e] = m_new

    def body(kp, carry):
        block(pl.multiple_of(kp * 2 * tq, 2 * tq), 2 * tq, False)
        return carry

    lax.fori_loop(0, qi // 2, body, 0)

    @pl.when(qi % 2 == 1)
    def _():
        block(pl.multiple_of((qi - 1) * tq, tq), tq, False)

    block(pl.multiple_of(qi * tq, tq), tq, True)
    for e in range(FOX_HEADS_PER_STEP):
        sl = slice(e * HD, (e + 1) * HD)
        o_ref[:, sl] = (acc_ref[e] / l_ref[e] * _sigmoid(g_ref[:, sl])).astype(BF16)


def _fox_flash(qn, kb, vb, c_col, c_row, p, mix, *, nseq, lpad, tq):
    nq = lpad // tq
    hw = FOX_HEADS_PER_STEP * HD
    g_block0 = (CB_FOX + 3 * NH) // FOX_HEADS_PER_STEP
    o_block0 = MIX_FOX * (GW // hw)
    return pl.pallas_call(
        functools.partial(_fox_flash_kernel, tq=tq),
        grid=(nseq, NH // FOX_HEADS_PER_STEP, nq),
        in_specs=[
            pl.BlockSpec((tq, hw), lambda s, h, i: (s * nq + i, h)),
            pl.BlockSpec((lpad, hw), lambda s, h, i: (s, h)),
            pl.BlockSpec((lpad, hw), lambda s, h, i: (s, h)),
            pl.BlockSpec((tq, LANES), lambda s, h, i: (s * nq + i, 0)),
            pl.BlockSpec((SUBLANES, lpad), lambda s, h, i: (0, s)),
            pl.BlockSpec((tq, hw), lambda s, h, i: (s * nq + i, g_block0 + h)),
            MIX_ANY,
        ],
        out_specs=pl.BlockSpec((tq, hw), lambda s, h, i: (s * nq + i, o_block0 + h)),
        out_shape=jax.ShapeDtypeStruct(mix.shape, mix.dtype),
        input_output_aliases={6: 0},
        scratch_shapes=[pltpu.VMEM((FOX_HEADS_PER_STEP, tq, 1), F32), pltpu.VMEM((FOX_HEADS_PER_STEP, tq, 1), F32),
                        pltpu.VMEM((FOX_HEADS_PER_STEP, tq, HD), F32)],
        compiler_params=_cparams(("parallel", "parallel", "arbitrary")),
        name="fox_flash",
    )(qn, kb, vb, c_col, c_row, p, mix)


PAGE_GROUP = 16
PAGE_COLS = PAGE * NH
NQ_PAD = SUBLANES


def _page_sums_kernel(lf_ref, upper_ref, heads_ref, o_ref):
    lf = lf_ref[...]
    o_ref[:, :PAGE_COLS] = _dot_exact_rhs(lf, upper_ref[...])
    o_ref[:, PAGE_COLS:] = _dot_exact_rhs(lf, heads_ref[...])


def _page_sums(cache_lf):
    depth, n_pool = cache_lf.shape[:2]
    rows = depth * n_pool
    tile = _row_tile(rows, 512)
    idx = np.arange(PAGE_COLS)
    same_head = (idx[:, None] % NH) == (idx[None, :] % NH)
    upper = jnp.asarray(same_head & (idx[:, None] // NH > idx[None, :] // NH), BF16)
    heads = jnp.asarray(same_head, BF16)
    const = pl.BlockSpec((PAGE_COLS, PAGE_COLS), lambda i: (0, 0))
    sums = pl.pallas_call(
        _page_sums_kernel,
        grid=(rows // tile,),
        in_specs=[pl.BlockSpec((tile, PAGE_COLS), lambda i: (i, 0)), const, const],
        out_specs=pl.BlockSpec((tile, 2 * PAGE_COLS), lambda i: (i, 0)),
        out_shape=jax.ShapeDtypeStruct((rows, 2 * PAGE_COLS), F32),
        compiler_params=_cparams(("parallel",)),
        name="page_sums",
    )(cache_lf.reshape(rows, PAGE_COLS), upper, heads)
    return sums.reshape(depth, n_pool, 2, PAGE_COLS)


def _fox_sample_kernel(pt_ref, pq_ref, sm_ref, *rest, lvalid, lpad, n_steps):
    del pt_ref
    g = PAGE_GROUP
    kps, vps, sums = rest[:g], rest[g:2 * g], rest[2 * g:3 * g]
    (qw_ref, kw_ref, bf_ref, mix_ref, o_ref, kn_ref, lf_ref,
     qn_s, cq_s, m_s, l_s, acc_s, carry_s) = rest[3 * g:]
    del mix_ref
    i = pl.program_id(1)
    scale = HD ** -0.5
    nq = NQ_PAD
    nrow = NH * nq

    @pl.when(i == 0)
    def _():
        logf = -_softplus(-(sm_ref[...] + bf_ref[...]))
        logf = jnp.where(_row_valid(0, lpad, lvalid, LANES), logf, 0.0)
        lf_ref[...] = logf[:nq]
        c = _dot_exact_lhs(_chunk_tri(lpad, lpad), logf)
        c_t = c.T
        carry_s[...] = jnp.zeros_like(carry_s)
        qi = lax.broadcasted_iota(jnp.int32, (nq, lpad), 0)
        kj = lax.broadcasted_iota(jnp.int32, (nq, lpad), 1)
        for h in range(NH):
            sl = slice(h * HD, (h + 1) * HD)
            rows = slice(h * nq, (h + 1) * nq)
            qn = _rms(pq_ref[:nq, sl], qw_ref[...])
            kn = _rms(pq_ref[:, GW + h * HD:GW + (h + 1) * HD], kw_ref[...])
            vn = pq_ref[:, 2 * GW + h * HD:2 * GW + (h + 1) * HD]
            c_h = c[:nq, SM_F + h:SM_F + h + 1]
            qn_s[rows, :] = qn
            cq_s[rows, :] = c_h
            kn_ref[:, sl] = kn[:nq]
            s = _dot_nt(qn, kn) * scale + c_h - c_t[SM_F + h:SM_F + h + 1, :]
            s = jnp.where(kj <= qi, s, -jnp.inf)
            m = jnp.max(s, axis=-1, keepdims=True)
            pr = jnp.exp(s - m)
            m_s[rows, :] = m
            l_s[rows, :] = jnp.sum(pr, axis=-1, keepdims=True)
            acc_s[rows, :] = _dot(pr, vn)

    run = carry_s[...]
    suffix = [None] * g
    for j in reversed(range(g)):
        suffix[j] = sums[j][0:1, :] + run
        run = run + sums[j][1:2, :]
    carry_s[...] = run

    row_head = lax.broadcasted_iota(jnp.int32, (nrow, PAGE_COLS), 0) // nq
    col_head = lax.broadcasted_iota(jnp.int32, (nrow, PAGE_COLS), 1) % NH
    own = row_head == col_head
    qs = qn_s[...].astype(BF16)
    bias = cq_s[...]
    tiles = [jnp.where(own, _dot_nt(qs, kps[j][...]) * scale + bias + suffix[j], -jnp.inf) for j in range(g)]
    m_old = m_s[...]
    m_new = m_old
    for t in tiles:
        m_new = jnp.maximum(m_new, jnp.max(t, axis=-1, keepdims=True))
    alpha = jnp.exp(m_old - m_new)
    l_new = alpha * l_s[...]
    acc = alpha * acc_s[...]
    for j, t in enumerate(tiles):
        pr = jnp.exp(t - m_new)
        l_new = l_new + jnp.sum(pr, axis=-1, keepdims=True)
        acc = acc + _dot(pr, vps[j][...])
    m_s[...] = m_new
    l_s[...] = l_new
    acc_s[...] = acc

    @pl.when(i == n_steps - 1)
    def _():
        o_ref[...] = jnp.zeros_like(o_ref)
        out = acc_s[...] / l_s[...]
        for h in range(NH):
            sl = slice(h * HD, (h + 1) * HD)
            gate = pq_ref[:nq, 3 * GW + h * HD:3 * GW + (h + 1) * HD]
            o_ref[:nq, sl] = (out[h * nq:(h + 1) * nq] * _sigmoid(gate)).astype(BF16)


def _fox_sample(p, mix, page_table, cache_k, cache_v, page_sums, q_w, k_w, b_f, *, layer, row0, nseq, lpad, lvalid):
    n_pages = page_table.shape[1]
    g = PAGE_GROUP
    assert n_pages % g == 0 and lvalid <= NQ_PAD
    n_steps = n_pages // g
    bf_lane = jnp.zeros((1, LANES), F32).at[0, SM_F:SM_F + NH].set(b_f)
    rb = lambda b: row0 // lpad + b
    vec = pl.BlockSpec((1, LANES), lambda b, i, pt: (0, 0))

    def page_spec(shape, j):
        zeros = (0,) * len(shape)
        return pl.BlockSpec((None, None) + shape,
                            lambda b, i, pt: (layer, pt[b, n_pages - g * (i + 1) + j]) + zeros)

    grid_spec = pltpu.PrefetchScalarGridSpec(
        num_scalar_prefetch=1,
        grid=(nseq, n_steps),
        in_specs=(
            [pl.BlockSpec((lpad, 4 * GW), lambda b, i, pt: (rb(b), CB_FOX)),
             pl.BlockSpec((lpad, LANES), lambda b, i, pt: (rb(b), CB_SMALL))]
            + [page_spec((PAGE_COLS, HD), j) for j in range(g)]
            + [page_spec((PAGE_COLS, HD), j) for j in range(g)]
            + [page_spec((2, PAGE_COLS), j) for j in range(g)]
            + [vec, vec, vec, MIX_ANY]
        ),
        out_specs=[
            pl.BlockSpec((lpad, GW), lambda b, i, pt: (rb(b), MIX_FOX)),
            pl.BlockSpec((None, NQ_PAD, GW), lambda b, i, pt: (b, 0, 0)),
            pl.BlockSpec((None, NQ_PAD, LANES), lambda b, i, pt: (b, 0, 0)),
        ],
        scratch_shapes=[
            pltpu.VMEM((NH * NQ_PAD, HD), F32),
            pltpu.VMEM((NH * NQ_PAD, 1), F32),
            pltpu.VMEM((NH * NQ_PAD, 1), F32),
            pltpu.VMEM((NH * NQ_PAD, 1), F32),
            pltpu.VMEM((NH * NQ_PAD, HD), F32),
            pltpu.VMEM((1, PAGE_COLS), F32),
        ],
    )
    return pl.pallas_call(
        functools.partial(_fox_sample_kernel, lvalid=lvalid, lpad=lpad, n_steps=n_steps),
        grid_spec=grid_spec,
        out_shape=[
            jax.ShapeDtypeStruct(mix.shape, mix.dtype),
            jax.ShapeDtypeStruct((nseq, NQ_PAD, GW), F32),
            jax.ShapeDtypeStruct((nseq, NQ_PAD, LANES), F32),
        ],
        input_output_aliases={3 * g + 6: 0},
        compiler_params=_cparams(("parallel", "arbitrary")),
        name="fox_sample",
    )(page_table, p, p, *([cache_k] * g), *([cache_v] * g), *([page_sums] * g),
      q_w.reshape(1, HD), k_w.reshape(1, HD), bf_lane, mix)


W_IN_SHIFT = 8
_COPY, _SHIFT, _SMALL, _ZERO = 0, 1, 2, 3


def _w_in_plan():
    kind = np.zeros(NP_COLS // LANES, np.int32)
    src = np.zeros(NP_COLS // LANES, np.int32)

    def put(cb, n, first_src_block, k):
        kind[cb:cb + n] = k
        src[cb:cb + n] = first_src_block + np.arange(n)

    put(CB_FOX, 16, (5896 - W_IN_SHIFT) // LANES, _SHIFT)
    put(CB_RET, 16, (2056 - W_IN_SHIFT) // LANES, _SHIFT)
    put(CB_GDN, 12, 0, _COPY)
    put(CB_GDN_Z, 4, (1544 - W_IN_SHIFT) // LANES, _SHIFT)
    put(CB_RWKV, 14, (4104 - W_IN_SHIFT) // LANES, _SHIFT)
    put(CB_SMALL, 1, 1536 // LANES, _SMALL)
    put(CB_SMALL + 1, 1, 0, _ZERO)
    src_b = np.where(kind == _SHIFT, src + 1, np.where(kind == _SMALL, (7944 - W_IN_SHIFT) // LANES, src))
    return jnp.asarray(kind), jnp.asarray(src), jnp.asarray(src_b.astype(np.int32))


def _prep_w_in_kernel(kind_ref, sa_ref, sb_ref, a_ref, b_ref, o_ref):
    del sa_ref, sb_ref
    kind = kind_ref[pl.program_id(0)]
    depth = o_ref.shape[0]
    row = lax.broadcasted_iota(jnp.int32, o_ref.shape[1:], 0)

    @pl.when(kind == _COPY)
    def _():
        for l in range(depth):
            o_ref[l] = a_ref[:, l, :].astype(BF16)

    @pl.when(kind == _SHIFT)
    def _():
        for l in range(depth):
            a = a_ref[:, l, :]
            b = b_ref[:, l, :]
            o_ref[l] = jnp.concatenate([a[W_IN_SHIFT:], b[:W_IN_SHIFT]], axis=0).astype(BF16)

    @pl.when(kind == _SMALL)
    def _():
        for l in range(depth):
            small = jnp.where(row < SM_F, a_ref[:, l, :], jnp.where(row < SM_F + NH, b_ref[:, l, :], 0.0))
            o_ref[l] = small.astype(BF16)

    @pl.when(kind == _ZERO)
    def _():
        o_ref[...] = jnp.zeros_like(o_ref)


def _prep_w_in(w_in):
    depth, k, _ = w_in.shape
    w_t = jnp.transpose(w_in, (2, 0, 1))
    kind, src_a, src_b = _w_in_plan()
    grid_spec = pltpu.PrefetchScalarGridSpec(
        num_scalar_prefetch=3,
        grid=(NP_COLS // LANES,),
        in_specs=[
            pl.BlockSpec((LANES, depth, k), lambda j, kd, sa, sb: (sa[j], 0, 0)),
            pl.BlockSpec((LANES, depth, k), lambda j, kd, sa, sb: (sb[j], 0, 0)),
        ],
        out_specs=pl.BlockSpec((depth, LANES, k), lambda j, kd, sa, sb: (0, j, 0)),
    )
    return pl.pallas_call(
        _prep_w_in_kernel,
        grid_spec=grid_spec,
        out_shape=jax.ShapeDtypeStruct((depth, NP_COLS, k), BF16),
        compiler_params=_cparams(("parallel",)),
        name="prep_w_in",
    )(kind, src_a, src_b, w_t, w_t)


def _rope_tables(pos):
    half = HD // 2
    inv = 1.0 / (ROPE_BASE ** jnp.linspace(0.0, 1.0, half, dtype=F32))
    ang = pos.astype(F32)[:, None] * inv[None, :]
    cos, sin = jnp.cos(ang), jnp.sin(ang)
    return jnp.concatenate([cos, cos], axis=-1), jnp.concatenate([-sin, sin], axis=-1)


def _state_tile(state, nrows):
    b, _, c = state.shape
    return jnp.concatenate([jnp.zeros((b, SUBLANES - nrows, c), F32), state], axis=1)


def _rwkv_pair_states(s):
    b = s.shape[0]
    s = s.reshape(b, 4, 2, RWKV_HEAD, RWKV_HEAD)
    z = jnp.zeros_like(s[:, :, 0])
    top = jnp.concatenate([s[:, :, 0], z], axis=-1)
    bot = jnp.concatenate([z, s[:, :, 1]], axis=-1)
    return jnp.concatenate([top, bot], axis=-2)


def _rwkv_unpair_states(sp):
    b = sp.shape[0]
    a = sp[:, :, :RWKV_HEAD, :RWKV_HEAD]
    c = sp[:, :, RWKV_HEAD:, RWKV_HEAD:]
    return jnp.stack([a, c], axis=2).reshape(b, 8, RWKV_HEAD, RWKV_HEAD)


def kernel(x_prompt, x_sample, cache_fox_k, cache_fox_v, cache_fox_logf, cache_mem_k, cache_mem_v, state_gdn_conv, state_gdn_S, state_ret_S, state_rwkv_shift, state_rwkv_S, page_table, mem_prompt, norm_mix, w_in, gdn_conv_w, gdn_A_log, gdn_dt_bias, gdn_norm, rwkv_mu, rwkv_w0, rwkv_w_up, rwkv_a0, rwkv_a_up, rwkv_g_up, rwkv_k_k, rwkv_k_a, rwkv_r_k, rwkv_ln_w, rwkv_ln_b, fox_b_f, fox_q_norm, fox_k_norm, w_out, norm_x, norm_mem, xattn_wq, xattn_wkv, xattn_q_norm, xattn_k_norm, xattn_wo, norm_ffn, ffn_w_gate, ffn_w_up, ffn_w_down):
    weights = {
        'norm_mix': norm_mix, 'w_in': w_in, 'gdn_conv_w': gdn_conv_w, 'gdn_A_log': gdn_A_log,
        'gdn_dt_bias': gdn_dt_bias, 'gdn_norm': gdn_norm, 'rwkv_mu': rwkv_mu, 'rwkv_w0': rwkv_w0,
        'rwkv_w_up': rwkv_w_up, 'rwkv_a0': rwkv_a0, 'rwkv_a_up': rwkv_a_up, 'rwkv_g_up': rwkv_g_up,
        'rwkv_k_k': rwkv_k_k, 'rwkv_k_a': rwkv_k_a, 'rwkv_r_k': rwkv_r_k, 'rwkv_ln_w': rwkv_ln_w,
        'rwkv_ln_b': rwkv_ln_b, 'fox_b_f': fox_b_f, 'fox_q_norm': fox_q_norm, 'fox_k_norm': fox_k_norm,
        'w_out': w_out, 'norm_x': norm_x, 'norm_mem': norm_mem, 'xattn_wq': xattn_wq, 'xattn_wkv': xattn_wkv,
        'xattn_q_norm': xattn_q_norm, 'xattn_k_norm': xattn_k_norm, 'xattn_wo': xattn_wo,
        'norm_ffn': norm_ffn, 'ffn_w_gate': ffn_w_gate, 'ffn_w_up': ffn_w_up, 'ffn_w_down': ffn_w_down,
    }
    depth = w_in.shape[0]
    bp, lp, d = x_prompt.shape
    bs, ls, _ = x_sample.shape
    n_pages = page_table.shape[1]
    past_len = n_pages * PAGE
    tp = bp * lp
    ts = bs * SAMPLE_PAD
    tt = tp + ts
    tm = _row_tile(tt)
    lb_p = min(256, lp)
    assert ls >= CONV_WIDTH - 1 and ls <= SUBLANES and lp % lb_p == 0

    xs_pad = jnp.zeros((bs, SAMPLE_PAD, d), F32).at[:, :ls].set(x_sample)
    x = jnp.concatenate([x_prompt.reshape(tp, d), xs_pad.reshape(ts, d)], axis=0)

    cos_p, sin_p = _rope_tables(jnp.arange(lp, dtype=jnp.int32))
    cos_s, sin_s = _rope_tables(jnp.tile(past_len + jnp.arange(SAMPLE_PAD, dtype=jnp.int32), bs))
    log_gamma = jnp.log(1.0 - jnp.exp2(-(RET_GAMMA_BASE + jnp.arange(NH, dtype=F32))))
    n_pool = cache_fox_k.shape[1]
    cache_k = cache_fox_k.reshape(depth, n_pool, PAGE_COLS, HD)
    cache_v = cache_fox_v.reshape(depth, n_pool, PAGE_COLS, HD)
    page_sums = _page_sums(cache_fox_logf.reshape(depth, n_pool, 1, PAGE_COLS))
    zeros_s = jnp.zeros((bp, NH, HD, HD), F32)
    zeros_conv = jnp.zeros((bp, SUBLANES, 3 * GW), F32)
    zeros_shift = jnp.zeros((bp, SUBLANES, 1792), F32)

    w_in_t = _prep_w_in(w_in)

    outs_p, outs_s, mem_ks, mem_vs = [], [], [], []
    for l in range(depth):
        lw = {name: arr[l] for name, arr in weights.items()}
        p = _norm_matmul(x, lw['norm_mix'], w_in_t, l, tm=tm, tn=1024,
                         transposed_w=True)

        o_mix = jnp.zeros((tt, 4 * GW), BF16)
        gp = dict(row0=0, nseq=bp, lpad=lp, lvalid=lp, lb=lb_p)
        o_mix, gdn_s_p = _gdn(p, o_mix, lw['gdn_conv_w'], zeros_conv, zeros_s, lw['gdn_A_log'], lw['gdn_dt_bias'],
                              lw['gdn_norm'], **{**gp, 'lb': min(GDN_LB, lp)})
        o_mix, ret_s_p = _ret(p, o_mix, cos_p, sin_p, log_gamma, zeros_s, **{**gp, 'lb': min(RET_LB, lp)})
        o_mix, rwkv_s_p = _rwkv(p, o_mix, lw, zeros_shift, zeros_s, **{**gp, 'lb': min(RWKV_LB, lp)})
        qn, kn, kb, vb, lf, c_col, c_row = _fox_prep(p, lw['fox_q_norm'], lw['fox_k_norm'], lw['fox_b_f'],
                                                    nseq=bp, lpad=lp, lb=lb_p)
        o_mix = _fox_flash(qn, kb, vb, c_col, c_row, p, o_mix, nseq=bp, lpad=lp, tq=min(FOX_TQ, lp))

        gs = dict(row0=tp, nseq=bs, lpad=SAMPLE_PAD, lvalid=ls, lb=ts, nsb=bs)
        o_mix, gdn_s_s = _gdn(p, o_mix, lw['gdn_conv_w'], _state_tile(state_gdn_conv[l], CONV_WIDTH - 1),
                              state_gdn_S[l], lw['gdn_A_log'], lw['gdn_dt_bias'], lw['gdn_norm'], **gs)
        o_mix, ret_s_s = _ret(p, o_mix, cos_s, sin_s, log_gamma, state_ret_S[l], **gs)
        o_mix, rwkv_s_s = _rwkv(p, o_mix, lw, _state_tile(state_rwkv_shift[l], 1),
                                _rwkv_pair_states(state_rwkv_S[l]), **gs)
        o_mix, kn_s, lf_s = _fox_sample(p, o_mix, page_table, cache_k, cache_v, page_sums,
                                        lw['fox_q_norm'], lw['fox_k_norm'], lw['fox_b_f'],
                                        layer=l, row0=tp, nseq=bs, lpad=SAMPLE_PAD, lvalid=ls)
        x = _matmul_res(o_mix, w_out, l, x, tm=tm, tn=512)

        kv = _norm_matmul(mem_prompt.reshape(bp * N_MEM, d), lw['norm_mem'], xattn_wkv, l,
                          tm=256, tn=XW, head_w=lw['xattn_k_norm'], norm_tiles=1)
        mk = kv[:, :XW].reshape(bp, N_MEM, XW)
        mv = kv[:, XW:].reshape(bp, N_MEM, XW)
        q = _norm_matmul(x, lw['norm_x'], xattn_wq, l, tm=tm, tn=XW,
                         head_w=lw['xattn_q_norm'], norm_tiles=1)
        xo_p = _xattn(q, mk, mv, row0=0, nrows=tp, tq=lb_p, rows_per_seq=lp)
        xo_s = _xattn(q, cache_mem_k[l].reshape(bs, N_MEM, XW), cache_mem_v[l].reshape(bs, N_MEM, XW),
                      row0=tp, nrows=ts, tq=SAMPLE_PAD, rows_per_seq=SAMPLE_PAD)
        x = _matmul_res(jnp.concatenate([xo_p, xo_s], axis=0), xattn_wo, l, x, tm=tm, tn=512)

        hidden = _swiglu_up(x, lw['norm_ffn'], ffn_w_gate, ffn_w_up, l, tm=tm, tn=512)
        x = _matmul_res(hidden, ffn_w_down, l, x, tm=tm, tn=256)

        c0 = CB_GDN * LANES
        r0 = CB_RWKV * LANES
        v0 = (CB_FOX + 2 * NH) * LANES

        def last_rows(row_end, n, col0, width, nseq, stride):
            return jnp.stack([lax.slice(p, (b * stride + row_end - n, col0), (b * stride + row_end, col0 + width))
                              for b in range(nseq)], axis=0)

        ps_v = lax.slice(p, (tp, v0), (tt, v0 + GW)).reshape(bs, SAMPLE_PAD, NH, HD)
        outs_p.append((
            kn.reshape(bp, lp, NH, HD),
            lax.slice(p, (0, v0), (tp, v0 + GW)).reshape(bp, lp, NH, HD),
            lf.reshape(bp, lp, LANES)[:, :, SM_F:SM_F + NH],
            last_rows(lp, CONV_WIDTH - 1, c0, 3 * GW, bp, lp),
            gdn_s_p, ret_s_p,
            last_rows(lp, 1, r0, 1792, bp, lp),
            _rwkv_unpair_states(rwkv_s_p),
        ))
        outs_s.append((
            kn_s[:, :ls].reshape(bs, ls, NH, HD),
            ps_v[:, :ls],
            lf_s[:, :ls, SM_F:SM_F + NH],
            last_rows(tp + ls, CONV_WIDTH - 1, c0, 3 * GW, bs, SAMPLE_PAD),
            gdn_s_s, ret_s_s,
            last_rows(tp + ls, 1, r0, 1792, bs, SAMPLE_PAD),
            _rwkv_unpair_states(rwkv_s_s),
        ))
        mem_ks.append(mk.reshape(bp, N_MEM, NH, HD))
        mem_vs.append(mv.reshape(bp, N_MEM, NH, HD))

    stk = lambda seq, i: jnp.stack([e[i] for e in seq], axis=0)
    yp = x[:tp].reshape(bp, lp, d)
    ys = x[tp:].reshape(bs, SAMPLE_PAD, d)[:, :ls]
    return (yp, ys, stk(outs_p, 0), stk(outs_p, 1), stk(outs_p, 2), jnp.stack(mem_ks, 0), jnp.stack(mem_vs, 0),
            stk(outs_p, 3), stk(outs_p, 4), stk(outs_p, 5), stk(outs_p, 6), stk(outs_p, 7),
            stk(outs_s, 0), stk(outs_s, 1), stk(outs_s, 2), stk(outs_s, 3), stk(outs_s, 4), stk(outs_s, 5),
            stk(outs_s, 6), stk(outs_s, 7))
```

```python
import functools
import math

import jax
import jax.numpy as jnp
import numpy as np
from jax import lax
from jax.experimental import pallas as pl
from jax.experimental.pallas import tpu as pltpu

F32 = jnp.float32
BF16 = jnp.bfloat16

LANES = 128
SUBLANES = 8
VMEM_LIMIT = 56 * 1024 * 1024

D_MODEL = 2048
GW = D_MODEL // 4
HD = 128
NH = GW // HD
RWKV_HEAD = 64
CONV_WIDTH = 4
PAGE = 128
N_MEM = 256
XW = 512
D_FF = 5632
NORM_EPS = 1e-6
GN_EPS = 64e-5
RET_GAMMA_BASE = 5.0
ROPE_BASE = 10000.0
CHUNK = 64
SAMPLE_PAD = 64
FOX_TQ = 512
FOX_HEADS_PER_STEP = 2
FOX_AUG = 2 * HD
LOG2E = 1.0 / math.log(2.0)
GDN_LB = 512
RWKV_LB = 512
RET_LB = 512

NP_COLS = 8192
CB_FOX, CB_RET, CB_GDN, CB_GDN_Z, CB_RWKV, CB_SMALL = 0, 16, 32, 44, 48, 62
SM_A, SM_B, SM_F = 0, 4, 8


def _cparams(sem):
    return pltpu.CompilerParams(dimension_semantics=sem, vmem_limit_bytes=VMEM_LIMIT)


def _dot(a, b):
    return jnp.dot(a.astype(BF16), b.astype(BF16), preferred_element_type=F32)


def _dot_nt(a, b):
    return lax.dot_general(a.astype(BF16), b.astype(BF16), (((1,), (1,)), ((), ())),
                           preferred_element_type=F32)


def _dot_tn(a, b):
    return lax.dot_general(a.astype(BF16), b.astype(BF16), (((0,), (0,)), ((), ())),
                           preferred_element_type=F32)


def _split3(x):
    hi = x.astype(BF16)
    r = x - hi.astype(F32)
    mid = r.astype(BF16)
    lo = (r - mid.astype(F32)).astype(BF16)
    return hi, mid, lo


def _dot_exact_lhs(m, x):
    hi, mid, lo = _split3(x)
    d = lambda p: jnp.dot(m, p, preferred_element_type=F32)
    return d(hi) + d(mid) + d(lo)


def _dot_exact_rhs(x, m):
    hi, mid, lo = _split3(x)
    d = lambda p: jnp.dot(p, m, preferred_element_type=F32)
    return d(hi) + d(mid) + d(lo)


def _bmm(a, b):
    return lax.dot_general(a.astype(BF16), b.astype(BF16), (((2,), (1,)), ((0,), (0,))),
                           preferred_element_type=F32)


def _bmm_nt(a, b):
    return lax.dot_general(a.astype(BF16), b.astype(BF16), (((2,), (2,)), ((0,), (0,))),
                           preferred_element_type=F32)


def _unit_lower_inv(n, nil):
    c = n.shape[-1]
    ii = lax.broadcasted_iota(jnp.int32, (c, c), 0)
    jj = lax.broadcasted_iota(jnp.int32, (c, c), 1)
    p = jnp.where(ii == jj, 1.0, 0.0).astype(F32) - n
    q = n
    for _ in range(int(math.log2(nil)) - 1):
        q = _bmm(q, q)
        p = p + _bmm(p, q)
    return p


def _chunk_tri(lb, chunk):
    ii = lax.broadcasted_iota(jnp.int32, (lb, lb), 0)
    jj = lax.broadcasted_iota(jnp.int32, (lb, lb), 1)
    same = (ii // chunk) == (jj // chunk)
    return jnp.where(jnp.logical_and(ii >= jj, same), 1.0, 0.0).astype(BF16)


def _shift_rows(x, prev, s):
    nsb = prev.shape[0]
    seq_rows = x.shape[0] // nsb
    rolled = pltpu.roll(x, s, axis=0)
    row = lax.broadcasted_iota(jnp.int32, (SUBLANES, x.shape[1]), 0)
    pieces = []
    for b in range(nsb):
        piece = rolled[b * seq_rows:(b + 1) * seq_rows]
        top = jnp.where(row < s, pltpu.roll(prev[b], s, axis=0), piece[:SUBLANES])
        pieces += [top, piece[SUBLANES:]]
    return jnp.concatenate(pieces, axis=0)


def _last_tiles(x, nsb):
    seq_rows = x.shape[0] // nsb
    return x.reshape(nsb, seq_rows, x.shape[1])[:, seq_rows - SUBLANES:, :]


def _softplus(x):
    return jnp.maximum(x, 0.0) + jnp.log1p(jnp.exp(-jnp.abs(x)))


def _sigmoid(x):
    return jax.nn.sigmoid(x)


def _silu(x):
    return x * jax.nn.sigmoid(x)


def _lane_col(x, idx):
    lane = lax.broadcasted_iota(jnp.int32, x.shape, 1)
    return jnp.sum(jnp.where(lane == idx, x, 0.0), axis=-1, keepdims=True)


def _rms(x, w=None):
    y = x * lax.rsqrt(jnp.mean(x * x, axis=-1, keepdims=True) + NORM_EPS)
    return y if w is None else y * w


def _row_valid(lb_index, lb, lvalid, width, nsb=1):
    row = lax.broadcasted_iota(jnp.int32, (lb, width), 0) % (lb // nsb) + lb_index * lb
    return row < lvalid


def _norm_matmul_kernel(x_ref, g_ref, w_ref, hw_ref, o_ref, xn_ref, *, norm_tiles, transposed_w):
    j = pl.program_id(1)

    @pl.when(j == 0)
    def _():
        xn_ref[...] = _rms(x_ref[...], g_ref[...]).astype(BF16)

    if transposed_w:
        acc = _dot_nt(xn_ref[...], w_ref[...])
    else:
        acc = jnp.dot(xn_ref[...], w_ref[...].astype(BF16), preferred_element_type=F32)
    if norm_tiles == 0:
        o_ref[...] = acc
    else:
        @pl.when(j < norm_tiles)
        def _():
            hw = hw_ref[...]
            for h in range(acc.shape[1] // HD):
                sl = slice(h * HD, (h + 1) * HD)
                o_ref[:, sl] = _rms(acc[:, sl], hw)

        @pl.when(j >= norm_tiles)
        def _():
            o_ref[...] = acc


def _row_tile(m, cap=1100):
    return next(t for t in range(cap - cap % 16, 0, -16) if m % t == 0)


def _norm_matmul(x, g, w, layer, *, tm, tn, head_w=None, norm_tiles=0, transposed_w=False):
    m, k = x.shape
    n = w.shape[1] if transposed_w else w.shape[2]
    if head_w is None:
        head_w = jnp.ones((HD,), F32)
    if transposed_w:
        w_spec = pl.BlockSpec((None, tn, k), lambda i, j: (layer, j, 0))
    else:
        w_spec = pl.BlockSpec((None, k, tn), lambda i, j: (layer, 0, j))
    return pl.pallas_call(
        functools.partial(_norm_matmul_kernel, norm_tiles=norm_tiles, transposed_w=transposed_w),
        grid=(m // tm, n // tn),
        in_specs=[
            pl.BlockSpec((tm, k), lambda i, j: (i, 0)),
            pl.BlockSpec((1, k), lambda i, j: (0, 0)),
            w_spec,
            pl.BlockSpec((1, HD), lambda i, j: (0, 0)),
        ],
        out_specs=pl.BlockSpec((tm, tn), lambda i, j: (i, j)),
        out_shape=jax.ShapeDtypeStruct((m, n), F32),
        scratch_shapes=[pltpu.VMEM((tm, k), BF16)],
        compiler_params=_cparams(("parallel", "arbitrary")),
        name="norm_matmul",
    )(x, g.reshape(1, k), w, head_w.reshape(1, HD))


def _matmul_res_kernel(a_ref, w_ref, r_ref, o_ref):
    o_ref[...] = r_ref[...] + jnp.dot(a_ref[...], w_ref[...].astype(BF16), preferred_element_type=F32)


def _matmul_res(a, w, layer, res, *, tm, tn):
    m, k = a.shape
    n = w.shape[2]
    return pl.pallas_call(
        _matmul_res_kernel,
        grid=(m // tm, n // tn),
        in_specs=[
            pl.BlockSpec((tm, k), lambda i, j: (i, 0)),
            pl.BlockSpec((None, k, tn), lambda i, j: (layer, 0, j)),
            pl.BlockSpec((tm, tn), lambda i, j: (i, j)),
        ],
        out_specs=pl.BlockSpec((tm, tn), lambda i, j: (i, j)),
        out_shape=jax.ShapeDtypeStruct((m, n), F32),
        compiler_params=_cparams(("parallel", "parallel")),
        name="matmul_res",
    )(a, w, res)


def _swiglu_up_kernel(x_ref, g_ref, wg_ref, wu_ref, o_ref, xn_ref):
    @pl.when(pl.program_id(1) == 0)
    def _():
        xn_ref[...] = _rms(x_ref[...], g_ref[...]).astype(BF16)

    xn = xn_ref[...]
    gate = jnp.dot(xn, wg_ref[...].astype(BF16), preferred_element_type=F32)
    up = jnp.dot(xn, wu_ref[...].astype(BF16), preferred_element_type=F32)
    o_ref[...] = (_silu(gate) * up).astype(BF16)


def _swiglu_up(x, g, wg, wu, layer, *, tm, tn):
    m, k = x.shape
    n = wg.shape[2]
    return pl.pallas_call(
        _swiglu_up_kernel,
        grid=(m // tm, n // tn),
        in_specs=[
            pl.BlockSpec((tm, k), lambda i, j: (i, 0)),
            pl.BlockSpec((1, k), lambda i, j: (0, 0)),
            pl.BlockSpec((None, k, tn), lambda i, j: (layer, 0, j)),
            pl.BlockSpec((None, k, tn), lambda i, j: (layer, 0, j)),
        ],
        out_specs=pl.BlockSpec((tm, tn), lambda i, j: (i, j)),
        out_shape=jax.ShapeDtypeStruct((m, n), BF16),
        scratch_shapes=[pltpu.VMEM((tm, k), BF16)],
        compiler_params=_cparams(("parallel", "arbitrary")),
        name="swiglu_up",
    )(x, g.reshape(1, k), wg, wu)


def _xattn_kernel(q_ref, k_ref, v_ref, o_ref):
    scale = HD ** -0.5
    for h in range(NH):
        sl = slice(h * HD, (h + 1) * HD)
        s = _dot_nt(q_ref[:, sl], k_ref[:, sl]) * scale
        m = jnp.max(s, axis=-1, keepdims=True)
        p = jnp.exp(s - m)
        l = jnp.sum(p, axis=-1, keepdims=True)
        o_ref[:, sl] = (_dot(p, v_ref[:, sl]) / l).astype(BF16)


def _xattn(q, mem_k, mem_v, *, row0, nrows, tq, rows_per_seq):
    tiles_per_seq = rows_per_seq // tq
    t0 = row0 // tq
    return pl.pallas_call(
        _xattn_kernel,
        grid=(nrows // tq,),
        in_specs=[
            pl.BlockSpec((tq, XW), lambda i: (t0 + i, 0)),
            pl.BlockSpec((None, N_MEM, XW), lambda i: (i // tiles_per_seq, 0, 0)),
            pl.BlockSpec((None, N_MEM, XW), lambda i: (i // tiles_per_seq, 0, 0)),
        ],
        out_specs=pl.BlockSpec((tq, XW), lambda i: (i, 0)),
        out_shape=jax.ShapeDtypeStruct((nrows, XW), BF16),
        compiler_params=_cparams(("parallel",)),
        name="xattn",
    )(q, mem_k, mem_v)


def _gdn_kernel(q_ref, k_ref, v_ref, z_ref, sm_ref, cwq_ref, cwk_ref, cwv_ref, cq_ref, ck_ref, cv_ref,
                s0_ref, alog_ref, dtb_ref, nw_ref, mix_ref, o_ref, so_ref, s_ref, prev_ref, gt_ref,
                *, lb, lvalid, nlb, nsb):
    del mix_ref
    h = pl.program_id(1)
    ib = pl.program_id(2)
    masked = lvalid < nlb * lb // nsb

    @pl.when(ib == 0)
    def _():
        s_ref[...] = s0_ref[...]
        prev_ref[0] = cq_ref[...]
        prev_ref[1] = ck_ref[...]
        prev_ref[2] = cv_ref[...]

    def conv(x_ref, w_ref, i):
        x = x_ref[...]
        w = w_ref[...]
        prev = prev_ref[i]
        y = x * w[3:4]
        for s in (1, 2, 3):
            y = y + _shift_rows(x, prev, s) * w[3 - s:4 - s]
        prev_ref[i] = _last_tiles(x, nsb)
        return _silu(y)

    q = conv(q_ref, cwq_ref, 0)
    k = conv(k_ref, cwk_ref, 1)
    v = conv(v_ref, cwv_ref, 2)
    q = q * lax.rsqrt(jnp.sum(q * q, axis=-1, keepdims=True) + NORM_EPS) * (HD ** -0.5)
    k = k * lax.rsqrt(jnp.sum(k * k, axis=-1, keepdims=True) + NORM_EPS)

    sm = sm_ref[...]
    g_blk = -jnp.exp(alog_ref[...]) * _softplus(sm + dtb_ref[...])
    beta_blk = _sigmoid(sm)
    if masked:
        valid = _row_valid(ib, lb, lvalid, LANES, nsb)
        g_blk = jnp.where(valid, g_blk, 0.0)
        beta_blk = jnp.where(valid, beta_blk, 0.0)
    gc_blk = _dot_exact_lhs(_chunk_tri(lb, CHUNK), g_blk)
    gt_ref[...] = gc_blk.T
    g_col_all = _lane_col(gc_blk, SM_A + h)
    beta_all = _lane_col(beta_blk, SM_B + h)
    g_row_all = gt_ref[pl.ds(SM_A + h, 1), :]

    group = min(2 * CHUNK, lb)
    ng = lb // group
    nchunk = lb // CHUNK
    ii = lax.broadcasted_iota(jnp.int32, (group, group), 0)
    jj = lax.broadcasted_iota(jnp.int32, (group, group), 1)
    lower = jnp.logical_and(ii >= jj, (ii // CHUNK) == (jj // CHUNK))
    to3 = lambda x: x.reshape(ng, group, x.shape[-1])
    q3, k3, v3 = to3(q), to3(k), to3(v)
    g_col3 = to3(g_col_all)
    beta3 = to3(beta_all)
    g_row3 = jnp.stack([g_row_all[:, i * group:(i + 1) * group] for i in range(ng)], axis=0)
    dec3 = jnp.exp(jnp.where(lower, g_col3 - g_row3, -jnp.inf))
    scores = _bmm_nt(jnp.concatenate([k3, q3], axis=1), k3)
    n3 = jnp.where(ii > jj, beta3 * scores[:, :group] * dec3, 0.0)
    qk3 = scores[:, group:] * dec3
    ainv3 = _unit_lower_inv(n3, CHUNK)
    eg3 = jnp.exp(g_col3)
    wuv3 = _bmm(ainv3, jnp.concatenate([beta3 * eg3 * k3, beta3 * v3], axis=2))
    qk_wuv = _bmm(qk3, wuv3)
    qp = (q3 * eg3 - qk_wuv[:, :, :HD]).reshape(lb, HD)
    op = qk_wuv[:, :, HD:].reshape(lb, HD)
    wuv_all = wuv3.reshape(lb, 2 * HD)
    g_chunks = g_col_all.reshape(nchunk, CHUNK, 1)
    g_end = jnp.broadcast_to(g_chunks[:, CHUNK - 1:, :], g_chunks.shape).reshape(lb, 1)
    kd = k * jnp.exp(g_end - g_col_all)
    s_decay = jnp.exp(g_end)
    nw = nw_ref[...]
    chunks_per_seq = nchunk // nsb
    for c in range(nchunk):
        r = slice(c * CHUNK, (c + 1) * CHUNK)
        b = c // chunks_per_seq
        an = _dot_tn(kd[r], wuv_all[r])
        s = s_ref[b]
        o = _dot(qp[r], s) + op[r]
        s_ref[b] = s * s_decay[c * CHUNK:c * CHUNK + 1] - _dot(an[:, :HD], s) + an[:, HD:]
        o_ref[r, :] = (_rms(o, nw) * _silu(z_ref[r, :])).astype(BF16)

    @pl.when(ib == nlb - 1)
    def _():
        so_ref[...] = s_ref[...]


MIX_ANY = pl.BlockSpec(memory_space=pl.ANY)
MIX_GDN, MIX_RET, MIX_RWKV, MIX_FOX = 0, 1, 2, 3


def _seq_blocking(row0, nseq, lpad, lb, nsb):
    assert (nsb == 1 and lpad % lb == 0) or (lb == nsb * lpad and nseq % nsb == 0)
    nlb = lpad * nsb // lb
    return nseq // nsb, nlb, lambda s, i: (row0 + s * lpad * nsb) // lb + i


def _gdn(p, mix, conv_w, conv_init, s0, a_log, dt_bias, norm_w, *, row0, nseq, lpad, lvalid, lb, nsb=1):
    ngroup, nlb, rb = _seq_blocking(row0, nseq, lpad, lb, nsb)
    pblk = lambda cb: pl.BlockSpec((lb, HD), lambda s, h, i: (rb(s, i), cb + h))
    cwblk = lambda j: pl.BlockSpec((CONV_WIDTH, HD), lambda s, h, i: (0, j * NH + h))
    ciblk = lambda j: pl.BlockSpec((nsb, SUBLANES, HD), lambda s, h, i: (s, 0, j * NH + h))
    sblk = pl.BlockSpec((nsb, None, HD, HD), lambda s, h, i: (s, h, 0, 0))
    vec = pl.BlockSpec((1, LANES), lambda s, h, i: (0, 0))
    lane_pad = lambda x: jnp.zeros((1, LANES), F32).at[0, :x.shape[0]].set(x)
    return pl.pallas_call(
        functools.partial(_gdn_kernel, lb=lb, lvalid=lvalid, nlb=nlb, nsb=nsb),
        grid=(ngroup, NH, nlb),
        in_specs=[
            pblk(CB_GDN), pblk(CB_GDN + NH), pblk(CB_GDN + 2 * NH), pblk(CB_GDN_Z),
            pl.BlockSpec((lb, LANES), lambda s, h, i: (rb(s, i), CB_SMALL)),
            cwblk(0), cwblk(1), cwblk(2), ciblk(0), ciblk(1), ciblk(2),
            sblk,
            vec, vec, vec, MIX_ANY,
        ],
        out_specs=[
            pl.BlockSpec((lb, HD), lambda s, h, i: (rb(s, i), MIX_GDN * NH + h)),
            sblk,
        ],
        out_shape=[
            jax.ShapeDtypeStruct(mix.shape, mix.dtype),
            jax.ShapeDtypeStruct((nseq, NH, HD, HD), F32),
        ],
        input_output_aliases={15: 0},
        scratch_shapes=[
            pltpu.VMEM((nsb, HD, HD), F32),
            pltpu.VMEM((3, nsb, SUBLANES, HD), F32),
            pltpu.VMEM((LANES, lb), F32),
        ],
        compiler_params=_cparams(("parallel", "parallel", "arbitrary")),
        name="gdn",
    )(p, p, p, p, p, conv_w, conv_w, conv_w, conv_init, conv_init, conv_init, s0,
      lane_pad(a_log), lane_pad(dt_bias), norm_w.reshape(1, HD), mix)


def _ret_kernel(q_ref, k_ref, v_ref, g_ref, cos_ref, sin_ref, lg_ref, s0_ref, mix_ref, o_ref, so_ref, s_ref,
                *, lb, cv, nlb, nsb):
    del mix_ref
    ib = pl.program_id(2)

    @pl.when(ib == 0)
    def _():
        s_ref[...] = s0_ref[...]

    cos = cos_ref[...]
    sin = sin_ref[...]
    rot = lambda x: x * cos + pltpu.roll(x, HD // 2, axis=1) * sin
    q = rot(q_ref[...])
    k = rot(k_ref[...]) * (HD ** -0.5)
    v = v_ref[...]
    lg = lg_ref[...][:, 0:1]

    ii = lax.broadcasted_iota(jnp.int32, (CHUNK, CHUNK), 0)
    jj = lax.broadcasted_iota(jnp.int32, (CHUNK, CHUNK), 1)
    rel = (ii - jj).astype(F32)
    dmat = jnp.where(rel >= 0, jnp.exp(jnp.maximum(rel, 0.0) * lg), 0.0)
    idx = lax.broadcasted_iota(jnp.int32, (CHUNK, 1), 0)
    idf = idx.astype(F32)
    xi = jnp.exp((idf + 1.0) * lg)
    zeta = jnp.where(idx < cv, jnp.exp((cv - 1.0 - idf) * lg), 0.0)
    gc = jnp.exp(cv * lg)
    nchunk = lb // CHUNK
    to3 = lambda x: x.reshape(nchunk, CHUNK, HD)
    q3, k3, v3 = to3(q), to3(k), to3(v)
    o_intra = _bmm(_bmm_nt(q3, k3) * dmat, v3)
    qx = q3 * xi
    kz = k3 * zeta
    chunks_per_seq = nchunk // nsb
    for b in range(nsb):
        s = s_ref[b]
        for c in range(b * chunks_per_seq, (b + 1) * chunks_per_seq):
            r = slice(c * CHUNK, (c + 1) * CHUNK)
            o = o_intra[c] + _dot(qx[c], s)
            s = s * gc + _dot_tn(kz[c], v3[c])
            o_ref[r, :] = (_rms(o) * _silu(g_ref[r, :])).astype(BF16)
        s_ref[b] = s

    @pl.when(ib == nlb - 1)
    def _():
        so_ref[...] = s_ref[...]


def _ret(p, mix, cos_t, sin_t, log_gamma, s0, *, row0, nseq, lpad, lvalid, lb, nsb=1):
    ngroup, nlb, rb = _seq_blocking(row0, nseq, lpad, lb, nsb)
    cv = CHUNK if lvalid == lpad else lvalid
    assert cv == CHUNK or (lpad == CHUNK and 0 < lvalid < CHUNK)
    pblk = lambda cb: pl.BlockSpec((lb, HD), lambda s, h, i: (rb(s, i), cb + h))
    tblk = pl.BlockSpec((lb, HD), lambda s, h, i: (i, 0))
    sblk = pl.BlockSpec((nsb, None, HD, HD), lambda s, h, i: (s, h, 0, 0))
    lg = jnp.broadcast_to(log_gamma[:, None, None], (NH, 1, LANES))
    return pl.pallas_call(
        functools.partial(_ret_kernel, lb=lb, cv=cv, nlb=nlb, nsb=nsb),
        grid=(ngroup, NH, nlb),
        in_specs=[
            pblk(CB_RET), pblk(CB_RET + NH), pblk(CB_RET + 2 * NH), pblk(CB_RET + 3 * NH),
            tblk, tblk,
            pl.BlockSpec((None, 1, LANES), lambda s, h, i: (h, 0, 0)),
            sblk, MIX_ANY,
        ],
        out_specs=[pl.BlockSpec((lb, HD), lambda s, h, i: (rb(s, i), MIX_RET * NH + h)), sblk],
        out_shape=[
            jax.ShapeDtypeStruct(mix.shape, mix.dtype),
            jax.ShapeDtypeStruct((nseq, NH, HD, HD), F32),
        ],
        input_output_aliases={8: 0},
        scratch_shapes=[pltpu.VMEM((nsb, HD, HD), F32)],
        compiler_params=_cparams(("parallel", "parallel", "arbitrary")),
        name="retention",
    )(p, p, p, p, cos_t, sin_t, lg, s0, mix)


def _half_sum(x):
    lane = lax.broadcasted_iota(jnp.int32, x.shape, 1)
    lo = lane < RWKV_HEAD
    s_lo = jnp.sum(jnp.where(lo, x, 0.0), axis=-1, keepdims=True)
    s_hi = jnp.sum(jnp.where(lo, 0.0, x), axis=-1, keepdims=True)
    return jnp.where(lo, s_lo, s_hi)


def _rwkv_kernel(r_ref, k_ref, v_ref, wa_ref, gd_ref, mur_ref, muk_ref, muv_ref, muwa_ref, mugd_ref,
                 sh_r_ref, sh_k_ref, sh_v_ref, sh_wa_ref, sh_gd_ref,
                 wup_ref, aup_ref, gup_ref, w0_ref, a0_ref, kk_ref, ka_ref, rk_ref, lnw_ref, lnb_ref,
                 s0_ref, mix_ref, o_ref, so_ref, s_ref, prev_ref, *, lb, lvalid, nlb, nsb):
    del mix_ref
    ib = pl.program_id(2)
    masked = lvalid < nlb * lb // nsb

    @pl.when(ib == 0)
    def _():
        s_ref[...] = s0_ref[...]
        prev_ref[0] = sh_r_ref[...]
        prev_ref[1] = sh_k_ref[...]
        prev_ref[2] = sh_v_ref[...]
        prev_ref[3] = sh_wa_ref[...]
        prev_ref[4] = sh_gd_ref[...]

    def shifted(x_ref, mu_ref, i):
        x = x_ref[...]
        prev = _shift_rows(x, prev_ref[i], 1)
        prev_ref[i] = _last_tiles(x, nsb)
        return x + (prev - x) * mu_ref[...]

    r = shifted(r_ref, mur_ref, 0)
    k = shifted(k_ref, muk_ref, 1)
    v = shifted(v_ref, muv_ref, 2)
    wa = shifted(wa_ref, muwa_ref, 3)
    gd = shifted(gd_ref, mugd_ref, 4)

    w_raw = -_softplus(-(w0_ref[...] + _dot(jnp.tanh(wa), wup_ref[...]))) - 0.5
    logw = -jnp.exp(w_raw)
    a_sig = _sigmoid(a0_ref[...] + _dot(wa, aup_ref[...]))
    gate = _dot(_sigmoid(gd), gup_ref[...])
    kk = k * kk_ref[...]
    kk = kk * lax.rsqrt(_half_sum(kk * kk) + NORM_EPS)
    kp = k * (1.0 + (a_sig - 1.0) * ka_ref[...])
    rec_a = -kk
    rec_b = kk * a_sig
    if masked:
        valid = _row_valid(ib, lb, lvalid, LANES, nsb)
        zero = lambda x: jnp.where(valid, x, 0.0)
        logw, rec_a, rec_b, kp, v = zero(logw), zero(rec_a), zero(rec_b), zero(kp), zero(v)

    cum = _dot_exact_lhs(_chunk_tri(lb, CHUNK), logw)
    e_pos = jnp.exp(cum)
    e_neg = jnp.exp(-cum)
    at_all = rec_a * jnp.exp(cum - logw)
    bt_all = rec_b * e_neg
    kt_all = kp * e_neg
    rt_all = r * e_pos

    stacked = 2 * CHUNK
    row_head = lax.broadcasted_iota(jnp.int32, (stacked, LANES), 0) // CHUNK
    lane_head = lax.broadcasted_iota(jnp.int32, (stacked, LANES), 1) // RWKV_HEAD
    own = row_head == lane_head
    nchunk = lb // CHUNK
    dup3 = lambda x: jnp.concatenate([x.reshape(nchunk, CHUNK, LANES)] * 2, axis=1)
    stack3 = lambda x: jnp.where(own, dup3(x), 0.0)
    block_diag = (lax.broadcasted_iota(jnp.int32, (LANES, LANES), 0) // RWKV_HEAD) == (
        lax.broadcasted_iota(jnp.int32, (LANES, LANES), 1) // RWKV_HEAD)
    ti = lax.broadcasted_iota(jnp.int32, (stacked, stacked), 0) % CHUNK
    tj = lax.broadcasted_iota(jnp.int32, (stacked, stacked), 1) % CHUNK

    at3, bt3, kt3, rt3, v3 = stack3(at_all), stack3(bt_all), stack3(kt_all), stack3(rt_all), dup3(v)
    scores = _bmm_nt(jnp.concatenate([at3, rt3], axis=1), jnp.concatenate([bt3, kt3], axis=1))
    l_ab = jnp.where(ti > tj, scores[:, :stacked, :stacked], 0.0)
    l_ak = jnp.where(ti > tj, scores[:, :stacked, stacked:], 0.0)
    l_rb = jnp.where(ti >= tj, scores[:, stacked:, :stacked], 0.0)
    l_rk = jnp.where(ti >= tj, scores[:, stacked:, stacked:], 0.0)
    inv = _unit_lower_inv(-l_ab, CHUNK)
    t12 = _bmm(inv, jnp.concatenate([at3, _bmm(l_ak, v3)], axis=2))
    t1 = t12[:, :, :LANES]
    t2 = jnp.where(own, t12[:, :, LANES:], 0.0)
    rb12 = _bmm(l_rb, jnp.concatenate([t1, t2], axis=2))
    rp = rt3 + rb12[:, :, :LANES]
    op = jnp.where(own, rb12[:, :, LANES:] + _bmm(l_rk, v3), 0.0)
    cum3 = cum.reshape(nchunk, CHUNK, LANES)
    cum_end = jnp.broadcast_to(cum3[:, CHUNK - 1:, :], cum3.shape).reshape(lb, LANES)
    to_end = jnp.exp(cum_end - cum)
    b_end = rec_b * to_end
    k_end = kp * to_end
    s_decay = jnp.exp(cum_end)

    chunks_per_seq = nchunk // nsb
    for c in range(nchunk):
        rs = slice(c * CHUNK, (c + 1) * CHUNK)
        b = c // chunks_per_seq
        vc = v[rs]
        b2 = jnp.concatenate([b_end[rs], b_end[rs]], axis=0)
        gh = _dot_tn(jnp.concatenate([t1[c], t2[c]], axis=1), b2)
        h = gh[LANES:] + _dot_tn(vc, k_end[rs])
        s = s_ref[b]
        o2 = jnp.where(own, _dot_nt(rp[c], s), 0.0) + op[c]
        o = o2[:CHUNK] + o2[CHUNK:]
        s_ref[b] = s * s_decay[c * CHUNK:c * CHUNK + 1] + jnp.where(block_diag, _dot(s, gh[:LANES]) + h, 0.0)

        inv_n = 1.0 / RWKV_HEAD
        mu = _half_sum(o) * inv_n
        var = _half_sum(jnp.square(o - mu)) * inv_n
        on = (o - mu) * lax.rsqrt(var + GN_EPS) * lnw_ref[...] + lnb_ref[...]
        bonus = _half_sum(r[rs] * kp[rs] * rk_ref[...]) * vc
        o_ref[rs, :] = ((on + bonus) * gate[rs]).astype(BF16)

    @pl.when(ib == nlb - 1)
    def _():
        so_ref[...] = s_ref[...]


def _rwkv(p, mix, lw, shift_init, s0, *, row0, nseq, lpad, lvalid, lb, nsb=1):
    ngroup, nlb, rb = _seq_blocking(row0, nseq, lpad, lb, nsb)
    npair = GW // LANES
    pblk = lambda cb, per_pair: pl.BlockSpec(
        (lb, LANES), lambda s, j, i: (rb(s, i), cb + (j if per_pair else 0)))
    mublk = lambda cb, per_pair: pl.BlockSpec((1, LANES), lambda s, j, i: (0, cb + (j if per_pair else 0)))
    shblk = lambda cb, per_pair: pl.BlockSpec(
        (nsb, SUBLANES, LANES), lambda s, j, i: (s, 0, cb + (j if per_pair else 0)))
    pair_vec = pl.BlockSpec((1, LANES), lambda s, j, i: (0, j))
    pair_mat = pl.BlockSpec((LANES, LANES), lambda s, j, i: (0, j))
    sblk = pl.BlockSpec((nsb, None, LANES, LANES), lambda s, j, i: (s, j, 0, 0))
    mu = lw['rwkv_mu'].reshape(1, -1)
    zeros64 = jnp.zeros((64, GW), F32)
    wup = jnp.concatenate([lw['rwkv_w_up'], zeros64], axis=0).astype(BF16)
    aup = jnp.concatenate([zeros64, lw['rwkv_a_up']], axis=0).astype(BF16)
    row = lambda x: x.reshape(1, GW)
    blocks = [(0, True), (4, True), (8, True), (12, False), (13, False)]
    return pl.pallas_call(
        functools.partial(_rwkv_kernel, lb=lb, lvalid=lvalid, nlb=nlb, nsb=nsb),
        grid=(ngroup, npair, nlb),
        in_specs=(
            [pblk(CB_RWKV + cb, pp) for cb, pp in blocks]
            + [mublk(cb, pp) for cb, pp in blocks]
            + [shblk(cb, pp) for cb, pp in blocks]
            + [pair_mat, pair_mat, pair_mat] + [pair_vec] * 7 + [sblk, MIX_ANY]
        ),
        out_specs=[pl.BlockSpec((lb, LANES), lambda s, j, i: (rb(s, i), MIX_RWKV * npair + j)), sblk],
        out_shape=[
            jax.ShapeDtypeStruct(mix.shape, mix.dtype),
            jax.ShapeDtypeStruct((nseq, npair, LANES, LANES), F32),
        ],
        input_output_aliases={26: 0},
        scratch_shapes=[pltpu.VMEM((nsb, LANES, LANES), F32), pltpu.VMEM((5, nsb, SUBLANES, LANES), F32)],
        compiler_params=_cparams(("parallel", "parallel", "arbitrary")),
        name="rwkv7",
    )(p, p, p, p, p, mu, mu, mu, mu, mu, shift_init, shift_init, shift_init, shift_init, shift_init,
      wup, aup, lw['rwkv_g_up'].astype(BF16), row(lw['rwkv_w0']), row(lw['rwkv_a0']), row(lw['rwkv_k_k']),
      row(lw['rwkv_k_a']), row(lw['rwkv_r_k']), row(lw['rwkv_ln_w']), row(lw['rwkv_ln_b']), s0, mix)


def _fox_prep_kernel(q_ref, k_ref, v_ref, sm_ref, qw_ref, kw_ref, bf_ref,
                     qa_ref, kn_ref, ka_ref, vb_ref, lf_ref, c_ref, carry_ref, *, lb):
    @pl.when(pl.program_id(1) == 0)
    def _():
        carry_ref[...] = jnp.zeros_like(carry_ref)

    logf = -_softplus(-(sm_ref[...] + bf_ref[...]))
    lf_ref[...] = logf
    c = _dot_exact_lhs(_chunk_tri(lb, lb), logf) + carry_ref[0:1, :]
    c_ref[...] = c
    carry_ref[...] = jnp.broadcast_to(c[lb - 1:], carry_ref.shape)

    qw = qw_ref[...]
    kw = kw_ref[...]
    lane = lax.broadcasted_iota(jnp.int32, (lb, HD), 1)
    q_tail = jnp.where(lane < 2, -1.0, 0.0).astype(BF16)
    for h in range(NH):
        sl = slice(h * HD, (h + 1) * HD)
        a0 = h * FOX_AUG
        qa_ref[:, a0:a0 + HD] = (_rms(q_ref[:, sl], qw) * (HD ** -0.5 * LOG2E)).astype(BF16)
        qa_ref[:, a0 + HD:a0 + FOX_AUG] = q_tail
        kn = _rms(k_ref[:, sl], kw)
        kn_ref[:, sl] = kn
        ka_ref[:, a0:a0 + HD] = kn.astype(BF16)
        c2 = c[:, SM_F + h:SM_F + h + 1] * LOG2E
        c_hi = c2.astype(BF16).astype(F32)
        ka_ref[:, a0 + HD:a0 + FOX_AUG] = jnp.where(lane == 0, c_hi, jnp.where(lane == 1, c2 - c_hi, 0.0)).astype(BF16)
    vb_ref[...] = v_ref[...].astype(BF16)


def _fox_prep(p, q_w, k_w, b_f, *, nseq, lpad, lb):
    nlb = lpad // lb
    rb = lambda s, i: s * nlb + i
    seg = lambda j: pl.BlockSpec((lb, GW), lambda s, i: (rb(s, i), j))
    vec = pl.BlockSpec((1, LANES), lambda s, i: (0, 0))
    rows = nseq * lpad
    bf_lane = jnp.zeros((1, LANES), F32).at[0, SM_F:SM_F + NH].set(b_f)
    return pl.pallas_call(
        functools.partial(_fox_prep_kernel, lb=lb),
        grid=(nseq, nlb),
        in_specs=[seg(0), seg(1), seg(2),
                  pl.BlockSpec((lb, LANES), lambda s, i: (rb(s, i), CB_SMALL)), vec, vec, vec],
        out_specs=[
            pl.BlockSpec((lb, NH * FOX_AUG), lambda s, i: (rb(s, i), 0)),
            pl.BlockSpec((lb, GW), lambda s, i: (rb(s, i), 0)),
            pl.BlockSpec((lb, NH * FOX_AUG), lambda s, i: (rb(s, i), 0)),
            pl.BlockSpec((lb, GW), lambda s, i: (rb(s, i), 0)),
            pl.BlockSpec((lb, LANES), lambda s, i: (rb(s, i), 0)),
            pl.BlockSpec((lb, LANES), lambda s, i: (rb(s, i), 0)),
        ],
        out_shape=[
            jax.ShapeDtypeStruct((rows, NH * FOX_AUG), BF16),
            jax.ShapeDtypeStruct((rows, GW), F32),
            jax.ShapeDtypeStruct((rows, NH * FOX_AUG), BF16),
            jax.ShapeDtypeStruct((rows, GW), BF16),
            jax.ShapeDtypeStruct((rows, LANES), F32),
            jax.ShapeDtypeStruct((rows, LANES), F32),
        ],
        scratch_shapes=[pltpu.VMEM((SUBLANES, LANES), F32)],
        compiler_params=_cparams(("parallel", "arbitrary")),
        name="fox_prep",
    )(p, p, p, p, q_w.reshape(1, HD), k_w.reshape(1, HD), bf_lane)


def _fox_flash_kernel(q_ref, k_ref, v_ref, c_ref, g_ref, mix_ref, o_ref, m_ref, l_ref, acc_ref, *, tq):
    del mix_ref
    h0 = pl.program_id(1) * FOX_HEADS_PER_STEP
    qi = pl.program_id(2)
    c_all = c_ref[...]
    c_cols = [_lane_col(c_all, SM_F + h0 + e) * LOG2E for e in range(FOX_HEADS_PER_STEP)]
    m_ref[...] = jnp.full_like(m_ref, -1e30)
    l_ref[...] = jnp.zeros_like(l_ref)
    acc_ref[...] = jnp.zeros_like(acc_ref)

    def block(ks, width, diagonal):
        for e in range(FOX_HEADS_PER_STEP):
            sl = slice(e * HD, (e + 1) * HD)
            sa = slice(e * FOX_AUG, (e + 1) * FOX_AUG)
            vb = v_ref[pl.ds(ks, width), sl]
            t = lax.dot_general(q_ref[:, sa], k_ref[pl.ds(ks, width), sa], (((1,), (1,)), ((), ())),
                                preferred_element_type=F32)
            if diagonal:
                ii = lax.broadcasted_iota(jnp.int32, (tq, width), 0)
                jj = lax.broadcasted_iota(jnp.int32, (tq, width), 1)
                t = jnp.where(jj <= ii + (width - tq), t, -jnp.inf)
            m_old = m_ref[e]
            m_new = jnp.maximum(m_old, jnp.max(t, axis=-1, keepdims=True) + c_cols[e])
            alpha = jnp.exp2(m_old - m_new)
            pr = jnp.exp2(t - (m_new - c_cols[e]))
            l_ref[e] = alpha * l_ref[e] + jnp.sum(pr, axis=-1, keepdims=True)
            acc_ref[e] = alpha * acc_ref[e] + _dot(pr, vb)
            m_ref[e] = m_new

    def body(kp, carry):
        block(pl.multiple_of(kp * 2 * tq, 2 * tq), 2 * tq, False)
        return carry

    lax.fori_loop(0, qi // 2, body, 0)

    @pl.when(qi % 2 == 1)
    def _():
        block(pl.multiple_of((qi - 1) * tq, tq), 2 * tq, True)

    @pl.when(qi % 2 == 0)
    def _():
        block(pl.multiple_of(qi * tq, tq), tq, True)
    for e in range(FOX_HEADS_PER_STEP):
        sl = slice(e * HD, (e + 1) * HD)
        o_ref[:, sl] = (acc_ref[e] / l_ref[e] * _sigmoid(g_ref[:, sl])).astype(BF16)


def _fox_flash(qa, ka, vb, c_col, p, mix, *, nseq, lpad, tq):
    nq = lpad // tq
    hw = FOX_HEADS_PER_STEP * HD
    aw = FOX_HEADS_PER_STEP * FOX_AUG
    g_block0 = (CB_FOX + 3 * NH) // FOX_HEADS_PER_STEP
    o_block0 = MIX_FOX * (GW // hw)
    return pl.pallas_call(
        functools.partial(_fox_flash_kernel, tq=tq),
        grid=(nseq, NH // FOX_HEADS_PER_STEP, nq),
        in_specs=[
            pl.BlockSpec((tq, aw), lambda s, h, i: (s * nq + i, h)),
            pl.BlockSpec((lpad, aw), lambda s, h, i: (s, h)),
            pl.BlockSpec((lpad, hw), lambda s, h, i: (s, h)),
            pl.BlockSpec((tq, LANES), lambda s, h, i: (s * nq + i, 0)),
            pl.BlockSpec((tq, hw), lambda s, h, i: (s * nq + i, g_block0 + h)),
            MIX_ANY,
        ],
        out_specs=pl.BlockSpec((tq, hw), lambda s, h, i: (s * nq + i, o_block0 + h)),
        out_shape=jax.ShapeDtypeStruct(mix.shape, mix.dtype),
        input_output_aliases={5: 0},
        scratch_shapes=[pltpu.VMEM((FOX_HEADS_PER_STEP, tq, 1), F32), pltpu.VMEM((FOX_HEADS_PER_STEP, tq, 1), F32),
                        pltpu.VMEM((FOX_HEADS_PER_STEP, tq, HD), F32)],
        compiler_params=_cparams(("parallel", "parallel", "arbitrary")),
        name="fox_flash",
    )(qa, ka, vb, c_col, p, mix)


PAGE_GROUP = 16
PAGE_COLS = PAGE * NH
NQ_PAD = SUBLANES


def _page_sums_kernel(lf_ref, upper_ref, heads_ref, o_ref):
    lf = lf_ref[...]
    o_ref[:, :PAGE_COLS] = _dot_exact_rhs(lf, upper_ref[...])
    o_ref[:, PAGE_COLS:] = _dot_exact_rhs(lf, heads_ref[...])


def _page_sums(cache_lf):
    depth, n_pool = cache_lf.shape[:2]
    rows = depth * n_pool
    tile = _row_tile(rows, 512)
    idx = np.arange(PAGE_COLS)
    same_head = (idx[:, None] % NH) == (idx[None, :] % NH)
    upper = jnp.asarray(same_head & (idx[:, None] // NH > idx[None, :] // NH), BF16)
    heads = jnp.asarray(same_head, BF16)
    const = pl.BlockSpec((PAGE_COLS, PAGE_COLS), lambda i: (0, 0))
    sums = pl.pallas_call(
        _page_sums_kernel,
        grid=(rows // tile,),
        in_specs=[pl.BlockSpec((tile, PAGE_COLS), lambda i: (i, 0)), const, const],
        out_specs=pl.BlockSpec((tile, 2 * PAGE_COLS), lambda i: (i, 0)),
        out_shape=jax.ShapeDtypeStruct((rows, 2 * PAGE_COLS), F32),
        compiler_params=_cparams(("parallel",)),
        name="page_sums",
    )(cache_lf.reshape(rows, PAGE_COLS), upper, heads)
    return sums.reshape(depth, n_pool, 2, PAGE_COLS)


def _fox_sample_kernel(pt_ref, pq_ref, sm_ref, *rest, lvalid, lpad, n_steps):
    del pt_ref
    g = PAGE_GROUP
    kps, vps, sums = rest[:g], rest[g:2 * g], rest[2 * g:3 * g]
    (qw_ref, kw_ref, bf_ref, mix_ref, o_ref, kn_ref, lf_ref,
     qn_s, cq_s, m_s, l_s, acc_s, carry_s) = rest[3 * g:]
    del mix_ref
    i = pl.program_id(1)
    scale = HD ** -0.5
    nq = NQ_PAD
    nrow = NH * nq

    @pl.when(i == 0)
    def _():
        logf = -_softplus(-(sm_ref[...] + bf_ref[...]))
        logf = jnp.where(_row_valid(0, lpad, lvalid, LANES), logf, 0.0)
        lf_ref[...] = logf[:nq]
        c = _dot_exact_lhs(_chunk_tri(lpad, lpad), logf)
        c_t = c.T
        carry_s[...] = jnp.zeros_like(carry_s)
        qi = lax.broadcasted_iota(jnp.int32, (nq, lpad), 0)
        kj = lax.broadcasted_iota(jnp.int32, (nq, lpad), 1)
        for h in range(NH):
            sl = slice(h * HD, (h + 1) * HD)
            rows = slice(h * nq, (h + 1) * nq)
            qn = _rms(pq_ref[:nq, sl], qw_ref[...])
            kn = _rms(pq_ref[:, GW + h * HD:GW + (h + 1) * HD], kw_ref[...])
            vn = pq_ref[:, 2 * GW + h * HD:2 * GW + (h + 1) * HD]
            c_h = c[:nq, SM_F + h:SM_F + h + 1]
            qn_s[rows, :] = qn
            cq_s[rows, :] = c_h
            kn_ref[:, sl] = kn[:nq]
            s = _dot_nt(qn, kn) * scale + c_h - c_t[SM_F + h:SM_F + h + 1, :]
            s = jnp.where(kj <= qi, s, -jnp.inf)
            m = jnp.max(s, axis=-1, keepdims=True)
            pr = jnp.exp(s - m)
            m_s[rows, :] = m
            l_s[rows, :] = jnp.sum(pr, axis=-1, keepdims=True)
            acc_s[rows, :] = _dot(pr, vn)

    run = carry_s[...]
    suffix = [None] * g
    for j in reversed(range(g)):
        suffix[j] = sums[j][0:1, :] + run
        run = run + sums[j][1:2, :]
    carry_s[...] = run

    row_head = lax.broadcasted_iota(jnp.int32, (nrow, PAGE_COLS), 0) // nq
    col_head = lax.broadcasted_iota(jnp.int32, (nrow, PAGE_COLS), 1) % NH
    own = row_head == col_head
    qs = qn_s[...].astype(BF16)
    bias = cq_s[...]
    tiles = [jnp.where(own, _dot_nt(qs, kps[j][...]) * scale + bias + suffix[j], -jnp.inf) for j in range(g)]
    m_old = m_s[...]
    m_new = m_old
    for t in tiles:
        m_new = jnp.maximum(m_new, jnp.max(t, axis=-1, keepdims=True))
    alpha = jnp.exp(m_old - m_new)
    l_new = alpha * l_s[...]
    acc = alpha * acc_s[...]
    for j, t in enumerate(tiles):
        pr = jnp.exp(t - m_new)
        l_new = l_new + jnp.sum(pr, axis=-1, keepdims=True)
        acc = acc + _dot(pr, vps[j][...])
    m_s[...] = m_new
    l_s[...] = l_new
    acc_s[...] = acc

    @pl.when(i == n_steps - 1)
    def _():
        o_ref[...] = jnp.zeros_like(o_ref)
        out = acc_s[...] / l_s[...]
        for h in range(NH):
            sl = slice(h * HD, (h + 1) * HD)
            gate = pq_ref[:nq, 3 * GW + h * HD:3 * GW + (h + 1) * HD]
            o_ref[:nq, sl] = (out[h * nq:(h + 1) * nq] * _sigmoid(gate)).astype(BF16)


def _fox_sample(p, mix, page_table, cache_k, cache_v, page_sums, q_w, k_w, b_f, *, layer, row0, nseq, lpad, lvalid):
    n_pages = page_table.shape[1]
    g = PAGE_GROUP
    assert n_pages % g == 0 and lvalid <= NQ_PAD
    n_steps = n_pages // g
    bf_lane = jnp.zeros((1, LANES), F32).at[0, SM_F:SM_F + NH].set(b_f)
    rb = lambda b: row0 // lpad + b
    vec = pl.BlockSpec((1, LANES), lambda b, i, pt: (0, 0))

    def page_spec(shape, j):
        zeros = (0,) * len(shape)
        return pl.BlockSpec((None, None) + shape,
                            lambda b, i, pt: (layer, pt[b, n_pages - g * (i + 1) + j]) + zeros)

    grid_spec = pltpu.PrefetchScalarGridSpec(
        num_scalar_prefetch=1,
        grid=(nseq, n_steps),
        in_specs=(
            [pl.BlockSpec((lpad, 4 * GW), lambda b, i, pt: (rb(b), CB_FOX)),
             pl.BlockSpec((lpad, LANES), lambda b, i, pt: (rb(b), CB_SMALL))]
            + [page_spec((PAGE_COLS, HD), j) for j in range(g)]
            + [page_spec((PAGE_COLS, HD), j) for j in range(g)]
            + [page_spec((2, PAGE_COLS), j) for j in range(g)]
            + [vec, vec, vec, MIX_ANY]
        ),
        out_specs=[
            pl.BlockSpec((lpad, GW), lambda b, i, pt: (rb(b), MIX_FOX)),
            pl.BlockSpec((None, NQ_PAD, GW), lambda b, i, pt: (b, 0, 0)),
            pl.BlockSpec((None, NQ_PAD, LANES), lambda b, i, pt: (b, 0, 0)),
        ],
        scratch_shapes=[
            pltpu.VMEM((NH * NQ_PAD, HD), F32),
            pltpu.VMEM((NH * NQ_PAD, 1), F32),
            pltpu.VMEM((NH * NQ_PAD, 1), F32),
            pltpu.VMEM((NH * NQ_PAD, 1), F32),
            pltpu.VMEM((NH * NQ_PAD, HD), F32),
            pltpu.VMEM((1, PAGE_COLS), F32),
        ],
    )
    return pl.pallas_call(
        functools.partial(_fox_sample_kernel, lvalid=lvalid, lpad=lpad, n_steps=n_steps),
        grid_spec=grid_spec,
        out_shape=[
            jax.ShapeDtypeStruct(mix.shape, mix.dtype),
            jax.ShapeDtypeStruct((nseq, NQ_PAD, GW), F32),
            jax.ShapeDtypeStruct((nseq, NQ_PAD, LANES), F32),
        ],
        input_output_aliases={3 * g + 6: 0},
        compiler_params=_cparams(("parallel", "arbitrary")),
        name="fox_sample",
    )(page_table, p, p, *([cache_k] * g), *([cache_v] * g), *([page_sums] * g),
      q_w.reshape(1, HD), k_w.reshape(1, HD), bf_lane, mix)


W_IN_SHIFT = 8
_COPY, _SHIFT, _SMALL, _ZERO = 0, 1, 2, 3


def _w_in_plan():
    kind = np.zeros(NP_COLS // LANES, np.int32)
    src = np.zeros(NP_COLS // LANES, np.int32)

    def put(cb, n, first_src_block, k):
        kind[cb:cb + n] = k
        src[cb:cb + n] = first_src_block + np.arange(n)

    put(CB_FOX, 16, (5896 - W_IN_SHIFT) // LANES, _SHIFT)
    put(CB_RET, 16, (2056 - W_IN_SHIFT) // LANES, _SHIFT)
    put(CB_GDN, 12, 0, _COPY)
    put(CB_GDN_Z, 4, (1544 - W_IN_SHIFT) // LANES, _SHIFT)
    put(CB_RWKV, 14, (4104 - W_IN_SHIFT) // LANES, _SHIFT)
    put(CB_SMALL, 1, 1536 // LANES, _SMALL)
    put(CB_SMALL + 1, 1, 0, _ZERO)
    src_b = np.where(kind == _SHIFT, (src + 1) * (LANES // SUBLANES),
                     np.where(kind == _SMALL, 7944 // SUBLANES, 0))
    return jnp.asarray(kind), jnp.asarray(src), jnp.asarray(src_b.astype(np.int32))


def _prep_w_in_kernel(kind_ref, sa_ref, sb_ref, a_ref, b_ref, o_ref):
    del sa_ref, sb_ref
    kind = kind_ref[pl.program_id(0)]
    depth = o_ref.shape[0]
    row8 = lax.broadcasted_iota(jnp.int32, (SUBLANES, o_ref.shape[2]), 0)

    @pl.when(kind == _COPY)
    def _():
        for l in range(depth):
            o_ref[l] = a_ref[:, l, :].astype(BF16)

    @pl.when(kind == _SHIFT)
    def _():
        for l in range(depth):
            o_ref[l] = jnp.concatenate([a_ref[:, l, :][W_IN_SHIFT:], b_ref[:, l, :]], axis=0).astype(BF16)

    @pl.when(kind == _SMALL)
    def _():
        for l in range(depth):
            f_rows = jnp.where(row8 < NH, b_ref[:, l, :], 0.0)
            zeros = jnp.zeros((LANES - SM_F - SUBLANES, o_ref.shape[2]), F32)
            o_ref[l] = jnp.concatenate([a_ref[:, l, :][:SM_F], f_rows, zeros], axis=0).astype(BF16)

    @pl.when(kind == _ZERO)
    def _():
        o_ref[...] = jnp.zeros_like(o_ref)


def _prep_w_in(w_in):
    depth, k, _ = w_in.shape
    w_t = jnp.transpose(w_in, (2, 0, 1))
    kind, src_a, src_b = _w_in_plan()
    grid_spec = pltpu.PrefetchScalarGridSpec(
        num_scalar_prefetch=3,
        grid=(NP_COLS // LANES,),
        in_specs=[
            pl.BlockSpec((LANES, depth, k), lambda j, kd, sa, sb: (sa[j], 0, 0)),
            pl.BlockSpec((SUBLANES, depth, k), lambda j, kd, sa, sb: (sb[j], 0, 0)),
        ],
        out_specs=pl.BlockSpec((depth, LANES, k), lambda j, kd, sa, sb: (0, j, 0)),
    )
    return pl.pallas_call(
        _prep_w_in_kernel,
        grid_spec=grid_spec,
        out_shape=jax.ShapeDtypeStruct((depth, NP_COLS, k), BF16),
        compiler_params=_cparams(("parallel",)),
        name="prep_w_in",
    )(kind, src_a, src_b, w_t, w_t)


def _rope_tables(pos):
    half = HD // 2
    inv = 1.0 / (ROPE_BASE ** jnp.linspace(0.0, 1.0, half, dtype=F32))
    ang = pos.astype(F32)[:, None] * inv[None, :]
    cos, sin = jnp.cos(ang), jnp.sin(ang)
    return jnp.concatenate([cos, cos], axis=-1), jnp.concatenate([-sin, sin], axis=-1)


def _state_tile(state, nrows):
    b, _, c = state.shape
    return jnp.concatenate([jnp.zeros((b, SUBLANES - nrows, c), F32), state], axis=1)


def _rwkv_pair_states(s):
    b = s.shape[0]
    s = s.reshape(b, 4, 2, RWKV_HEAD, RWKV_HEAD)
    z = jnp.zeros_like(s[:, :, 0])
    top = jnp.concatenate([s[:, :, 0], z], axis=-1)
    bot = jnp.concatenate([z, s[:, :, 1]], axis=-1)
    return jnp.concatenate([top, bot], axis=-2)


def _rwkv_unpair_states(sp):
    b = sp.shape[0]
    a = sp[:, :, :RWKV_HEAD, :RWKV_HEAD]
    c = sp[:, :, RWKV_HEAD:, RWKV_HEAD:]
    return jnp.stack([a, c], axis=2).reshape(b, 8, RWKV_HEAD, RWKV_HEAD)


def kernel(x_prompt, x_sample, cache_fox_k, cache_fox_v, cache_fox_logf, cache_mem_k, cache_mem_v, state_gdn_conv, state_gdn_S, state_ret_S, state_rwkv_shift, state_rwkv_S, page_table, mem_prompt, norm_mix, w_in, gdn_conv_w, gdn_A_log, gdn_dt_bias, gdn_norm, rwkv_mu, rwkv_w0, rwkv_w_up, rwkv_a0, rwkv_a_up, rwkv_g_up, rwkv_k_k, rwkv_k_a, rwkv_r_k, rwkv_ln_w, rwkv_ln_b, fox_b_f, fox_q_norm, fox_k_norm, w_out, norm_x, norm_mem, xattn_wq, xattn_wkv, xattn_q_norm, xattn_k_norm, xattn_wo, norm_ffn, ffn_w_gate, ffn_w_up, ffn_w_down):
    weights = {
        'norm_mix': norm_mix, 'w_in': w_in, 'gdn_conv_w': gdn_conv_w, 'gdn_A_log': gdn_A_log,
        'gdn_dt_bias': gdn_dt_bias, 'gdn_norm': gdn_norm, 'rwkv_mu': rwkv_mu, 'rwkv_w0': rwkv_w0,
        'rwkv_w_up': rwkv_w_up, 'rwkv_a0': rwkv_a0, 'rwkv_a_up': rwkv_a_up, 'rwkv_g_up': rwkv_g_up,
        'rwkv_k_k': rwkv_k_k, 'rwkv_k_a': rwkv_k_a, 'rwkv_r_k': rwkv_r_k, 'rwkv_ln_w': rwkv_ln_w,
        'rwkv_ln_b': rwkv_ln_b, 'fox_b_f': fox_b_f, 'fox_q_norm': fox_q_norm, 'fox_k_norm': fox_k_norm,
        'w_out': w_out, 'norm_x': norm_x, 'norm_mem': norm_mem, 'xattn_wq': xattn_wq, 'xattn_wkv': xattn_wkv,
        'xattn_q_norm': xattn_q_norm, 'xattn_k_norm': xattn_k_norm, 'xattn_wo': xattn_wo,
        'norm_ffn': norm_ffn, 'ffn_w_gate': ffn_w_gate, 'ffn_w_up': ffn_w_up, 'ffn_w_down': ffn_w_down,
    }
    depth = w_in.shape[0]
    bp, lp, d = x_prompt.shape
    bs, ls, _ = x_sample.shape
    n_pages = page_table.shape[1]
    past_len = n_pages * PAGE
    tp = bp * lp
    ts = bs * SAMPLE_PAD
    tt = tp + ts
    tm = _row_tile(tt)
    lb_p = min(256, lp)
    assert ls >= CONV_WIDTH - 1 and ls <= SUBLANES and lp % lb_p == 0

    xs_pad = jnp.zeros((bs, SAMPLE_PAD, d), F32).at[:, :ls].set(x_sample)
    x = jnp.concatenate([x_prompt.reshape(tp, d), xs_pad.reshape(ts, d)], axis=0)

    cos_p, sin_p = _rope_tables(jnp.arange(lp, dtype=jnp.int32))
    cos_s, sin_s = _rope_tables(jnp.tile(past_len + jnp.arange(SAMPLE_PAD, dtype=jnp.int32), bs))
    log_gamma = jnp.log(1.0 - jnp.exp2(-(RET_GAMMA_BASE + jnp.arange(NH, dtype=F32))))
    n_pool = cache_fox_k.shape[1]
    cache_k = cache_fox_k.reshape(depth, n_pool, PAGE_COLS, HD)
    cache_v = cache_fox_v.reshape(depth, n_pool, PAGE_COLS, HD)
    page_sums = _page_sums(cache_fox_logf.reshape(depth, n_pool, 1, PAGE_COLS))
    zeros_s = jnp.zeros((bp, NH, HD, HD), F32)
    zeros_conv = jnp.zeros((bp, SUBLANES, 3 * GW), F32)
    zeros_shift = jnp.zeros((bp, SUBLANES, 1792), F32)

    w_in_t = _prep_w_in(w_in)

    outs_p, outs_s, mem_ks, mem_vs = [], [], [], []
    for l in range(depth):
        lw = {name: arr[l] for name, arr in weights.items()}
        p = _norm_matmul(x, lw['norm_mix'], w_in_t, l, tm=tm, tn=1024,
                         transposed_w=True)

        o_mix = jnp.zeros((tt, 4 * GW), BF16)
        gp = dict(row0=0, nseq=bp, lpad=lp, lvalid=lp, lb=lb_p)
        o_mix, gdn_s_p = _gdn(p, o_mix, lw['gdn_conv_w'], zeros_conv, zeros_s, lw['gdn_A_log'], lw['gdn_dt_bias'],
                              lw['gdn_norm'], **{**gp, 'lb': min(GDN_LB, lp)})
        o_mix, ret_s_p = _ret(p, o_mix, cos_p, sin_p, log_gamma, zeros_s, **{**gp, 'lb': min(RET_LB, lp)})
        o_mix, rwkv_s_p = _rwkv(p, o_mix, lw, zeros_shift, zeros_s, **{**gp, 'lb': min(RWKV_LB, lp)})
        qa, kn, ka, vb, lf, c_col = _fox_prep(p, lw['fox_q_norm'], lw['fox_k_norm'], lw['fox_b_f'],
                                              nseq=bp, lpad=lp, lb=lb_p)
        o_mix = _fox_flash(qa, ka, vb, c_col, p, o_mix, nseq=bp, lpad=lp, tq=min(FOX_TQ, lp))

        gs = dict(row0=tp, nseq=bs, lpad=SAMPLE_PAD, lvalid=ls, lb=ts, nsb=bs)
        o_mix, gdn_s_s = _gdn(p, o_mix, lw['gdn_conv_w'], _state_tile(state_gdn_conv[l], CONV_WIDTH - 1),
                              state_gdn_S[l], lw['gdn_A_log'], lw['gdn_dt_bias'], lw['gdn_norm'], **gs)
        o_mix, ret_s_s = _ret(p, o_mix, cos_s, sin_s, log_gamma, state_ret_S[l], **gs)
        o_mix, rwkv_s_s = _rwkv(p, o_mix, lw, _state_tile(state_rwkv_shift[l], 1),
                                _rwkv_pair_states(state_rwkv_S[l]), **gs)
        o_mix, kn_s, lf_s = _fox_sample(p, o_mix, page_table, cache_k, cache_v, page_sums,
                                        lw['fox_q_norm'], lw['fox_k_norm'], lw['fox_b_f'],
                                        layer=l, row0=tp, nseq=bs, lpad=SAMPLE_PAD, lvalid=ls)
        x = _matmul_res(o_mix, w_out, l, x, tm=tm, tn=512)

        kv = _norm_matmul(mem_prompt.reshape(bp * N_MEM, d), lw['norm_mem'], xattn_wkv, l,
                          tm=256, tn=XW, head_w=lw['xattn_k_norm'], norm_tiles=1)
        mk = kv[:, :XW].reshape(bp, N_MEM, XW)
        mv = kv[:, XW:].reshape(bp, N_MEM, XW)
        q = _norm_matmul(x, lw['norm_x'], xattn_wq, l, tm=tm, tn=XW,
                         head_w=lw['xattn_q_norm'], norm_tiles=1)
        xo_p = _xattn(q, mk, mv, row0=0, nrows=tp, tq=lb_p, rows_per_seq=lp)
        xo_s = _xattn(q, cache_mem_k[l].reshape(bs, N_MEM, XW), cache_mem_v[l].reshape(bs, N_MEM, XW),
                      row0=tp, nrows=ts, tq=SAMPLE_PAD, rows_per_seq=SAMPLE_PAD)
        x = _matmul_res(jnp.concatenate([xo_p, xo_s], axis=0), xattn_wo, l, x, tm=tm, tn=512)

        hidden = _swiglu_up(x, lw['norm_ffn'], ffn_w_gate, ffn_w_up, l, tm=tm, tn=512)
        x = _matmul_res(hidden, ffn_w_down, l, x, tm=tm, tn=256)

        c0 = CB_GDN * LANES
        r0 = CB_RWKV * LANES
        v0 = (CB_FOX + 2 * NH) * LANES

        def last_rows(row_end, n, col0, width, nseq, stride):
            return jnp.stack([lax.slice(p, (b * stride + row_end - n, col0), (b * stride + row_end, col0 + width))
                              for b in range(nseq)], axis=0)

        ps_v = lax.slice(p, (tp, v0), (tt, v0 + GW)).reshape(bs, SAMPLE_PAD, NH, HD)
        outs_p.append((
            kn.reshape(bp, lp, NH, HD),
            lax.slice(p, (0, v0), (tp, v0 + GW)).reshape(bp, lp, NH, HD),
            lf.reshape(bp, lp, LANES)[:, :, SM_F:SM_F + NH],
            last_rows(lp, CONV_WIDTH - 1, c0, 3 * GW, bp, lp),
            gdn_s_p, ret_s_p,
            last_rows(lp, 1, r0, 1792, bp, lp),
            _rwkv_unpair_states(rwkv_s_p),
        ))
        outs_s.append((
            kn_s[:, :ls].reshape(bs, ls, NH, HD),
            ps_v[:, :ls],
            lf_s[:, :ls, SM_F:SM_F + NH],
            last_rows(tp + ls, CONV_WIDTH - 1, c0, 3 * GW, bs, SAMPLE_PAD),
            gdn_s_s, ret_s_s,
            last_rows(tp + ls, 1, r0, 1792, bs, SAMPLE_PAD),
            _rwkv_unpair_states(rwkv_s_s),
        ))
        mem_ks.append(mk.reshape(bp, N_MEM, NH, HD))
        mem_vs.append(mv.reshape(bp, N_MEM, NH, HD))

    stk = lambda seq, i: jnp.stack([e[i] for e in seq], axis=0)
    yp = x[:tp].reshape(bp, lp, d)
    ys = x[tp:].reshape(bs, SAMPLE_PAD, d)[:, :ls]
    return (yp, ys, stk(outs_p, 0), stk(outs_p, 1), stk(outs_p, 2), jnp.stack(mem_ks, 0), jnp.stack(mem_vs, 0),
            stk(outs_p, 3), stk(outs_p, 4), stk(outs_p, 5), stk(outs_p, 6), stk(outs_p, 7),
            stk(outs_s, 0), stk(outs_s, 1), stk(outs_s, 2), stk(outs_s, 3), stk(outs_s, 4), stk(outs_s, 5),
            stk(outs_s, 6), stk(outs_s, 7))
```

```python
import functools
import math

import jax
import jax.numpy as jnp
import numpy as np
from jax import lax
from jax.experimental import pallas as pl
from jax.experimental.pallas import tpu as pltpu

F32 = jnp.float32
BF16 = jnp.bfloat16

LANES = 128
SUBLANES = 8
VMEM_LIMIT = 56 * 1024 * 1024

D_MODEL = 2048
GW = D_MODEL // 4
HD = 128
NH = GW // HD
RWKV_HEAD = 64
CONV_WIDTH = 4
PAGE = 128
N_MEM = 256
XW = 512
D_FF = 5632
NORM_EPS = 1e-6
GN_EPS = 64e-5
RET_GAMMA_BASE = 5.0
ROPE_BASE = 10000.0
CHUNK = 64
SAMPLE_PAD = 64
FOX_TQ = 512
FOX_HEADS_PER_STEP = 4
FOX_AUG = 2 * HD
LOG2E = 1.0 / math.log(2.0)
GDN_LB = 512
RWKV_LB = 512
RET_LB = 512

NP_COLS = 8192
CB_FOX, CB_RET, CB_GDN, CB_GDN_Z, CB_RWKV, CB_SMALL = 0, 16, 32, 44, 48, 62
SM_A, SM_B, SM_F = 0, 4, 8


def _cparams(sem):
    return pltpu.CompilerParams(dimension_semantics=sem, vmem_limit_bytes=VMEM_LIMIT)


def _dot(a, b):
    return jnp.dot(a.astype(BF16), b.astype(BF16), preferred_element_type=F32)


def _dot_nt(a, b):
    return lax.dot_general(a.astype(BF16), b.astype(BF16), (((1,), (1,)), ((), ())),
                           preferred_element_type=F32)


def _dot_tn(a, b):
    return lax.dot_general(a.astype(BF16), b.astype(BF16), (((0,), (0,)), ((), ())),
                           preferred_element_type=F32)


def _split3(x):
    hi = x.astype(BF16)
    r = x - hi.astype(F32)
    mid = r.astype(BF16)
    lo = (r - mid.astype(F32)).astype(BF16)
    return hi, mid, lo


def _dot_exact_lhs(m, x):
    hi, mid, lo = _split3(x)
    d = lambda p: jnp.dot(m, p, preferred_element_type=F32)
    return d(hi) + d(mid) + d(lo)


def _dot_exact_rhs(x, m):
    hi, mid, lo = _split3(x)
    d = lambda p: jnp.dot(p, m, preferred_element_type=F32)
    return d(hi) + d(mid) + d(lo)


def _bmm(a, b):
    return lax.dot_general(a.astype(BF16), b.astype(BF16), (((2,), (1,)), ((0,), (0,))),
                           preferred_element_type=F32)


def _bmm_nt(a, b):
    return lax.dot_general(a.astype(BF16), b.astype(BF16), (((2,), (2,)), ((0,), (0,))),
                           preferred_element_type=F32)


def _unit_lower_inv(n, nil):
    c = n.shape[-1]
    ii = lax.broadcasted_iota(jnp.int32, (c, c), 0)
    jj = lax.broadcasted_iota(jnp.int32, (c, c), 1)
    p = jnp.where(ii == jj, 1.0, 0.0).astype(F32) - n
    q = n
    for _ in range(int(math.log2(nil)) - 1):
        q = _bmm(q, q)
        p = p + _bmm(p, q)
    return p


def _chunk_tri(lb, chunk):
    ii = lax.broadcasted_iota(jnp.int32, (lb, lb), 0)
    jj = lax.broadcasted_iota(jnp.int32, (lb, lb), 1)
    same = (ii // chunk) == (jj // chunk)
    return jnp.where(jnp.logical_and(ii >= jj, same), 1.0, 0.0).astype(BF16)


def _shift_rows(x, prev, s):
    nsb = prev.shape[0]
    seq_rows = x.shape[0] // nsb
    rolled = pltpu.roll(x, s, axis=0)
    row = lax.broadcasted_iota(jnp.int32, (SUBLANES, x.shape[1]), 0)
    pieces = []
    for b in range(nsb):
        piece = rolled[b * seq_rows:(b + 1) * seq_rows]
        top = jnp.where(row < s, pltpu.roll(prev[b], s, axis=0), piece[:SUBLANES])
        pieces += [top, piece[SUBLANES:]]
    return jnp.concatenate(pieces, axis=0)


def _last_tiles(x, nsb):
    seq_rows = x.shape[0] // nsb
    return x.reshape(nsb, seq_rows, x.shape[1])[:, seq_rows - SUBLANES:, :]


def _softplus(x):
    return jnp.maximum(x, 0.0) + jnp.log1p(jnp.exp(-jnp.abs(x)))


def _sigmoid(x):
    return jax.nn.sigmoid(x)


def _silu(x):
    return x * jax.nn.sigmoid(x)


def _lane_col(x, idx):
    lane = lax.broadcasted_iota(jnp.int32, x.shape, 1)
    return jnp.sum(jnp.where(lane == idx, x, 0.0), axis=-1, keepdims=True)


def _rms(x, w=None):
    y = x * lax.rsqrt(jnp.mean(x * x, axis=-1, keepdims=True) + NORM_EPS)
    return y if w is None else y * w


def _row_valid(lb_index, lb, lvalid, width, nsb=1):
    row = lax.broadcasted_iota(jnp.int32, (lb, width), 0) % (lb // nsb) + lb_index * lb
    return row < lvalid


def _norm_matmul_kernel(x_ref, g_ref, w_ref, hw_ref, o_ref, xn_ref, *, norm_tiles, transposed_w):
    j = pl.program_id(1)

    @pl.when(j == 0)
    def _():
        xn_ref[...] = _rms(x_ref[...], g_ref[...]).astype(BF16)

    if transposed_w:
        acc = _dot_nt(xn_ref[...], w_ref[...])
    else:
        acc = jnp.dot(xn_ref[...], w_ref[...].astype(BF16), preferred_element_type=F32)
    if norm_tiles == 0:
        o_ref[...] = acc
    else:
        @pl.when(j < norm_tiles)
        def _():
            hw = hw_ref[...]
            for h in range(acc.shape[1] // HD):
                sl = slice(h * HD, (h + 1) * HD)
                o_ref[:, sl] = _rms(acc[:, sl], hw)

        @pl.when(j >= norm_tiles)
        def _():
            o_ref[...] = acc


def _row_tile(m, cap=1100):
    return next(t for t in range(cap - cap % 16, 0, -16) if m % t == 0)


def _norm_matmul(x, g, w, layer, *, tm, tn, head_w=None, norm_tiles=0, transposed_w=False):
    m, k = x.shape
    n = w.shape[1] if transposed_w else w.shape[2]
    if head_w is None:
        head_w = jnp.ones((HD,), F32)
    if transposed_w:
        w_spec = pl.BlockSpec((None, tn, k), lambda i, j: (layer, j, 0))
    else:
        w_spec = pl.BlockSpec((None, k, tn), lambda i, j: (layer, 0, j))
    return pl.pallas_call(
        functools.partial(_norm_matmul_kernel, norm_tiles=norm_tiles, transposed_w=transposed_w),
        grid=(m // tm, n // tn),
        in_specs=[
            pl.BlockSpec((tm, k), lambda i, j: (i, 0)),
            pl.BlockSpec((1, k), lambda i, j: (0, 0)),
            w_spec,
            pl.BlockSpec((1, HD), lambda i, j: (0, 0)),
        ],
        out_specs=pl.BlockSpec((tm, tn), lambda i, j: (i, j)),
        out_shape=jax.ShapeDtypeStruct((m, n), F32),
        scratch_shapes=[pltpu.VMEM((tm, k), BF16)],
        compiler_params=_cparams(("parallel", "arbitrary")),
        name="norm_matmul",
    )(x, g.reshape(1, k), w, head_w.reshape(1, HD))


def _matmul_res_kernel(a_ref, w_ref, r_ref, o_ref):
    o_ref[...] = r_ref[...] + jnp.dot(a_ref[...], w_ref[...].astype(BF16), preferred_element_type=F32)


def _matmul_res(a, w, layer, res, *, tm, tn):
    m, k = a.shape
    n = w.shape[2]
    return pl.pallas_call(
        _matmul_res_kernel,
        grid=(m // tm, n // tn),
        in_specs=[
            pl.BlockSpec((tm, k), lambda i, j: (i, 0)),
            pl.BlockSpec((None, k, tn), lambda i, j: (layer, 0, j)),
            pl.BlockSpec((tm, tn), lambda i, j: (i, j)),
        ],
        out_specs=pl.BlockSpec((tm, tn), lambda i, j: (i, j)),
        out_shape=jax.ShapeDtypeStruct((m, n), F32),
        compiler_params=_cparams(("parallel", "parallel")),
        name="matmul_res",
    )(a, w, res)


def _swiglu_up_kernel(x_ref, g_ref, wg_ref, wu_ref, o_ref, xn_ref):
    @pl.when(pl.program_id(1) == 0)
    def _():
        xn_ref[...] = _rms(x_ref[...], g_ref[...]).astype(BF16)

    xn = xn_ref[...]
    gate = jnp.dot(xn, wg_ref[...].astype(BF16), preferred_element_type=F32)
    up = jnp.dot(xn, wu_ref[...].astype(BF16), preferred_element_type=F32)
    o_ref[...] = (_silu(gate) * up).astype(BF16)


def _swiglu_up(x, g, wg, wu, layer, *, tm, tn):
    m, k = x.shape
    n = wg.shape[2]
    return pl.pallas_call(
        _swiglu_up_kernel,
        grid=(m // tm, n // tn),
        in_specs=[
            pl.BlockSpec((tm, k), lambda i, j: (i, 0)),
            pl.BlockSpec((1, k), lambda i, j: (0, 0)),
            pl.BlockSpec((None, k, tn), lambda i, j: (layer, 0, j)),
            pl.BlockSpec((None, k, tn), lambda i, j: (layer, 0, j)),
        ],
        out_specs=pl.BlockSpec((tm, tn), lambda i, j: (i, j)),
        out_shape=jax.ShapeDtypeStruct((m, n), BF16),
        scratch_shapes=[pltpu.VMEM((tm, k), BF16)],
        compiler_params=_cparams(("parallel", "arbitrary")),
        name="swiglu_up",
    )(x, g.reshape(1, k), wg, wu)


def _xattn_kernel(q_ref, k_ref, v_ref, o_ref):
    scale = HD ** -0.5
    for h in range(NH):
        sl = slice(h * HD, (h + 1) * HD)
        s = _dot_nt(q_ref[:, sl], k_ref[:, sl]) * scale
        m = jnp.max(s, axis=-1, keepdims=True)
        p = jnp.exp(s - m)
        l = jnp.sum(p, axis=-1, keepdims=True)
        o_ref[:, sl] = (_dot(p, v_ref[:, sl]) / l).astype(BF16)


def _xattn(q, mem_k, mem_v, *, row0, nrows, tq, rows_per_seq):
    tiles_per_seq = rows_per_seq // tq
    t0 = row0 // tq
    return pl.pallas_call(
        _xattn_kernel,
        grid=(nrows // tq,),
        in_specs=[
            pl.BlockSpec((tq, XW), lambda i: (t0 + i, 0)),
            pl.BlockSpec((None, N_MEM, XW), lambda i: (i // tiles_per_seq, 0, 0)),
            pl.BlockSpec((None, N_MEM, XW), lambda i: (i // tiles_per_seq, 0, 0)),
        ],
        out_specs=pl.BlockSpec((tq, XW), lambda i: (i, 0)),
        out_shape=jax.ShapeDtypeStruct((nrows, XW), BF16),
        compiler_params=_cparams(("parallel",)),
        name="xattn",
    )(q, mem_k, mem_v)


def _gdn_kernel(q_ref, k_ref, v_ref, z_ref, sm_ref, cwq_ref, cwk_ref, cwv_ref, cq_ref, ck_ref, cv_ref,
                s0_ref, alog_ref, dtb_ref, nw_ref, mix_ref, o_ref, so_ref, s_ref, prev_ref, gt_ref,
                *, lb, lvalid, nlb, nsb):
    del mix_ref
    h = pl.program_id(1)
    ib = pl.program_id(2)
    masked = lvalid < nlb * lb // nsb

    @pl.when(ib == 0)
    def _():
        s_ref[...] = s0_ref[...]
        prev_ref[0] = cq_ref[...]
        prev_ref[1] = ck_ref[...]
        prev_ref[2] = cv_ref[...]

    def conv(x_ref, w_ref, i):
        x = x_ref[...]
        w = w_ref[...]
        prev = prev_ref[i]
        y = x * w[3:4]
        for s in (1, 2, 3):
            y = y + _shift_rows(x, prev, s) * w[3 - s:4 - s]
        prev_ref[i] = _last_tiles(x, nsb)
        return _silu(y)

    q = conv(q_ref, cwq_ref, 0)
    k = conv(k_ref, cwk_ref, 1)
    v = conv(v_ref, cwv_ref, 2)
    q = q * lax.rsqrt(jnp.sum(q * q, axis=-1, keepdims=True) + NORM_EPS) * (HD ** -0.5)
    k = k * lax.rsqrt(jnp.sum(k * k, axis=-1, keepdims=True) + NORM_EPS)

    sm = sm_ref[...]
    g_blk = -jnp.exp(alog_ref[...]) * _softplus(sm + dtb_ref[...])
    beta_blk = _sigmoid(sm)
    if masked:
        valid = _row_valid(ib, lb, lvalid, LANES, nsb)
        g_blk = jnp.where(valid, g_blk, 0.0)
        beta_blk = jnp.where(valid, beta_blk, 0.0)
    gc_blk = _dot_exact_lhs(_chunk_tri(lb, CHUNK), g_blk)
    gt_ref[...] = gc_blk.T
    g_col_all = _lane_col(gc_blk, SM_A + h)
    beta_all = _lane_col(beta_blk, SM_B + h)
    g_row_all = gt_ref[pl.ds(SM_A + h, 1), :]

    group = min(2 * CHUNK, lb)
    ng = lb // group
    nchunk = lb // CHUNK
    ii = lax.broadcasted_iota(jnp.int32, (group, group), 0)
    jj = lax.broadcasted_iota(jnp.int32, (group, group), 1)
    lower = jnp.logical_and(ii >= jj, (ii // CHUNK) == (jj // CHUNK))
    to3 = lambda x: x.reshape(ng, group, x.shape[-1])
    q3, k3, v3 = to3(q), to3(k), to3(v)
    g_col3 = to3(g_col_all)
    beta3 = to3(beta_all)
    g_row3 = jnp.stack([g_row_all[:, i * group:(i + 1) * group] for i in range(ng)], axis=0)
    dec3 = jnp.exp(jnp.where(lower, g_col3 - g_row3, -jnp.inf))
    scores = _bmm_nt(jnp.concatenate([k3, q3], axis=1), k3)
    n3 = jnp.where(ii > jj, beta3 * scores[:, :group] * dec3, 0.0)
    qk3 = scores[:, group:] * dec3
    ainv3 = _unit_lower_inv(n3, CHUNK)
    eg3 = jnp.exp(g_col3)
    wuv3 = _bmm(ainv3, jnp.concatenate([beta3 * eg3 * k3, beta3 * v3], axis=2))
    qk_wuv = _bmm(qk3, wuv3)
    qp = (q3 * eg3 - qk_wuv[:, :, :HD]).reshape(lb, HD)
    op = qk_wuv[:, :, HD:].reshape(lb, HD)
    wuv_all = wuv3.reshape(lb, 2 * HD)
    g_chunks = g_col_all.reshape(nchunk, CHUNK, 1)
    g_end = jnp.broadcast_to(g_chunks[:, CHUNK - 1:, :], g_chunks.shape).reshape(lb, 1)
    kd = k * jnp.exp(g_end - g_col_all)
    s_decay = jnp.exp(g_end)
    nw = nw_ref[...]
    chunks_per_seq = nchunk // nsb
    for c in range(nchunk):
        r = slice(c * CHUNK, (c + 1) * CHUNK)
        b = c // chunks_per_seq
        an = _dot_tn(kd[r], wuv_all[r])
        s = s_ref[b]
        o = _dot(qp[r], s) + op[r]
        s_ref[b] = s * s_decay[c * CHUNK:c * CHUNK + 1] - _dot(an[:, :HD], s) + an[:, HD:]
        o_ref[r, :] = (_rms(o, nw) * _silu(z_ref[r, :])).astype(BF16)

    @pl.when(ib == nlb - 1)
    def _():
        so_ref[...] = s_ref[...]


MIX_ANY = pl.BlockSpec(memory_space=pl.ANY)
MIX_GDN, MIX_RET, MIX_RWKV, MIX_FOX = 0, 1, 2, 3


def _seq_blocking(row0, nseq, lpad, lb, nsb):
    assert (nsb == 1 and lpad % lb == 0) or (lb == nsb * lpad and nseq % nsb == 0)
    nlb = lpad * nsb // lb
    return nseq // nsb, nlb, lambda s, i: (row0 + s * lpad * nsb) // lb + i


def _gdn(p, mix, conv_w, conv_init, s0, a_log, dt_bias, norm_w, *, row0, nseq, lpad, lvalid, lb, nsb=1):
    ngroup, nlb, rb = _seq_blocking(row0, nseq, lpad, lb, nsb)
    pblk = lambda cb: pl.BlockSpec((lb, HD), lambda s, h, i: (rb(s, i), cb + h))
    cwblk = lambda j: pl.BlockSpec((CONV_WIDTH, HD), lambda s, h, i: (0, j * NH + h))
    ciblk = lambda j: pl.BlockSpec((nsb, SUBLANES, HD), lambda s, h, i: (s, 0, j * NH + h))
    sblk = pl.BlockSpec((nsb, None, HD, HD), lambda s, h, i: (s, h, 0, 0))
    vec = pl.BlockSpec((1, LANES), lambda s, h, i: (0, 0))
    lane_pad = lambda x: jnp.zeros((1, LANES), F32).at[0, :x.shape[0]].set(x)
    return pl.pallas_call(
        functools.partial(_gdn_kernel, lb=lb, lvalid=lvalid, nlb=nlb, nsb=nsb),
        grid=(ngroup, NH, nlb),
        in_specs=[
            pblk(CB_GDN), pblk(CB_GDN + NH), pblk(CB_GDN + 2 * NH), pblk(CB_GDN_Z),
            pl.BlockSpec((lb, LANES), lambda s, h, i: (rb(s, i), CB_SMALL)),
            cwblk(0), cwblk(1), cwblk(2), ciblk(0), ciblk(1), ciblk(2),
            sblk,
            vec, vec, vec, MIX_ANY,
        ],
        out_specs=[
            pl.BlockSpec((lb, HD), lambda s, h, i: (rb(s, i), MIX_GDN * NH + h)),
            sblk,
        ],
        out_shape=[
            jax.ShapeDtypeStruct(mix.shape, mix.dtype),
            jax.ShapeDtypeStruct((nseq, NH, HD, HD), F32),
        ],
        input_output_aliases={15: 0},
        scratch_shapes=[
            pltpu.VMEM((nsb, HD, HD), F32),
            pltpu.VMEM((3, nsb, SUBLANES, HD), F32),
            pltpu.VMEM((LANES, lb), F32),
        ],
        compiler_params=_cparams(("parallel", "parallel", "arbitrary")),
        name="gdn",
    )(p, p, p, p, p, conv_w, conv_w, conv_w, conv_init, conv_init, conv_init, s0,
      lane_pad(a_log), lane_pad(dt_bias), norm_w.reshape(1, HD), mix)


def _ret_kernel(q_ref, k_ref, v_ref, g_ref, cos_ref, sin_ref, lg_ref, s0_ref, mix_ref, o_ref, so_ref, s_ref,
                *, lb, cv, nlb, nsb):
    del mix_ref
    ib = pl.program_id(2)

    @pl.when(ib == 0)
    def _():
        s_ref[...] = s0_ref[...]

    cos = cos_ref[...]
    sin = sin_ref[...]
    rot = lambda x: x * cos + pltpu.roll(x, HD // 2, axis=1) * sin
    q = rot(q_ref[...])
    k = rot(k_ref[...]) * (HD ** -0.5)
    v = v_ref[...]
    lg = lg_ref[...][:, 0:1]

    ii = lax.broadcasted_iota(jnp.int32, (CHUNK, CHUNK), 0)
    jj = lax.broadcasted_iota(jnp.int32, (CHUNK, CHUNK), 1)
    rel = (ii - jj).astype(F32)
    dmat = jnp.where(rel >= 0, jnp.exp(jnp.maximum(rel, 0.0) * lg), 0.0)
    idx = lax.broadcasted_iota(jnp.int32, (CHUNK, 1), 0)
    idf = idx.astype(F32)
    xi = jnp.exp((idf + 1.0) * lg)
    zeta = jnp.where(idx < cv, jnp.exp((cv - 1.0 - idf) * lg), 0.0)
    gc = jnp.exp(cv * lg)
    nchunk = lb // CHUNK
    to3 = lambda x: x.reshape(nchunk, CHUNK, HD)
    q3, k3, v3 = to3(q), to3(k), to3(v)
    o_intra = _bmm(_bmm_nt(q3, k3) * dmat, v3)
    qx = q3 * xi
    kz = k3 * zeta
    chunks_per_seq = nchunk // nsb
    for b in range(nsb):
        s = s_ref[b]
        for c in range(b * chunks_per_seq, (b + 1) * chunks_per_seq):
            r = slice(c * CHUNK, (c + 1) * CHUNK)
            o = o_intra[c] + _dot(qx[c], s)
            s = s * gc + _dot_tn(kz[c], v3[c])
            o_ref[r, :] = (_rms(o) * _silu(g_ref[r, :])).astype(BF16)
        s_ref[b] = s

    @pl.when(ib == nlb - 1)
    def _():
        so_ref[...] = s_ref[...]


def _ret(p, mix, cos_t, sin_t, log_gamma, s0, *, row0, nseq, lpad, lvalid, lb, nsb=1):
    ngroup, nlb, rb = _seq_blocking(row0, nseq, lpad, lb, nsb)
    cv = CHUNK if lvalid == lpad else lvalid
    assert cv == CHUNK or (lpad == CHUNK and 0 < lvalid < CHUNK)
    pblk = lambda cb: pl.BlockSpec((lb, HD), lambda s, h, i: (rb(s, i), cb + h))
    tblk = pl.BlockSpec((lb, HD), lambda s, h, i: (i, 0))
    sblk = pl.BlockSpec((nsb, None, HD, HD), lambda s, h, i: (s, h, 0, 0))
    lg = jnp.broadcast_to(log_gamma[:, None, None], (NH, 1, LANES))
    return pl.pallas_call(
        functools.partial(_ret_kernel, lb=lb, cv=cv, nlb=nlb, nsb=nsb),
        grid=(ngroup, NH, nlb),
        in_specs=[
            pblk(CB_RET), pblk(CB_RET + NH), pblk(CB_RET + 2 * NH), pblk(CB_RET + 3 * NH),
            tblk, tblk,
            pl.BlockSpec((None, 1, LANES), lambda s, h, i: (h, 0, 0)),
            sblk, MIX_ANY,
        ],
        out_specs=[pl.BlockSpec((lb, HD), lambda s, h, i: (rb(s, i), MIX_RET * NH + h)), sblk],
        out_shape=[
            jax.ShapeDtypeStruct(mix.shape, mix.dtype),
            jax.ShapeDtypeStruct((nseq, NH, HD, HD), F32),
        ],
        input_output_aliases={8: 0},
        scratch_shapes=[pltpu.VMEM((nsb, HD, HD), F32)],
        compiler_params=_cparams(("parallel", "parallel", "arbitrary")),
        name="retention",
    )(p, p, p, p, cos_t, sin_t, lg, s0, mix)


def _half_sum(x):
    lane = lax.broadcasted_iota(jnp.int32, x.shape, 1)
    lo = lane < RWKV_HEAD
    s_lo = jnp.sum(jnp.where(lo, x, 0.0), axis=-1, keepdims=True)
    s_hi = jnp.sum(jnp.where(lo, 0.0, x), axis=-1, keepdims=True)
    return jnp.where(lo, s_lo, s_hi)


def _rwkv_kernel(r_ref, k_ref, v_ref, wa_ref, gd_ref, mur_ref, muk_ref, muv_ref, muwa_ref, mugd_ref,
                 sh_r_ref, sh_k_ref, sh_v_ref, sh_wa_ref, sh_gd_ref,
                 wup_ref, aup_ref, gup_ref, w0_ref, a0_ref, kk_ref, ka_ref, rk_ref, lnw_ref, lnb_ref,
                 s0_ref, mix_ref, o_ref, so_ref, s_ref, prev_ref, *, lb, lvalid, nlb, nsb):
    del mix_ref
    ib = pl.program_id(2)
    masked = lvalid < nlb * lb // nsb

    @pl.when(ib == 0)
    def _():
        s_ref[...] = s0_ref[...]
        prev_ref[0] = sh_r_ref[...]
        prev_ref[1] = sh_k_ref[...]
        prev_ref[2] = sh_v_ref[...]
        prev_ref[3] = sh_wa_ref[...]
        prev_ref[4] = sh_gd_ref[...]

    def shifted(x_ref, mu_ref, i):
        x = x_ref[...]
        prev = _shift_rows(x, prev_ref[i], 1)
        prev_ref[i] = _last_tiles(x, nsb)
        return x + (prev - x) * mu_ref[...]

    r = shifted(r_ref, mur_ref, 0)
    k = shifted(k_ref, muk_ref, 1)
    v = shifted(v_ref, muv_ref, 2)
    wa = shifted(wa_ref, muwa_ref, 3)
    gd = shifted(gd_ref, mugd_ref, 4)

    w_raw = -_softplus(-(w0_ref[...] + _dot(jnp.tanh(wa), wup_ref[...]))) - 0.5
    logw = -jnp.exp(w_raw)
    a_sig = _sigmoid(a0_ref[...] + _dot(wa, aup_ref[...]))
    gate = _dot(_sigmoid(gd), gup_ref[...])
    kk = k * kk_ref[...]
    kk = kk * lax.rsqrt(_half_sum(kk * kk) + NORM_EPS)
    kp = k * (1.0 + (a_sig - 1.0) * ka_ref[...])
    rec_a = -kk
    rec_b = kk * a_sig
    if masked:
        valid = _row_valid(ib, lb, lvalid, LANES, nsb)
        zero = lambda x: jnp.where(valid, x, 0.0)
        logw, rec_a, rec_b, kp, v = zero(logw), zero(rec_a), zero(rec_b), zero(kp), zero(v)

    cum = _dot_exact_lhs(_chunk_tri(lb, CHUNK), logw)
    e_pos = jnp.exp(cum)
    e_neg = jnp.exp(-cum)
    at_all = rec_a * jnp.exp(cum - logw)
    bt_all = rec_b * e_neg
    kt_all = kp * e_neg
    rt_all = r * e_pos

    stacked = 2 * CHUNK
    row_head = lax.broadcasted_iota(jnp.int32, (stacked, LANES), 0) // CHUNK
    lane_head = lax.broadcasted_iota(jnp.int32, (stacked, LANES), 1) // RWKV_HEAD
    own = row_head == lane_head
    nchunk = lb // CHUNK
    dup3 = lambda x: jnp.concatenate([x.reshape(nchunk, CHUNK, LANES)] * 2, axis=1)
    stack3 = lambda x: jnp.where(own, dup3(x), 0.0)
    block_diag = (lax.broadcasted_iota(jnp.int32, (LANES, LANES), 0) // RWKV_HEAD) == (
        lax.broadcasted_iota(jnp.int32, (LANES, LANES), 1) // RWKV_HEAD)
    ti = lax.broadcasted_iota(jnp.int32, (stacked, stacked), 0) % CHUNK
    tj = lax.broadcasted_iota(jnp.int32, (stacked, stacked), 1) % CHUNK

    at3, bt3, kt3, rt3, v3 = stack3(at_all), stack3(bt_all), stack3(kt_all), stack3(rt_all), dup3(v)
    scores = _bmm_nt(jnp.concatenate([at3, rt3], axis=1), jnp.concatenate([bt3, kt3], axis=1))
    l_ab = jnp.where(ti > tj, scores[:, :stacked, :stacked], 0.0)
    l_ak = jnp.where(ti > tj, scores[:, :stacked, stacked:], 0.0)
    l_rb = jnp.where(ti >= tj, scores[:, stacked:, :stacked], 0.0)
    l_rk = jnp.where(ti >= tj, scores[:, stacked:, stacked:], 0.0)
    inv = _unit_lower_inv(-l_ab, CHUNK)
    t12 = _bmm(inv, jnp.concatenate([at3, _bmm(l_ak, v3)], axis=2))
    t1 = t12[:, :, :LANES]
    t2 = jnp.where(own, t12[:, :, LANES:], 0.0)
    rb12 = _bmm(l_rb, jnp.concatenate([t1, t2], axis=2))
    rp = rt3 + rb12[:, :, :LANES]
    op = jnp.where(own, rb12[:, :, LANES:] + _bmm(l_rk, v3), 0.0)
    cum3 = cum.reshape(nchunk, CHUNK, LANES)
    cum_end = jnp.broadcast_to(cum3[:, CHUNK - 1:, :], cum3.shape).reshape(lb, LANES)
    to_end = jnp.exp(cum_end - cum)
    b_end = rec_b * to_end
    k_end = kp * to_end
    s_decay = jnp.exp(cum_end)

    chunks_per_seq = nchunk // nsb
    for c in range(nchunk):
        rs = slice(c * CHUNK, (c + 1) * CHUNK)
        b = c // chunks_per_seq
        vc = v[rs]
        b2 = jnp.concatenate([b_end[rs], b_end[rs]], axis=0)
        gh = _dot_tn(jnp.concatenate([t1[c], t2[c]], axis=1), b2)
        h = gh[LANES:] + _dot_tn(vc, k_end[rs])
        s = s_ref[b]
        o2 = jnp.where(own, _dot_nt(rp[c], s), 0.0) + op[c]
        o = o2[:CHUNK] + o2[CHUNK:]
        s_ref[b] = s * s_decay[c * CHUNK:c * CHUNK + 1] + jnp.where(block_diag, _dot(s, gh[:LANES]) + h, 0.0)

        inv_n = 1.0 / RWKV_HEAD
        mu = _half_sum(o) * inv_n
        var = _half_sum(jnp.square(o - mu)) * inv_n
        on = (o - mu) * lax.rsqrt(var + GN_EPS) * lnw_ref[...] + lnb_ref[...]
        bonus = _half_sum(r[rs] * kp[rs] * rk_ref[...]) * vc
        o_ref[rs, :] = ((on + bonus) * gate[rs]).astype(BF16)

    @pl.when(ib == nlb - 1)
    def _():
        so_ref[...] = s_ref[...]


def _rwkv(p, mix, lw, shift_init, s0, *, row0, nseq, lpad, lvalid, lb, nsb=1):
    ngroup, nlb, rb = _seq_blocking(row0, nseq, lpad, lb, nsb)
    npair = GW // LANES
    pblk = lambda cb, per_pair: pl.BlockSpec(
        (lb, LANES), lambda s, j, i: (rb(s, i), cb + (j if per_pair else 0)))
    mublk = lambda cb, per_pair: pl.BlockSpec((1, LANES), lambda s, j, i: (0, cb + (j if per_pair else 0)))
    shblk = lambda cb, per_pair: pl.BlockSpec(
        (nsb, SUBLANES, LANES), lambda s, j, i: (s, 0, cb + (j if per_pair else 0)))
    pair_vec = pl.BlockSpec((1, LANES), lambda s, j, i: (0, j))
    pair_mat = pl.BlockSpec((LANES, LANES), lambda s, j, i: (0, j))
    sblk = pl.BlockSpec((nsb, None, LANES, LANES), lambda s, j, i: (s, j, 0, 0))
    mu = lw['rwkv_mu'].reshape(1, -1)
    zeros64 = jnp.zeros((64, GW), F32)
    wup = jnp.concatenate([lw['rwkv_w_up'], zeros64], axis=0).astype(BF16)
    aup = jnp.concatenate([zeros64, lw['rwkv_a_up']], axis=0).astype(BF16)
    row = lambda x: x.reshape(1, GW)
    blocks = [(0, True), (4, True), (8, True), (12, False), (13, False)]
    return pl.pallas_call(
        functools.partial(_rwkv_kernel, lb=lb, lvalid=lvalid, nlb=nlb, nsb=nsb),
        grid=(ngroup, npair, nlb),
        in_specs=(
            [pblk(CB_RWKV + cb, pp) for cb, pp in blocks]
            + [mublk(cb, pp) for cb, pp in blocks]
            + [shblk(cb, pp) for cb, pp in blocks]
            + [pair_mat, pair_mat, pair_mat] + [pair_vec] * 7 + [sblk, MIX_ANY]
        ),
        out_specs=[pl.BlockSpec((lb, LANES), lambda s, j, i: (rb(s, i), MIX_RWKV * npair + j)), sblk],
        out_shape=[
            jax.ShapeDtypeStruct(mix.shape, mix.dtype),
            jax.ShapeDtypeStruct((nseq, npair, LANES, LANES), F32),
        ],
        input_output_aliases={26: 0},
        scratch_shapes=[pltpu.VMEM((nsb, LANES, LANES), F32), pltpu.VMEM((5, nsb, SUBLANES, LANES), F32)],
        compiler_params=_cparams(("parallel", "parallel", "arbitrary")),
        name="rwkv7",
    )(p, p, p, p, p, mu, mu, mu, mu, mu, shift_init, shift_init, shift_init, shift_init, shift_init,
      wup, aup, lw['rwkv_g_up'].astype(BF16), row(lw['rwkv_w0']), row(lw['rwkv_a0']), row(lw['rwkv_k_k']),
      row(lw['rwkv_k_a']), row(lw['rwkv_r_k']), row(lw['rwkv_ln_w']), row(lw['rwkv_ln_b']), s0, mix)


def _fox_prep_kernel(q_ref, k_ref, v_ref, sm_ref, qw_ref, kw_ref, bf_ref, k_all_ref, v_all_ref,
                     qa_ref, k4_ref, ka_ref, vb_ref, lf_ref, c_ref, v4_ref, carry_ref, *, lb):
    del k_all_ref, v_all_ref
    @pl.when(pl.program_id(1) == 0)
    def _():
        carry_ref[...] = jnp.zeros_like(carry_ref)

    logf = -_softplus(-(sm_ref[...] + bf_ref[...]))
    lf_ref[...] = logf
    c = _dot_exact_lhs(_chunk_tri(lb, lb), logf) + carry_ref[0:1, :]
    c_ref[...] = c
    carry_ref[...] = jnp.broadcast_to(c[lb - 1:], carry_ref.shape)

    qw = qw_ref[...]
    kw = kw_ref[...]
    lane = lax.broadcasted_iota(jnp.int32, (lb, HD), 1)
    q_tail = jnp.where(lane < 2, -1.0, 0.0).astype(BF16)
    for h in range(NH):
        sl = slice(h * HD, (h + 1) * HD)
        a0 = h * FOX_AUG
        qa_ref[:, a0:a0 + HD] = (_rms(q_ref[:, sl], qw) * (HD ** -0.5 * LOG2E)).astype(BF16)
        qa_ref[:, a0 + HD:a0 + FOX_AUG] = q_tail
        kn = _rms(k_ref[:, sl], kw)
        k4_ref[:, h, :] = kn
        v4_ref[:, h, :] = v_ref[:, sl]
        ka_ref[:, a0:a0 + HD] = kn.astype(BF16)
        c2 = c[:, SM_F + h:SM_F + h + 1] * LOG2E
        c_hi = c2.astype(BF16).astype(F32)
        ka_ref[:, a0 + HD:a0 + FOX_AUG] = jnp.where(lane == 0, c_hi, jnp.where(lane == 1, c2 - c_hi, 0.0)).astype(BF16)
    vb_ref[...] = v_ref[...].astype(BF16)


def _fox_prep(p, k_all, v_all, q_w, k_w, b_f, *, layer, nseq, lpad, lb):
    nlb = lpad // lb
    rb = lambda s, i: s * nlb + i
    seg = lambda j: pl.BlockSpec((lb, GW), lambda s, i: (rb(s, i), j))
    vec = pl.BlockSpec((1, LANES), lambda s, i: (0, 0))
    rows = nseq * lpad
    cache_blk = pl.BlockSpec((lb, NH, HD), lambda s, i: (layer * nseq * nlb + rb(s, i), 0, 0))
    bf_lane = jnp.zeros((1, LANES), F32).at[0, SM_F:SM_F + NH].set(b_f)
    return pl.pallas_call(
        functools.partial(_fox_prep_kernel, lb=lb),
        grid=(nseq, nlb),
        in_specs=[seg(0), seg(1), seg(2),
                  pl.BlockSpec((lb, LANES), lambda s, i: (rb(s, i), CB_SMALL)), vec, vec, vec,
                  MIX_ANY, MIX_ANY],
        out_specs=[
            pl.BlockSpec((lb, NH * FOX_AUG), lambda s, i: (rb(s, i), 0)),
            cache_blk,
            pl.BlockSpec((lb, NH * FOX_AUG), lambda s, i: (rb(s, i), 0)),
            pl.BlockSpec((lb, GW), lambda s, i: (rb(s, i), 0)),
            pl.BlockSpec((lb, LANES), lambda s, i: (rb(s, i), 0)),
            pl.BlockSpec((lb, LANES), lambda s, i: (rb(s, i), 0)),
            cache_blk,
        ],
        out_shape=[
            jax.ShapeDtypeStruct((rows, NH * FOX_AUG), BF16),
            jax.ShapeDtypeStruct(k_all.shape, F32),
            jax.ShapeDtypeStruct((rows, NH * FOX_AUG), BF16),
            jax.ShapeDtypeStruct((rows, GW), BF16),
            jax.ShapeDtypeStruct((rows, LANES), F32),
            jax.ShapeDtypeStruct((rows, LANES), F32),
            jax.ShapeDtypeStruct(v_all.shape, F32),
        ],
        input_output_aliases={7: 1, 8: 6},
        scratch_shapes=[pltpu.VMEM((SUBLANES, LANES), F32)],
        compiler_params=_cparams(("parallel", "arbitrary")),
        name="fox_prep",
    )(p, p, p, p, q_w.reshape(1, HD), k_w.reshape(1, HD), bf_lane, k_all, v_all)


def _fox_flash_kernel(q_ref, k_ref, v_ref, c_ref, g_ref, mix_ref, o_ref, m_ref, l_ref, acc_ref, *, tq):
    del mix_ref
    h0 = pl.program_id(1) * FOX_HEADS_PER_STEP
    qi = pl.program_id(2)
    c_all = c_ref[...]
    c_cols = [_lane_col(c_all, SM_F + h0 + e) * LOG2E for e in range(FOX_HEADS_PER_STEP)]
    m_ref[...] = jnp.full_like(m_ref, -1e30)
    l_ref[...] = jnp.zeros_like(l_ref)
    acc_ref[...] = jnp.zeros_like(acc_ref)

    def block(ks, width, diagonal):
        for e in range(FOX_HEADS_PER_STEP):
            sl = slice(e * HD, (e + 1) * HD)
            sa = slice(e * FOX_AUG, (e + 1) * FOX_AUG)
            vb = v_ref[pl.ds(ks, width), sl]
            t = lax.dot_general(q_ref[:, sa], k_ref[pl.ds(ks, width), sa], (((1,), (1,)), ((), ())),
                                preferred_element_type=F32)
            if diagonal:
                ii = lax.broadcasted_iota(jnp.int32, (tq, width), 0)
                jj = lax.broadcasted_iota(jnp.int32, (tq, width), 1)
                t = jnp.where(jj <= ii + (width - tq), t, -jnp.inf)
            m_old = m_ref[e]
            m_new = jnp.maximum(m_old, jnp.max(t, axis=-1, keepdims=True) + c_cols[e])
            alpha = jnp.exp2(m_old - m_new)
            pr = jnp.exp2(t - (m_new - c_cols[e]))
            l_ref[e] = alpha * l_ref[e] + jnp.sum(pr, axis=-1, keepdims=True)
            acc_ref[e] = alpha * acc_ref[e] + _dot(pr, vb)
            m_ref[e] = m_new

    def body(kp, carry):
        block(pl.multiple_of(kp * 2 * tq, 2 * tq), 2 * tq, False)
        return carry

    odd = qi % 2 == 1
    lax.fori_loop(0, jnp.where(odd, qi // 2, jnp.maximum(qi // 2 - 1, 0)), body, 0)

    @pl.when(odd)
    def _():
        block(pl.multiple_of((qi - 1) * tq, tq), 2 * tq, True)

    @pl.when(qi == 0)
    def _():
        block(0, tq, True)

    @pl.when(jnp.logical_and(jnp.logical_not(odd), qi > 0))
    def _():
        block(pl.multiple_of((qi - 2) * tq, tq), 3 * tq, True)
    for e in range(FOX_HEADS_PER_STEP):
        sl = slice(e * HD, (e + 1) * HD)
        o_ref[:, sl] = (acc_ref[e] / l_ref[e] * _sigmoid(g_ref[:, sl])).astype(BF16)


def _fox_flash(qa, ka, vb, c_col, p, mix, *, nseq, lpad, tq):
    nq = lpad // tq
    hw = FOX_HEADS_PER_STEP * HD
    aw = FOX_HEADS_PER_STEP * FOX_AUG
    g_block0 = (CB_FOX + 3 * NH) // FOX_HEADS_PER_STEP
    o_block0 = MIX_FOX * (GW // hw)
    return pl.pallas_call(
        functools.partial(_fox_flash_kernel, tq=tq),
        grid=(nseq, NH // FOX_HEADS_PER_STEP, nq),
        in_specs=[
            pl.BlockSpec((tq, aw), lambda s, h, i: (s * nq + i, h)),
            pl.BlockSpec((lpad, aw), lambda s, h, i: (s, h)),
            pl.BlockSpec((lpad, hw), lambda s, h, i: (s, h)),
            pl.BlockSpec((tq, LANES), lambda s, h, i: (s * nq + i, 0)),
            pl.BlockSpec((tq, hw), lambda s, h, i: (s * nq + i, g_block0 + h)),
            MIX_ANY,
        ],
        out_specs=pl.BlockSpec((tq, hw), lambda s, h, i: (s * nq + i, o_block0 + h)),
        out_shape=jax.ShapeDtypeStruct(mix.shape, mix.dtype),
        input_output_aliases={5: 0},
        scratch_shapes=[pltpu.VMEM((FOX_HEADS_PER_STEP, tq, 1), F32), pltpu.VMEM((FOX_HEADS_PER_STEP, tq, 1), F32),
                        pltpu.VMEM((FOX_HEADS_PER_STEP, tq, HD), F32)],
        compiler_params=_cparams(("parallel", "parallel", "arbitrary")),
        name="fox_flash",
    )(qa, ka, vb, c_col, p, mix)


PAGE_GROUP = 16
PAGE_COLS = PAGE * NH
NQ_PAD = SUBLANES


def _page_sums_kernel(lf_ref, upper_ref, heads_ref, o_ref):
    lf = lf_ref[...]
    o_ref[:, :PAGE_COLS] = _dot_exact_rhs(lf, upper_ref[...])
    o_ref[:, PAGE_COLS:] = _dot_exact_rhs(lf, heads_ref[...])


def _page_sums(cache_lf):
    depth, n_pool = cache_lf.shape[:2]
    rows = depth * n_pool
    tile = _row_tile(rows, 512)
    idx = np.arange(PAGE_COLS)
    same_head = (idx[:, None] % NH) == (idx[None, :] % NH)
    upper = jnp.asarray(same_head & (idx[:, None] // NH > idx[None, :] // NH), BF16)
    heads = jnp.asarray(same_head, BF16)
    const = pl.BlockSpec((PAGE_COLS, PAGE_COLS), lambda i: (0, 0))
    sums = pl.pallas_call(
        _page_sums_kernel,
        grid=(rows // tile,),
        in_specs=[pl.BlockSpec((tile, PAGE_COLS), lambda i: (i, 0)), const, const],
        out_specs=pl.BlockSpec((tile, 2 * PAGE_COLS), lambda i: (i, 0)),
        out_shape=jax.ShapeDtypeStruct((rows, 2 * PAGE_COLS), F32),
        compiler_params=_cparams(("parallel",)),
        name="page_sums",
    )(cache_lf.reshape(rows, PAGE_COLS), upper, heads)
    return sums.reshape(depth, n_pool, 2, PAGE_COLS)


def _fox_sample_kernel(pt_ref, pq_ref, sm_ref, *rest, lvalid, lpad, n_steps):
    del pt_ref
    g = PAGE_GROUP
    kps, vps, sums = rest[:g], rest[g:2 * g], rest[2 * g:3 * g]
    (qw_ref, kw_ref, bf_ref, mix_ref, o_ref, kn_ref, lf_ref,
     qn_s, cq_s, m_s, l_s, acc_s, carry_s) = rest[3 * g:]
    del mix_ref
    i = pl.program_id(1)
    scale = HD ** -0.5
    nq = NQ_PAD
    nrow = NH * nq

    @pl.when(i == 0)
    def _():
        logf = -_softplus(-(sm_ref[...] + bf_ref[...]))
        logf = jnp.where(_row_valid(0, lpad, lvalid, LANES), logf, 0.0)
        lf_ref[...] = logf[:nq]
        c = _dot_exact_lhs(_chunk_tri(lpad, lpad), logf)
        c_t = c.T
        carry_s[...] = jnp.zeros_like(carry_s)
        qi = lax.broadcasted_iota(jnp.int32, (nq, lpad), 0)
        kj = lax.broadcasted_iota(jnp.int32, (nq, lpad), 1)
        for h in range(NH):
            sl = slice(h * HD, (h + 1) * HD)
            rows = slice(h * nq, (h + 1) * nq)
            qn = _rms(pq_ref[:nq, sl], qw_ref[...])
            kn = _rms(pq_ref[:, GW + h * HD:GW + (h + 1) * HD], kw_ref[...])
            vn = pq_ref[:, 2 * GW + h * HD:2 * GW + (h + 1) * HD]
            c_h = c[:nq, SM_F + h:SM_F + h + 1]
            qn_s[rows, :] = qn
            cq_s[rows, :] = c_h
            kn_ref[:, sl] = kn[:nq]
            s = _dot_nt(qn, kn) * scale + c_h - c_t[SM_F + h:SM_F + h + 1, :]
            s = jnp.where(kj <= qi, s, -jnp.inf)
            m = jnp.max(s, axis=-1, keepdims=True)
            pr = jnp.exp(s - m)
            m_s[rows, :] = m
            l_s[rows, :] = jnp.sum(pr, axis=-1, keepdims=True)
            acc_s[rows, :] = _dot(pr, vn)

    run = carry_s[...]
    suffix = [None] * g
    for j in reversed(range(g)):
        suffix[j] = sums[j][0:1, :] + run
        run = run + sums[j][1:2, :]
    carry_s[...] = run

    row_head = lax.broadcasted_iota(jnp.int32, (nrow, PAGE_COLS), 0) // nq
    col_head = lax.broadcasted_iota(jnp.int32, (nrow, PAGE_COLS), 1) % NH
    own = row_head == col_head
    qs = qn_s[...].astype(BF16)
    bias = cq_s[...]
    tiles = [jnp.where(own, _dot_nt(qs, kps[j][...]) * scale + bias + suffix[j], -jnp.inf) for j in range(g)]
    m_old = m_s[...]
    m_new = m_old
    for t in tiles:
        m_new = jnp.maximum(m_new, jnp.max(t, axis=-1, keepdims=True))
    alpha = jnp.exp(m_old - m_new)
    l_new = alpha * l_s[...]
    acc = alpha * acc_s[...]
    for j, t in enumerate(tiles):
        pr = jnp.exp(t - m_new)
        l_new = l_new + jnp.sum(pr, axis=-1, keepdims=True)
        acc = acc + _dot(pr, vps[j][...])
    m_s[...] = m_new
    l_s[...] = l_new
    acc_s[...] = acc

    @pl.when(i == n_steps - 1)
    def _():
        o_ref[...] = jnp.zeros_like(o_ref)
        out = acc_s[...] / l_s[...]
        for h in range(NH):
            sl = slice(h * HD, (h + 1) * HD)
            gate = pq_ref[:nq, 3 * GW + h * HD:3 * GW + (h + 1) * HD]
            o_ref[:nq, sl] = (out[h * nq:(h + 1) * nq] * _sigmoid(gate)).astype(BF16)


def _fox_sample(p, mix, page_table, cache_k, cache_v, page_sums, q_w, k_w, b_f, *, layer, row0, nseq, lpad, lvalid):
    n_pages = page_table.shape[1]
    g = PAGE_GROUP
    assert n_pages % g == 0 and lvalid <= NQ_PAD
    n_steps = n_pages // g
    bf_lane = jnp.zeros((1, LANES), F32).at[0, SM_F:SM_F + NH].set(b_f)
    rb = lambda b: row0 // lpad + b
    vec = pl.BlockSpec((1, LANES), lambda b, i, pt: (0, 0))

    def page_spec(shape, j):
        zeros = (0,) * len(shape)
        return pl.BlockSpec((None, None) + shape,
                            lambda b, i, pt: (layer, pt[b, n_pages - g * (i + 1) + j]) + zeros)

    grid_spec = pltpu.PrefetchScalarGridSpec(
        num_scalar_prefetch=1,
        grid=(nseq, n_steps),
        in_specs=(
            [pl.BlockSpec((lpad, 4 * GW), lambda b, i, pt: (rb(b), CB_FOX)),
             pl.BlockSpec((lpad, LANES), lambda b, i, pt: (rb(b), CB_SMALL))]
            + [page_spec((PAGE_COLS, HD), j) for j in range(g)]
            + [page_spec((PAGE_COLS, HD), j) for j in range(g)]
            + [page_spec((2, PAGE_COLS), j) for j in range(g)]
            + [vec, vec, vec, MIX_ANY]
        ),
        out_specs=[
            pl.BlockSpec((lpad, GW), lambda b, i, pt: (rb(b), MIX_FOX)),
            pl.BlockSpec((None, NQ_PAD, GW), lambda b, i, pt: (b, 0, 0)),
            pl.BlockSpec((None, NQ_PAD, LANES), lambda b, i, pt: (b, 0, 0)),
        ],
        scratch_shapes=[
            pltpu.VMEM((NH * NQ_PAD, HD), F32),
            pltpu.VMEM((NH * NQ_PAD, 1), F32),
            pltpu.VMEM((NH * NQ_PAD, 1), F32),
            pltpu.VMEM((NH * NQ_PAD, 1), F32),
            pltpu.VMEM((NH * NQ_PAD, HD), F32),
            pltpu.VMEM((1, PAGE_COLS), F32),
        ],
    )
    return pl.pallas_call(
        functools.partial(_fox_sample_kernel, lvalid=lvalid, lpad=lpad, n_steps=n_steps),
        grid_spec=grid_spec,
        out_shape=[
            jax.ShapeDtypeStruct(mix.shape, mix.dtype),
            jax.ShapeDtypeStruct((nseq, NQ_PAD, GW), F32),
            jax.ShapeDtypeStruct((nseq, NQ_PAD, LANES), F32),
        ],
        input_output_aliases={3 * g + 6: 0},
        compiler_params=_cparams(("parallel", "arbitrary")),
        name="fox_sample",
    )(page_table, p, p, *([cache_k] * g), *([cache_v] * g), *([page_sums] * g),
      q_w.reshape(1, HD), k_w.reshape(1, HD), bf_lane, mix)


W_IN_SHIFT = 8
_COPY, _SHIFT, _SMALL, _ZERO = 0, 1, 2, 3


def _w_in_plan():
    kind = np.zeros(NP_COLS // LANES, np.int32)
    src = np.zeros(NP_COLS // LANES, np.int32)

    def put(cb, n, first_src_block, k):
        kind[cb:cb + n] = k
        src[cb:cb + n] = first_src_block + np.arange(n)

    put(CB_FOX, 16, (5896 - W_IN_SHIFT) // LANES, _SHIFT)
    put(CB_RET, 16, (2056 - W_IN_SHIFT) // LANES, _SHIFT)
    put(CB_GDN, 12, 0, _COPY)
    put(CB_GDN_Z, 4, (1544 - W_IN_SHIFT) // LANES, _SHIFT)
    put(CB_RWKV, 14, (4104 - W_IN_SHIFT) // LANES, _SHIFT)
    put(CB_SMALL, 1, 1536 // LANES, _SMALL)
    put(CB_SMALL + 1, 1, 0, _ZERO)
    src_b = np.where(kind == _SHIFT, (src + 1) * (LANES // SUBLANES),
                     np.where(kind == _SMALL, 7944 // SUBLANES, 0))
    return jnp.asarray(kind), jnp.asarray(src), jnp.asarray(src_b.astype(np.int32))


def _prep_w_in_kernel(kind_ref, sa_ref, sb_ref, a_ref, b_ref, o_ref):
    del sa_ref, sb_ref
    kind = kind_ref[pl.program_id(0)]
    depth = o_ref.shape[0]
    row8 = lax.broadcasted_iota(jnp.int32, (SUBLANES, o_ref.shape[2]), 0)

    @pl.when(kind == _COPY)
    def _():
        for l in range(depth):
            o_ref[l] = a_ref[:, l, :].astype(BF16)

    @pl.when(kind == _SHIFT)
    def _():
        for l in range(depth):
            o_ref[l] = jnp.concatenate([a_ref[:, l, :][W_IN_SHIFT:], b_ref[:, l, :]], axis=0).astype(BF16)

    @pl.when(kind == _SMALL)
    def _():
        for l in range(depth):
            f_rows = jnp.where(row8 < NH, b_ref[:, l, :], 0.0)
            zeros = jnp.zeros((LANES - SM_F - SUBLANES, o_ref.shape[2]), F32)
            o_ref[l] = jnp.concatenate([a_ref[:, l, :][:SM_F], f_rows, zeros], axis=0).astype(BF16)

    @pl.when(kind == _ZERO)
    def _():
        o_ref[...] = jnp.zeros_like(o_ref)


def _prep_w_in(w_in):
    depth, k, _ = w_in.shape
    w_t = jnp.transpose(w_in, (2, 0, 1))
    kind, src_a, src_b = _w_in_plan()
    grid_spec = pltpu.PrefetchScalarGridSpec(
        num_scalar_prefetch=3,
        grid=(NP_COLS // LANES,),
        in_specs=[
            pl.BlockSpec((LANES, depth, k), lambda j, kd, sa, sb: (sa[j], 0, 0)),
            pl.BlockSpec((SUBLANES, depth, k), lambda j, kd, sa, sb: (sb[j], 0, 0)),
        ],
        out_specs=pl.BlockSpec((depth, LANES, k), lambda j, kd, sa, sb: (0, j, 0)),
    )
    return pl.pallas_call(
        _prep_w_in_kernel,
        grid_spec=grid_spec,
        out_shape=jax.ShapeDtypeStruct((depth, NP_COLS, k), BF16),
        compiler_params=_cparams(("parallel",)),
        name="prep_w_in",
    )(kind, src_a, src_b, w_t, w_t)


def _rope_tables(pos):
    half = HD // 2
    inv = 1.0 / (ROPE_BASE ** jnp.linspace(0.0, 1.0, half, dtype=F32))
    ang = pos.astype(F32)[:, None] * inv[None, :]
    cos, sin = jnp.cos(ang), jnp.sin(ang)
    return jnp.concatenate([cos, cos], axis=-1), jnp.concatenate([-sin, sin], axis=-1)


def _state_tile(state, nrows):
    b, _, c = state.shape
    return jnp.concatenate([jnp.zeros((b, SUBLANES - nrows, c), F32), state], axis=1)


def _rwkv_pair_states(s):
    b = s.shape[0]
    s = s.reshape(b, 4, 2, RWKV_HEAD, RWKV_HEAD)
    z = jnp.zeros_like(s[:, :, 0])
    top = jnp.concatenate([s[:, :, 0], z], axis=-1)
    bot = jnp.concatenate([z, s[:, :, 1]], axis=-1)
    return jnp.concatenate([top, bot], axis=-2)


def _rwkv_unpair_states(sp):
    b = sp.shape[0]
    a = sp[:, :, :RWKV_HEAD, :RWKV_HEAD]
    c = sp[:, :, RWKV_HEAD:, RWKV_HEAD:]
    return jnp.stack([a, c], axis=2).reshape(b, 8, RWKV_HEAD, RWKV_HEAD)


def kernel(x_prompt, x_sample, cache_fox_k, cache_fox_v, cache_fox_logf, cache_mem_k, cache_mem_v, state_gdn_conv, state_gdn_S, state_ret_S, state_rwkv_shift, state_rwkv_S, page_table, mem_prompt, norm_mix, w_in, gdn_conv_w, gdn_A_log, gdn_dt_bias, gdn_norm, rwkv_mu, rwkv_w0, rwkv_w_up, rwkv_a0, rwkv_a_up, rwkv_g_up, rwkv_k_k, rwkv_k_a, rwkv_r_k, rwkv_ln_w, rwkv_ln_b, fox_b_f, fox_q_norm, fox_k_norm, w_out, norm_x, norm_mem, xattn_wq, xattn_wkv, xattn_q_norm, xattn_k_norm, xattn_wo, norm_ffn, ffn_w_gate, ffn_w_up, ffn_w_down):
    weights = {
        'norm_mix': norm_mix, 'w_in': w_in, 'gdn_conv_w': gdn_conv_w, 'gdn_A_log': gdn_A_log,
        'gdn_dt_bias': gdn_dt_bias, 'gdn_norm': gdn_norm, 'rwkv_mu': rwkv_mu, 'rwkv_w0': rwkv_w0,
        'rwkv_w_up': rwkv_w_up, 'rwkv_a0': rwkv_a0, 'rwkv_a_up': rwkv_a_up, 'rwkv_g_up': rwkv_g_up,
        'rwkv_k_k': rwkv_k_k, 'rwkv_k_a': rwkv_k_a, 'rwkv_r_k': rwkv_r_k, 'rwkv_ln_w': rwkv_ln_w,
        'rwkv_ln_b': rwkv_ln_b, 'fox_b_f': fox_b_f, 'fox_q_norm': fox_q_norm, 'fox_k_norm': fox_k_norm,
        'w_out': w_out, 'norm_x': norm_x, 'norm_mem': norm_mem, 'xattn_wq': xattn_wq, 'xattn_wkv': xattn_wkv,
        'xattn_q_norm': xattn_q_norm, 'xattn_k_norm': xattn_k_norm, 'xattn_wo': xattn_wo,
        'norm_ffn': norm_ffn, 'ffn_w_gate': ffn_w_gate, 'ffn_w_up': ffn_w_up, 'ffn_w_down': ffn_w_down,
    }
    depth = w_in.shape[0]
    bp, lp, d = x_prompt.shape
    bs, ls, _ = x_sample.shape
    n_pages = page_table.shape[1]
    past_len = n_pages * PAGE
    tp = bp * lp
    ts = bs * SAMPLE_PAD
    tt = tp + ts
    tm = _row_tile(tt)
    lb_p = min(256, lp)
    assert ls >= CONV_WIDTH - 1 and ls <= SUBLANES and lp % lb_p == 0

    xs_pad = jnp.zeros((bs, SAMPLE_PAD, d), F32).at[:, :ls].set(x_sample)
    x = jnp.concatenate([x_prompt.reshape(tp, d), xs_pad.reshape(ts, d)], axis=0)

    cos_p, sin_p = _rope_tables(jnp.arange(lp, dtype=jnp.int32))
    cos_s, sin_s = _rope_tables(jnp.tile(past_len + jnp.arange(SAMPLE_PAD, dtype=jnp.int32), bs))
    log_gamma = jnp.log(1.0 - jnp.exp2(-(RET_GAMMA_BASE + jnp.arange(NH, dtype=F32))))
    n_pool = cache_fox_k.shape[1]
    cache_k = cache_fox_k.reshape(depth, n_pool, PAGE_COLS, HD)
    cache_v = cache_fox_v.reshape(depth, n_pool, PAGE_COLS, HD)
    page_sums = _page_sums(cache_fox_logf.reshape(depth, n_pool, 1, PAGE_COLS))
    zeros_s = jnp.zeros((bp, NH, HD, HD), F32)
    zeros_conv = jnp.zeros((bp, SUBLANES, 3 * GW), F32)
    zeros_shift = jnp.zeros((bp, SUBLANES, 1792), F32)

    w_in_t = _prep_w_in(w_in)
    fox_k_all = jnp.zeros((depth * tp, NH, HD), F32)
    fox_v_all = jnp.zeros((depth * tp, NH, HD), F32)

    outs_p, outs_s, mem_ks, mem_vs = [], [], [], []
    for l in range(depth):
        lw = {name: arr[l] for name, arr in weights.items()}
        p = _norm_matmul(x, lw['norm_mix'], w_in_t, l, tm=tm, tn=1024,
                         transposed_w=True)

        o_mix = jnp.zeros((tt, 4 * GW), BF16)
        gp = dict(row0=0, nseq=bp, lpad=lp, lvalid=lp, lb=lb_p)
        o_mix, gdn_s_p = _gdn(p, o_mix, lw['gdn_conv_w'], zeros_conv, zeros_s, lw['gdn_A_log'], lw['gdn_dt_bias'],
                              lw['gdn_norm'], **{**gp, 'lb': min(GDN_LB, lp)})
        o_mix, ret_s_p = _ret(p, o_mix, cos_p, sin_p, log_gamma, zeros_s, **{**gp, 'lb': min(RET_LB, lp)})
        o_mix, rwkv_s_p = _rwkv(p, o_mix, lw, zeros_shift, zeros_s, **{**gp, 'lb': min(RWKV_LB, lp)})
        qa, fox_k_all, ka, vb, lf, c_col, fox_v_all = _fox_prep(
            p, fox_k_all, fox_v_all, lw['fox_q_norm'], lw['fox_k_norm'], lw['fox_b_f'],
            layer=l, nseq=bp, lpad=lp, lb=lb_p)
        o_mix = _fox_flash(qa, ka, vb, c_col, p, o_mix, nseq=bp, lpad=lp, tq=min(FOX_TQ, lp))

        gs = dict(row0=tp, nseq=bs, lpad=SAMPLE_PAD, lvalid=ls, lb=ts, nsb=bs)
        o_mix, gdn_s_s = _gdn(p, o_mix, lw['gdn_conv_w'], _state_tile(state_gdn_conv[l], CONV_WIDTH - 1),
                              state_gdn_S[l], lw['gdn_A_log'], lw['gdn_dt_bias'], lw['gdn_norm'], **gs)
        o_mix, ret_s_s = _ret(p, o_mix, cos_s, sin_s, log_gamma, state_ret_S[l], **gs)
        o_mix, rwkv_s_s = _rwkv(p, o_mix, lw, _state_tile(state_rwkv_shift[l], 1),
                                _rwkv_pair_states(state_rwkv_S[l]), **gs)
        o_mix, kn_s, lf_s = _fox_sample(p, o_mix, page_table, cache_k, cache_v, page_sums,
                                        lw['fox_q_norm'], lw['fox_k_norm'], lw['fox_b_f'],
                                        layer=l, row0=tp, nseq=bs, lpad=SAMPLE_PAD, lvalid=ls)
        x = _matmul_res(o_mix, w_out, l, x, tm=tm, tn=512)

        kv = _norm_matmul(mem_prompt.reshape(bp * N_MEM, d), lw['norm_mem'], xattn_wkv, l,
                          tm=256, tn=XW, head_w=lw['xattn_k_norm'], norm_tiles=1)
        mk = kv[:, :XW].reshape(bp, N_MEM, XW)
        mv = kv[:, XW:].reshape(bp, N_MEM, XW)
        q = _norm_matmul(x, lw['norm_x'], xattn_wq, l, tm=tm, tn=XW,
                         head_w=lw['xattn_q_norm'], norm_tiles=1)
        xo_p = _xattn(q, mk, mv, row0=0, nrows=tp, tq=lb_p, rows_per_seq=lp)
        xo_s = _xattn(q, cache_mem_k[l].reshape(bs, N_MEM, XW), cache_mem_v[l].reshape(bs, N_MEM, XW),
                      row0=tp, nrows=ts, tq=SAMPLE_PAD, rows_per_seq=SAMPLE_PAD)
        x = _matmul_res(jnp.concatenate([xo_p, xo_s], axis=0), xattn_wo, l, x, tm=tm, tn=512)

        hidden = _swiglu_up(x, lw['norm_ffn'], ffn_w_gate, ffn_w_up, l, tm=tm, tn=512)
        x = _matmul_res(hidden, ffn_w_down, l, x, tm=tm, tn=256)

        c0 = CB_GDN * LANES
        r0 = CB_RWKV * LANES
        v0 = (CB_FOX + 2 * NH) * LANES

        def last_rows(row_end, n, col0, width, nseq, stride):
            return jnp.stack([lax.slice(p, (b * stride + row_end - n, col0), (b * stride + row_end, col0 + width))
                              for b in range(nseq)], axis=0)

        ps_v = lax.slice(p, (tp, v0), (tt, v0 + GW)).reshape(bs, SAMPLE_PAD, NH, HD)
        outs_p.append((
            None, None,
            lf.reshape(bp, lp, LANES)[:, :, SM_F:SM_F + NH],
            last_rows(lp, CONV_WIDTH - 1, c0, 3 * GW, bp, lp),
            gdn_s_p, ret_s_p,
            last_rows(lp, 1, r0, 1792, bp, lp),
            _rwkv_unpair_states(rwkv_s_p),
        ))
        outs_s.append((
            kn_s[:, :ls].reshape(bs, ls, NH, HD),
            ps_v[:, :ls],
            lf_s[:, :ls, SM_F:SM_F + NH],
            last_rows(tp + ls, CONV_WIDTH - 1, c0, 3 * GW, bs, SAMPLE_PAD),
            gdn_s_s, ret_s_s,
            last_rows(tp + ls, 1, r0, 1792, bs, SAMPLE_PAD),
            _rwkv_unpair_states(rwkv_s_s),
        ))
        mem_ks.append(mk.reshape(bp, N_MEM, NH, HD))
        mem_vs.append(mv.reshape(bp, N_MEM, NH, HD))

    stk = lambda seq, i: jnp.stack([e[i] for e in seq], axis=0)
    yp = x[:tp].reshape(bp, lp, d)
    ys = x[tp:].reshape(bs, SAMPLE_PAD, d)[:, :ls]
    cache_shape = (depth, bp, lp, NH, HD)
    return (yp, ys, fox_k_all.reshape(cache_shape), fox_v_all.reshape(cache_shape), stk(outs_p, 2),
            jnp.stack(mem_ks, 0), jnp.stack(mem_vs, 0),
            stk(outs_p, 3), stk(outs_p, 4), stk(outs_p, 5), stk(outs_p, 6), stk(outs_p, 7),
            stk(outs_s, 0), stk(outs_s, 1), stk(outs_s, 2), stk(outs_s, 3), stk(outs_s, 4), stk(outs_s, 5),
            stk(outs_s, 6), stk(outs_s, 7))
```

```python
import functools
import math

import jax
import jax.numpy as jnp
import numpy as np
from jax import lax
from jax.experimental import pallas as pl
from jax.experimental.pallas import tpu as pltpu

F32 = jnp.float32
BF16 = jnp.bfloat16

LANES = 128
SUBLANES = 8
VMEM_LIMIT = 56 * 1024 * 1024

D_MODEL = 2048
GW = D_MODEL // 4
HD = 128
NH = GW // HD
RWKV_HEAD = 64
CONV_WIDTH = 4
PAGE = 128
N_MEM = 256
XW = 512
D_FF = 5632
NORM_EPS = 1e-6
GN_EPS = 64e-5
RET_GAMMA_BASE = 5.0
ROPE_BASE = 10000.0
CHUNK = 64
SAMPLE_PAD = 64
FOX_TQ = 512
FOX_HEADS_PER_STEP = 4
FOX_AUG = 2 * HD
LOG2E = 1.0 / math.log(2.0)
GDN_LB = 512
GDN_HEADS_PER_STEP = 4
RWKV_LB = 512
RWKV_PAIRS_PER_STEP = 2
RET_LB = 512

NP_COLS = 8192
CB_FOX, CB_RET, CB_GDN, CB_GDN_Z, CB_RWKV, CB_SMALL = 0, 16, 32, 44, 48, 62
SM_A, SM_B, SM_F = 0, 4, 8


def _cparams(sem):
    return pltpu.CompilerParams(dimension_semantics=sem, vmem_limit_bytes=VMEM_LIMIT)


def _dot(a, b):
    return jnp.dot(a.astype(BF16), b.astype(BF16), preferred_element_type=F32)


def _dot_nt(a, b):
    return lax.dot_general(a.astype(BF16), b.astype(BF16), (((1,), (1,)), ((), ())),
                           preferred_element_type=F32)


def _dot_tn(a, b):
    return lax.dot_general(a.astype(BF16), b.astype(BF16), (((0,), (0,)), ((), ())),
                           preferred_element_type=F32)


def _split3(x):
    hi = x.astype(BF16)
    r = x - hi.astype(F32)
    mid = r.astype(BF16)
    lo = (r - mid.astype(F32)).astype(BF16)
    return hi, mid, lo


def _dot_exact_lhs(m, x):
    hi, mid, lo = _split3(x)
    d = lambda p: jnp.dot(m, p, preferred_element_type=F32)
    return d(hi) + d(mid) + d(lo)


def _dot_exact_rhs(x, m):
    hi, mid, lo = _split3(x)
    d = lambda p: jnp.dot(p, m, preferred_element_type=F32)
    return d(hi) + d(mid) + d(lo)


def _bmm(a, b):
    return lax.dot_general(a.astype(BF16), b.astype(BF16), (((2,), (1,)), ((0,), (0,))),
                           preferred_element_type=F32)


def _bmm_nt(a, b):
    return lax.dot_general(a.astype(BF16), b.astype(BF16), (((2,), (2,)), ((0,), (0,))),
                           preferred_element_type=F32)


def _unit_lower_inv(n, nil):
    c = n.shape[-1]
    ii = lax.broadcasted_iota(jnp.int32, (c, c), 0)
    jj = lax.broadcasted_iota(jnp.int32, (c, c), 1)
    p = jnp.where(ii == jj, 1.0, 0.0).astype(F32) - n
    levels = int(math.log2(nil)) - 1
    q = _bmm(n, n)
    for level in range(levels):
        if level == levels - 1:
            return p + _bmm(p, q)
        both = _bmm(jnp.concatenate([q, p], axis=1), q)
        q, p = both[:, :c], p + both[:, c:]


def _chunk_tri(lb, chunk):
    ii = lax.broadcasted_iota(jnp.int32, (lb, lb), 0)
    jj = lax.broadcasted_iota(jnp.int32, (lb, lb), 1)
    same = (ii // chunk) == (jj // chunk)
    return jnp.where(jnp.logical_and(ii >= jj, same), 1.0, 0.0).astype(BF16)


def _shift_rows(x, prev, s):
    nsb = prev.shape[0]
    seq_rows = x.shape[0] // nsb
    rolled = pltpu.roll(x, s, axis=0)
    row = lax.broadcasted_iota(jnp.int32, (SUBLANES, x.shape[1]), 0)
    pieces = []
    for b in range(nsb):
        piece = rolled[b * seq_rows:(b + 1) * seq_rows]
        top = jnp.where(row < s, pltpu.roll(prev[b], s, axis=0), piece[:SUBLANES])
        pieces += [top, piece[SUBLANES:]]
    return jnp.concatenate(pieces, axis=0)


def _last_tiles(x, nsb):
    seq_rows = x.shape[0] // nsb
    return x.reshape(nsb, seq_rows, x.shape[1])[:, seq_rows - SUBLANES:, :]


def _softplus(x):
    return jnp.maximum(x, 0.0) + jnp.log1p(jnp.exp(-jnp.abs(x)))


def _sigmoid(x):
    return jax.nn.sigmoid(x)


def _silu(x):
    return x * jax.nn.sigmoid(x)


def _lane_col(x, idx):
    lane = lax.broadcasted_iota(jnp.int32, x.shape, 1)
    return jnp.sum(jnp.where(lane == idx, x, 0.0), axis=-1, keepdims=True)


def _rms(x, w=None):
    y = x * lax.rsqrt(jnp.mean(x * x, axis=-1, keepdims=True) + NORM_EPS)
    return y if w is None else y * w


def _row_valid(lb_index, lb, lvalid, width, nsb=1):
    row = lax.broadcasted_iota(jnp.int32, (lb, width), 0) % (lb // nsb) + lb_index * lb
    return row < lvalid


def _norm_matmul_kernel(x_ref, g_ref, w_ref, hw_ref, o_ref, xn_ref, *, norm_tiles, transposed_w):
    j = pl.program_id(1)

    @pl.when(j == 0)
    def _():
        xn_ref[...] = _rms(x_ref[...], g_ref[...]).astype(BF16)

    if transposed_w:
        acc = _dot_nt(xn_ref[...], w_ref[...])
    else:
        acc = jnp.dot(xn_ref[...], w_ref[...].astype(BF16), preferred_element_type=F32)
    if norm_tiles == 0:
        o_ref[...] = acc
    else:
        @pl.when(j < norm_tiles)
        def _():
            hw = hw_ref[...]
            for h in range(acc.shape[1] // HD):
                sl = slice(h * HD, (h + 1) * HD)
                o_ref[:, sl] = _rms(acc[:, sl], hw)

        @pl.when(j >= norm_tiles)
        def _():
            o_ref[...] = acc


def _row_tile(m, cap=1100):
    return next(t for t in range(cap - cap % 16, 0, -16) if m % t == 0)


def _norm_matmul(x, g, w, layer, *, tm, tn, head_w=None, norm_tiles=0, transposed_w=False):
    m, k = x.shape
    n = w.shape[1] if transposed_w else w.shape[2]
    if head_w is None:
        head_w = jnp.ones((HD,), F32)
    if transposed_w:
        w_spec = pl.BlockSpec((None, tn, k), lambda i, j: (layer, j, 0))
    else:
        w_spec = pl.BlockSpec((None, k, tn), lambda i, j: (layer, 0, j))
    return pl.pallas_call(
        functools.partial(_norm_matmul_kernel, norm_tiles=norm_tiles, transposed_w=transposed_w),
        grid=(m // tm, n // tn),
        in_specs=[
            pl.BlockSpec((tm, k), lambda i, j: (i, 0)),
            pl.BlockSpec((1, k), lambda i, j: (0, 0)),
            w_spec,
            pl.BlockSpec((1, HD), lambda i, j: (0, 0)),
        ],
        out_specs=pl.BlockSpec((tm, tn), lambda i, j: (i, j)),
        out_shape=jax.ShapeDtypeStruct((m, n), F32),
        scratch_shapes=[pltpu.VMEM((tm, k), BF16)],
        compiler_params=_cparams(("parallel", "arbitrary")),
        name="norm_matmul",
    )(x, g.reshape(1, k), w, head_w.reshape(1, HD))


def _matmul_res_kernel(a_ref, w_ref, r_ref, o_ref):
    o_ref[...] = r_ref[...] + jnp.dot(a_ref[...], w_ref[...].astype(BF16), preferred_element_type=F32)


def _matmul_res(a, w, layer, res, *, tm, tn):
    m, k = a.shape
    n = w.shape[2]
    return pl.pallas_call(
        _matmul_res_kernel,
        grid=(m // tm, n // tn),
        in_specs=[
            pl.BlockSpec((tm, k), lambda i, j: (i, 0)),
            pl.BlockSpec((None, k, tn), lambda i, j: (layer, 0, j)),
            pl.BlockSpec((tm, tn), lambda i, j: (i, j)),
        ],
        out_specs=pl.BlockSpec((tm, tn), lambda i, j: (i, j)),
        out_shape=jax.ShapeDtypeStruct((m, n), F32),
        compiler_params=_cparams(("parallel", "parallel")),
        name="matmul_res",
    )(a, w, res)


def _swiglu_up_kernel(x_ref, g_ref, wg_ref, wu_ref, o_ref, xn_ref):
    @pl.when(pl.program_id(1) == 0)
    def _():
        xn_ref[...] = _rms(x_ref[...], g_ref[...]).astype(BF16)

    xn = xn_ref[...]
    gate = jnp.dot(xn, wg_ref[...].astype(BF16), preferred_element_type=F32)
    up = jnp.dot(xn, wu_ref[...].astype(BF16), preferred_element_type=F32)
    o_ref[...] = (_silu(gate) * up).astype(BF16)


def _swiglu_up(x, g, wg, wu, layer, *, tm, tn):
    m, k = x.shape
    n = wg.shape[2]
    return pl.pallas_call(
        _swiglu_up_kernel,
        grid=(m // tm, n // tn),
        in_specs=[
            pl.BlockSpec((tm, k), lambda i, j: (i, 0)),
            pl.BlockSpec((1, k), lambda i, j: (0, 0)),
            pl.BlockSpec((None, k, tn), lambda i, j: (layer, 0, j)),
            pl.BlockSpec((None, k, tn), lambda i, j: (layer, 0, j)),
        ],
        out_specs=pl.BlockSpec((tm, tn), lambda i, j: (i, j)),
        out_shape=jax.ShapeDtypeStruct((m, n), BF16),
        scratch_shapes=[pltpu.VMEM((tm, k), BF16)],
        compiler_params=_cparams(("parallel", "arbitrary")),
        name="swiglu_up",
    )(x, g.reshape(1, k), wg, wu)


def _xattn_kernel(q_ref, k_ref, v_ref, o_ref):
    scale = HD ** -0.5
    for h in range(NH):
        sl = slice(h * HD, (h + 1) * HD)
        s = _dot_nt(q_ref[:, sl], k_ref[:, sl]) * scale
        m = jnp.max(s, axis=-1, keepdims=True)
        p = jnp.exp(s - m)
        l = jnp.sum(p, axis=-1, keepdims=True)
        o_ref[:, sl] = (_dot(p, v_ref[:, sl]) / l).astype(BF16)


def _xattn(q, mem_k, mem_v, *, row0, nrows, tq, rows_per_seq):
    tiles_per_seq = rows_per_seq // tq
    t0 = row0 // tq
    return pl.pallas_call(
        _xattn_kernel,
        grid=(nrows // tq,),
        in_specs=[
            pl.BlockSpec((tq, XW), lambda i: (t0 + i, 0)),
            pl.BlockSpec((None, N_MEM, XW), lambda i: (i // tiles_per_seq, 0, 0)),
            pl.BlockSpec((None, N_MEM, XW), lambda i: (i // tiles_per_seq, 0, 0)),
        ],
        out_specs=pl.BlockSpec((tq, XW), lambda i: (i, 0)),
        out_shape=jax.ShapeDtypeStruct((nrows, XW), BF16),
        compiler_params=_cparams(("parallel",)),
        name="xattn",
    )(q, mem_k, mem_v)


def _gdn_kernel(q_ref, k_ref, v_ref, z_ref, sm_ref, cwq_ref, cwk_ref, cwv_ref, cq_ref, ck_ref, cv_ref,
                s0_ref, alog_ref, dtb_ref, nw_ref, mix_ref, o_ref, so_ref, s_ref, prev_ref, gt_ref,
                *, lb, lvalid, nlb, nsb):
    del mix_ref
    h0 = pl.program_id(1) * GDN_HEADS_PER_STEP
    ib = pl.program_id(2)
    masked = lvalid < nlb * lb // nsb

    @pl.when(ib == 0)
    def _():
        s_ref[...] = s0_ref[...]
        prev_ref[0] = cq_ref[...]
        prev_ref[1] = ck_ref[...]
        prev_ref[2] = cv_ref[...]

    def conv(x_ref, w_ref, i, sl):
        x = x_ref[:, sl]
        w = w_ref[:, sl]
        prev = prev_ref[i, :, :, sl]
        y = x * w[3:4]
        for s in (1, 2, 3):
            y = y + _shift_rows(x, prev, s) * w[3 - s:4 - s]
        prev_ref[i, :, :, sl] = _last_tiles(x, nsb)
        return _silu(y)

    sm = sm_ref[...]
    g_blk = -jnp.exp(alog_ref[...]) * _softplus(sm + dtb_ref[...])
    beta_blk = _sigmoid(sm)
    if masked:
        valid = _row_valid(ib, lb, lvalid, LANES, nsb)
        g_blk = jnp.where(valid, g_blk, 0.0)
        beta_blk = jnp.where(valid, beta_blk, 0.0)
    gc_blk = _dot_exact_lhs(_chunk_tri(lb, CHUNK), g_blk)
    gt_ref[...] = gc_blk.T
    group = min(2 * CHUNK, lb)
    ng = lb // group
    nchunk = lb // CHUNK
    ii = lax.broadcasted_iota(jnp.int32, (group, group), 0)
    jj = lax.broadcasted_iota(jnp.int32, (group, group), 1)
    lower = jnp.logical_and(ii >= jj, (ii // CHUNK) == (jj // CHUNK))
    nw = nw_ref[...]
    for e in range(GDN_HEADS_PER_STEP):
        _gdn_head(h0 + e, e, slice(e * HD, (e + 1) * HD), conv, (q_ref, k_ref, v_ref), (cwq_ref, cwk_ref, cwv_ref),
                  z_ref, o_ref, s_ref, gt_ref, gc_blk, beta_blk, lower, ii > jj, nw,
                  lb=lb, nsb=nsb, group=group, ng=ng, nchunk=nchunk)

    @pl.when(ib == nlb - 1)
    def _():
        so_ref[...] = s_ref[...]


def _gdn_head(h, e, sl, conv, qkv_refs, cw_refs, z_ref, o_ref, s_ref, gt_ref, gc_blk, beta_blk, lower, strict, nw,
              *, lb, nsb, group, ng, nchunk):
    q = conv(qkv_refs[0], cw_refs[0], 0, sl)
    k = conv(qkv_refs[1], cw_refs[1], 1, sl)
    v = conv(qkv_refs[2], cw_refs[2], 2, sl)
    q = q * lax.rsqrt(jnp.sum(q * q, axis=-1, keepdims=True) + NORM_EPS) * (HD ** -0.5)
    k = k * lax.rsqrt(jnp.sum(k * k, axis=-1, keepdims=True) + NORM_EPS)
    g_col_all = _lane_col(gc_blk, SM_A + h)
    beta_all = _lane_col(beta_blk, SM_B + h)
    g_row_all = gt_ref[pl.ds(SM_A + h, 1), :]

    to3 = lambda x: x.reshape(ng, group, x.shape[-1])
    q3, k3, v3 = to3(q), to3(k), to3(v)
    g_col3 = to3(g_col_all)
    beta3 = to3(beta_all)
    g_row3 = jnp.stack([g_row_all[:, i * group:(i + 1) * group] for i in range(ng)], axis=0)
    dec3 = jnp.exp(jnp.where(lower, g_col3 - g_row3, -jnp.inf))
    scores = _bmm_nt(jnp.concatenate([k3, q3], axis=1), k3)
    n3 = jnp.where(strict, beta3 * scores[:, :group] * dec3, 0.0)
    qk3 = scores[:, group:] * dec3
    ainv3 = _unit_lower_inv(n3, CHUNK)
    eg3 = jnp.exp(g_col3)
    wuv3 = _bmm(ainv3, jnp.concatenate([beta3 * eg3 * k3, beta3 * v3], axis=2))
    qk_wuv = _bmm(qk3, wuv3)
    qp = (q3 * eg3 - qk_wuv[:, :, :HD]).reshape(lb, HD)
    op = qk_wuv[:, :, HD:].reshape(lb, HD)
    wuv_all = wuv3.reshape(lb, 2 * HD)
    g_chunks = g_col_all.reshape(nchunk, CHUNK, 1)
    g_end = jnp.broadcast_to(g_chunks[:, CHUNK - 1:, :], g_chunks.shape).reshape(lb, 1)
    kd = k * jnp.exp(g_end - g_col_all)
    s_decay = jnp.exp(g_end)
    chunks_per_seq = nchunk // nsb
    for c in range(nchunk):
        r = slice(c * CHUNK, (c + 1) * CHUNK)
        b = c // chunks_per_seq
        an = _dot_tn(kd[r], wuv_all[r])
        s = s_ref[b, e]
        o = _dot(qp[r], s) + op[r]
        s_ref[b, e] = s * s_decay[c * CHUNK:c * CHUNK + 1] - _dot(an[:, :HD], s) + an[:, HD:]
        o_ref[r, sl] = (_rms(o, nw) * _silu(z_ref[r, sl])).astype(BF16)


MIX_ANY =pl.BlockSpec(memory_space=pl.ANY)
MIX_GDN, MIX_RET, MIX_RWKV, MIX_FOX = 0, 1, 2, 3


def _seq_blocking(row0, nseq, lpad, lb, nsb):
    assert (nsb == 1 and lpad % lb == 0) or (lb == nsb * lpad and nseq % nsb == 0)
    nlb = lpad * nsb // lb
    return nseq // nsb, nlb, lambda s, i: (row0 + s * lpad * nsb) // lb + i


def _gdn(p, mix, conv_w, conv_init, s0, a_log, dt_bias, norm_w, *, row0, nseq, lpad, lvalid, lb, nsb=1):
    ngroup, nlb, rb = _seq_blocking(row0, nseq, lpad, lb, nsb)
    hps = GDN_HEADS_PER_STEP
    hw = hps * HD
    steps = NH // hps
    pblk = lambda cb: pl.BlockSpec((lb, hw), lambda s, h, i: (rb(s, i), cb // hps + h))
    cwblk = lambda j: pl.BlockSpec((CONV_WIDTH, hw), lambda s, h, i: (0, j * steps + h))
    ciblk = lambda j: pl.BlockSpec((nsb, SUBLANES, hw), lambda s, h, i: (s, 0, j * steps + h))
    sblk = pl.BlockSpec((nsb, hps, HD, HD), lambda s, h, i: (s, h, 0, 0))
    vec = pl.BlockSpec((1, LANES), lambda s, h, i: (0, 0))
    lane_pad = lambda x: jnp.zeros((1, LANES), F32).at[0, :x.shape[0]].set(x)
    return pl.pallas_call(
        functools.partial(_gdn_kernel, lb=lb, lvalid=lvalid, nlb=nlb, nsb=nsb),
        grid=(ngroup, steps, nlb),
        in_specs=[
            pblk(CB_GDN), pblk(CB_GDN + NH), pblk(CB_GDN + 2 * NH), pblk(CB_GDN_Z),
            pl.BlockSpec((lb, LANES), lambda s, h, i: (rb(s, i), CB_SMALL)),
            cwblk(0), cwblk(1), cwblk(2), ciblk(0), ciblk(1), ciblk(2),
            sblk,
            vec, vec, vec, MIX_ANY,
        ],
        out_specs=[
            pl.BlockSpec((lb, hw), lambda s, h, i: (rb(s, i), MIX_GDN * steps + h)),
            sblk,
        ],
        out_shape=[
            jax.ShapeDtypeStruct(mix.shape, mix.dtype),
            jax.ShapeDtypeStruct((nseq, NH, HD, HD), F32),
        ],
        input_output_aliases={15: 0},
        scratch_shapes=[
            pltpu.VMEM((nsb, hps, HD, HD), F32),
            pltpu.VMEM((3, nsb, SUBLANES, hw), F32),
            pltpu.VMEM((LANES, lb), F32),
        ],
        compiler_params=_cparams(("parallel", "parallel", "arbitrary")),
        name="gdn",
    )(p, p, p, p, p, conv_w, conv_w, conv_w, conv_init, conv_init, conv_init, s0,
      lane_pad(a_log), lane_pad(dt_bias), norm_w.reshape(1, HD), mix)


def _ret_kernel(q_ref, k_ref, v_ref, g_ref, cos_ref, sin_ref, lg_ref, s0_ref, mix_ref, o_ref, so_ref, s_ref,
                *, lb, cv, nlb, nsb):
    del mix_ref
    ib = pl.program_id(2)

    @pl.when(ib == 0)
    def _():
        s_ref[...] = s0_ref[...]

    cos = cos_ref[...]
    sin = sin_ref[...]
    rot = lambda x: x * cos + pltpu.roll(x, HD // 2, axis=1) * sin
    q = rot(q_ref[...])
    k = rot(k_ref[...]) * (HD ** -0.5)
    v = v_ref[...]
    lg = lg_ref[...][:, 0:1]

    ii = lax.broadcasted_iota(jnp.int32, (CHUNK, CHUNK), 0)
    jj = lax.broadcasted_iota(jnp.int32, (CHUNK, CHUNK), 1)
    rel = (ii - jj).astype(F32)
    dmat = jnp.where(rel >= 0, jnp.exp(jnp.maximum(rel, 0.0) * lg), 0.0)
    idx = lax.broadcasted_iota(jnp.int32, (CHUNK, 1), 0)
    idf = idx.astype(F32)
    xi = jnp.exp((idf + 1.0) * lg)
    zeta = jnp.where(idx < cv, jnp.exp((cv - 1.0 - idf) * lg), 0.0)
    gc = jnp.exp(cv * lg)
    nchunk = lb // CHUNK
    to3 = lambda x: x.reshape(nchunk, CHUNK, HD)
    q3, k3, v3 = to3(q), to3(k), to3(v)
    o_intra = _bmm(_bmm_nt(q3, k3) * dmat, v3)
    qx = q3 * xi
    kz = k3 * zeta
    chunks_per_seq = nchunk // nsb
    for b in range(nsb):
        s = s_ref[b]
        for c in range(b * chunks_per_seq, (b + 1) * chunks_per_seq):
            r = slice(c * CHUNK, (c + 1) * CHUNK)
            o = o_intra[c] + _dot(qx[c], s)
            s = s * gc + _dot_tn(kz[c], v3[c])
            o_ref[r, :] = (_rms(o) * _silu(g_ref[r, :])).astype(BF16)
        s_ref[b] = s

    @pl.when(ib == nlb - 1)
    def _():
        so_ref[...] = s_ref[...]


def _ret(p, mix, cos_t, sin_t, log_gamma, s0, *, row0, nseq, lpad, lvalid, lb, nsb=1):
    ngroup, nlb, rb = _seq_blocking(row0, nseq, lpad, lb, nsb)
    cv = CHUNK if lvalid == lpad else lvalid
    assert cv == CHUNK or (lpad == CHUNK and 0 < lvalid < CHUNK)
    pblk = lambda cb: pl.BlockSpec((lb, HD), lambda s, h, i: (rb(s, i), cb + h))
    tblk = pl.BlockSpec((lb, HD), lambda s, h, i: (i, 0))
    sblk = pl.BlockSpec((nsb, None, HD, HD), lambda s, h, i: (s, h, 0, 0))
    lg = jnp.broadcast_to(log_gamma[:, None, None], (NH, 1, LANES))
    return pl.pallas_call(
        functools.partial(_ret_kernel, lb=lb, cv=cv, nlb=nlb, nsb=nsb),
        grid=(ngroup, NH, nlb),
        in_specs=[
            pblk(CB_RET), pblk(CB_RET + NH), pblk(CB_RET + 2 * NH), pblk(CB_RET + 3 * NH),
            tblk, tblk,
            pl.BlockSpec((None, 1, LANES), lambda s, h, i: (h, 0, 0)),
            sblk, MIX_ANY,
        ],
        out_specs=[pl.BlockSpec((lb, HD), lambda s, h, i: (rb(s, i), MIX_RET * NH + h)), sblk],
        out_shape=[
            jax.ShapeDtypeStruct(mix.shape, mix.dtype),
            jax.ShapeDtypeStruct((nseq, NH, HD, HD), F32),
        ],
        input_output_aliases={8: 0},
        scratch_shapes=[pltpu.VMEM((nsb, HD, HD), F32)],
        compiler_params=_cparams(("parallel", "parallel", "arbitrary")),
        name="retention",
    )(p, p, p, p, cos_t, sin_t, lg, s0, mix)


def _half_sum(x):
    lane = lax.broadcasted_iota(jnp.int32, x.shape, 1)
    lo = lane < RWKV_HEAD
    s_lo = jnp.sum(jnp.where(lo, x, 0.0), axis=-1, keepdims=True)
    s_hi = jnp.sum(jnp.where(lo, 0.0, x), axis=-1, keepdims=True)
    return jnp.where(lo, s_lo, s_hi)


def _rwkv_kernel(r_ref, k_ref, v_ref, wa_ref, gd_ref, mur_ref, muk_ref, muv_ref, muwa_ref, mugd_ref,
                 sh_r_ref, sh_k_ref, sh_v_ref, sh_wa_ref, sh_gd_ref,
                 wup_ref, aup_ref, gup_ref, w0_ref, a0_ref, kk_ref, ka_ref, rk_ref, lnw_ref, lnb_ref,
                 s0_ref, mix_ref, o_ref, so_ref, s_ref, prev_ref, prev2_ref, *, lb, lvalid, nlb, nsb):
    del mix_ref
    ib = pl.program_id(2)
    masked = lvalid < nlb * lb // nsb

    @pl.when(ib == 0)
    def _():
        s_ref[...] = s0_ref[...]
        prev_ref[0] = sh_r_ref[...]
        prev_ref[1] = sh_k_ref[...]
        prev_ref[2] = sh_v_ref[...]
        prev2_ref[0] = sh_wa_ref[...]
        prev2_ref[1] = sh_gd_ref[...]

    def shifted(x_ref, mu_ref, carry_ref, i, sl):
        x = x_ref[:, sl]
        prev = _shift_rows(x, carry_ref[i, :, :, sl], 1)
        carry_ref[i, :, :, sl] = _last_tiles(x, nsb)
        return x + (prev - x) * mu_ref[:, sl]

    all_lanes = slice(0, LANES)
    wa = shifted(wa_ref, muwa_ref, prev2_ref, 0, all_lanes)
    gd = shifted(gd_ref, mugd_ref, prev2_ref, 1, all_lanes)
    tanh_wa = jnp.tanh(wa)
    sig_gd = _sigmoid(gd)
    valid = _row_valid(ib, lb, lvalid, LANES, nsb) if masked else None
    for e in range(RWKV_PAIRS_PER_STEP):
        sl = slice(e * LANES, (e + 1) * LANES)
        r = shifted(r_ref, mur_ref, prev_ref, 0, sl)
        k = shifted(k_ref, muk_ref, prev_ref, 1, sl)
        v = shifted(v_ref, muv_ref, prev_ref, 2, sl)
        w_raw = -_softplus(-(w0_ref[:, sl] + _dot(tanh_wa, wup_ref[:, sl]))) - 0.5
        logw = -jnp.exp(w_raw)
        a_sig = _sigmoid(a0_ref[:, sl] + _dot(wa, aup_ref[:, sl]))
        gate = _dot(sig_gd, gup_ref[:, sl])
        kk = k * kk_ref[:, sl]
        kk = kk * lax.rsqrt(_half_sum(kk * kk) + NORM_EPS)
        kp = k * (1.0 + (a_sig - 1.0) * ka_ref[:, sl])
        rec_a = -kk
        rec_b = kk * a_sig
        if masked:
            zero = lambda x: jnp.where(valid, x, 0.0)
            logw, rec_a, rec_b, kp, v = zero(logw), zero(rec_a), zero(rec_b), zero(kp), zero(v)
        _rwkv_pair(e, sl, r, v, kp, logw, rec_a, rec_b, gate, rk_ref[:, sl], lnw_ref[:, sl], lnb_ref[:, sl],
                   o_ref, s_ref, lb=lb, nsb=nsb)

    @pl.when(ib == nlb - 1)
    def _():
        so_ref[...] = s_ref[...]


def _rwkv_pair(e, sl, r, v, kp, logw, rec_a, rec_b, gate, r_k, ln_w, ln_b, o_ref, s_ref, *, lb, nsb):
    cum = _dot_exact_lhs(_chunk_tri(lb, CHUNK), logw)
    e_pos = jnp.exp(cum)
    e_neg = jnp.exp(-cum)
    at_all = rec_a * jnp.exp(cum - logw)
    bt_all = rec_b * e_neg
    kt_all = kp * e_neg
    rt_all = r * e_pos

    stacked = 2 * CHUNK
    row_head = lax.broadcasted_iota(jnp.int32, (stacked, LANES), 0) // CHUNK
    lane_head = lax.broadcasted_iota(jnp.int32, (stacked, LANES), 1) // RWKV_HEAD
    own = row_head == lane_head
    nchunk = lb // CHUNK
    dup3 = lambda x: jnp.concatenate([x.reshape(nchunk, CHUNK, LANES)] * 2, axis=1)
    stack3 = lambda x: jnp.where(own, dup3(x), 0.0)
    block_diag = (lax.broadcasted_iota(jnp.int32, (LANES, LANES), 0) // RWKV_HEAD) == (
        lax.broadcasted_iota(jnp.int32, (LANES, LANES), 1) // RWKV_HEAD)
    ti = lax.broadcasted_iota(jnp.int32, (stacked, stacked), 0) % CHUNK
    tj = lax.broadcasted_iota(jnp.int32, (stacked, stacked), 1) % CHUNK

    at3, bt3, kt3, rt3, v3 = stack3(at_all), stack3(bt_all), stack3(kt_all), stack3(rt_all), dup3(v)
    scores = _bmm_nt(jnp.concatenate([at3, rt3], axis=1), jnp.concatenate([bt3, kt3], axis=1))
    l_ab = jnp.where(ti > tj, scores[:, :stacked, :stacked], 0.0)
    l_ak = jnp.where(ti > tj, scores[:, :stacked, stacked:], 0.0)
    l_rb = jnp.where(ti >= tj, scores[:, stacked:, :stacked], 0.0)
    l_rk = jnp.where(ti >= tj, scores[:, stacked:, stacked:], 0.0)
    inv = _unit_lower_inv(-l_ab, CHUNK)
    t12 = _bmm(inv, jnp.concatenate([at3, _bmm(l_ak, v3)], axis=2))
    t1 = t12[:, :, :LANES]
    t2 = jnp.where(own, t12[:, :, LANES:], 0.0)
    rb12 = _bmm(l_rb, jnp.concatenate([t1, t2], axis=2))
    rp = rt3 + rb12[:, :, :LANES]
    op = jnp.where(own, rb12[:, :, LANES:] + _bmm(l_rk, v3), 0.0)
    cum3 = cum.reshape(nchunk, CHUNK, LANES)
    cum_end = jnp.broadcast_to(cum3[:, CHUNK - 1:, :], cum3.shape).reshape(lb, LANES)
    to_end = jnp.exp(cum_end - cum)
    b_end = rec_b * to_end
    k_end = kp * to_end
    s_decay = jnp.exp(cum_end)

    chunks_per_seq = nchunk // nsb
    for c in range(nchunk):
        rs = slice(c * CHUNK, (c + 1) * CHUNK)
        b = c // chunks_per_seq
        vc = v[rs]
        b2 = jnp.concatenate([b_end[rs], b_end[rs]], axis=0)
        gh = _dot_tn(jnp.concatenate([t1[c], t2[c]], axis=1), b2)
        h = gh[LANES:] + _dot_tn(vc, k_end[rs])
        s = s_ref[b, e]
        o2 = jnp.where(own, _dot_nt(rp[c], s), 0.0) + op[c]
        o = o2[:CHUNK] + o2[CHUNK:]
        s_ref[b, e] = s * s_decay[c * CHUNK:c * CHUNK + 1] + jnp.where(block_diag, _dot(s, gh[:LANES]) + h, 0.0)

        inv_n = 1.0 / RWKV_HEAD
        mu = _half_sum(o) * inv_n
        var = _half_sum(jnp.square(o - mu)) * inv_n
        on = (o - mu) * lax.rsqrt(var + GN_EPS) * ln_w + ln_b
        bonus = _half_sum(r[rs] * kp[rs] * r_k) * vc
        o_ref[rs, sl] = ((on + bonus) * gate[rs]).astype(BF16)


def _rwkv(p, mix, lw, shift_init, s0, *, row0, nseq, lpad, lvalid, lb, nsb=1):
    ngroup, nlb, rb = _seq_blocking(row0, nseq, lpad, lb, nsb)
    npair = GW // LANES
    pps = RWKV_PAIRS_PER_STEP
    steps = npair // pps
    pw = pps * LANES

    def lane_blk(cb, per_pair):
        width = pw if per_pair else LANES
        return width, (lambda j: cb // pps + j) if per_pair else (lambda j: cb)

    def pblk(cb, per_pair):
        width, col = lane_blk(CB_RWKV + cb, per_pair)
        return pl.BlockSpec((lb, width), lambda s, j, i: (rb(s, i), col(j)))

    def mublk(cb, per_pair):
        width, col = lane_blk(cb, per_pair)
        return pl.BlockSpec((1, width), lambda s, j, i: (0, col(j)))

    def shblk(cb, per_pair):
        width, col = lane_blk(cb, per_pair)
        return pl.BlockSpec((nsb, SUBLANES, width), lambda s, j, i: (s, 0, col(j)))

    pair_vec = pl.BlockSpec((1, pw), lambda s, j, i: (0, j))
    pair_mat = pl.BlockSpec((LANES, pw), lambda s, j, i: (0, j))
    sblk = pl.BlockSpec((nsb, pps, LANES, LANES), lambda s, j, i: (s, j, 0, 0))
    mu = lw['rwkv_mu'].reshape(1, -1)
    zeros64 = jnp.zeros((64, GW), F32)
    wup = jnp.concatenate([lw['rwkv_w_up'], zeros64], axis=0).astype(BF16)
    aup = jnp.concatenate([zeros64, lw['rwkv_a_up']], axis=0).astype(BF16)
    row = lambda x: x.reshape(1, GW)
    blocks = [(0, True), (4, True), (8, True), (12, False), (13, False)]
    return pl.pallas_call(
        functools.partial(_rwkv_kernel, lb=lb, lvalid=lvalid, nlb=nlb, nsb=nsb),
        grid=(ngroup, steps, nlb),
        in_specs=(
            [pblk(cb, pp) for cb, pp in blocks]
            + [mublk(cb, pp) for cb, pp in blocks]
            + [shblk(cb, pp) for cb, pp in blocks]
            + [pair_mat, pair_mat, pair_mat] + [pair_vec] * 7 + [sblk, MIX_ANY]
        ),
        out_specs=[pl.BlockSpec((lb, pw), lambda s, j, i: (rb(s, i), MIX_RWKV * steps + j)), sblk],
        out_shape=[
            jax.ShapeDtypeStruct(mix.shape, mix.dtype),
            jax.ShapeDtypeStruct((nseq, npair, LANES, LANES), F32),
        ],
        input_output_aliases={26: 0},
        scratch_shapes=[pltpu.VMEM((nsb, pps, LANES, LANES), F32),
                        pltpu.VMEM((3, nsb, SUBLANES, pw), F32),
                        pltpu.VMEM((2, nsb, SUBLANES, LANES), F32)],
        compiler_params=_cparams(("parallel", "parallel", "arbitrary")),
        name="rwkv7",
    )(p, p, p, p, p, mu, mu, mu, mu, mu, shift_init, shift_init, shift_init, shift_init, shift_init,
      wup, aup, lw['rwkv_g_up'].astype(BF16), row(lw['rwkv_w0']), row(lw['rwkv_a0']), row(lw['rwkv_k_k']),
      row(lw['rwkv_k_a']), row(lw['rwkv_r_k']), row(lw['rwkv_ln_w']), row(lw['rwkv_ln_b']), s0, mix)


def _fox_prep_kernel(q_ref, k_ref, v_ref, sm_ref, qw_ref, kw_ref, bf_ref, k_all_ref, v_all_ref,
                     qa_ref, k4_ref, ka_ref, vb_ref, lf_ref, c_ref, v4_ref, carry_ref, *, lb):
    del k_all_ref, v_all_ref
    @pl.when(pl.program_id(1) == 0)
    def _():
        carry_ref[...] = jnp.zeros_like(carry_ref)

    logf = -_softplus(-(sm_ref[...] + bf_ref[...]))
    lf_ref[...] = logf
    c = _dot_exact_lhs(_chunk_tri(lb, lb), logf) + carry_ref[0:1, :]
    c_ref[...] = c
    carry_ref[...] = jnp.broadcast_to(c[lb - 1:], carry_ref.shape)

    qw = qw_ref[...]
    kw = kw_ref[...]
    lane = lax.broadcasted_iota(jnp.int32, (lb, HD), 1)
    q_tail = jnp.where(lane < 2, -1.0, 0.0).astype(BF16)
    for h in range(NH):
        sl = slice(h * HD, (h + 1) * HD)
        a0 = h * FOX_AUG
        qa_ref[:, a0:a0 + HD] = (_rms(q_ref[:, sl], qw) * (HD ** -0.5 * LOG2E)).astype(BF16)
        qa_ref[:, a0 + HD:a0 + FOX_AUG] = q_tail
        kn = _rms(k_ref[:, sl], kw)
        k4_ref[:, h, :] = kn
        v4_ref[:, h, :] = v_ref[:, sl]
        ka_ref[:, a0:a0 + HD] = kn.astype(BF16)
        c2 = c[:, SM_F + h:SM_F + h + 1] * LOG2E
        c_hi = c2.astype(BF16).astype(F32)
        ka_ref[:, a0 + HD:a0 + FOX_AUG] = jnp.where(lane == 0, c_hi, jnp.where(lane == 1, c2 - c_hi, 0.0)).astype(BF16)
    vb_ref[...] = v_ref[...].astype(BF16)


def _fox_prep(p, k_all, v_all, q_w, k_w, b_f, *, layer, nseq, lpad, lb):
    nlb = lpad // lb
    rb = lambda s, i: s * nlb + i
    seg = lambda j: pl.BlockSpec((lb, GW), lambda s, i: (rb(s, i), j))
    vec = pl.BlockSpec((1, LANES), lambda s, i: (0, 0))
    rows = nseq * lpad
    cache_blk = pl.BlockSpec((lb, NH, HD), lambda s, i: (layer * nseq * nlb + rb(s, i), 0, 0))
    bf_lane = jnp.zeros((1, LANES), F32).at[0, SM_F:SM_F + NH].set(b_f)
    return pl.pallas_call(
        functools.partial(_fox_prep_kernel, lb=lb),
        grid=(nseq, nlb),
        in_specs=[seg(0), seg(1), seg(2),
                  pl.BlockSpec((lb, LANES), lambda s, i: (rb(s, i), CB_SMALL)), vec, vec, vec,
                  MIX_ANY, MIX_ANY],
        out_specs=[
            pl.BlockSpec((lb, NH * FOX_AUG), lambda s, i: (rb(s, i), 0)),
            cache_blk,
            pl.BlockSpec((lb, NH * FOX_AUG), lambda s, i: (rb(s, i), 0)),
            pl.BlockSpec((lb, GW), lambda s, i: (rb(s, i), 0)),
            pl.BlockSpec((lb, LANES), lambda s, i: (rb(s, i), 0)),
            pl.BlockSpec((lb, LANES), lambda s, i: (rb(s, i), 0)),
            cache_blk,
        ],
        out_shape=[
            jax.ShapeDtypeStruct((rows, NH * FOX_AUG), BF16),
            jax.ShapeDtypeStruct(k_all.shape, F32),
            jax.ShapeDtypeStruct((rows, NH * FOX_AUG), BF16),
            jax.ShapeDtypeStruct((rows, GW), BF16),
            jax.ShapeDtypeStruct((rows, LANES), F32),
            jax.ShapeDtypeStruct((rows, LANES), F32),
            jax.ShapeDtypeStruct(v_all.shape, F32),
        ],
        input_output_aliases={7: 1, 8: 6},
        scratch_shapes=[pltpu.VMEM((SUBLANES, LANES), F32)],
        compiler_params=_cparams(("parallel", "arbitrary")),
        name="fox_prep",
    )(p, p, p, p, q_w.reshape(1, HD), k_w.reshape(1, HD), bf_lane, k_all, v_all)


def _fox_flash_kernel(q_ref, k_ref, v_ref, c_ref, g_ref, mix_ref, o_ref, m_ref, l_ref, acc_ref, *, tq):
    del mix_ref
    h0 = pl.program_id(1) * FOX_HEADS_PER_STEP
    qi = pl.program_id(2)
    c_all = c_ref[...]
    c_cols = [_lane_col(c_all, SM_F + h0 + e) * LOG2E for e in range(FOX_HEADS_PER_STEP)]
    m_ref[...] = jnp.full_like(m_ref, -1e30)
    l_ref[...] = jnp.zeros_like(l_ref)
    acc_ref[...] = jnp.zeros_like(acc_ref)

    def block(ks, width, diagonal):
        for e in range(FOX_HEADS_PER_STEP):
            sl = slice(e * HD, (e + 1) * HD)
            sa = slice(e * FOX_AUG, (e + 1) * FOX_AUG)
            vb = v_ref[pl.ds(ks, width), sl]
            t = lax.dot_general(q_ref[:, sa], k_ref[pl.ds(ks, width), sa], (((1,), (1,)), ((), ())),
                                preferred_element_type=F32)
            if diagonal:
                ii = lax.broadcasted_iota(jnp.int32, (tq, width), 0)
                jj = lax.broadcasted_iota(jnp.int32, (tq, width), 1)
                t = jnp.where(jj <= ii + (width - tq), t, -jnp.inf)
            m_old = m_ref[e]
            m_new = jnp.maximum(m_old, jnp.max(t, axis=-1, keepdims=True) + c_cols[e])
            alpha = jnp.exp2(m_old - m_new)
            pr = jnp.exp2(t - (m_new - c_cols[e]))
            l_ref[e] = alpha * l_ref[e] + jnp.sum(pr, axis=-1, keepdims=True)
            acc_ref[e] = alpha * acc_ref[e] + _dot(pr, vb)
            m_ref[e] = m_new

    def body(kp, carry):
        block(pl.multiple_of(kp * 2 * tq, 2 * tq), 2 * tq, False)
        return carry

    odd = qi % 2 == 1
    lax.fori_loop(0, jnp.where(odd, qi // 2, jnp.maximum(qi // 2 - 1, 0)), body, 0)

    @pl.when(odd)
    def _():
        block(pl.multiple_of((qi - 1) * tq, tq), 2 * tq, True)

    @pl.when(qi == 0)
    def _():
        block(0, tq, True)

    @pl.when(jnp.logical_and(jnp.logical_not(odd), qi > 0))
    def _():
        block(pl.multiple_of((qi - 2) * tq, tq), 3 * tq, True)
    for e in range(FOX_HEADS_PER_STEP):
        sl = slice(e * HD, (e + 1) * HD)
        o_ref[:, sl] = (acc_ref[e] / l_ref[e] * _sigmoid(g_ref[:, sl])).astype(BF16)


def _fox_flash(qa, ka, vb, c_col, p, mix, *, nseq, lpad, tq):
    nq = lpad // tq
    hw = FOX_HEADS_PER_STEP * HD
    aw = FOX_HEADS_PER_STEP * FOX_AUG
    g_block0 = (CB_FOX + 3 * NH) // FOX_HEADS_PER_STEP
    o_block0 = MIX_FOX * (GW // hw)
    return pl.pallas_call(
        functools.partial(_fox_flash_kernel, tq=tq),
        grid=(nseq, NH // FOX_HEADS_PER_STEP, nq),
        in_specs=[
            pl.BlockSpec((tq, aw), lambda s, h, i: (s * nq + i, h)),
            pl.BlockSpec((lpad, aw), lambda s, h, i: (s, h)),
            pl.BlockSpec((lpad, hw), lambda s, h, i: (s, h)),
            pl.BlockSpec((tq, LANES), lambda s, h, i: (s * nq + i, 0)),
            pl.BlockSpec((tq, hw), lambda s, h, i: (s * nq + i, g_block0 + h)),
            MIX_ANY,
        ],
        out_specs=pl.BlockSpec((tq, hw), lambda s, h, i: (s * nq + i, o_block0 + h)),
        out_shape=jax.ShapeDtypeStruct(mix.shape, mix.dtype),
        input_output_aliases={5: 0},
        scratch_shapes=[pltpu.VMEM((FOX_HEADS_PER_STEP, tq, 1), F32), pltpu.VMEM((FOX_HEADS_PER_STEP, tq, 1), F32),
                        pltpu.VMEM((FOX_HEADS_PER_STEP, tq, HD), F32)],
        compiler_params=_cparams(("parallel", "parallel", "arbitrary")),
        name="fox_flash",
    )(qa, ka, vb, c_col, p, mix)


PAGE_GROUP = 16
PAGE_COLS = PAGE * NH
NQ_PAD = SUBLANES


def _page_sums_kernel(lf_ref, upper_ref, heads_ref, o_ref):
    lf = lf_ref[...]
    o_ref[:, :PAGE_COLS] = _dot_exact_rhs(lf, upper_ref[...])
    o_ref[:, PAGE_COLS:] = _dot_exact_rhs(lf, heads_ref[...])


def _page_sums(cache_lf):
    depth, n_pool = cache_lf.shape[:2]
    rows = depth * n_pool
    tile = _row_tile(rows, 512)
    idx = np.arange(PAGE_COLS)
    same_head = (idx[:, None] % NH) == (idx[None, :] % NH)
    upper = jnp.asarray(same_head & (idx[:, None] // NH > idx[None, :] // NH), BF16)
    heads = jnp.asarray(same_head, BF16)
    const = pl.BlockSpec((PAGE_COLS, PAGE_COLS), lambda i: (0, 0))
    sums = pl.pallas_call(
        _page_sums_kernel,
        grid=(rows // tile,),
        in_specs=[pl.BlockSpec((tile, PAGE_COLS), lambda i: (i, 0)), const, const],
        out_specs=pl.BlockSpec((tile, 2 * PAGE_COLS), lambda i: (i, 0)),
        out_shape=jax.ShapeDtypeStruct((rows, 2 * PAGE_COLS), F32),
        compiler_params=_cparams(("parallel",)),
        name="page_sums",
    )(cache_lf.reshape(rows, PAGE_COLS), upper, heads)
    return sums.reshape(depth, n_pool, 2, PAGE_COLS)


def _fox_sample_kernel(pt_ref, pq_ref, sm_ref, *rest, lvalid, lpad, n_steps):
    del pt_ref
    g = PAGE_GROUP
    kps, vps, sums = rest[:g], rest[g:2 * g], rest[2 * g:3 * g]
    (qw_ref, kw_ref, bf_ref, mix_ref, o_ref, kn_ref, lf_ref,
     qn_s, cq_s, m_s, l_s, acc_s, carry_s) = rest[3 * g:]
    del mix_ref
    i = pl.program_id(1)
    scale = HD ** -0.5
    nq = NQ_PAD
    nrow = NH * nq

    @pl.when(i == 0)
    def _():
        logf = -_softplus(-(sm_ref[...] + bf_ref[...]))
        logf = jnp.where(_row_valid(0, lpad, lvalid, LANES), logf, 0.0)
        lf_ref[...] = logf[:nq]
        c = _dot_exact_lhs(_chunk_tri(lpad, lpad), logf)
        c_t = c.T
        carry_s[...] = jnp.zeros_like(carry_s)
        qi = lax.broadcasted_iota(jnp.int32, (nq, lpad), 0)
        kj = lax.broadcasted_iota(jnp.int32, (nq, lpad), 1)
        for h in range(NH):
            sl = slice(h * HD, (h + 1) * HD)
            rows = slice(h * nq, (h + 1) * nq)
            qn = _rms(pq_ref[:nq, sl], qw_ref[...])
            kn = _rms(pq_ref[:, GW + h * HD:GW + (h + 1) * HD], kw_ref[...])
            vn = pq_ref[:, 2 * GW + h * HD:2 * GW + (h + 1) * HD]
            c_h = c[:nq, SM_F + h:SM_F + h + 1]
            qn_s[rows, :] = qn
            cq_s[rows, :] = c_h
            kn_ref[:, sl] = kn[:nq]
            s = _dot_nt(qn, kn) * scale + c_h - c_t[SM_F + h:SM_F + h + 1, :]
            s = jnp.where(kj <= qi, s, -jnp.inf)
            m = jnp.max(s, axis=-1, keepdims=True)
            pr = jnp.exp(s - m)
            m_s[rows, :] = m
            l_s[rows, :] = jnp.sum(pr, axis=-1, keepdims=True)
            acc_s[rows, :] = _dot(pr, vn)

    run = carry_s[...]
    suffix = [None] * g
    for j in reversed(range(g)):
        suffix[j] = sums[j][0:1, :] + run
        run = run + sums[j][1:2, :]
    carry_s[...] = run

    row_head = lax.broadcasted_iota(jnp.int32, (nrow, PAGE_COLS), 0) // nq
    col_head = lax.broadcasted_iota(jnp.int32, (nrow, PAGE_COLS), 1) % NH
    own = row_head == col_head
    qs = qn_s[...].astype(BF16)
    bias = cq_s[...]
    tiles = [jnp.where(own, _dot_nt(qs, kps[j][...]) * scale + bias + suffix[j], -jnp.inf) for j in range(g)]
    m_old = m_s[...]
    m_new = m_old
    for t in tiles:
        m_new = jnp.maximum(m_new, jnp.max(t, axis=-1, keepdims=True))
    alpha = jnp.exp(m_old - m_new)
    l_new = alpha * l_s[...]
    acc = alpha * acc_s[...]
    for j, t in enumerate(tiles):
        pr = jnp.exp(t - m_new)
        l_new = l_new + jnp.sum(pr, axis=-1, keepdims=True)
        acc = acc + _dot(pr, vps[j][...])
    m_s[...] = m_new
    l_s[...] = l_new
    acc_s[...] = acc

    @pl.when(i == n_steps - 1)
    def _():
        o_ref[...] = jnp.zeros_like(o_ref)
        out = acc_s[...] / l_s[...]
        for h in range(NH):
            sl = slice(h * HD, (h + 1) * HD)
            gate = pq_ref[:nq, 3 * GW + h * HD:3 * GW + (h + 1) * HD]
            o_ref[:nq, sl] = (out[h * nq:(h + 1) * nq] * _sigmoid(gate)).astype(BF16)


def _fox_sample(p, mix, page_table, cache_k, cache_v, page_sums, q_w, k_w, b_f, *, layer, row0, nseq, lpad, lvalid):
    n_pages = page_table.shape[1]
    g = PAGE_GROUP
    assert n_pages % g == 0 and lvalid <= NQ_PAD
    n_steps = n_pages // g
    bf_lane = jnp.zeros((1, LANES), F32).at[0, SM_F:SM_F + NH].set(b_f)
    rb = lambda b: row0 // lpad + b
    vec = pl.BlockSpec((1, LANES), lambda b, i, pt: (0, 0))

    def page_spec(shape, j):
        zeros = (0,) * len(shape)
        return pl.BlockSpec((None, None) + shape,
                            lambda b, i, pt: (layer, pt[b, n_pages - g * (i + 1) + j]) + zeros)

    grid_spec = pltpu.PrefetchScalarGridSpec(
        num_scalar_prefetch=1,
        grid=(nseq, n_steps),
        in_specs=(
            [pl.BlockSpec((lpad, 4 * GW), lambda b, i, pt: (rb(b), CB_FOX)),
             pl.BlockSpec((lpad, LANES), lambda b, i, pt: (rb(b), CB_SMALL))]
            + [page_spec((PAGE_COLS, HD), j) for j in range(g)]
            + [page_spec((PAGE_COLS, HD), j) for j in range(g)]
            + [page_spec((2, PAGE_COLS), j) for j in range(g)]
            + [vec, vec, vec, MIX_ANY]
        ),
        out_specs=[
            pl.BlockSpec((lpad, GW), lambda b, i, pt: (rb(b), MIX_FOX)),
            pl.BlockSpec((None, NQ_PAD, GW), lambda b, i, pt: (b, 0, 0)),
            pl.BlockSpec((None, NQ_PAD, LANES), lambda b, i, pt: (b, 0, 0)),
        ],
        scratch_shapes=[
            pltpu.VMEM((NH * NQ_PAD, HD), F32),
            pltpu.VMEM((NH * NQ_PAD, 1), F32),
            pltpu.VMEM((NH * NQ_PAD, 1), F32),
            pltpu.VMEM((NH * NQ_PAD, 1), F32),
            pltpu.VMEM((NH * NQ_PAD, HD), F32),
            pltpu.VMEM((1, PAGE_COLS), F32),
        ],
    )
    return pl.pallas_call(
        functools.partial(_fox_sample_kernel, lvalid=lvalid, lpad=lpad, n_steps=n_steps),
        grid_spec=grid_spec,
        out_shape=[
            jax.ShapeDtypeStruct(mix.shape, mix.dtype),
            jax.ShapeDtypeStruct((nseq, NQ_PAD, GW), F32),
            jax.ShapeDtypeStruct((nseq, NQ_PAD, LANES), F32),
        ],
        input_output_aliases={3 * g + 6: 0},
        compiler_params=_cparams(("parallel", "arbitrary")),
        name="fox_sample",
    )(page_table, p, p, *([cache_k] * g), *([cache_v] * g), *([page_sums] * g),
      q_w.reshape(1, HD), k_w.reshape(1, HD), bf_lane, mix)


W_IN_SHIFT = 8
_COPY, _SHIFT, _SMALL, _ZERO = 0, 1, 2, 3


def _w_in_plan():
    kind = np.zeros(NP_COLS // LANES, np.int32)
    src = np.zeros(NP_COLS // LANES, np.int32)

    def put(cb, n, first_src_block, k):
        kind[cb:cb + n] = k
        src[cb:cb + n] = first_src_block + np.arange(n)

    put(CB_FOX, 16, (5896 - W_IN_SHIFT) // LANES, _SHIFT)
    put(CB_RET, 16, (2056 - W_IN_SHIFT) // LANES, _SHIFT)
    put(CB_GDN, 12, 0, _COPY)
    put(CB_GDN_Z, 4, (1544 - W_IN_SHIFT) // LANES, _SHIFT)
    put(CB_RWKV, 14, (4104 - W_IN_SHIFT) // LANES, _SHIFT)
    put(CB_SMALL, 1, 1536 // LANES, _SMALL)
    put(CB_SMALL + 1, 1, 0, _ZERO)
    src_b = np.where(kind == _SHIFT, (src + 1) * (LANES // SUBLANES),
                     np.where(kind == _SMALL, 7944 // SUBLANES, 0))
    return jnp.asarray(kind), jnp.asarray(src), jnp.asarray(src_b.astype(np.int32))


def _prep_w_in_kernel(kind_ref, sa_ref, sb_ref, a_ref, b_ref, o_ref):
    del sa_ref, sb_ref
    kind = kind_ref[pl.program_id(0)]
    depth = o_ref.shape[0]
    row8 = lax.broadcasted_iota(jnp.int32, (SUBLANES, o_ref.shape[2]), 0)

    @pl.when(kind == _COPY)
    def _():
        for l in range(depth):
            o_ref[l] = a_ref[:, l, :].astype(BF16)

    @pl.when(kind == _SHIFT)
    def _():
        for l in range(depth):
            o_ref[l] = jnp.concatenate([a_ref[:, l, :][W_IN_SHIFT:], b_ref[:, l, :]], axis=0).astype(BF16)

    @pl.when(kind == _SMALL)
    def _():
        for l in range(depth):
            f_rows = jnp.where(row8 < NH, b_ref[:, l, :], 0.0)
            zeros = jnp.zeros((LANES - SM_F - SUBLANES, o_ref.shape[2]), F32)
            o_ref[l] = jnp.concatenate([a_ref[:, l, :][:SM_F], f_rows, zeros], axis=0).astype(BF16)

    @pl.when(kind == _ZERO)
    def _():
        o_ref[...] = jnp.zeros_like(o_ref)


def _prep_w_in(w_in):
    depth, k, _ = w_in.shape
    w_t = jnp.transpose(w_in, (2, 0, 1))
    kind, src_a, src_b = _w_in_plan()
    grid_spec = pltpu.PrefetchScalarGridSpec(
        num_scalar_prefetch=3,
        grid=(NP_COLS // LANES,),
        in_specs=[
            pl.BlockSpec((LANES, depth, k), lambda j, kd, sa, sb: (sa[j], 0, 0)),
            pl.BlockSpec((SUBLANES, depth, k), lambda j, kd, sa, sb: (sb[j], 0, 0)),
        ],
        out_specs=pl.BlockSpec((depth, LANES, k), lambda j, kd, sa, sb: (0, j, 0)),
    )
    return pl.pallas_call(
        _prep_w_in_kernel,
        grid_spec=grid_spec,
        out_shape=jax.ShapeDtypeStruct((depth, NP_COLS, k), BF16),
        compiler_params=_cparams(("parallel",)),
        name="prep_w_in",
    )(kind, src_a, src_b, w_t, w_t)


def _rope_tables(pos):
    half = HD // 2
    inv = 1.0 / (ROPE_BASE ** jnp.linspace(0.0, 1.0, half, dtype=F32))
    ang = pos.astype(F32)[:, None] * inv[None, :]
    cos, sin = jnp.cos(ang), jnp.sin(ang)
    return jnp.concatenate([cos, cos], axis=-1), jnp.concatenate([-sin, sin], axis=-1)


def _state_tile(state, nrows):
    b, _, c = state.shape
    return jnp.concatenate([jnp.zeros((b, SUBLANES - nrows, c), F32), state], axis=1)


def _rwkv_pair_states(s):
    b = s.shape[0]
    s = s.reshape(b, 4, 2, RWKV_HEAD, RWKV_HEAD)
    z = jnp.zeros_like(s[:, :, 0])
    top = jnp.concatenate([s[:, :, 0], z], axis=-1)
    bot = jnp.concatenate([z, s[:, :, 1]], axis=-1)
    return jnp.concatenate([top, bot], axis=-2)


def _rwkv_unpair_states(sp):
    b = sp.shape[0]
    a = sp[:, :, :RWKV_HEAD, :RWKV_HEAD]
    c = sp[:, :, RWKV_HEAD:, RWKV_HEAD:]
    return jnp.stack([a, c], axis=2).reshape(b, 8, RWKV_HEAD, RWKV_HEAD)


def kernel(x_prompt, x_sample, cache_fox_k, cache_fox_v, cache_fox_logf, cache_mem_k, cache_mem_v, state_gdn_conv, state_gdn_S, state_ret_S, state_rwkv_shift, state_rwkv_S, page_table, mem_prompt, norm_mix, w_in, gdn_conv_w, gdn_A_log, gdn_dt_bias, gdn_norm, rwkv_mu, rwkv_w0, rwkv_w_up, rwkv_a0, rwkv_a_up, rwkv_g_up, rwkv_k_k, rwkv_k_a, rwkv_r_k, rwkv_ln_w, rwkv_ln_b, fox_b_f, fox_q_norm, fox_k_norm, w_out, norm_x, norm_mem, xattn_wq, xattn_wkv, xattn_q_norm, xattn_k_norm, xattn_wo, norm_ffn, ffn_w_gate, ffn_w_up, ffn_w_down):
    weights = {
        'norm_mix': norm_mix, 'w_in': w_in, 'gdn_conv_w': gdn_conv_w, 'gdn_A_log': gdn_A_log,
        'gdn_dt_bias': gdn_dt_bias, 'gdn_norm': gdn_norm, 'rwkv_mu': rwkv_mu, 'rwkv_w0': rwkv_w0,
        'rwkv_w_up': rwkv_w_up, 'rwkv_a0': rwkv_a0, 'rwkv_a_up': rwkv_a_up, 'rwkv_g_up': rwkv_g_up,
        'rwkv_k_k': rwkv_k_k, 'rwkv_k_a': rwkv_k_a, 'rwkv_r_k': rwkv_r_k, 'rwkv_ln_w': rwkv_ln_w,
        'rwkv_ln_b': rwkv_ln_b, 'fox_b_f': fox_b_f, 'fox_q_norm': fox_q_norm, 'fox_k_norm': fox_k_norm,
        'w_out': w_out, 'norm_x': norm_x, 'norm_mem': norm_mem, 'xattn_wq': xattn_wq, 'xattn_wkv': xattn_wkv,
        'xattn_q_norm': xattn_q_norm, 'xattn_k_norm': xattn_k_norm, 'xattn_wo': xattn_wo,
        'norm_ffn': norm_ffn, 'ffn_w_gate': ffn_w_gate, 'ffn_w_up': ffn_w_up, 'ffn_w_down': ffn_w_down,
    }
    depth = w_in.shape[0]
    bp, lp, d = x_prompt.shape
    bs, ls, _ = x_sample.shape
    n_pages = page_table.shape[1]
    past_len = n_pages * PAGE
    tp = bp * lp
    ts = bs * SAMPLE_PAD
    tt = tp + ts
    tm = _row_tile(tt)
    lb_p = min(256, lp)
    assert ls >= CONV_WIDTH - 1 and ls <= SUBLANES and lp % lb_p == 0

    xs_pad = jnp.zeros((bs, SAMPLE_PAD, d), F32).at[:, :ls].set(x_sample)
    x = jnp.concatenate([x_prompt.reshape(tp, d), xs_pad.reshape(ts, d)], axis=0)

    cos_p, sin_p = _rope_tables(jnp.arange(lp, dtype=jnp.int32))
    cos_s, sin_s = _rope_tables(jnp.tile(past_len + jnp.arange(SAMPLE_PAD, dtype=jnp.int32), bs))
    log_gamma = jnp.log(1.0 - jnp.exp2(-(RET_GAMMA_BASE + jnp.arange(NH, dtype=F32))))
    n_pool = cache_fox_k.shape[1]
    cache_k = cache_fox_k.reshape(depth, n_pool, PAGE_COLS, HD)
    cache_v = cache_fox_v.reshape(depth, n_pool, PAGE_COLS, HD)
    page_sums = _page_sums(cache_fox_logf.reshape(depth, n_pool, 1, PAGE_COLS))
    zeros_s = jnp.zeros((bp, NH, HD, HD), F32)
    zeros_conv = jnp.zeros((bp, SUBLANES, 3 * GW), F32)
    zeros_shift = jnp.zeros((bp, SUBLANES, 1792), F32)

    w_in_t = _prep_w_in(w_in)
    fox_k_all = jnp.zeros((depth * tp, NH, HD), F32)
    fox_v_all = jnp.zeros((depth * tp, NH, HD), F32)

    outs_p, outs_s, mem_ks, mem_vs = [], [], [], []
    for l in range(depth):
        lw = {name: arr[l] for name, arr in weights.items()}
        p = _norm_matmul(x, lw['norm_mix'], w_in_t, l, tm=tm, tn=1024,
                         transposed_w=True)

        o_mix = jnp.zeros((tt, 4 * GW), BF16)
        gp = dict(row0=0, nseq=bp, lpad=lp, lvalid=lp, lb=lb_p)
        o_mix, gdn_s_p = _gdn(p, o_mix, lw['gdn_conv_w'], zeros_conv, zeros_s, lw['gdn_A_log'], lw['gdn_dt_bias'],
                              lw['gdn_norm'], **{**gp, 'lb': min(GDN_LB, lp)})
        o_mix, ret_s_p = _ret(p, o_mix, cos_p, sin_p, log_gamma, zeros_s, **{**gp, 'lb': min(RET_LB, lp)})
        o_mix, rwkv_s_p = _rwkv(p, o_mix, lw, zeros_shift, zeros_s, **{**gp, 'lb': min(RWKV_LB, lp)})
        qa, fox_k_all, ka, vb, lf, c_col, fox_v_all = _fox_prep(
            p, fox_k_all, fox_v_all, lw['fox_q_norm'], lw['fox_k_norm'], lw['fox_b_f'],
            layer=l, nseq=bp, lpad=lp, lb=lb_p)
        o_mix = _fox_flash(qa, ka, vb, c_col, p, o_mix, nseq=bp, lpad=lp, tq=min(FOX_TQ, lp))

        gs = dict(row0=tp, nseq=bs, lpad=SAMPLE_PAD, lvalid=ls, lb=ts, nsb=bs)
        o_mix, gdn_s_s = _gdn(p, o_mix, lw['gdn_conv_w'], _state_tile(state_gdn_conv[l], CONV_WIDTH - 1),
                              state_gdn_S[l], lw['gdn_A_log'], lw['gdn_dt_bias'], lw['gdn_norm'], **gs)
        o_mix, ret_s_s = _ret(p, o_mix, cos_s, sin_s, log_gamma, state_ret_S[l], **gs)
        o_mix, rwkv_s_s = _rwkv(p, o_mix, lw, _state_tile(state_rwkv_shift[l], 1),
                                _rwkv_pair_states(state_rwkv_S[l]), **gs)
        o_mix, kn_s, lf_s = _fox_sample(p, o_mix, page_table, cache_k, cache_v, page_sums,
                                        lw['fox_q_norm'], lw['fox_k_norm'], lw['fox_b_f'],
                                        layer=l, row0=tp, nseq=bs, lpad=SAMPLE_PAD, lvalid=ls)
        x = _matmul_res(o_mix, w_out, l, x, tm=tm, tn=512)

        kv = _norm_matmul(mem_prompt.reshape(bp * N_MEM, d), lw['norm_mem'], xattn_wkv, l,
                          tm=256, tn=XW, head_w=lw['xattn_k_norm'], norm_tiles=1)
        mk = kv[:, :XW].reshape(bp, N_MEM, XW)
        mv = kv[:, XW:].reshape(bp, N_MEM, XW)
        q = _norm_matmul(x, lw['norm_x'], xattn_wq, l, tm=tm, tn=XW,
                         head_w=lw['xattn_q_norm'], norm_tiles=1)
        xo_p = _xattn(q, mk, mv, row0=0, nrows=tp, tq=lb_p, rows_per_seq=lp)
        xo_s = _xattn(q, cache_mem_k[l].reshape(bs, N_MEM, XW), cache_mem_v[l].reshape(bs, N_MEM, XW),
                      row0=tp, nrows=ts, tq=SAMPLE_PAD, rows_per_seq=SAMPLE_PAD)
        x = _matmul_res(jnp.concatenate([xo_p, xo_s], axis=0), xattn_wo, l, x, tm=tm, tn=512)

        hidden = _swiglu_up(x, lw['norm_ffn'], ffn_w_gate, ffn_w_up, l, tm=tm, tn=512)
        x = _matmul_res(hidden, ffn_w_down, l, x, tm=tm, tn=256)

        c0 = CB_GDN * LANES
        r0 = CB_RWKV * LANES
        v0 = (CB_FOX + 2 * NH) * LANES

        def last_rows(row_end, n, col0, width, nseq, stride):
            return jnp.stack([lax.slice(p, (b * stride + row_end - n, col0), (b * stride + row_end, col0 + width))
                              for b in range(nseq)], axis=0)

        ps_v = lax.slice(p, (tp, v0), (tt, v0 + GW)).reshape(bs, SAMPLE_PAD, NH, HD)
        outs_p.append((
            None, None,
            lf.reshape(bp, lp, LANES)[:, :, SM_F:SM_F + NH],
            last_rows(lp, CONV_WIDTH - 1, c0, 3 * GW, bp, lp),
            gdn_s_p, ret_s_p,
            last_rows(lp, 1, r0, 1792, bp, lp),
            _rwkv_unpair_states(rwkv_s_p),
        ))
        outs_s.append((
            kn_s[:, :ls].reshape(bs, ls, NH, HD),
            ps_v[:, :ls],
            lf_s[:, :ls, SM_F:SM_F + NH],
            last_rows(tp + ls, CONV_WIDTH - 1, c0, 3 * GW, bs, SAMPLE_PAD),
            gdn_s_s, ret_s_s,
            last_rows(tp + ls, 1, r0, 1792, bs, SAMPLE_PAD),
            _rwkv_unpair_states(rwkv_s_s),
        ))
        mem_ks.append(mk.reshape(bp, N_MEM, NH, HD))
        mem_vs.append(mv.reshape(bp, N_MEM, NH, HD))

    stk = lambda seq, i: jnp.stack([e[i] for e in seq], axis=0)
    yp = x[:tp].reshape(bp, lp, d)
    ys = x[tp:].reshape(bs, SAMPLE_PAD, d)[:, :ls]
    cache_shape = (depth, bp, lp, NH, HD)
    return (yp, ys, fox_k_all.reshape(cache_shape), fox_v_all.reshape(cache_shape), stk(outs_p, 2),
            jnp.stack(mem_ks, 0), jnp.stack(mem_vs, 0),
            stk(outs_p, 3), stk(outs_p, 4), stk(outs_p, 5), stk(outs_p, 6), stk(outs_p, 7),
            stk(outs_s, 0), stk(outs_s, 1), stk(outs_s, 2), stk(outs_s, 3), stk(outs_s, 4), stk(outs_s, 5),
            stk(outs_s, 6), stk(outs_s, 7))
```

```python
import functools
import math

import jax
import jax.numpy as jnp
import numpy as np
from jax import lax
from jax.experimental import pallas as pl
from jax.experimental.pallas import tpu as pltpu

F32 = jnp.float32
BF16 = jnp.bfloat16

LANES = 128
SUBLANES = 8
VMEM_LIMIT = 56 * 1024 * 1024

D_MODEL = 2048
GW = D_MODEL // 4
HD = 128
NH = GW // HD
RWKV_HEAD = 64
CONV_WIDTH = 4
PAGE = 128
N_MEM = 256
XW = 512
RWKV_COLS = 3 * GW + 64 + 64 + 128
W_GDN_AB = 3 * GW
W_GDN_Z = W_GDN_AB + 2 * NH
W_RET = W_GDN_Z + GW
W_RWKV = W_RET + 4 * GW
W_FOX = W_RWKV + RWKV_COLS
W_FOX_F = W_FOX + 4 * GW
NORM_EPS = 1e-6
GN_EPS = 64e-5
RET_GAMMA_BASE = 5.0
ROPE_BASE = 10000.0
CHUNK = 64
SAMPLE_PAD = 64
FOX_TQ = 512
FOX_HEADS_PER_STEP = 4
FOX_AUG = 2 * HD
LOG2E = 1.0 / math.log(2.0)
GDN_LB = 512
GDN_HEADS_PER_STEP = 4
RWKV_LB = 512
RWKV_PAIRS_PER_STEP = 2
RET_LB = 512

NP_COLS = 8192
CB_FOX, CB_RET, CB_GDN, CB_GDN_Z, CB_RWKV, CB_SMALL = 0, 16, 32, 44, 48, 62
SM_A, SM_B, SM_F = 0, 4, 8


def _cparams(sem):
    return pltpu.CompilerParams(dimension_semantics=sem, vmem_limit_bytes=VMEM_LIMIT)


def _dot(a, b):
    return jnp.dot(a.astype(BF16), b.astype(BF16), preferred_element_type=F32)


def _dot_nt(a, b):
    return lax.dot_general(a.astype(BF16), b.astype(BF16), (((1,), (1,)), ((), ())),
                           preferred_element_type=F32)


def _dot_tn(a, b):
    return lax.dot_general(a.astype(BF16), b.astype(BF16), (((0,), (0,)), ((), ())),
                           preferred_element_type=F32)


def _split3(x):
    hi = x.astype(BF16)
    r = x - hi.astype(F32)
    mid = r.astype(BF16)
    lo = (r - mid.astype(F32)).astype(BF16)
    return hi, mid, lo


def _dot_exact_lhs(m, x):
    hi, mid, lo = _split3(x)
    d = lambda p: jnp.dot(m, p, preferred_element_type=F32)
    return d(hi) + d(mid) + d(lo)


def _dot_exact_rhs(x, m):
    hi, mid, lo = _split3(x)
    d = lambda p: jnp.dot(p, m, preferred_element_type=F32)
    return d(hi) + d(mid) + d(lo)


def _bmm(a, b):
    return lax.dot_general(a.astype(BF16), b.astype(BF16), (((2,), (1,)), ((0,), (0,))),
                           preferred_element_type=F32)


def _bmm_nt(a, b):
    return lax.dot_general(a.astype(BF16), b.astype(BF16), (((2,), (2,)), ((0,), (0,))),
                           preferred_element_type=F32)


def _unit_lower_inv(n, nil):
    c = n.shape[-1]
    ii = lax.broadcasted_iota(jnp.int32, (c, c), 0)
    jj = lax.broadcasted_iota(jnp.int32, (c, c), 1)
    p = jnp.where(ii == jj, 1.0, 0.0).astype(F32) - n
    levels = int(math.log2(nil)) - 1
    q = _bmm(n, n)
    for level in range(levels):
        if level == levels - 1:
            return p + _bmm(p, q)
        both = _bmm(jnp.concatenate([q, p], axis=1), q)
        q, p = both[:, :c], p + both[:, c:]


def _chunk_tri(lb, chunk):
    ii = lax.broadcasted_iota(jnp.int32, (lb, lb), 0)
    jj = lax.broadcasted_iota(jnp.int32, (lb, lb), 1)
    same = (ii // chunk) == (jj // chunk)
    return jnp.where(jnp.logical_and(ii >= jj, same), 1.0, 0.0).astype(BF16)


def _shift_rows(x, prev, s):
    nsb = prev.shape[0]
    seq_rows = x.shape[0] // nsb
    rolled = pltpu.roll(x, s, axis=0)
    row = lax.broadcasted_iota(jnp.int32, (SUBLANES, x.shape[1]), 0)
    pieces = []
    for b in range(nsb):
        piece = rolled[b * seq_rows:(b + 1) * seq_rows]
        top = jnp.where(row < s, pltpu.roll(prev[b], s, axis=0), piece[:SUBLANES])
        pieces += [top, piece[SUBLANES:]]
    return jnp.concatenate(pieces, axis=0)


def _last_tiles(x, nsb):
    seq_rows = x.shape[0] // nsb
    return x.reshape(nsb, seq_rows, x.shape[1])[:, seq_rows - SUBLANES:, :]


def _softplus(x):
    return jnp.maximum(x, 0.0) + jnp.log1p(jnp.exp(-jnp.abs(x)))


def _sigmoid(x):
    return jax.nn.sigmoid(x)


def _silu(x):
    return x * jax.nn.sigmoid(x)


def _lane_col(x, idx):
    lane = lax.broadcasted_iota(jnp.int32, x.shape, 1)
    return jnp.sum(jnp.where(lane == idx, x, 0.0), axis=-1, keepdims=True)


def _rms(x, w=None):
    y = x * lax.rsqrt(jnp.mean(x * x, axis=-1, keepdims=True) + NORM_EPS)
    return y if w is None else y * w


def _row_valid(lb_index, lb, lvalid, width, nsb=1):
    row = lax.broadcasted_iota(jnp.int32, (lb, width), 0) % (lb // nsb) + lb_index * lb
    return row < lvalid


def _norm_matmul_kernel(x_ref, g_ref, w_ref, hw_ref, o_ref, xn_ref, *, norm_tiles, transposed_w):
    j = pl.program_id(1)

    @pl.when(j == 0)
    def _():
        xn_ref[...] = _rms(x_ref[...], g_ref[...]).astype(BF16)

    if transposed_w:
        acc = _dot_nt(xn_ref[...], w_ref[...])
    else:
        acc = jnp.dot(xn_ref[...], w_ref[...].astype(BF16), preferred_element_type=F32)
    if norm_tiles == 0:
        o_ref[...] = acc
    else:
        @pl.when(j < norm_tiles)
        def _():
            hw = hw_ref[...]
            for h in range(acc.shape[1] // HD):
                sl = slice(h * HD, (h + 1) * HD)
                o_ref[:, sl] = _rms(acc[:, sl], hw)

        @pl.when(j >= norm_tiles)
        def _():
            o_ref[...] = acc


def _row_tile(m, cap=1100):
    return next(t for t in range(cap - cap % 16, 0, -16) if m % t == 0)


def _norm_matmul(x, g, w, layer, *, tm, tn, head_w=None, norm_tiles=0, transposed_w=False):
    m, k = x.shape
    n = w.shape[1] if transposed_w else w.shape[2]
    if head_w is None:
        head_w = jnp.ones((HD,), F32)
    if transposed_w:
        w_spec = pl.BlockSpec((None, tn, k), lambda i, j: (layer, j, 0))
    else:
        w_spec = pl.BlockSpec((None, k, tn), lambda i, j: (layer, 0, j))
    return pl.pallas_call(
        functools.partial(_norm_matmul_kernel, norm_tiles=norm_tiles, transposed_w=transposed_w),
        grid=(m // tm, n // tn),
        in_specs=[
            pl.BlockSpec((tm, k), lambda i, j: (i, 0)),
            pl.BlockSpec((1, k), lambda i, j: (0, 0)),
            w_spec,
            pl.BlockSpec((1, HD), lambda i, j: (0, 0)),
        ],
        out_specs=pl.BlockSpec((tm, tn), lambda i, j: (i, j)),
        out_shape=jax.ShapeDtypeStruct((m, n), F32),
        scratch_shapes=[pltpu.VMEM((tm, k), BF16)],
        compiler_params=_cparams(("parallel", "arbitrary")),
        name="norm_matmul",
    )(x, g.reshape(1, k), w, head_w.reshape(1, HD))


def _matmul_res_kernel(a_ref, w_ref, r_ref, o_ref):
    o_ref[...] = r_ref[...] + jnp.dot(a_ref[...], w_ref[...].astype(BF16), preferred_element_type=F32)


def _matmul_res(a, w, layer, res, *, tm, tn):
    m, k = a.shape
    n = w.shape[2]
    return pl.pallas_call(
        _matmul_res_kernel,
        grid=(m // tm, n // tn),
        in_specs=[
            pl.BlockSpec((tm, k), lambda i, j: (i, 0)),
            pl.BlockSpec((None, k, tn), lambda i, j: (layer, 0, j)),
            pl.BlockSpec((tm, tn), lambda i, j: (i, j)),
        ],
        out_specs=pl.BlockSpec((tm, tn), lambda i, j: (i, j)),
        out_shape=jax.ShapeDtypeStruct((m, n), F32),
        compiler_params=_cparams(("parallel", "parallel")),
        name="matmul_res",
    )(a, w, res)


def _swiglu_up_kernel(x_ref, g_ref, wg_ref, wu_ref, o_ref, xn_ref):
    @pl.when(pl.program_id(1) == 0)
    def _():
        xn_ref[...] = _rms(x_ref[...], g_ref[...]).astype(BF16)

    xn = xn_ref[...]
    gate = jnp.dot(xn, wg_ref[...].astype(BF16), preferred_element_type=F32)
    up = jnp.dot(xn, wu_ref[...].astype(BF16), preferred_element_type=F32)
    o_ref[...] = (_silu(gate) * up).astype(BF16)


def _swiglu_up(x, g, wg, wu, layer, *, tm, tn):
    m, k = x.shape
    n = wg.shape[2]
    return pl.pallas_call(
        _swiglu_up_kernel,
        grid=(m // tm, n // tn),
        in_specs=[
            pl.BlockSpec((tm, k), lambda i, j: (i, 0)),
            pl.BlockSpec((1, k), lambda i, j: (0, 0)),
            pl.BlockSpec((None, k, tn), lambda i, j: (layer, 0, j)),
            pl.BlockSpec((None, k, tn), lambda i, j: (layer, 0, j)),
        ],
        out_specs=pl.BlockSpec((tm, tn), lambda i, j: (i, j)),
        out_shape=jax.ShapeDtypeStruct((m, n), BF16),
        scratch_shapes=[pltpu.VMEM((tm, k), BF16)],
        compiler_params=_cparams(("parallel", "arbitrary")),
        name="swiglu_up",
    )(x, g.reshape(1, k), wg, wu)


def _xattn_kernel(q_ref, k_ref, v_ref, o_ref):
    scale = HD ** -0.5
    for h in range(NH):
        sl = slice(h * HD, (h + 1) * HD)
        s = _dot_nt(q_ref[:, sl], k_ref[:, sl]) * scale
        m = jnp.max(s, axis=-1, keepdims=True)
        p = jnp.exp(s - m)
        l = jnp.sum(p, axis=-1, keepdims=True)
        o_ref[:, sl] = (_dot(p, v_ref[:, sl]) / l).astype(BF16)


def _xattn(q, mem_k, mem_v, *, row0, nrows, tq, rows_per_seq):
    tiles_per_seq = rows_per_seq // tq
    t0 = row0 // tq
    return pl.pallas_call(
        _xattn_kernel,
        grid=(nrows // tq,),
        in_specs=[
            pl.BlockSpec((tq, XW), lambda i: (t0 + i, 0)),
            pl.BlockSpec((None, N_MEM, XW), lambda i: (i // tiles_per_seq, 0, 0)),
            pl.BlockSpec((None, N_MEM, XW), lambda i: (i // tiles_per_seq, 0, 0)),
        ],
        out_specs=pl.BlockSpec((tq, XW), lambda i: (i, 0)),
        out_shape=jax.ShapeDtypeStruct((nrows, XW), BF16),
        compiler_params=_cparams(("parallel",)),
        name="xattn",
    )(q, mem_k, mem_v)


def _gdn_kernel(q_ref, k_ref, v_ref, z_ref, sm_ref, cwq_ref, cwk_ref, cwv_ref, cq_ref, ck_ref, cv_ref,
                s0_ref, alog_ref, dtb_ref, nw_ref, mix_ref, o_ref, so_ref, s_ref, prev_ref, gt_ref,
                *, lb, lvalid, nlb, nsb):
    del mix_ref
    h0 = pl.program_id(1) * GDN_HEADS_PER_STEP
    ib = pl.program_id(2)
    masked = lvalid < nlb * lb // nsb

    @pl.when(ib == 0)
    def _():
        s_ref[...] = s0_ref[...]
        prev_ref[0] = cq_ref[...]
        prev_ref[1] = ck_ref[...]
        prev_ref[2] = cv_ref[...]

    def conv(x_ref, w_ref, i, sl):
        x = x_ref[:, sl]
        w = w_ref[:, sl]
        prev = prev_ref[i, :, :, sl]
        y = x * w[3:4]
        for s in (1, 2, 3):
            y = y + _shift_rows(x, prev, s) * w[3 - s:4 - s]
        prev_ref[i, :, :, sl] = _last_tiles(x, nsb)
        return _silu(y)

    sm = sm_ref[...]
    g_blk = -jnp.exp(alog_ref[...]) * _softplus(sm + dtb_ref[...])
    beta_blk = _sigmoid(sm)
    if masked:
        valid = _row_valid(ib, lb, lvalid, LANES, nsb)
        g_blk = jnp.where(valid, g_blk, 0.0)
        beta_blk = jnp.where(valid, beta_blk, 0.0)
    gc_blk = _dot_exact_lhs(_chunk_tri(lb, CHUNK), g_blk)
    gt_ref[...] = gc_blk.T
    group = min(2 * CHUNK, lb)
    ng = lb // group
    nchunk = lb // CHUNK
    ii = lax.broadcasted_iota(jnp.int32, (group, group), 0)
    jj = lax.broadcasted_iota(jnp.int32, (group, group), 1)
    lower = jnp.logical_and(ii >= jj, (ii // CHUNK) == (jj // CHUNK))
    nw = nw_ref[...]
    for e in range(GDN_HEADS_PER_STEP):
        _gdn_head(h0 + e, e, slice(e * HD, (e + 1) * HD), conv, (q_ref, k_ref, v_ref), (cwq_ref, cwk_ref, cwv_ref),
                  z_ref, o_ref, s_ref, gt_ref, gc_blk, beta_blk, lower, ii > jj, nw,
                  lb=lb, nsb=nsb, group=group, ng=ng, nchunk=nchunk)

    @pl.when(ib == nlb - 1)
    def _():
        so_ref[...] = s_ref[...]


def _gdn_head(h, e, sl, conv, qkv_refs, cw_refs, z_ref, o_ref, s_ref, gt_ref, gc_blk, beta_blk, lower, strict, nw,
              *, lb, nsb, group, ng, nchunk):
    q = conv(qkv_refs[0], cw_refs[0], 0, sl)
    k = conv(qkv_refs[1], cw_refs[1], 1, sl)
    v = conv(qkv_refs[2], cw_refs[2], 2, sl)
    q = q * lax.rsqrt(jnp.sum(q * q, axis=-1, keepdims=True) + NORM_EPS) * (HD ** -0.5)
    k = k * lax.rsqrt(jnp.sum(k * k, axis=-1, keepdims=True) + NORM_EPS)
    g_col_all = _lane_col(gc_blk, SM_A + h)
    beta_all = _lane_col(beta_blk, SM_B + h)
    g_row_all = gt_ref[pl.ds(SM_A + h, 1), :]

    to3 = lambda x: x.reshape(ng, group, x.shape[-1])
    q3, k3, v3 = to3(q), to3(k), to3(v)
    g_col3 = to3(g_col_all)
    beta3 = to3(beta_all)
    g_row3 = jnp.stack([g_row_all[:, i * group:(i + 1) * group] for i in range(ng)], axis=0)
    dec3 = jnp.exp(jnp.where(lower, g_col3 - g_row3, -jnp.inf))
    scores = _bmm_nt(jnp.concatenate([k3, q3], axis=1), k3)
    n3 = jnp.where(strict, beta3 * scores[:, :group] * dec3, 0.0)
    qk3 = scores[:, group:] * dec3
    ainv3 = _unit_lower_inv(n3, CHUNK)
    eg3 = jnp.exp(g_col3)
    wuv3 = _bmm(ainv3, jnp.concatenate([beta3 * eg3 * k3, beta3 * v3], axis=2))
    qk_wuv = _bmm(qk3, wuv3)
    qp = (q3 * eg3 - qk_wuv[:, :, :HD]).reshape(lb, HD)
    op = qk_wuv[:, :, HD:].reshape(lb, HD)
    wuv_all = wuv3.reshape(lb, 2 * HD)
    g_chunks = g_col_all.reshape(nchunk, CHUNK, 1)
    g_end = jnp.broadcast_to(g_chunks[:, CHUNK - 1:, :], g_chunks.shape).reshape(lb, 1)
    kd = k * jnp.exp(g_end - g_col_all)
    s_decay = jnp.exp(g_end)
    chunks_per_seq = nchunk // nsb
    for c in range(nchunk):
        r = slice(c * CHUNK, (c + 1) * CHUNK)
        b = c // chunks_per_seq
        an = _dot_tn(kd[r], wuv_all[r])
        s = s_ref[b, e]
        o = _dot(qp[r], s) + op[r]
        s_ref[b, e] = s * s_decay[c * CHUNK:c * CHUNK + 1] - _dot(an[:, :HD], s) + an[:, HD:]
        o_ref[r, sl] = (_rms(o, nw) * _silu(z_ref[r, sl])).astype(BF16)


MIX_ANY =pl.BlockSpec(memory_space=pl.ANY)
MIX_GDN, MIX_RET, MIX_RWKV, MIX_FOX = 0, 1, 2, 3


def _seq_blocking(row0, nseq, lpad, lb, nsb):
    assert (nsb == 1 and lpad % lb == 0) or (lb == nsb * lpad and nseq % nsb == 0)
    nlb = lpad * nsb // lb
    return nseq // nsb, nlb, lambda s, i: (row0 + s * lpad * nsb) // lb + i


def _gdn(p, mix, conv_w, conv_init, s0, a_log, dt_bias, norm_w, *, row0, nseq, lpad, lvalid, lb, nsb=1):
    ngroup, nlb, rb = _seq_blocking(row0, nseq, lpad, lb, nsb)
    hps = GDN_HEADS_PER_STEP
    hw = hps * HD
    steps = NH // hps
    pblk = lambda cb: pl.BlockSpec((lb, hw), lambda s, h, i: (rb(s, i), cb // hps + h))
    cwblk = lambda j: pl.BlockSpec((CONV_WIDTH, hw), lambda s, h, i: (0, j * steps + h))
    ciblk = lambda j: pl.BlockSpec((nsb, SUBLANES, hw), lambda s, h, i: (s, 0, j * steps + h))
    sblk = pl.BlockSpec((nsb, hps, HD, HD), lambda s, h, i: (s, h, 0, 0))
    vec = pl.BlockSpec((1, LANES), lambda s, h, i: (0, 0))
    lane_pad = lambda x: jnp.zeros((1, LANES), F32).at[0, :x.shape[0]].set(x)
    return pl.pallas_call(
        functools.partial(_gdn_kernel, lb=lb, lvalid=lvalid, nlb=nlb, nsb=nsb),
        grid=(ngroup, steps, nlb),
        in_specs=[
            pblk(CB_GDN), pblk(CB_GDN + NH), pblk(CB_GDN + 2 * NH), pblk(CB_GDN_Z),
            pl.BlockSpec((lb, LANES), lambda s, h, i: (rb(s, i), CB_SMALL)),
            cwblk(0), cwblk(1), cwblk(2), ciblk(0), ciblk(1), ciblk(2),
            sblk,
            vec, vec, vec, MIX_ANY,
        ],
        out_specs=[
            pl.BlockSpec((lb, hw), lambda s, h, i: (rb(s, i), MIX_GDN * steps + h)),
            sblk,
        ],
        out_shape=[
            jax.ShapeDtypeStruct(mix.shape, mix.dtype),
            jax.ShapeDtypeStruct((nseq, NH, HD, HD), F32),
        ],
        input_output_aliases={15: 0},
        scratch_shapes=[
            pltpu.VMEM((nsb, hps, HD, HD), F32),
            pltpu.VMEM((3, nsb, SUBLANES, hw), F32),
            pltpu.VMEM((LANES, lb), F32),
        ],
        compiler_params=_cparams(("parallel", "parallel", "arbitrary")),
        name="gdn",
    )(p, p, p, p, p, conv_w, conv_w, conv_w, conv_init, conv_init, conv_init, s0,
      lane_pad(a_log), lane_pad(dt_bias), norm_w.reshape(1, HD), mix)


def _ret_kernel(q_ref, k_ref, v_ref, g_ref, cos_ref, sin_ref, lg_ref, s0_ref, mix_ref, o_ref, so_ref, s_ref,
                *, lb, cv, nlb, nsb):
    del mix_ref
    ib = pl.program_id(2)

    @pl.when(ib == 0)
    def _():
        s_ref[...] = s0_ref[...]

    cos = cos_ref[...]
    sin = sin_ref[...]
    rot = lambda x: x * cos + pltpu.roll(x, HD // 2, axis=1) * sin
    q = rot(q_ref[...])
    k = rot(k_ref[...]) * (HD ** -0.5)
    v = v_ref[...]
    lg = lg_ref[...][:, 0:1]

    ii = lax.broadcasted_iota(jnp.int32, (CHUNK, CHUNK), 0)
    jj = lax.broadcasted_iota(jnp.int32, (CHUNK, CHUNK), 1)
    rel = (ii - jj).astype(F32)
    dmat = jnp.where(rel >= 0, jnp.exp(jnp.maximum(rel, 0.0) * lg), 0.0)
    idx = lax.broadcasted_iota(jnp.int32, (CHUNK, 1), 0)
    idf = idx.astype(F32)
    xi = jnp.exp((idf + 1.0) * lg)
    zeta = jnp.where(idx < cv, jnp.exp((cv - 1.0 - idf) * lg), 0.0)
    gc = jnp.exp(cv * lg)
    nchunk = lb // CHUNK
    to3 = lambda x: x.reshape(nchunk, CHUNK, HD)
    q3, k3, v3 = to3(q), to3(k), to3(v)
    o_intra = _bmm(_bmm_nt(q3, k3) * dmat, v3)
    qx = q3 * xi
    kz = k3 * zeta
    chunks_per_seq = nchunk // nsb
    for b in range(nsb):
        s = s_ref[b]
        for c in range(b * chunks_per_seq, (b + 1) * chunks_per_seq):
            r = slice(c * CHUNK, (c + 1) * CHUNK)
            o = o_intra[c] + _dot(qx[c], s)
            s = s * gc + _dot_tn(kz[c], v3[c])
            o_ref[r, :] = (_rms(o) * _silu(g_ref[r, :])).astype(BF16)
        s_ref[b] = s

    @pl.when(ib == nlb - 1)
    def _():
        so_ref[...] = s_ref[...]


def _ret(p, mix, cos_t, sin_t, log_gamma, s0, *, row0, nseq, lpad, lvalid, lb, nsb=1):
    ngroup, nlb, rb = _seq_blocking(row0, nseq, lpad, lb, nsb)
    cv = CHUNK if lvalid == lpad else lvalid
    assert cv == CHUNK or (lpad == CHUNK and 0 < lvalid < CHUNK)
    pblk = lambda cb: pl.BlockSpec((lb, HD), lambda s, h, i: (rb(s, i), cb + h))
    tblk = pl.BlockSpec((lb, HD), lambda s, h, i: (i, 0))
    sblk = pl.BlockSpec((nsb, None, HD, HD), lambda s, h, i: (s, h, 0, 0))
    lg = jnp.broadcast_to(log_gamma[:, None, None], (NH, 1, LANES))
    return pl.pallas_call(
        functools.partial(_ret_kernel, lb=lb, cv=cv, nlb=nlb, nsb=nsb),
        grid=(ngroup, NH, nlb),
        in_specs=[
            pblk(CB_RET), pblk(CB_RET + NH), pblk(CB_RET + 2 * NH), pblk(CB_RET + 3 * NH),
            tblk, tblk,
            pl.BlockSpec((None, 1, LANES), lambda s, h, i: (h, 0, 0)),
            sblk, MIX_ANY,
        ],
        out_specs=[pl.BlockSpec((lb, HD), lambda s, h, i: (rb(s, i), MIX_RET * NH + h)), sblk],
        out_shape=[
            jax.ShapeDtypeStruct(mix.shape, mix.dtype),
            jax.ShapeDtypeStruct((nseq, NH, HD, HD), F32),
        ],
        input_output_aliases={8: 0},
        scratch_shapes=[pltpu.VMEM((nsb, HD, HD), F32)],
        compiler_params=_cparams(("parallel", "parallel", "arbitrary")),
        name="retention",
    )(p, p, p, p, cos_t, sin_t, lg, s0, mix)


def _half_sum(x):
    lane = lax.broadcasted_iota(jnp.int32, x.shape, 1)
    lo = lane < RWKV_HEAD
    s_lo = jnp.sum(jnp.where(lo, x, 0.0), axis=-1, keepdims=True)
    s_hi = jnp.sum(jnp.where(lo, 0.0, x), axis=-1, keepdims=True)
    return jnp.where(lo, s_lo, s_hi)


def _rwkv_kernel(r_ref, k_ref, v_ref, wa_ref, gd_ref, mur_ref, muk_ref, muv_ref, muwa_ref, mugd_ref,
                 sh_r_ref, sh_k_ref, sh_v_ref, sh_wa_ref, sh_gd_ref,
                 wup_ref, aup_ref, gup_ref, w0_ref, a0_ref, kk_ref, ka_ref, rk_ref, lnw_ref, lnb_ref,
                 s0_ref, mix_ref, o_ref, so_ref, s_ref, prev_ref, prev2_ref, *, lb, lvalid, nlb, nsb):
    del mix_ref
    ib = pl.program_id(2)
    masked = lvalid < nlb * lb // nsb

    @pl.when(ib == 0)
    def _():
        s_ref[...] = s0_ref[...]
        prev_ref[0] = sh_r_ref[...]
        prev_ref[1] = sh_k_ref[...]
        prev_ref[2] = sh_v_ref[...]
        prev2_ref[0] = sh_wa_ref[...]
        prev2_ref[1] = sh_gd_ref[...]

    def shifted(x_ref, mu_ref, carry_ref, i, sl):
        x = x_ref[:, sl]
        prev = _shift_rows(x, carry_ref[i, :, :, sl], 1)
        carry_ref[i, :, :, sl] = _last_tiles(x, nsb)
        return x + (prev - x) * mu_ref[:, sl]

    all_lanes = slice(0, LANES)
    wa = shifted(wa_ref, muwa_ref, prev2_ref, 0, all_lanes)
    gd = shifted(gd_ref, mugd_ref, prev2_ref, 1, all_lanes)
    tanh_wa = jnp.tanh(wa)
    sig_gd = _sigmoid(gd)
    valid = _row_valid(ib, lb, lvalid, LANES, nsb) if masked else None
    for e in range(RWKV_PAIRS_PER_STEP):
        sl = slice(e * LANES, (e + 1) * LANES)
        r = shifted(r_ref, mur_ref, prev_ref, 0, sl)
        k = shifted(k_ref, muk_ref, prev_ref, 1, sl)
        v = shifted(v_ref, muv_ref, prev_ref, 2, sl)
        w_raw = -_softplus(-(w0_ref[:, sl] + _dot(tanh_wa, wup_ref[:, sl]))) - 0.5
        logw = -jnp.exp(w_raw)
        a_sig = _sigmoid(a0_ref[:, sl] + _dot(wa, aup_ref[:, sl]))
        gate = _dot(sig_gd, gup_ref[:, sl])
        kk = k * kk_ref[:, sl]
        kk = kk * lax.rsqrt(_half_sum(kk * kk) + NORM_EPS)
        kp = k * (1.0 + (a_sig - 1.0) * ka_ref[:, sl])
        rec_a = -kk
        rec_b = kk * a_sig
        if masked:
            zero = lambda x: jnp.where(valid, x, 0.0)
            logw, rec_a, rec_b, kp, v = zero(logw), zero(rec_a), zero(rec_b), zero(kp), zero(v)
        _rwkv_pair(e, sl, r, v, kp, logw, rec_a, rec_b, gate, rk_ref[:, sl], lnw_ref[:, sl], lnb_ref[:, sl],
                   o_ref, s_ref, lb=lb, nsb=nsb)

    @pl.when(ib == nlb - 1)
    def _():
        so_ref[...] = s_ref[...]


def _rwkv_pair(e, sl, r, v, kp, logw, rec_a, rec_b, gate, r_k, ln_w, ln_b, o_ref, s_ref, *, lb, nsb):
    cum = _dot_exact_lhs(_chunk_tri(lb, CHUNK), logw)
    e_pos = jnp.exp(cum)
    e_neg = jnp.exp(-cum)
    at_all = rec_a * jnp.exp(cum - logw)
    bt_all = rec_b * e_neg
    kt_all = kp * e_neg
    rt_all = r * e_pos

    stacked = 2 * CHUNK
    row_head = lax.broadcasted_iota(jnp.int32, (stacked, LANES), 0) // CHUNK
    lane_head = lax.broadcasted_iota(jnp.int32, (stacked, LANES), 1) // RWKV_HEAD
    own = row_head == lane_head
    nchunk = lb // CHUNK
    dup3 = lambda x: jnp.concatenate([x.reshape(nchunk, CHUNK, LANES)] * 2, axis=1)
    stack3 = lambda x: jnp.where(own, dup3(x), 0.0)
    block_diag = (lax.broadcasted_iota(jnp.int32, (LANES, LANES), 0) // RWKV_HEAD) == (
        lax.broadcasted_iota(jnp.int32, (LANES, LANES), 1) // RWKV_HEAD)
    ti = lax.broadcasted_iota(jnp.int32, (stacked, stacked), 0) % CHUNK
    tj = lax.broadcasted_iota(jnp.int32, (stacked, stacked), 1) % CHUNK

    at3, bt3, kt3, rt3, v3 = stack3(at_all), stack3(bt_all), stack3(kt_all), stack3(rt_all), dup3(v)
    scores = _bmm_nt(jnp.concatenate([at3, rt3], axis=1), jnp.concatenate([bt3, kt3], axis=1))
    l_ab = jnp.where(ti > tj, scores[:, :stacked, :stacked], 0.0)
    l_ak = jnp.where(ti > tj, scores[:, :stacked, stacked:], 0.0)
    l_rb = jnp.where(ti >= tj, scores[:, stacked:, :stacked], 0.0)
    l_rk = jnp.where(ti >= tj, scores[:, stacked:, stacked:], 0.0)
    inv = _unit_lower_inv(-l_ab, CHUNK)
    t12 = _bmm(inv, jnp.concatenate([at3, _bmm(l_ak, v3)], axis=2))
    t1 = t12[:, :, :LANES]
    t2 = jnp.where(own, t12[:, :, LANES:], 0.0)
    rb12 = _bmm(l_rb, jnp.concatenate([t1, t2], axis=2))
    rp = rt3 + rb12[:, :, :LANES]
    op = jnp.where(own, rb12[:, :, LANES:] + _bmm(l_rk, v3), 0.0)
    cum3 = cum.reshape(nchunk, CHUNK, LANES)
    cum_end = jnp.broadcast_to(cum3[:, CHUNK - 1:, :], cum3.shape).reshape(lb, LANES)
    to_end = jnp.exp(cum_end - cum)
    b_end = rec_b * to_end
    k_end = kp * to_end
    s_decay = jnp.exp(cum_end)

    chunks_per_seq = nchunk // nsb
    for c in range(nchunk):
        rs = slice(c * CHUNK, (c + 1) * CHUNK)
        b = c // chunks_per_seq
        vc = v[rs]
        b2 = jnp.concatenate([b_end[rs], b_end[rs]], axis=0)
        gh = _dot_tn(jnp.concatenate([t1[c], t2[c]], axis=1), b2)
        h = gh[LANES:] + _dot_tn(vc, k_end[rs])
        s = s_ref[b, e]
        o2 = jnp.where(own, _dot_nt(rp[c], s), 0.0) + op[c]
        o = o2[:CHUNK] + o2[CHUNK:]
        s_ref[b, e] = s * s_decay[c * CHUNK:c * CHUNK + 1] + jnp.where(block_diag, _dot(s, gh[:LANES]) + h, 0.0)

        inv_n = 1.0 / RWKV_HEAD
        mu = _half_sum(o) * inv_n
        var = _half_sum(jnp.square(o - mu)) * inv_n
        on = (o - mu) * lax.rsqrt(var + GN_EPS) * ln_w + ln_b
        bonus = _half_sum(r[rs] * kp[rs] * r_k) * vc
        o_ref[rs, sl] = ((on + bonus) * gate[rs]).astype(BF16)


def _rwkv(p, mix, lw, shift_init, s0, *, row0, nseq, lpad, lvalid, lb, nsb=1):
    ngroup, nlb, rb = _seq_blocking(row0, nseq, lpad, lb, nsb)
    npair = GW // LANES
    pps = RWKV_PAIRS_PER_STEP
    steps = npair // pps
    pw = pps * LANES

    def lane_blk(cb, per_pair):
        width = pw if per_pair else LANES
        return width, (lambda j: cb // pps + j) if per_pair else (lambda j: cb)

    def pblk(cb, per_pair):
        width, col = lane_blk(CB_RWKV + cb, per_pair)
        return pl.BlockSpec((lb, width), lambda s, j, i: (rb(s, i), col(j)))

    def mublk(cb, per_pair):
        width, col = lane_blk(cb, per_pair)
        return pl.BlockSpec((1, width), lambda s, j, i: (0, col(j)))

    def shblk(cb, per_pair):
        width, col = lane_blk(cb, per_pair)
        return pl.BlockSpec((nsb, SUBLANES, width), lambda s, j, i: (s, 0, col(j)))

    pair_vec = pl.BlockSpec((1, pw), lambda s, j, i: (0, j))
    pair_mat = pl.BlockSpec((LANES, pw), lambda s, j, i: (0, j))
    sblk = pl.BlockSpec((nsb, pps, LANES, LANES), lambda s, j, i: (s, j, 0, 0))
    mu = lw['rwkv_mu'].reshape(1, -1)
    zeros64 = jnp.zeros((64, GW), F32)
    wup = jnp.concatenate([lw['rwkv_w_up'], zeros64], axis=0).astype(BF16)
    aup = jnp.concatenate([zeros64, lw['rwkv_a_up']], axis=0).astype(BF16)
    row = lambda x: x.reshape(1, GW)
    blocks = [(0, True), (4, True), (8, True), (12, False), (13, False)]
    return pl.pallas_call(
        functools.partial(_rwkv_kernel, lb=lb, lvalid=lvalid, nlb=nlb, nsb=nsb),
        grid=(ngroup, steps, nlb),
        in_specs=(
            [pblk(cb, pp) for cb, pp in blocks]
            + [mublk(cb, pp) for cb, pp in blocks]
            + [shblk(cb, pp) for cb, pp in blocks]
            + [pair_mat, pair_mat, pair_mat] + [pair_vec] * 7 + [sblk, MIX_ANY]
        ),
        out_specs=[pl.BlockSpec((lb, pw), lambda s, j, i: (rb(s, i), MIX_RWKV * steps + j)), sblk],
        out_shape=[
            jax.ShapeDtypeStruct(mix.shape, mix.dtype),
            jax.ShapeDtypeStruct((nseq, npair, LANES, LANES), F32),
        ],
        input_output_aliases={26: 0},
        scratch_shapes=[pltpu.VMEM((nsb, pps, LANES, LANES), F32),
                        pltpu.VMEM((3, nsb, SUBLANES, pw), F32),
                        pltpu.VMEM((2, nsb, SUBLANES, LANES), F32)],
        compiler_params=_cparams(("parallel", "parallel", "arbitrary")),
        name="rwkv7",
    )(p, p, p, p, p, mu, mu, mu, mu, mu, shift_init, shift_init, shift_init, shift_init, shift_init,
      wup, aup, lw['rwkv_g_up'].astype(BF16), row(lw['rwkv_w0']), row(lw['rwkv_a0']), row(lw['rwkv_k_k']),
      row(lw['rwkv_k_a']), row(lw['rwkv_r_k']), row(lw['rwkv_ln_w']), row(lw['rwkv_ln_b']), s0, mix)


def _fox_prep_kernel(q_ref, k_ref, v_ref, sm_ref, qw_ref, kw_ref, bf_ref, k_all_ref, v_all_ref,
                     qa_ref, k4_ref, ka_ref, vb_ref, lf_ref, c_ref, v4_ref, carry_ref, *, lb):
    del k_all_ref, v_all_ref
    @pl.when(pl.program_id(1) == 0)
    def _():
        carry_ref[...] = jnp.zeros_like(carry_ref)

    logf = -_softplus(-(sm_ref[...] + bf_ref[...]))
    lf_ref[...] = logf
    c = _dot_exact_lhs(_chunk_tri(lb, lb), logf) + carry_ref[0:1, :]
    c_ref[...] = c
    carry_ref[...] = jnp.broadcast_to(c[lb - 1:], carry_ref.shape)

    qw = qw_ref[...]
    kw = kw_ref[...]
    lane = lax.broadcasted_iota(jnp.int32, (lb, HD), 1)
    q_tail = jnp.where(lane < 2, -1.0, 0.0).astype(BF16)
    for h in range(NH):
        sl = slice(h * HD, (h + 1) * HD)
        a0 = h * FOX_AUG
        qa_ref[:, a0:a0 + HD] = (_rms(q_ref[:, sl], qw) * (HD ** -0.5 * LOG2E)).astype(BF16)
        qa_ref[:, a0 + HD:a0 + FOX_AUG] = q_tail
        kn = _rms(k_ref[:, sl], kw)
        k4_ref[:, h, :] = kn
        v4_ref[:, h, :] = v_ref[:, sl]
        ka_ref[:, a0:a0 + HD] = kn.astype(BF16)
        c2 = c[:, SM_F + h:SM_F + h + 1] * LOG2E
        c_hi = c2.astype(BF16).astype(F32)
        ka_ref[:, a0 + HD:a0 + FOX_AUG] = jnp.where(lane == 0, c_hi, jnp.where(lane == 1, c2 - c_hi, 0.0)).astype(BF16)
    vb_ref[...] = v_ref[...].astype(BF16)


def _fox_prep(p, k_all, v_all, q_w, k_w, b_f, *, layer, nseq, lpad, lb):
    nlb = lpad // lb
    rb = lambda s, i: s * nlb + i
    seg = lambda j: pl.BlockSpec((lb, GW), lambda s, i: (rb(s, i), j))
    vec = pl.BlockSpec((1, LANES), lambda s, i: (0, 0))
    rows = nseq * lpad
    cache_blk = pl.BlockSpec((lb, NH, HD), lambda s, i: (layer * nseq * nlb + rb(s, i), 0, 0))
    bf_lane = jnp.zeros((1, LANES), F32).at[0, SM_F:SM_F + NH].set(b_f)
    return pl.pallas_call(
        functools.partial(_fox_prep_kernel, lb=lb),
        grid=(nseq, nlb),
        in_specs=[seg(0), seg(1), seg(2),
                  pl.BlockSpec((lb, LANES), lambda s, i: (rb(s, i), CB_SMALL)), vec, vec, vec,
                  MIX_ANY, MIX_ANY],
        out_specs=[
            pl.BlockSpec((lb, NH * FOX_AUG), lambda s, i: (rb(s, i), 0)),
            cache_blk,
            pl.BlockSpec((lb, NH * FOX_AUG), lambda s, i: (rb(s, i), 0)),
            pl.BlockSpec((lb, GW), lambda s, i: (rb(s, i), 0)),
            pl.BlockSpec((lb, LANES), lambda s, i: (rb(s, i), 0)),
            pl.BlockSpec((lb, LANES), lambda s, i: (rb(s, i), 0)),
            cache_blk,
        ],
        out_shape=[
            jax.ShapeDtypeStruct((rows, NH * FOX_AUG), BF16),
            jax.ShapeDtypeStruct(k_all.shape, F32),
            jax.ShapeDtypeStruct((rows, NH * FOX_AUG), BF16),
            jax.ShapeDtypeStruct((rows, GW), BF16),
            jax.ShapeDtypeStruct((rows, LANES), F32),
            jax.ShapeDtypeStruct((rows, LANES), F32),
            jax.ShapeDtypeStruct(v_all.shape, F32),
        ],
        input_output_aliases={7: 1, 8: 6},
        scratch_shapes=[pltpu.VMEM((SUBLANES, LANES), F32)],
        compiler_params=_cparams(("parallel", "arbitrary")),
        name="fox_prep",
    )(p, p, p, p, q_w.reshape(1, HD), k_w.reshape(1, HD), bf_lane, k_all, v_all)


def _fox_flash_kernel(q_ref, k_ref, v_ref, c_ref, g_ref, mix_ref, o_ref, m_ref, l_ref, acc_ref, *, tq):
    del mix_ref
    h0 = pl.program_id(1) * FOX_HEADS_PER_STEP
    qi = pl.program_id(2)
    c_all = c_ref[...]
    c_cols = [_lane_col(c_all, SM_F + h0 + e) * LOG2E for e in range(FOX_HEADS_PER_STEP)]
    m_ref[...] = jnp.full_like(m_ref, -1e30)
    l_ref[...] = jnp.zeros_like(l_ref)
    acc_ref[...] = jnp.zeros_like(acc_ref)

    def block(ks, width, diagonal):
        for e in range(FOX_HEADS_PER_STEP):
            sl = slice(e * HD, (e + 1) * HD)
            sa = slice(e * FOX_AUG, (e + 1) * FOX_AUG)
            vb = v_ref[pl.ds(ks, width), sl]
            t = lax.dot_general(q_ref[:, sa], k_ref[pl.ds(ks, width), sa], (((1,), (1,)), ((), ())),
                                preferred_element_type=F32)
            if diagonal:
                ii = lax.broadcasted_iota(jnp.int32, (tq, width), 0)
                jj = lax.broadcasted_iota(jnp.int32, (tq, width), 1)
                t = jnp.where(jj <= ii + (width - tq), t, -jnp.inf)
            m_old = m_ref[e]
            m_new = jnp.maximum(m_old, jnp.max(t, axis=-1, keepdims=True) + c_cols[e])
            alpha = jnp.exp2(m_old - m_new)
            pr = jnp.exp2(t - (m_new - c_cols[e]))
            l_ref[e] = alpha * l_ref[e] + jnp.sum(pr, axis=-1, keepdims=True)
            acc_ref[e] = alpha * acc_ref[e] + _dot(pr, vb)
            m_ref[e] = m_new

    def body(kp, carry):
        block(pl.multiple_of(kp * 2 * tq, 2 * tq), 2 * tq, False)
        return carry

    odd = qi % 2 == 1
    lax.fori_loop(0, jnp.where(odd, qi // 2, jnp.maximum(qi // 2 - 1, 0)), body, 0)

    @pl.when(odd)
    def _():
        block(pl.multiple_of((qi - 1) * tq, tq), 2 * tq, True)

    @pl.when(qi == 0)
    def _():
        block(0, tq, True)

    @pl.when(jnp.logical_and(jnp.logical_not(odd), qi > 0))
    def _():
        block(pl.multiple_of((qi - 2) * tq, tq), 3 * tq, True)
    for e in range(FOX_HEADS_PER_STEP):
        sl = slice(e * HD, (e + 1) * HD)
        o_ref[:, sl] = (acc_ref[e] / l_ref[e] * _sigmoid(g_ref[:, sl])).astype(BF16)


def _fox_flash(qa, ka, vb, c_col, p, mix, *, nseq, lpad, tq):
    nq = lpad // tq
    hw = FOX_HEADS_PER_STEP * HD
    aw = FOX_HEADS_PER_STEP * FOX_AUG
    g_block0 = (CB_FOX + 3 * NH) // FOX_HEADS_PER_STEP
    o_block0 = MIX_FOX * (GW // hw)
    return pl.pallas_call(
        functools.partial(_fox_flash_kernel, tq=tq),
        grid=(nseq, NH // FOX_HEADS_PER_STEP, nq),
        in_specs=[
            pl.BlockSpec((tq, aw), lambda s, h, i: (s * nq + i, h)),
            pl.BlockSpec((lpad, aw), lambda s, h, i: (s, h)),
            pl.BlockSpec((lpad, hw), lambda s, h, i: (s, h)),
            pl.BlockSpec((tq, LANES), lambda s, h, i: (s * nq + i, 0)),
            pl.BlockSpec((tq, hw), lambda s, h, i: (s * nq + i, g_block0 + h)),
            MIX_ANY,
        ],
        out_specs=pl.BlockSpec((tq, hw), lambda s, h, i: (s * nq + i, o_block0 + h)),
        out_shape=jax.ShapeDtypeStruct(mix.shape, mix.dtype),
        input_output_aliases={5: 0},
        scratch_shapes=[pltpu.VMEM((FOX_HEADS_PER_STEP, tq, 1), F32), pltpu.VMEM((FOX_HEADS_PER_STEP, tq, 1), F32),
                        pltpu.VMEM((FOX_HEADS_PER_STEP, tq, HD), F32)],
        compiler_params=_cparams(("parallel", "parallel", "arbitrary")),
        name="fox_flash",
    )(qa, ka, vb, c_col, p, mix)


PAGE_GROUP = 16
PAGE_COLS = PAGE * NH
NQ_PAD = SUBLANES


def _page_sums_kernel(lf_ref, upper_ref, heads_ref, o_ref):
    lf = lf_ref[...]
    o_ref[:, :PAGE_COLS] = _dot_exact_rhs(lf, upper_ref[...])
    o_ref[:, PAGE_COLS:] = _dot_exact_rhs(lf, heads_ref[...])


def _page_sums(cache_lf):
    depth, n_pool = cache_lf.shape[:2]
    rows = depth * n_pool
    tile = _row_tile(rows, 512)
    idx = np.arange(PAGE_COLS)
    same_head = (idx[:, None] % NH) == (idx[None, :] % NH)
    upper = jnp.asarray(same_head & (idx[:, None] // NH > idx[None, :] // NH), BF16)
    heads = jnp.asarray(same_head, BF16)
    const = pl.BlockSpec((PAGE_COLS, PAGE_COLS), lambda i: (0, 0))
    sums = pl.pallas_call(
        _page_sums_kernel,
        grid=(rows // tile,),
        in_specs=[pl.BlockSpec((tile, PAGE_COLS), lambda i: (i, 0)), const, const],
        out_specs=pl.BlockSpec((tile, 2 * PAGE_COLS), lambda i: (i, 0)),
        out_shape=jax.ShapeDtypeStruct((rows, 2 * PAGE_COLS), F32),
        compiler_params=_cparams(("parallel",)),
        name="page_sums",
    )(cache_lf.reshape(rows, PAGE_COLS), upper, heads)
    return sums.reshape(depth, n_pool, 2, PAGE_COLS)


def _fox_sample_kernel(pt_ref, pq_ref, sm_ref, *rest, lvalid, lpad, n_steps):
    del pt_ref
    g = PAGE_GROUP
    kps, vps, sums = rest[:g], rest[g:2 * g], rest[2 * g:3 * g]
    (qw_ref, kw_ref, bf_ref, mix_ref, o_ref, kn_ref, lf_ref,
     qn_s, cq_s, m_s, l_s, acc_s, carry_s) = rest[3 * g:]
    del mix_ref
    i = pl.program_id(1)
    scale = HD ** -0.5
    nq = NQ_PAD
    nrow = NH * nq

    @pl.when(i == 0)
    def _():
        logf = -_softplus(-(sm_ref[...] + bf_ref[...]))
        logf = jnp.where(_row_valid(0, lpad, lvalid, LANES), logf, 0.0)
        lf_ref[...] = logf[:nq]
        c = _dot_exact_lhs(_chunk_tri(lpad, lpad), logf)
        c_t = c.T
        carry_s[...] = jnp.zeros_like(carry_s)
        qi = lax.broadcasted_iota(jnp.int32, (nq, lpad), 0)
        kj = lax.broadcasted_iota(jnp.int32, (nq, lpad), 1)
        for h in range(NH):
            sl = slice(h * HD, (h + 1) * HD)
            rows = slice(h * nq, (h + 1) * nq)
            qn = _rms(pq_ref[:nq, sl], qw_ref[...])
            kn = _rms(pq_ref[:, GW + h * HD:GW + (h + 1) * HD], kw_ref[...])
            vn = pq_ref[:, 2 * GW + h * HD:2 * GW + (h + 1) * HD]
            c_h = c[:nq, SM_F + h:SM_F + h + 1]
            qn_s[rows, :] = qn
            cq_s[rows, :] = c_h
            kn_ref[:, sl] = kn[:nq]
            s = _dot_nt(qn, kn) * scale + c_h - c_t[SM_F + h:SM_F + h + 1, :]
            s = jnp.where(kj <= qi, s, -jnp.inf)
            m = jnp.max(s, axis=-1, keepdims=True)
            pr = jnp.exp(s - m)
            m_s[rows, :] = m
            l_s[rows, :] = jnp.sum(pr, axis=-1, keepdims=True)
            acc_s[rows, :] = _dot(pr, vn)

    run = carry_s[...]
    suffix = [None] * g
    for j in reversed(range(g)):
        suffix[j] = sums[j][0:1, :] + run
        run = run + sums[j][1:2, :]
    carry_s[...] = run

    row_head = lax.broadcasted_iota(jnp.int32, (nrow, PAGE_COLS), 0) // nq
    col_head = lax.broadcasted_iota(jnp.int32, (nrow, PAGE_COLS), 1) % NH
    own = row_head == col_head
    qs = qn_s[...].astype(BF16)
    bias = cq_s[...]
    tiles = [jnp.where(own, _dot_nt(qs, kps[j][...]) * scale + bias + suffix[j], -jnp.inf) for j in range(g)]
    m_old = m_s[...]
    m_new = m_old
    for t in tiles:
        m_new = jnp.maximum(m_new, jnp.max(t, axis=-1, keepdims=True))
    alpha = jnp.exp(m_old - m_new)
    l_new = alpha * l_s[...]
    acc = alpha * acc_s[...]
    for j, t in enumerate(tiles):
        pr = jnp.exp(t - m_new)
        l_new = l_new + jnp.sum(pr, axis=-1, keepdims=True)
        acc = acc + _dot(pr, vps[j][...])
    m_s[...] = m_new
    l_s[...] = l_new
    acc_s[...] = acc

    @pl.when(i == n_steps - 1)
    def _():
        o_ref[...] = jnp.zeros_like(o_ref)
        out = acc_s[...] / l_s[...]
        for h in range(NH):
            sl = slice(h * HD, (h + 1) * HD)
            gate = pq_ref[:nq, 3 * GW + h * HD:3 * GW + (h + 1) * HD]
            o_ref[:nq, sl] = (out[h * nq:(h + 1) * nq] * _sigmoid(gate)).astype(BF16)


def _fox_sample(p, mix, page_table, cache_k, cache_v, page_sums, q_w, k_w, b_f, *, layer, row0, nseq, lpad, lvalid):
    n_pages = page_table.shape[1]
    g = PAGE_GROUP
    assert n_pages % g == 0 and lvalid <= NQ_PAD
    n_steps = n_pages // g
    bf_lane = jnp.zeros((1, LANES), F32).at[0, SM_F:SM_F + NH].set(b_f)
    rb = lambda b: row0 // lpad + b
    vec = pl.BlockSpec((1, LANES), lambda b, i, pt: (0, 0))

    def page_spec(shape, j):
        zeros = (0,) * len(shape)
        return pl.BlockSpec((None, None) + shape,
                            lambda b, i, pt: (layer, pt[b, n_pages - g * (i + 1) + j]) + zeros)

    grid_spec = pltpu.PrefetchScalarGridSpec(
        num_scalar_prefetch=1,
        grid=(nseq, n_steps),
        in_specs=(
            [pl.BlockSpec((lpad, 4 * GW), lambda b, i, pt: (rb(b), CB_FOX)),
             pl.BlockSpec((lpad, LANES), lambda b, i, pt: (rb(b), CB_SMALL))]
            + [page_spec((PAGE_COLS, HD), j) for j in range(g)]
            + [page_spec((PAGE_COLS, HD), j) for j in range(g)]
            + [page_spec((2, PAGE_COLS), j) for j in range(g)]
            + [vec, vec, vec, MIX_ANY]
        ),
        out_specs=[
            pl.BlockSpec((lpad, GW), lambda b, i, pt: (rb(b), MIX_FOX)),
            pl.BlockSpec((None, NQ_PAD, GW), lambda b, i, pt: (b, 0, 0)),
            pl.BlockSpec((None, NQ_PAD, LANES), lambda b, i, pt: (b, 0, 0)),
        ],
        scratch_shapes=[
            pltpu.VMEM((NH * NQ_PAD, HD), F32),
            pltpu.VMEM((NH * NQ_PAD, 1), F32),
            pltpu.VMEM((NH * NQ_PAD, 1), F32),
            pltpu.VMEM((NH * NQ_PAD, 1), F32),
            pltpu.VMEM((NH * NQ_PAD, HD), F32),
            pltpu.VMEM((1, PAGE_COLS), F32),
        ],
    )
    return pl.pallas_call(
        functools.partial(_fox_sample_kernel, lvalid=lvalid, lpad=lpad, n_steps=n_steps),
        grid_spec=grid_spec,
        out_shape=[
            jax.ShapeDtypeStruct(mix.shape, mix.dtype),
            jax.ShapeDtypeStruct((nseq, NQ_PAD, GW), F32),
            jax.ShapeDtypeStruct((nseq, NQ_PAD, LANES), F32),
        ],
        input_output_aliases={3 * g + 6: 0},
        compiler_params=_cparams(("parallel", "arbitrary")),
        name="fox_sample",
    )(page_table, p, p, *([cache_k] * g), *([cache_v] * g), *([page_sums] * g),
      q_w.reshape(1, HD), k_w.reshape(1, HD), bf_lane, mix)


W_IN_SHIFT = W_GDN_Z % LANES
_COPY, _SHIFT, _SMALL, _ZERO = 0, 1, 2, 3


def _w_in_plan():
    kind = np.zeros(NP_COLS // LANES, np.int32)
    src = np.zeros(NP_COLS // LANES, np.int32)

    def put(cb, n, first_src_block, k):
        kind[cb:cb + n] = k
        src[cb:cb + n] = first_src_block + np.arange(n)

    put(CB_FOX, 16, (W_FOX - W_IN_SHIFT) // LANES, _SHIFT)
    put(CB_RET, 16, (W_RET - W_IN_SHIFT) // LANES, _SHIFT)
    put(CB_GDN, 12, 0, _COPY)
    put(CB_GDN_Z, 4, (W_GDN_Z - W_IN_SHIFT) // LANES, _SHIFT)
    put(CB_RWKV, RWKV_COLS // LANES, (W_RWKV - W_IN_SHIFT) // LANES, _SHIFT)
    put(CB_SMALL, 1, W_GDN_AB // LANES, _SMALL)
    put(CB_SMALL + 1, 1, 0, _ZERO)
    src_b = np.where(kind == _SHIFT, (src + 1) * (LANES // SUBLANES),
                     np.where(kind == _SMALL, W_FOX_F // SUBLANES, 0))
    return jnp.asarray(kind), jnp.asarray(src), jnp.asarray(src_b.astype(np.int32))


def _prep_w_in_kernel(kind_ref, sa_ref, sb_ref, a_ref, b_ref, o_ref):
    del sa_ref, sb_ref
    kind = kind_ref[pl.program_id(0)]
    depth = o_ref.shape[0]
    row8 = lax.broadcasted_iota(jnp.int32, (SUBLANES, o_ref.shape[2]), 0)

    @pl.when(kind == _COPY)
    def _():
        for l in range(depth):
            o_ref[l] = a_ref[:, l, :].astype(BF16)

    @pl.when(kind == _SHIFT)
    def _():
        for l in range(depth):
            o_ref[l] = jnp.concatenate([a_ref[:, l, :][W_IN_SHIFT:], b_ref[:, l, :]], axis=0).astype(BF16)

    @pl.when(kind == _SMALL)
    def _():
        for l in range(depth):
            f_rows = jnp.where(row8 < NH, b_ref[:, l, :], 0.0)
            zeros = jnp.zeros((LANES - SM_F - SUBLANES, o_ref.shape[2]), F32)
            o_ref[l] = jnp.concatenate([a_ref[:, l, :][:SM_F], f_rows, zeros], axis=0).astype(BF16)

    @pl.when(kind == _ZERO)
    def _():
        o_ref[...] = jnp.zeros_like(o_ref)


def _prep_w_in(w_in):
    depth, k, _ = w_in.shape
    w_t = jnp.transpose(w_in, (2, 0, 1))
    kind, src_a, src_b = _w_in_plan()
    grid_spec = pltpu.PrefetchScalarGridSpec(
        num_scalar_prefetch=3,
        grid=(NP_COLS // LANES,),
        in_specs=[
            pl.BlockSpec((LANES, depth, k), lambda j, kd, sa, sb: (sa[j], 0, 0)),
            pl.BlockSpec((SUBLANES, depth, k), lambda j, kd, sa, sb: (sb[j], 0, 0)),
        ],
        out_specs=pl.BlockSpec((depth, LANES, k), lambda j, kd, sa, sb: (0, j, 0)),
    )
    return pl.pallas_call(
        _prep_w_in_kernel,
        grid_spec=grid_spec,
        out_shape=jax.ShapeDtypeStruct((depth, NP_COLS, k), BF16),
        compiler_params=_cparams(("parallel",)),
        name="prep_w_in",
    )(kind, src_a, src_b, w_t, w_t)


def _rope_tables(pos):
    half = HD // 2
    inv = 1.0 / (ROPE_BASE ** jnp.linspace(0.0, 1.0, half, dtype=F32))
    ang = pos.astype(F32)[:, None] * inv[None, :]
    cos, sin = jnp.cos(ang), jnp.sin(ang)
    return jnp.concatenate([cos, cos], axis=-1), jnp.concatenate([-sin, sin], axis=-1)


def _state_tile(state, nrows):
    b, _, c = state.shape
    return jnp.concatenate([jnp.zeros((b, SUBLANES - nrows, c), F32), state], axis=1)


def _rwkv_pair_states(s):
    b = s.shape[0]
    s = s.reshape(b, 4, 2, RWKV_HEAD, RWKV_HEAD)
    z = jnp.zeros_like(s[:, :, 0])
    top = jnp.concatenate([s[:, :, 0], z], axis=-1)
    bot = jnp.concatenate([z, s[:, :, 1]], axis=-1)
    return jnp.concatenate([top, bot], axis=-2)


def _rwkv_unpair_states(sp):
    b = sp.shape[0]
    a = sp[:, :, :RWKV_HEAD, :RWKV_HEAD]
    c = sp[:, :, RWKV_HEAD:, RWKV_HEAD:]
    return jnp.stack([a, c], axis=2).reshape(b, 8, RWKV_HEAD, RWKV_HEAD)


def kernel(x_prompt, x_sample, cache_fox_k, cache_fox_v, cache_fox_logf, cache_mem_k, cache_mem_v, state_gdn_conv, state_gdn_S, state_ret_S, state_rwkv_shift, state_rwkv_S, page_table, mem_prompt, norm_mix, w_in, gdn_conv_w, gdn_A_log, gdn_dt_bias, gdn_norm, rwkv_mu, rwkv_w0, rwkv_w_up, rwkv_a0, rwkv_a_up, rwkv_g_up, rwkv_k_k, rwkv_k_a, rwkv_r_k, rwkv_ln_w, rwkv_ln_b, fox_b_f, fox_q_norm, fox_k_norm, w_out, norm_x, norm_mem, xattn_wq, xattn_wkv, xattn_q_norm, xattn_k_norm, xattn_wo, norm_ffn, ffn_w_gate, ffn_w_up, ffn_w_down):
    weights = {
        'norm_mix': norm_mix, 'w_in': w_in, 'gdn_conv_w': gdn_conv_w, 'gdn_A_log': gdn_A_log,
        'gdn_dt_bias': gdn_dt_bias, 'gdn_norm': gdn_norm, 'rwkv_mu': rwkv_mu, 'rwkv_w0': rwkv_w0,
        'rwkv_w_up': rwkv_w_up, 'rwkv_a0': rwkv_a0, 'rwkv_a_up': rwkv_a_up, 'rwkv_g_up': rwkv_g_up,
        'rwkv_k_k': rwkv_k_k, 'rwkv_k_a': rwkv_k_a, 'rwkv_r_k': rwkv_r_k, 'rwkv_ln_w': rwkv_ln_w,
        'rwkv_ln_b': rwkv_ln_b, 'fox_b_f': fox_b_f, 'fox_q_norm': fox_q_norm, 'fox_k_norm': fox_k_norm,
        'w_out': w_out, 'norm_x': norm_x, 'norm_mem': norm_mem, 'xattn_wq': xattn_wq, 'xattn_wkv': xattn_wkv,
        'xattn_q_norm': xattn_q_norm, 'xattn_k_norm': xattn_k_norm, 'xattn_wo': xattn_wo,
        'norm_ffn': norm_ffn, 'ffn_w_gate': ffn_w_gate, 'ffn_w_up': ffn_w_up, 'ffn_w_down': ffn_w_down,
    }
    depth = w_in.shape[0]
    bp, lp, d = x_prompt.shape
    bs, ls, _ = x_sample.shape
    n_pages = page_table.shape[1]
    past_len = n_pages * PAGE
    tp = bp * lp
    ts = bs * SAMPLE_PAD
    tt = tp + ts
    tm = _row_tile(tt)
    lb_p = min(256, lp)
    assert ls >= CONV_WIDTH - 1 and ls <= SUBLANES and lp % lb_p == 0

    xs_pad = jnp.zeros((bs, SAMPLE_PAD, d), F32).at[:, :ls].set(x_sample)
    x = jnp.concatenate([x_prompt.reshape(tp, d), xs_pad.reshape(ts, d)], axis=0)

    cos_p, sin_p = _rope_tables(jnp.arange(lp, dtype=jnp.int32))
    cos_s, sin_s = _rope_tables(jnp.tile(past_len + jnp.arange(SAMPLE_PAD, dtype=jnp.int32), bs))
    log_gamma = jnp.log(1.0 - jnp.exp2(-(RET_GAMMA_BASE + jnp.arange(NH, dtype=F32))))
    n_pool = cache_fox_k.shape[1]
    cache_k = cache_fox_k.reshape(depth, n_pool, PAGE_COLS, HD)
    cache_v = cache_fox_v.reshape(depth, n_pool, PAGE_COLS, HD)
    page_sums = _page_sums(cache_fox_logf.reshape(depth, n_pool, 1, PAGE_COLS))
    zeros_s = jnp.zeros((bp, NH, HD, HD), F32)
    zeros_conv = jnp.zeros((bp, SUBLANES, 3 * GW), F32)
    zeros_shift = jnp.zeros((bp, SUBLANES, RWKV_COLS), F32)

    w_in_t = _prep_w_in(w_in)
    fox_k_all = jnp.zeros((depth * tp, NH, HD), F32)
    fox_v_all = jnp.zeros((depth * tp, NH, HD), F32)

    outs_p, outs_s, mem_ks, mem_vs = [], [], [], []
    for l in range(depth):
        lw = {name: arr[l] for name, arr in weights.items()}
        p = _norm_matmul(x, lw['norm_mix'], w_in_t, l, tm=tm, tn=1024,
                         transposed_w=True)

        o_mix = jnp.zeros((tt, 4 * GW), BF16)
        gp = dict(row0=0, nseq=bp, lpad=lp, lvalid=lp, lb=lb_p)
        o_mix, gdn_s_p = _gdn(p, o_mix, lw['gdn_conv_w'], zeros_conv, zeros_s, lw['gdn_A_log'], lw['gdn_dt_bias'],
                              lw['gdn_norm'], **{**gp, 'lb': min(GDN_LB, lp)})
        o_mix, ret_s_p = _ret(p, o_mix, cos_p, sin_p, log_gamma, zeros_s, **{**gp, 'lb': min(RET_LB, lp)})
        o_mix, rwkv_s_p = _rwkv(p, o_mix, lw, zeros_shift, zeros_s, **{**gp, 'lb': min(RWKV_LB, lp)})
        qa, fox_k_all, ka, vb, lf, c_col, fox_v_all = _fox_prep(
            p, fox_k_all, fox_v_all, lw['fox_q_norm'], lw['fox_k_norm'], lw['fox_b_f'],
            layer=l, nseq=bp, lpad=lp, lb=lb_p)
        o_mix = _fox_flash(qa, ka, vb, c_col, p, o_mix, nseq=bp, lpad=lp, tq=min(FOX_TQ, lp))

        gs = dict(row0=tp, nseq=bs, lpad=SAMPLE_PAD, lvalid=ls, lb=ts, nsb=bs)
        o_mix, gdn_s_s = _gdn(p, o_mix, lw['gdn_conv_w'], _state_tile(state_gdn_conv[l], CONV_WIDTH - 1),
                              state_gdn_S[l], lw['gdn_A_log'], lw['gdn_dt_bias'], lw['gdn_norm'], **gs)
        o_mix, ret_s_s = _ret(p, o_mix, cos_s, sin_s, log_gamma, state_ret_S[l], **gs)
        o_mix, rwkv_s_s = _rwkv(p, o_mix, lw, _state_tile(state_rwkv_shift[l], 1),
                                _rwkv_pair_states(state_rwkv_S[l]), **gs)
        o_mix, kn_s, lf_s = _fox_sample(p, o_mix, page_table, cache_k, cache_v, page_sums,
                                        lw['fox_q_norm'], lw['fox_k_norm'], lw['fox_b_f'],
                                        layer=l, row0=tp, nseq=bs, lpad=SAMPLE_PAD, lvalid=ls)
        x = _matmul_res(o_mix, w_out, l, x, tm=tm, tn=512)

        kv = _norm_matmul(mem_prompt.reshape(bp * N_MEM, d), lw['norm_mem'], xattn_wkv, l,
                          tm=256, tn=XW, head_w=lw['xattn_k_norm'], norm_tiles=1)
        mk = kv[:, :XW].reshape(bp, N_MEM, XW)
        mv = kv[:, XW:].reshape(bp, N_MEM, XW)
        q = _norm_matmul(x, lw['norm_x'], xattn_wq, l, tm=tm, tn=XW,
                         head_w=lw['xattn_q_norm'], norm_tiles=1)
        xo_p = _xattn(q, mk, mv, row0=0, nrows=tp, tq=lb_p, rows_per_seq=lp)
        xo_s = _xattn(q, cache_mem_k[l].reshape(bs, N_MEM, XW), cache_mem_v[l].reshape(bs, N_MEM, XW),
                      row0=tp, nrows=ts, tq=SAMPLE_PAD, rows_per_seq=SAMPLE_PAD)
        x = _matmul_res(jnp.concatenate([xo_p, xo_s], axis=0), xattn_wo, l, x, tm=tm, tn=512)

        hidden = _swiglu_up(x, lw['norm_ffn'], ffn_w_gate, ffn_w_up, l, tm=tm, tn=512)
        x = _matmul_res(hidden, ffn_w_down, l, x, tm=tm, tn=256)

        c0 = CB_GDN * LANES
        r0 = CB_RWKV * LANES
        v0 = (CB_FOX + 2 * NH) * LANES

        def last_rows(row_end, n, col0, width, nseq, stride):
            return jnp.stack([lax.slice(p, (b * stride + row_end - n, col0), (b * stride + row_end, col0 + width))
                              for b in range(nseq)], axis=0)

        ps_v = lax.slice(p, (tp, v0), (tt, v0 + GW)).reshape(bs, SAMPLE_PAD, NH, HD)
        outs_p.append((
            None, None,
            lf.reshape(bp, lp, LANES)[:, :, SM_F:SM_F + NH],
            last_rows(lp, CONV_WIDTH - 1, c0, 3 * GW, bp, lp),
            gdn_s_p, ret_s_p,
            last_rows(lp, 1, r0, RWKV_COLS, bp, lp),
            _rwkv_unpair_states(rwkv_s_p),
        ))
        outs_s.append((
            kn_s[:, :ls].reshape(bs, ls, NH, HD),
            ps_v[:, :ls],
            lf_s[:, :ls, SM_F:SM_F + NH],
            last_rows(tp + ls, CONV_WIDTH - 1, c0, 3 * GW, bs, SAMPLE_PAD),
            gdn_s_s, ret_s_s,
            last_rows(tp + ls, 1, r0, RWKV_COLS, bs, SAMPLE_PAD),
            _rwkv_unpair_states(rwkv_s_s),
        ))
        mem_ks.append(mk.reshape(bp, N_MEM, NH, HD))
        mem_vs.append(mv.reshape(bp, N_MEM, NH, HD))

    stk = lambda seq, i: jnp.stack([e[i] for e in seq], axis=0)
    yp = x[:tp].reshape(bp, lp, d)
    ys = x[tp:].reshape(bs, SAMPLE_PAD, d)[:, :ls]
    cache_shape = (depth, bp, lp, NH, HD)
    return (yp, ys, fox_k_all.reshape(cache_shape), fox_v_all.reshape(cache_shape), stk(outs_p, 2),
            jnp.stack(mem_ks, 0), jnp.stack(mem_vs, 0),
            stk(outs_p, 3), stk(outs_p, 4), stk(outs_p, 5), stk(outs_p, 6), stk(outs_p, 7),
            stk(outs_s, 0), stk(outs_s, 1), stk(outs_s, 2), stk(outs_s, 3), stk(outs_s, 4), stk(outs_s, 5),
            stk(outs_s, 6), stk(outs_s, 7))
```
